```python
import math
import jax, jax.numpy as jnp
from jax import lax
import numpy as np

D_MODEL = 1024
BATCH = 8
SEQ = 2048
DEPTH = 1
DEC_BATCH = 128
DEC_SEQ = 4
PAST_LEN = 2048
PAGE_SIZE = 128

N_META = 16
ATTN_WIDTH = D_MODEL // 2
SSM_WIDTH = D_MODEL - ATTN_WIDTH
N_HEADS = 4
QK_DIM = 64
HEAD_DIM = 2 * QK_DIM
SSM_GROUP = 16
N_SSM_GROUPS = SSM_WIDTH // SSM_GROUP
SSM_STATE = 64
D_FF = 2816
CONV_WIDTH = 3
Q_BLOCK = 128
EPS = 1e-6
NEG = -1e30
PROJ_WIDTH = 3 * N_HEADS * HEAD_DIM + SSM_WIDTH

kernel_name = "hymba_s5_diffattn_convffn_step"


def _rmsnorm(x, g):
    xf = x.astype(jnp.float32)
    r = lax.rsqrt(jnp.mean(xf * xf, axis=-1, keepdims=True) + EPS)
    return (xf * r).astype(x.dtype) * g


def _project(xn, w_in, q_norm, k_norm):
    b, t, _ = xn.shape
    a = N_HEADS * HEAD_DIM
    proj = xn @ w_in
    q, k, v, u = jnp.split(proj, [a, 2 * a, 3 * a], axis=-1)
    q = _rmsnorm(q.reshape(b, t, N_HEADS, 2, QK_DIM), q_norm)
    k = _rmsnorm(k.reshape(b, t, N_HEADS, 2, QK_DIM), k_norm)
    v = v.reshape(b, t, N_HEADS, HEAD_DIM)
    return q, k, v, u


def _scores(q, k):
    return jnp.einsum('bthcd,bshcd->bhcts', q, k).astype(jnp.float32) * (QK_DIM ** -0.5)


def _diff_weights(s, lam):
    p = jax.nn.softmax(s, axis=-1)
    return p[:, :, 0] - lam * p[:, :, 1]


def _attn_prompt(q, k, v, lam):
    L = q.shape[1]
    bounds = [(0, N_META)] + [(N_META + i * Q_BLOCK, N_META + (i + 1) * Q_BLOCK)
                              for i in range((L - N_META) // Q_BLOCK)]
    outs = []
    for s0, e0 in bounds:
        s = _scores(q[:, s0:e0], k[:, :e0])
        mask = jnp.arange(e0)[None, :] <= jnp.arange(s0, e0)[:, None]
        w = _diff_weights(jnp.where(mask, s, NEG), lam)
        outs.append(jnp.einsum('bhts,bshe->bthe', w.astype(v.dtype), v[:, :e0]))
    return jnp.concatenate(outs, axis=1)


def _attn_sample(q, k_new, v_new, k_past, v_past, lam):
    T = q.shape[1]
    P = k_past.shape[1]
    s_past = _scores(q, k_past)
    s_new = jnp.where(jnp.tril(jnp.ones((T, T), dtype=bool)), _scores(q, k_new), NEG)
    w = _diff_weights(jnp.concatenate([s_past, s_new], axis=-1), lam).astype(v_new.dtype)
    return (jnp.einsum('bhts,bshe->bthe', w[..., :P], v_past)
            + jnp.einsum('bhts,bshe->bthe', w[..., P:], v_new))


def _head_norm(o, sub_g, lam_init):
    b, t = o.shape[:2]
    return (_rmsnorm(o, sub_g) * (1.0 - lam_init)).reshape(b, t, N_HEADS * HEAD_DIM)


def _cplx_combine(e1, e2):
    a1r, a1i, b1r, b1i = e1
    a2r, a2i, b2r, b2i = e2
    return (a2r * a1r - a2i * a1i, a2r * a1i + a2i * a1r,
            a2r * b1r - a2i * b1i + b2r, a2r * b1i + a2i * b1r + b2i)


def _ssm(u, h0r, h0i, a_re, a_im, log_dt, b_re, b_im, c_re, c_im, d_skip, w_glu, b_glu):
    f32 = jnp.float32
    b, t, _ = u.shape
    ug = u.astype(f32).reshape(b, t, N_SSM_GROUPS, SSM_GROUP)
    ar, ai = a_re.astype(f32), a_im.astype(f32)
    dt = jnp.exp(log_dt.astype(f32))[:, None]
    mag = jnp.exp(ar * dt)
    abr, abi = mag * jnp.cos(ai * dt), mag * jnp.sin(ai * dt)
    den = ar * ar + ai * ai
    nr, ni = abr - 1.0, abi
    gr, gi = (nr * ar + ni * ai) / den, (ni * ar - nr * ai) / den
    br, bi = b_re.astype(f32), b_im.astype(f32)
    bbr = gr[..., None] * br - gi[..., None] * bi
    bbi = gr[..., None] * bi + gi[..., None] * br
    xr = jnp.einsum('btgc,gpc->btgp', ug, bbr)
    xi = jnp.einsum('btgc,gpc->btgp', ug, bbi)
    if h0r is not None:
        h0r, h0i = h0r.astype(f32), h0i.astype(f32)
        xr = xr.at[:, 0].add(abr * h0r - abi * h0i)
        xi = xi.at[:, 0].add(abr * h0i + abi * h0r)
    A_r = jnp.broadcast_to(abr, xr.shape)
    A_i = jnp.broadcast_to(abi, xr.shape)
    _, _, hr, hi = lax.associative_scan(_cplx_combine, (A_r, A_i, xr, xi), axis=1)
    y = (jnp.einsum('btgp,gcp->btgc', hr, c_re.astype(f32))
         - jnp.einsum('btgp,gcp->btgc', hi, c_im.astype(f32))
         + d_skip.astype(f32) * ug).reshape(b, t, SSM_WIDTH)
    g = jax.nn.gelu(y)
    out = g * jax.nn.sigmoid(g @ w_glu.astype(f32) + b_glu.astype(f32))
    return out.astype(u.dtype), hr[:, -1], hi[:, -1]


def _conv_ffn(xn, conv_state, w_gate, w_up, conv_w, conv_b, w_down):
    a = xn @ w_gate
    c = xn @ w_up
    b, T, _ = a.shape
    hist = jnp.zeros((b, CONV_WIDTH - 1, D_FF), a.dtype) if conv_state is None else conv_state.astype(a.dtype)
    ap = jnp.concatenate([hist, a], axis=1)
    conv = conv_b + sum(conv_w[j] * ap[:, j:j + T] for j in range(CONV_WIDTH))
    y = (jax.nn.gelu(conv) * c) @ w_down
    return y, ap[:, -(CONV_WIDTH - 1):]


def setup_inputs(seed: int = 0) -> dict:
    key = jax.random.key(seed)
    ks = jax.random.split(key, 40)
    f32 = jnp.float32
    nrm = lambda k, shape, s: jax.random.normal(k, shape, f32) * s
    n_pages = PAST_LEN // PAGE_SIZE
    n_pool = (DEC_BATCH * n_pages * 5 + 3) // 4
    G, P, C = N_SSM_GROUPS, SSM_STATE, SSM_GROUP
    page_table = jax.random.permutation(ks[0], n_pool)[:DEC_BATCH * n_pages].reshape(DEC_BATCH, n_pages).astype(jnp.int32)
    a_im = jnp.broadcast_to(math.pi * jnp.arange(P, dtype=f32), (DEPTH, G, P)) + nrm(ks[1], (DEPTH, G, P), 0.01)
    return {
        "x_prompt": nrm(ks[2], (BATCH, SEQ, D_MODEL), 1.0),
        "x_sample": nrm(ks[3], (DEC_BATCH, DEC_SEQ, D_MODEL), 1.0),
        "cache_k": nrm(ks[4], (DEPTH, n_pool, PAGE_SIZE, N_HEADS, HEAD_DIM), 1.0),
        "cache_v": nrm(ks[5], (DEPTH, n_pool, PAGE_SIZE, N_HEADS, HEAD_DIM), 1.0),
        "state_ssm_re": nrm(ks[6], (DEPTH, DEC_BATCH, G, P), 1.0),
        "state_ssm_im": nrm(ks[7], (DEPTH, DEC_BATCH, G, P), 1.0),
        "state_ffn_conv": nrm(ks[8], (DEPTH, DEC_BATCH, CONV_WIDTH - 1, D_FF), 1.0),
        "page_table": page_table,
        "meta_tokens": nrm(ks[9], (N_META, D_MODEL), 1.0),
        "norm1": 1.0 + nrm(ks[10], (DEPTH, D_MODEL), 0.01),
        "w_in": nrm(ks[11], (DEPTH, D_MODEL, PROJ_WIDTH), D_MODEL ** -0.5),
        "q_norm": 1.0 + nrm(ks[12], (DEPTH, 2, QK_DIM), 0.01),
        "k_norm": 1.0 + nrm(ks[13], (DEPTH, 2, QK_DIM), 0.01),
        "lam_q": nrm(ks[14], (DEPTH, 2, QK_DIM), 0.1),
        "lam_k": nrm(ks[15], (DEPTH, 2, QK_DIM), 0.1),
        "sub_norm": 1.0 + nrm(ks[16], (DEPTH, HEAD_DIM), 0.01),
        "ssm_a_re": -0.5 + nrm(ks[17], (DEPTH, G, P), 0.01),
        "ssm_a_im": a_im,
        "ssm_log_dt": jax.random.uniform(ks[18], (DEPTH, G), f32, math.log(1e-3), math.log(1e-1)),
        "ssm_b_re": nrm(ks[19], (DEPTH, G, P, C), (2 * C) ** -0.5),
        "ssm_b_im": nrm(ks[20], (DEPTH, G, P, C), (2 * C) ** -0.5),
        "ssm_c_re": nrm(ks[21], (DEPTH, G, C, P), P ** -0.5),
        "ssm_c_im": nrm(ks[22], (DEPTH, G, C, P), P ** -0.5),
        "ssm_d": nrm(ks[23], (DEPTH, G, C), 1.0),
        "w_glu": nrm(ks[24], (DEPTH, SSM_WIDTH, SSM_WIDTH), SSM_WIDTH ** -0.5),
        "b_glu": nrm(ks[25], (DEPTH, SSM_WIDTH), 0.01),
        "w_out": nrm(ks[26], (DEPTH, D_MODEL, D_MODEL), D_MODEL ** -0.5),
        "norm2": 1.0 + nrm(ks[27], (DEPTH, D_MODEL), 0.01),
        "w_gate": nrm(ks[28], (DEPTH, D_MODEL, D_FF), D_MODEL ** -0.5),
        "w_up": nrm(ks[29], (DEPTH, D_MODEL, D_FF), D_MODEL ** -0.5),
        "ffn_conv_w": nrm(ks[30], (DEPTH, CONV_WIDTH, D_FF), 0.5),
        "ffn_conv_b": nrm(ks[31], (DEPTH, D_FF), 0.01),
        "w_down": nrm(ks[32], (DEPTH, D_FF, D_MODEL), D_FF ** -0.5),
    }


def reference(x_prompt, x_sample, cache_k, cache_v, state_ssm_re, state_ssm_im, state_ffn_conv,
              page_table, meta_tokens, norm1, w_in, q_norm, k_norm, lam_q, lam_k, sub_norm,
              ssm_a_re, ssm_a_im, ssm_log_dt, ssm_b_re, ssm_b_im, ssm_c_re, ssm_c_im, ssm_d,
              w_glu, b_glu, w_out, norm2, w_gate, w_up, ffn_conv_w, ffn_conv_b, w_down):
    f32 = jnp.float32
    b = x_prompt.shape[0]
    db, T = x_sample.shape[:2]
    meta = jnp.broadcast_to(meta_tokens.astype(x_prompt.dtype), (b, N_META, D_MODEL))
    xp = jnp.concatenate([meta, x_prompt], axis=1)
    xs = x_sample
    kp_l, vp_l, ks_l, vs_l, srp_l, sip_l, srs_l, sis_l, cp_l, cs_l = ([] for _ in range(10))
    for l in range(DEPTH):
        lam_init = 0.8 - 0.6 * math.exp(-0.3 * l)
        lq, lk = lam_q[l].astype(f32), lam_k[l].astype(f32)
        lam = jnp.exp(jnp.sum(lq[0] * lk[0])) - jnp.exp(jnp.sum(lq[1] * lk[1])) + lam_init
        ssm_p = (ssm_a_re[l], ssm_a_im[l], ssm_log_dt[l], ssm_b_re[l], ssm_b_im[l],
                 ssm_c_re[l], ssm_c_im[l], ssm_d[l], w_glu[l], b_glu[l])
        ffn_p = (w_gate[l], w_up[l], ffn_conv_w[l], ffn_conv_b[l], w_down[l])

        q, k, v, u = _project(_rmsnorm(xp, norm1[l]), w_in[l], q_norm[l], k_norm[l])
        att = _head_norm(_attn_prompt(q, k, v, lam), sub_norm[l], lam_init)
        ssm_out, hr, hi = _ssm(u, None, None, *ssm_p)
        xp = xp + jnp.concatenate([att, ssm_out.astype(att.dtype)], axis=-1) @ w_out[l]
        f, cst = _conv_ffn(_rmsnorm(xp, norm2[l]), None, *ffn_p)
        xp = xp + f
        L = xp.shape[1]
        kp_l.append(k.reshape(b, L, N_HEADS, HEAD_DIM)); vp_l.append(v)
        srp_l.append(hr); sip_l.append(hi); cp_l.append(cst)

        q, k, v, u = _project(_rmsnorm(xs, norm1[l]), w_in[l], q_norm[l], k_norm[l])
        k_past = cache_k[l, page_table].reshape(db, -1, N_HEADS, 2, QK_DIM)
        v_past = cache_v[l, page_table].reshape(db, -1, N_HEADS, HEAD_DIM)
        att = _head_norm(_attn_sample(q, k, v, k_past.astype(q.dtype), v_past.astype(v.dtype), lam),
                         sub_norm[l], lam_init)
        ssm_out, hr, hi = _ssm(u, state_ssm_re[l], state_ssm_im[l], *ssm_p)
        xs = xs + jnp.concatenate([att, ssm_out.astype(att.dtype)], axis=-1) @ w_out[l]
        f, cst = _conv_ffn(_rmsnorm(xs, norm2[l]), state_ffn_conv[l], *ffn_p)
        xs = xs + f
        ks_l.append(k.reshape(db, T, N_HEADS, HEAD_DIM)); vs_l.append(v)
        srs_l.append(hr); sis_l.append(hi); cs_l.append(cst)

    y_prompt = xp[:, N_META:]
    y_sample = xs
    return (y_prompt, y_sample,
            jnp.stack(kp_l), jnp.stack(vp_l), jnp.stack(ks_l), jnp.stack(vs_l),
            jnp.stack(srp_l), jnp.stack(sip_l), jnp.stack(srs_l), jnp.stack(sis_l),
            jnp.stack(cp_l), jnp.stack(cs_l))
```

```python
import functools
import math

import jax
import jax.numpy as jnp
from jax import lax
from jax.experimental import pallas as pl
from jax.experimental.pallas import tpu as pltpu

N_META = 16
N_HEADS = 4
QK_DIM = 64
HEAD_DIM = 2 * QK_DIM
SSM_GROUP = 16
SSM_STATE = 64
EPS = 1e-6
NEG = -1e30
LAM_INIT = 0.8 - 0.6 * math.exp(-0.3 * 0)

LANES = 128
MXU = 256
VMEM_LIMIT = 56 * 1024 * 1024

F32 = jnp.float32
BF16 = jnp.bfloat16


def _dot(a, b):
    return jnp.dot(a, b, preferred_element_type=F32)


def _dot_nt(a, b):
    return lax.dot_general(a, b, (((1,), (1,)), ((), ())), preferred_element_type=F32)


def _const_spec(shape):
    nd = len(shape)
    return pl.BlockSpec(shape, lambda *_: (0,) * nd, pipeline_mode=pl.Buffered(1))


def _lam(lq_ref, lk_ref):
    e = jnp.exp(jnp.sum(lq_ref[...] * lk_ref[...], axis=1, keepdims=True))
    return e[0:1] - e[1:2] + LAM_INIT


def _proj_kernel(x_ref, g1_ref, w_ref, qg_ref, kg_ref, ones_ref,
                 qb_ref, k_ref, kb_ref, v_ref, vb_ref, u_ref):
    a = N_HEADS * HEAD_DIM
    x = x_ref[...]
    r = lax.rsqrt(jnp.mean(x * x, axis=-1, keepdims=True) + EPS)
    xn = ((x * r) * g1_ref[...]).astype(BF16)
    proj = _dot(xn, w_ref[...])

    def comp_norm(z, g):
        z2 = (z * z).astype(BF16)
        ms = jnp.concatenate(
            [_dot(z2[:, j * MXU:(j + 1) * MXU], ones_ref[...]) for j in range(a // MXU)], axis=1)
        return (z * lax.rsqrt(ms + EPS)) * g

    qn = comp_norm(proj[:, :a], qg_ref[...])
    kn = comp_norm(proj[:, a:2 * a], kg_ref[...])
    v = proj[:, 2 * a:3 * a]
    qb_ref[...] = (qn * (QK_DIM ** -0.5)).astype(BF16)
    k_ref[...] = kn
    kb_ref[...] = kn.astype(BF16)
    v_ref[...] = v
    vb_ref[...] = v.astype(BF16)
    u_ref[...] = proj[:, 3 * a:]


def _project(x2d, tm, nb, p, u_time_major):
    rows, d = x2d.shape
    a = N_HEADS * HEAD_DIM
    per_b = rows // nb
    nt = per_b // tm
    grid = (nb, nt)
    row_spec = lambda w: pl.BlockSpec((tm, w), lambda b, t: (b * nt + t, 0))
    if u_time_major:
        u_shape = jax.ShapeDtypeStruct((per_b, nb * a), F32)
        u_spec = pl.BlockSpec((tm, a), lambda b, t: (t, b))
    else:
        u_shape = jax.ShapeDtypeStruct((rows, a), F32)
        u_spec = row_spec(a)
    out_shape = (jax.ShapeDtypeStruct((rows, a), BF16), jax.ShapeDtypeStruct((rows, a), F32),
                 jax.ShapeDtypeStruct((rows, a), BF16), jax.ShapeDtypeStruct((rows, a), F32),
                 jax.ShapeDtypeStruct((rows, a), BF16), u_shape)
    return pl.pallas_call(
        _proj_kernel,
        grid=grid,
        in_specs=[row_spec(d), _const_spec((1, d)), _const_spec(p["w_in"].shape),
                  _const_spec((1, a)), _const_spec((1, a)), _const_spec((MXU, MXU))],
        out_specs=(row_spec(a), row_spec(a), row_spec(a), row_spec(a), row_spec(a), u_spec),
        out_shape=out_shape,
        compiler_params=pltpu.CompilerParams(
            dimension_semantics=("arbitrary", "arbitrary"), vmem_limit_bytes=VMEM_LIMIT),
        name="proj",
    )(x2d, p["norm1"], p["w_in"], p["q_gain"], p["k_gain"], p["ones_blk"])


def _attn_prompt_kernel(q_ref, k_ref, v_ref, lq_ref, lk_ref, sg_ref, o_ref, acc_ref, m_ref, l_ref,
                        *, tq, n_tiles):
    lam = _lam(lq_ref, lk_ref)
    sg = sg_ref[...]
    lo = lax.broadcasted_iota(jnp.int32, (1, HEAD_DIM), 1) < QK_DIM

    def stack_q(q):
        z = jnp.zeros_like(q)
        return jnp.concatenate([jnp.where(lo, q, z), jnp.where(lo, z, q)], axis=0)

    def init(t2):
        m_ref[0:t2, :] = jnp.full((t2, 1), NEG, F32)
        l_ref[0:t2, :] = jnp.zeros((t2, 1), F32)
        acc_ref[0:t2, :] = jnp.zeros((t2, HEAD_DIM), F32)

    def update(q2, kt, vt, mask):
        t2 = q2.shape[0]
        s = _dot_nt(q2, kt)
        if mask is not None:
            s = jnp.where(mask, s, NEG)
        m_prev = m_ref[0:t2, :]
        m_new = jnp.maximum(m_prev, jnp.max(s, axis=1, keepdims=True))
        alpha = jnp.exp(m_prev - m_new)
        pr = jnp.exp(s - m_new)
        l_ref[0:t2, :] = alpha * l_ref[0:t2, :] + jnp.sum(pr, axis=1, keepdims=True)
        acc_ref[0:t2, :] = alpha * acc_ref[0:t2, :] + _dot(pr.astype(BF16), vt)
        m_ref[0:t2, :] = m_new

    def finish(t):
        on = acc_ref[0:2 * t, :] / l_ref[0:2 * t, :]
        o = on[0:t] - lam * on[t:2 * t]
        r = lax.rsqrt(jnp.mean(o * o, axis=-1, keepdims=True) + EPS)
        return ((o * r) * sg).astype(o_ref.dtype)

    def causal_mask(t, nk, offset):
        qi = lax.broadcasted_iota(jnp.int32, (2 * t, nk), 0)
        qi = jnp.where(qi >= t, qi - t, qi)
        ki = lax.broadcasted_iota(jnp.int32, (2 * t, nk), 1)
        return ki <= qi + offset

    tmq = LANES
    init(2 * tmq)
    update(stack_q(q_ref[0:tmq, :]), k_ref[0:tmq, :], v_ref[0:tmq, :], causal_mask(tmq, tmq, 0))
    o_ref[0:N_META, :] = finish(tmq)[0:N_META]

    dk = tq + N_META
    dmask = causal_mask(tq, dk, N_META)

    def q_tile(j, carry):
        qs = pl.multiple_of(N_META + j * tq, N_META)
        q2 = stack_q(q_ref[pl.ds(qs, tq), :])
        init(2 * tq)

        def k_tile(i, c):
            ks = pl.multiple_of(i * tq, tq)
            update(q2, k_ref[pl.ds(ks, tq), :], v_ref[pl.ds(ks, tq), :], None)
            return c

        lax.fori_loop(0, j, k_tile, 0)
        ds = pl.multiple_of(j * tq, tq)
        update(q2, k_ref[pl.ds(ds, dk), :], v_ref[pl.ds(ds, dk), :], dmask)
        o_ref[pl.ds(qs, tq), :] = finish(tq)
        return carry

    lax.fori_loop(0, n_tiles, q_tile, 0)


def _attn_prompt(qb, kb, vb, p, tq=256):
    nb, length, a = qb.shape
    n_tiles = (length - N_META) // tq
    assert N_META + n_tiles * tq == length and tq % LANES == 0
    spec = pl.BlockSpec((None, length, HEAD_DIM), lambda b, h: (b, 0, h))
    return pl.pallas_call(
        functools.partial(_attn_prompt_kernel, tq=tq, n_tiles=n_tiles),
        grid=(nb, N_HEADS),
        in_specs=[spec, spec, spec, _const_spec((2, QK_DIM)), _const_spec((2, QK_DIM)),
                  _const_spec((1, HEAD_DIM))],
        out_specs=spec,
        out_shape=jax.ShapeDtypeStruct((nb, length, a), BF16),
        scratch_shapes=[pltpu.VMEM((2 * tq, HEAD_DIM), F32), pltpu.VMEM((2 * tq, 1), F32),
                        pltpu.VMEM((2 * tq, 1), F32)],
        compiler_params=pltpu.CompilerParams(
            dimension_semantics=("arbitrary", "arbitrary"), vmem_limit_bytes=VMEM_LIMIT),
        name="attn_prompt",
    )(qb, kb, vb, p["lam_q"], p["lam_k"], p["sub_gain"])


def _attn_decode_kernel(pt_ref, q_ref, kn_ref, vn_ref, lq_ref, lk_ref, sg_ref, *refs, n_pages, t_new):
    del pt_ref
    k_refs, v_refs, o_ref = refs[:n_pages], refs[n_pages:2 * n_pages], refs[2 * n_pages]
    lam = _lam(lq_ref, lk_ref)
    a = N_HEADS * HEAD_DIM
    nrow = N_HEADS * 2 * t_new
    row = lax.broadcasted_iota(jnp.int32, (nrow, a), 0)
    col = lax.broadcasted_iota(jnp.int32, (nrow, a), 1)
    blk = (col // QK_DIM) == (row // t_new)
    q32 = jnp.where(blk, q_ref[...].astype(F32), 0.0)
    qblk = q32.astype(BF16)
    t_of_row = lax.broadcasted_iota(jnp.int32, (nrow, 1), 0) % t_new

    s_pages = [_dot_nt(qblk, k_refs[pg][...].astype(BF16)) for pg in range(n_pages)]
    kn = kn_ref[...].astype(F32)
    vn = vn_ref[...].astype(F32)
    s_new = []
    for t in range(t_new):
        sc = jnp.sum(q32 * kn[t:t + 1, :], axis=1, keepdims=True)
        s_new.append(jnp.where(t_of_row >= t, sc, NEG))

    m = s_pages[0]
    for s in s_pages[1:]:
        m = jnp.maximum(m, s)
    m = jnp.max(m, axis=1, keepdims=True)
    for s in s_new:
        m = jnp.maximum(m, s)

    lsum = jnp.zeros((nrow, 1), F32)
    acc = jnp.zeros((nrow, a), F32)
    psum = None
    for pg in range(n_pages):
        pr = jnp.exp(s_pages[pg] - m)
        psum = pr if psum is None else psum + pr
        acc = acc + _dot(pr.astype(BF16), v_refs[pg][...].astype(BF16))
    lsum = lsum + jnp.sum(psum, axis=1, keepdims=True)
    for t in range(t_new):
        pr = jnp.exp(s_new[t] - m)
        lsum = lsum + pr
        acc = acc + pr * vn[t:t + 1, :]
    on = acc / lsum
    sg = sg_ref[...]
    for h in range(N_HEADS):
        r0 = h * 2 * t_new
        c0 = h * HEAD_DIM
        o = on[r0:r0 + t_new, c0:c0 + HEAD_DIM] - lam * on[r0 + t_new:r0 + 2 * t_new, c0:c0 + HEAD_DIM]
        r = lax.rsqrt(jnp.mean(o * o, axis=-1, keepdims=True) + EPS)
        o_ref[:, c0:c0 + HEAD_DIM] = ((o * r) * sg).astype(o_ref.dtype)


def _attn_decode(q_rep, kn, vn, cache_k, cache_v, page_table, p):
    db, nrow, a = q_rep.shape
    t_new = kn.shape[1]
    n_pages = page_table.shape[1]
    page = cache_k.shape[1]
    pt = page_table.reshape(-1)

    def page_spec(pg):
        return pl.BlockSpec((None, page, a), lambda b, pt_ref: (pt_ref[b * n_pages + pg], 0, 0))

    const = lambda shape: pl.BlockSpec(shape, lambda b, pt_ref: (0,) * len(shape))
    grid_spec = pltpu.PrefetchScalarGridSpec(
        num_scalar_prefetch=1,
        grid=(db,),
        in_specs=[pl.BlockSpec((None, nrow, a), lambda b, pt_ref: (b, 0, 0)),
                  pl.BlockSpec((None, t_new, a), lambda b, pt_ref: (b, 0, 0)),
                  pl.BlockSpec((None, t_new, a), lambda b, pt_ref: (b, 0, 0)),
                  const((2, QK_DIM)), const((2, QK_DIM)), const((1, HEAD_DIM))]
        + [page_spec(pg) for pg in range(n_pages)] + [page_spec(pg) for pg in range(n_pages)],
        out_specs=pl.BlockSpec((None, t_new, a), lambda b, pt_ref: (b, 0, 0)),
    )
    return pl.pallas_call(
        functools.partial(_attn_decode_kernel, n_pages=n_pages, t_new=t_new),
        grid_spec=grid_spec,
        out_shape=jax.ShapeDtypeStruct((db, t_new, a), BF16),
        compiler_params=pltpu.CompilerParams(
            dimension_semantics=("arbitrary",), vmem_limit_bytes=VMEM_LIMIT),
        name="attn_decode",
    )(pt, q_rep, kn, vn, p["lam_q"], p["lam_k"], p["sub_gain"],
      *([cache_k] * n_pages), *([cache_v] * n_pages))


def _ssm_param_kernel(ar_ref, ai_ref, ldt_ref, br_ref, bi_ref, abr_ref, abi_ref, bbr_ref, bbi_ref):
    ar, ai = ar_ref[...], ai_ref[...]
    dt = jnp.exp(ldt_ref[...])
    mag = jnp.exp(ar * dt)
    abr, abi = mag * jnp.cos(ai * dt), mag * jnp.sin(ai * dt)
    den = ar * ar + ai * ai
    nr, ni = abr - 1.0, abi
    gr, gi = (nr * ar + ni * ai) / den, (ni * ar - nr * ai) / den
    abr_ref[...] = abr
    abi_ref[...] = abi
    for c in range(SSM_GROUP):
        br, bi = br_ref[c], bi_ref[c]
        bbr_ref[c] = gr * br - gi * bi
        bbi_ref[c] = gr * bi + gi * br


def _ssm_params(a_re, a_im, log_dt, b_re, b_im):
    g, pdim = a_re.shape
    c = b_re.shape[-1]
    b_re_t = jnp.transpose(b_re, (2, 0, 1))
    b_im_t = jnp.transpose(b_im, (2, 0, 1))
    gp = jax.ShapeDtypeStruct((g, pdim), F32)
    cgp = jax.ShapeDtypeStruct((c, g, pdim), F32)
    return pl.pallas_call(_ssm_param_kernel, out_shape=(gp, gp, cgp, cgp), name="ssm_params")(
        a_re, a_im, log_dt.reshape(g, 1), b_re_t, b_im_t)


def _ssm_kernel(u_ref, h0r_ref, h0i_ref, abr_ref, abi_ref, bre_ref, bim_ref, crt_ref, cit_ref,
                d_ref, wg_ref, bg_ref, y_ref, hr_ref, hi_ref, xr_s, xi_s, *, tc, nb):
    i = pl.program_id(0)
    rows = tc * nb
    n_state = xr_s.shape[1]
    width = d_ref.shape[1]

    @pl.when(i == 0)
    def _():
        hr_ref[...] = h0r_ref[...]
        hi_ref[...] = h0i_ref[...]

    u = u_ref[...].reshape(rows, width)
    ub = u.astype(BF16)
    n_in = bre_ref.shape[0]
    kin, nout = bre_ref.shape[1], bre_ref.shape[2]
    for mblk in range(n_in):
        um = ub[:, mblk * kin:(mblk + 1) * kin]
        xr_s[:, mblk * nout:(mblk + 1) * nout] = _dot(um, bre_ref[mblk])
        xi_s[:, mblk * nout:(mblk + 1) * nout] = _dot(um, bim_ref[mblk])

    abr = jnp.broadcast_to(abr_ref[...], (nb, n_state))
    abi = jnp.broadcast_to(abi_ref[...], (nb, n_state))

    def step(t, carry):
        hr, hi = carry
        r0 = pl.multiple_of(t * nb, nb)
        xr = xr_s[pl.ds(r0, nb), :]
        xi = xi_s[pl.ds(r0, nb), :]
        nhr = abr * hr - abi * hi + xr
        nhi = abr * hi + abi * hr + xi
        xr_s[pl.ds(r0, nb), :] = nhr
        xi_s[pl.ds(r0, nb), :] = nhi
        return nhr, nhi

    hr, hi = lax.fori_loop(0, tc, step, (hr_ref[...], hi_ref[...]))
    hr_ref[...] = hr
    hi_ref[...] = hi

    n_out = crt_ref.shape[0]
    kout, wout = crt_ref.shape[1], crt_ref.shape[2]
    ys = []
    for j in range(n_out):
        hrj = xr_s[:, j * kout:(j + 1) * kout].astype(BF16)
        hij = xi_s[:, j * kout:(j + 1) * kout].astype(BF16)
        ys.append(_dot(hrj, crt_ref[j]) - _dot(hij, cit_ref[j]))
    y = jnp.concatenate(ys, axis=1) + d_ref[...] * u
    g = jax.nn.gelu(y)
    out = g * jax.nn.sigmoid(_dot(g.astype(BF16), wg_ref[...]) + bg_ref[...])
    y_ref[...] = out.reshape(tc, nb, width)


def _ssm(u3, h0r, h0i, p, tc):
    t_len, nb, width = u3.shape
    n_state = h0r.shape[1]
    assert t_len % tc == 0
    st = jax.ShapeDtypeStruct((nb, n_state), F32)
    st_spec = pl.BlockSpec((nb, n_state), lambda i: (0, 0))
    return pl.pallas_call(
        functools.partial(_ssm_kernel, tc=tc, nb=nb),
        grid=(t_len // tc,),
        in_specs=[pl.BlockSpec((tc, nb, width), lambda i: (i, 0, 0)),
                  _const_spec((nb, n_state)), _const_spec((nb, n_state)),
                  _const_spec((1, n_state)), _const_spec((1, n_state)),
                  _const_spec(p["bre"].shape), _const_spec(p["bim"].shape),
                  _const_spec(p["crt"].shape), _const_spec(p["cit"].shape),
                  _const_spec((1, width)), _const_spec((width, width)), _const_spec((1, width))],
        out_specs=(pl.BlockSpec((tc, nb, width), lambda i: (i, 0, 0)), st_spec, st_spec),
        out_shape=(jax.ShapeDtypeStruct((t_len, nb, width), F32), st, st),
        scratch_shapes=[pltpu.VMEM((tc * nb, n_state), F32), pltpu.VMEM((tc * nb, n_state), F32)],
        compiler_params=pltpu.CompilerParams(
            dimension_semantics=("arbitrary",), vmem_limit_bytes=VMEM_LIMIT),
        name="ssm",
    )(u3, h0r, h0i, p["abr"], p["abi"], p["bre"], p["bim"], p["crt"], p["cit"],
      p["d_skip"], p["w_glu"], p["b_glu"])


def _post_kernel(x_ref, att_ref, ssm_ref, hist_ref, wo_ref, g2_ref, wgate_ref, wup_ref, cw_ref, cb_ref,
                 wd_ref, y_ref, cst_ref, acc_s, ap_s, hist_s, *, shift, fc):
    t = pl.program_id(1)
    tm = x_ref.shape[0]
    d_ff = wgate_ref.shape[1]
    off = ap_s.shape[0] - tm

    @pl.when(t == 0)
    def _():
        hist_s[...] = hist_ref[...]

    mix = jnp.concatenate([att_ref[...], ssm_ref[...].astype(BF16)], axis=1)
    xm = x_ref[...] + _dot(mix, wo_ref[...])
    r = lax.rsqrt(jnp.mean(xm * xm, axis=-1, keepdims=True) + EPS)
    xn = ((xm * r) * g2_ref[...]).astype(BF16)
    acc_s[...] = xm
    for j in range(d_ff // fc):
        cs = slice(j * fc, (j + 1) * fc)
        a = _dot(xn, wgate_ref[:, cs])
        c = _dot(xn, wup_ref[:, cs])
        ap_s[off - 2 * shift:off, :] = hist_s[:, cs]
        ap_s[off:off + tm, :] = a
        conv = (cb_ref[:, cs] + cw_ref[0:1, cs] * ap_s[off - 2 * shift:off - 2 * shift + tm, :]
                + cw_ref[1:2, cs] * ap_s[off - shift:off - shift + tm, :] + cw_ref[2:3, cs] * a)
        hist_s[:, cs] = ap_s[off + tm - 2 * shift:off + tm, :]
        hcol = (jax.nn.gelu(conv) * c).astype(BF16)
        acc_s[...] += _dot(hcol, wd_ref[cs, :])
    y_ref[...] = acc_s[...]
    cst_ref[...] = hist_s[...]


def _post(x2d, att2d, ssm2d, hist, p, tm, nb, shift, ssm_time_major, fc=256):
    rows, d = x2d.shape
    a = att2d.shape[1]
    d_ff = p["w_gate"].shape[1]
    per_b = rows // nb
    nt = per_b // tm
    assert nt * tm == per_b and d_ff % fc == 0 and tm >= 2 * shift
    off = -(-2 * shift // 8) * 8
    row_spec = lambda w: pl.BlockSpec((tm, w), lambda b, t: (b * nt + t, 0))
    ssm_spec = pl.BlockSpec((tm, a), (lambda b, t: (t, b)) if ssm_time_major else (lambda b, t: (b * nt + t, 0)))
    hist_spec = pl.BlockSpec((None, 2 * shift, d_ff), lambda b, t: (b, 0, 0))
    return pl.pallas_call(
        functools.partial(_post_kernel, shift=shift, fc=fc),
        grid=(nb, nt),
        in_specs=[row_spec(d), row_spec(a), ssm_spec, hist_spec,
                  _const_spec(p["w_out"].shape), _const_spec((1, d)),
                  _const_spec(p["w_gate"].shape), _const_spec(p["w_up"].shape),
                  _const_spec((3, d_ff)), _const_spec((1, d_ff)), _const_spec(p["w_down"].shape)],
        out_specs=(row_spec(d), hist_spec),
        out_shape=(jax.ShapeDtypeStruct((rows, d), F32), jax.ShapeDtypeStruct((nb, 2 * shift, d_ff), F32)),
        scratch_shapes=[pltpu.VMEM((tm, d), F32), pltpu.VMEM((off + tm, fc), F32),
                        pltpu.VMEM((2 * shift, d_ff), F32)],
        compiler_params=pltpu.CompilerParams(
            dimension_semantics=("arbitrary", "arbitrary"), vmem_limit_bytes=VMEM_LIMIT),
        name="post",
    )(x2d, att2d, ssm2d, hist, p["w_out"], p["norm2"], p["w_gate"], p["w_up"], p["conv_w"], p["conv_b"],
      p["w_down"])


def _prepare_params(norm1, w_in, q_norm, k_norm, lam_q, lam_k, sub_norm, ssm_a_re, ssm_a_im, ssm_log_dt,
                    ssm_b_re, ssm_b_im, ssm_c_re, ssm_c_im, ssm_d, w_glu, b_glu, w_out, norm2, w_gate,
                    w_up, ffn_conv_w, ffn_conv_b, w_down):
    l = 0
    g, pdim = ssm_a_re[l].shape
    c = SSM_GROUP
    a = N_HEADS * HEAD_DIM
    abr, abi, bbr, bbi = _ssm_params(ssm_a_re[l], ssm_a_im[l], ssm_log_dt[l], ssm_b_re[l], ssm_b_im[l])
    gi = LANES // c
    eye_i = jnp.eye(gi, dtype=F32)

    def in_blocks(bb):
        bb = jnp.transpose(bb, (1, 0, 2)).reshape(g // gi, gi, c, pdim)
        return jnp.einsum("mgcp,gh->mgchp", bb, eye_i).reshape(g // gi, gi * c, gi * pdim).astype(BF16)

    go = MXU // c
    eye_o = jnp.eye(go, dtype=F32)

    def out_blocks(cc):
        cc = cc.reshape(g // go, go, c, pdim)
        return jnp.einsum("jgcp,gh->jgphc", cc, eye_o).reshape(g // go, go * pdim, go * c).astype(BF16)

    comp = jnp.arange(MXU) // QK_DIM
    ones_blk = (comp[:, None] == comp[None, :]).astype(BF16) * (1.0 / QK_DIM)
    return {
        "norm1": norm1[l].reshape(1, -1), "w_in": w_in[l].astype(BF16),
        "q_gain": jnp.tile(q_norm[l].reshape(-1), N_HEADS).reshape(1, a),
        "k_gain": jnp.tile(k_norm[l].reshape(-1), N_HEADS).reshape(1, a),
        "ones_blk": ones_blk.astype(BF16),
        "lam_q": lam_q[l], "lam_k": lam_k[l],
        "sub_gain": (sub_norm[l] * (1.0 - LAM_INIT)).reshape(1, HEAD_DIM),
        "abr": abr.reshape(1, g * pdim), "abi": abi.reshape(1, g * pdim),
        "bre": in_blocks(bbr), "bim": in_blocks(bbi),
        "crt": out_blocks(ssm_c_re[l]), "cit": out_blocks(ssm_c_im[l]),
        "d_skip": ssm_d[l].reshape(1, g * c), "w_glu": w_glu[l].astype(BF16), "b_glu": b_glu[l].reshape(1, -1),
        "w_out": w_out[l].astype(BF16), "norm2": norm2[l].reshape(1, -1),
        "w_gate": w_gate[l].astype(BF16), "w_up": w_up[l].astype(BF16),
        "conv_w": ffn_conv_w[l], "conv_b": ffn_conv_b[l].reshape(1, -1), "w_down": w_down[l].astype(BF16),
    }


def _row_tile(length, cap=768):
    best = None
    for t in range(16, cap + 1, 16):
        if length % t == 0:
            best = t
    assert best is not None
    return best


def kernel(x_prompt, x_sample, cache_k, cache_v, state_ssm_re, state_ssm_im, state_ffn_conv, page_table, meta_tokens, norm1, w_in, q_norm, k_norm, lam_q, lam_k, sub_norm, ssm_a_re, ssm_a_im, ssm_log_dt, ssm_b_re, ssm_b_im, ssm_c_re, ssm_c_im, ssm_d, w_glu, b_glu, w_out, norm2, w_gate, w_up, ffn_conv_w, ffn_conv_b, w_down):
    assert norm1.shape[0] == 1, "single-layer stack"
    p = _prepare_params(norm1, w_in, q_norm, k_norm, lam_q, lam_k, sub_norm, ssm_a_re, ssm_a_im, ssm_log_dt,
                        ssm_b_re, ssm_b_im, ssm_c_re, ssm_c_im, ssm_d, w_glu, b_glu, w_out, norm2, w_gate,
                        w_up, ffn_conv_w, ffn_conv_b, w_down)
    nb, seq, d = x_prompt.shape
    db, t_new, _ = x_sample.shape
    a = N_HEADS * HEAD_DIM
    g, pdim = ssm_a_re.shape[1:]
    n_state = g * pdim
    d_ff = w_gate.shape[-1]
    length = seq + N_META

    xp = jnp.concatenate([jnp.broadcast_to(meta_tokens.astype(x_prompt.dtype), (nb, N_META, d)), x_prompt], axis=1)
    xp2 = xp.reshape(nb * length, d)
    tm = _row_tile(length)
    qb, k, kb, v, vb, u = _project(xp2, tm, nb, p, u_time_major=True)
    att = _attn_prompt(qb.reshape(nb, length, a), kb.reshape(nb, length, a), vb.reshape(nb, length, a), p)
    zst = jnp.zeros((nb, n_state), F32)
    tc = _row_tile(length * nb, cap=768) // nb
    ys, hr, hi = _ssm(u.reshape(length, nb, a), zst, zst, p, tc)
    yp, cst_p = _post(xp2, att.reshape(nb * length, a), ys.reshape(length, nb * a),
                      jnp.zeros((nb, 2, d_ff), F32), p, tm, nb, shift=1, ssm_time_major=True)
    y_prompt = yp.reshape(nb, length, d)[:, N_META:]
    k_prompt = k.reshape(1, nb, length, N_HEADS, HEAD_DIM)
    v_prompt = v.reshape(1, nb, length, N_HEADS, HEAD_DIM)
    ssm_re_p = hr.reshape(1, nb, g, pdim)
    ssm_im_p = hi.reshape(1, nb, g, pdim)
    conv_p = cst_p[None]

    rows_s = db * t_new
    xs2 = jnp.transpose(x_sample, (1, 0, 2)).reshape(rows_s, d)
    qb, k, kb, v, vb, u = _project(xs2, rows_s, 1, p, u_time_major=False)
    to_bm = lambda z: jnp.transpose(z.reshape(t_new, db, a), (1, 0, 2))
    q_rep = jnp.tile(to_bm(qb), (1, 2 * N_HEADS, 1))
    n_pool, page = cache_k.shape[1:3]
    att = _attn_decode(q_rep, to_bm(kb), to_bm(vb), cache_k[0].reshape(n_pool, page, a),
                       cache_v[0].reshape(n_pool, page, a), page_table, p)
    att_tm = jnp.transpose(att, (1, 0, 2)).reshape(rows_s, a)
    ys, hr, hi = _ssm(u.reshape(t_new, db, a), state_ssm_re[0].reshape(db, n_state),
                      state_ssm_im[0].reshape(db, n_state), p, t_new)
    hist = jnp.transpose(state_ffn_conv[0], (1, 0, 2)).reshape(1, 2 * db, d_ff)
    ysm, cst_s = _post(xs2, att_tm, ys.reshape(rows_s, a), hist, p, rows_s, 1, shift=db, ssm_time_major=False)
    y_sample = jnp.transpose(ysm.reshape(t_new, db, d), (1, 0, 2))
    k_sample = to_bm(k).reshape(1, db, t_new, N_HEADS, HEAD_DIM)
    v_sample = to_bm(v).reshape(1, db, t_new, N_HEADS, HEAD_DIM)
    ssm_re_s = hr.reshape(1, db, g, pdim)
    ssm_im_s = hi.reshape(1, db, g, pdim)
    conv_s = jnp.transpose(cst_s.reshape(2, db, d_ff), (1, 0, 2))[None]

    return (y_prompt, y_sample, k_prompt, v_prompt, k_sample, v_sample,
            ssm_re_p, ssm_im_p, ssm_re_s, ssm_im_s, conv_p, conv_s)
```

```python
import functools
import math

import jax
import jax.numpy as jnp
from jax import lax
from jax.experimental import pallas as pl
from jax.experimental.pallas import tpu as pltpu

N_META = 16
N_HEADS = 4
QK_DIM = 64
HEAD_DIM = 2 * QK_DIM
SSM_GROUP = 16
SSM_STATE = 64
EPS = 1e-6
NEG = -1e30
LAM_INIT = 0.8 - 0.6 * math.exp(-0.3 * 0)

LANES = 128
MXU = 256
VMEM_LIMIT = 56 * 1024 * 1024

F32 = jnp.float32
BF16 = jnp.bfloat16


def _dot(a, b):
    return jnp.dot(a, b, preferred_element_type=F32)


def _dot_nt(a, b):
    return lax.dot_general(a, b, (((1,), (1,)), ((), ())), preferred_element_type=F32)


def _const_spec(shape):
    nd = len(shape)
    return pl.BlockSpec(shape, lambda *_: (0,) * nd, pipeline_mode=pl.Buffered(1))


def _lam(lq_ref, lk_ref):
    e = jnp.exp(jnp.sum(lq_ref[...] * lk_ref[...], axis=1, keepdims=True))
    return e[0:1] - e[1:2] + LAM_INIT


def _proj_kernel(x_ref, g1_ref, w_ref, qg_ref, kg_ref, ones_ref,
                 qb_ref, k_ref, kb_ref, v_ref, vb_ref, u_ref):
    a = N_HEADS * HEAD_DIM
    x = x_ref[...]
    r = lax.rsqrt(jnp.mean(x * x, axis=-1, keepdims=True) + EPS)
    xn = ((x * r) * g1_ref[...]).astype(BF16)
    proj = _dot(xn, w_ref[...])

    def comp_norm(z, g):
        z2 = (z * z).astype(BF16)
        ms = jnp.concatenate(
            [_dot(z2[:, j * MXU:(j + 1) * MXU], ones_ref[...]) for j in range(a // MXU)], axis=1)
        return (z * lax.rsqrt(ms + EPS)) * g

    qn = comp_norm(proj[:, :a], qg_ref[...])
    kn = comp_norm(proj[:, a:2 * a], kg_ref[...])
    v = proj[:, 2 * a:3 * a]
    qb_ref[...] = (qn * (QK_DIM ** -0.5)).astype(BF16)
    kb_ref[...] = kn.astype(BF16)
    vb_ref[...] = v.astype(BF16)
    u_ref[...] = proj[:, 3 * a:]
    tm = x.shape[0]
    for h in range(N_HEADS):
        k_ref[pl.ds(h, tm, stride=N_HEADS), :] = kn[:, h * HEAD_DIM:(h + 1) * HEAD_DIM]
        v_ref[pl.ds(h, tm, stride=N_HEADS), :] = v[:, h * HEAD_DIM:(h + 1) * HEAD_DIM]


def _project(x2d, tm, nb, p, u_time_major):
    rows, d = x2d.shape
    a = N_HEADS * HEAD_DIM
    per_b = rows // nb
    nt = per_b // tm
    grid = (nb, nt)
    row_spec = lambda w: pl.BlockSpec((tm, w), lambda b, t: (b * nt + t, 0))
    head_spec = pl.BlockSpec((tm * N_HEADS, HEAD_DIM), lambda b, t: (b * nt + t, 0))
    if u_time_major:
        u_shape = jax.ShapeDtypeStruct((per_b, nb * a), F32)
        u_spec = pl.BlockSpec((tm, a), lambda b, t: (t, b))
    else:
        u_shape = jax.ShapeDtypeStruct((rows, a), F32)
        u_spec = row_spec(a)
    out_shape = (jax.ShapeDtypeStruct((rows, a), BF16), jax.ShapeDtypeStruct((rows * N_HEADS, HEAD_DIM), F32),
                 jax.ShapeDtypeStruct((rows, a), BF16), jax.ShapeDtypeStruct((rows * N_HEADS, HEAD_DIM), F32),
                 jax.ShapeDtypeStruct((rows, a), BF16), u_shape)
    return pl.pallas_call(
        _proj_kernel,
        grid=grid,
        in_specs=[row_spec(d), _const_spec((1, d)), _const_spec(p["w_in"].shape),
                  _const_spec((1, a)), _const_spec((1, a)), _const_spec((MXU, MXU))],
        out_specs=(row_spec(a), head_spec, row_spec(a), head_spec, row_spec(a), u_spec),
        out_shape=out_shape,
        compiler_params=pltpu.CompilerParams(
            dimension_semantics=("arbitrary", "arbitrary"), vmem_limit_bytes=VMEM_LIMIT),
        name="proj",
    )(x2d, p["norm1"], p["w_in"], p["q_gain"], p["k_gain"], p["ones_blk"])


def _attn_prompt_kernel(q_ref, k_ref, v_ref, lq_ref, lk_ref, sg_ref, o_ref, acc_s, m_s, l_s, *, tq, n_tiles):
    lam = _lam(lq_ref, lk_ref)
    sg = sg_ref[...]
    lo = lax.broadcasted_iota(jnp.int32, (1, HEAD_DIM), 1) < QK_DIM

    def hcols(h):
        return slice(h * HEAD_DIM, (h + 1) * HEAD_DIM)

    def stack_q(q):
        z = jnp.zeros_like(q)
        return jnp.concatenate([jnp.where(lo, q, z), jnp.where(lo, z, q)], axis=0)

    def init(h, t2):
        m_s[h, :, 0:t2] = jnp.full((1, t2), NEG, F32)
        l_s[h, :, 0:t2] = jnp.zeros((1, t2), F32)
        acc_s[h, :, 0:t2] = jnp.zeros((HEAD_DIM, t2), F32)

    def update(h, q2, kt, vt, mask):
        t2 = q2.shape[0]
        s = _dot_nt(kt, q2)
        if mask is not None:
            s = jnp.where(mask, s, NEG)
        m_prev = m_s[h, :, 0:t2]
        m_new = jnp.maximum(m_prev, jnp.max(s, axis=0, keepdims=True))
        alpha = jnp.exp(m_prev - m_new)
        pr = jnp.exp(s - m_new)
        l_s[h, :, 0:t2] = alpha * l_s[h, :, 0:t2] + jnp.sum(pr, axis=0, keepdims=True)
        pv = lax.dot_general(vt, pr.astype(BF16), (((0,), (0,)), ((), ())), preferred_element_type=F32)
        acc_s[h, :, 0:t2] = alpha * acc_s[h, :, 0:t2] + pv
        m_s[h, :, 0:t2] = m_new

    def finish(h, t):
        on = acc_s[h, :, 0:2 * t] / l_s[h, :, 0:2 * t]
        o = on[:, 0:t] - lam * on[:, t:2 * t]
        r = lax.rsqrt(jnp.mean(o * o, axis=0, keepdims=True) + EPS)
        return ((o * r).T * sg).astype(o_ref.dtype)

    def causal_mask(t, nk, offset):
        qi = lax.broadcasted_iota(jnp.int32, (nk, 2 * t), 1)
        qi = jnp.where(qi >= t, qi - t, qi)
        ki = lax.broadcasted_iota(jnp.int32, (nk, 2 * t), 0)
        return ki <= qi + offset

    tmq = LANES
    for h in range(N_HEADS):
        init(h, 2 * tmq)
        update(h, stack_q(q_ref[0:tmq, hcols(h)]), k_ref[0:tmq, hcols(h)], v_ref[0:tmq, hcols(h)],
               causal_mask(tmq, tmq, 0))
        o_ref[0:N_META, hcols(h)] = finish(h, tmq)[0:N_META]

    dk = tq + N_META

    def q_tile(j, carry):
        qs = pl.multiple_of(N_META + j * tq, N_META)
        q2 = [stack_q(q_ref[pl.ds(qs, tq), hcols(h)]) for h in range(N_HEADS)]
        for h in range(N_HEADS):
            init(h, 2 * tq)

        def k_tile(i, c):
            ks = pl.multiple_of(i * tq, tq)
            for h in range(N_HEADS):
                update(h, q2[h], k_ref[pl.ds(ks, tq), hcols(h)], v_ref[pl.ds(ks, tq), hcols(h)], None)
            return c

        lax.fori_loop(0, j, k_tile, 0)
        ds = pl.multiple_of(j * tq, tq)
        dmask = causal_mask(tq, dk, N_META)
        for h in range(N_HEADS):
            update(h, q2[h], k_ref[pl.ds(ds, dk), hcols(h)], v_ref[pl.ds(ds, dk), hcols(h)], dmask)
            o_ref[pl.ds(qs, tq), hcols(h)] = finish(h, tq)
        return carry

    lax.fori_loop(0, n_tiles, q_tile, 0)


def _attn_prompt(qb, kb, vb, p, tq=256):
    nb, length, a = qb.shape
    n_tiles = (length - N_META) // tq
    assert N_META + n_tiles * tq == length and tq % LANES == 0
    spec = pl.BlockSpec((None, length, a), lambda b: (b, 0, 0))
    return pl.pallas_call(
        functools.partial(_attn_prompt_kernel, tq=tq, n_tiles=n_tiles),
        grid=(nb,),
        in_specs=[spec, spec, spec, _const_spec((2, QK_DIM)), _const_spec((2, QK_DIM)),
                  _const_spec((1, HEAD_DIM))],
        out_specs=spec,
        out_shape=jax.ShapeDtypeStruct((nb, length, a), BF16),
        scratch_shapes=[pltpu.VMEM((N_HEADS, HEAD_DIM, 2 * tq), F32), pltpu.VMEM((N_HEADS, 1, 2 * tq), F32),
                        pltpu.VMEM((N_HEADS, 1, 2 * tq), F32)],
        compiler_params=pltpu.CompilerParams(
            dimension_semantics=("arbitrary",), vmem_limit_bytes=VMEM_LIMIT),
        name="attn_prompt",
    )(qb, kb, vb, p["lam_q"], p["lam_k"], p["sub_gain"])


def _attn_decode_kernel(pt_ref, q_ref, kn_ref, vn_ref, lq_ref, lk_ref, sg_ref, *refs, n_pages, t_new):
    del pt_ref
    k_refs, v_refs, o_ref = refs[:n_pages], refs[n_pages:2 * n_pages], refs[2 * n_pages]
    lam = _lam(lq_ref, lk_ref)
    grp = 2 * t_new
    nrow = N_HEADS * grp
    ncol = k_refs[0].shape[0]
    row = lax.broadcasted_iota(jnp.int32, (nrow, HEAD_DIM), 0)
    lane = lax.broadcasted_iota(jnp.int32, (nrow, HEAD_DIM), 1)
    comp_ok = (lane >= QK_DIM) == ((row % grp) >= t_new)
    q32 = jnp.where(comp_ok, q_ref[...].astype(F32), 0.0)
    qb = q32.astype(BF16)
    t_of_row = lax.broadcasted_iota(jnp.int32, (nrow, 1), 0) % t_new
    head_ok = (lax.broadcasted_iota(jnp.int32, (nrow, ncol), 1) % N_HEADS
               == lax.broadcasted_iota(jnp.int32, (nrow, ncol), 0) // grp)

    s_pages = [jnp.where(head_ok, _dot_nt(qb, k_refs[pg][...].astype(BF16)), NEG) for pg in range(n_pages)]

    def per_row_head(z_ref, t):
        z = z_ref[...].astype(F32)
        return jnp.concatenate(
            [jnp.broadcast_to(z[t:t + 1, h * HEAD_DIM:(h + 1) * HEAD_DIM], (grp, HEAD_DIM)) for h in range(N_HEADS)],
            axis=0)

    s_new = []
    for t in range(t_new):
        sc = jnp.sum(q32 * per_row_head(kn_ref, t), axis=1, keepdims=True)
        s_new.append(jnp.where(t_of_row >= t, sc, NEG))

    m = s_pages[0]
    for s in s_pages[1:]:
        m = jnp.maximum(m, s)
    m = jnp.max(m, axis=1, keepdims=True)
    for s in s_new:
        m = jnp.maximum(m, s)

    acc = jnp.zeros((nrow, HEAD_DIM), F32)
    psum = None
    for pg in range(n_pages):
        pr = jnp.exp(s_pages[pg] - m)
        psum = pr if psum is None else psum + pr
        acc = acc + _dot(pr.astype(BF16), v_refs[pg][...].astype(BF16))
    lsum = jnp.sum(psum, axis=1, keepdims=True)
    for t in range(t_new):
        pr = jnp.exp(s_new[t] - m)
        lsum = lsum + pr
        acc = acc + pr * per_row_head(vn_ref, t)
    on = acc / lsum
    sg = sg_ref[...]
    for h in range(N_HEADS):
        r0 = h * grp
        o = on[r0:r0 + t_new] - lam * on[r0 + t_new:r0 + grp]
        r = lax.rsqrt(jnp.mean(o * o, axis=-1, keepdims=True) + EPS)
        o_ref[:, h * HEAD_DIM:(h + 1) * HEAD_DIM] = ((o * r) * sg).astype(o_ref.dtype)


def _attn_decode(q_rep, kn, vn, cache_k, cache_v, page_table, p):
    db, nrow, _ = q_rep.shape
    t_new, a = kn.shape[1:]
    n_pages = page_table.shape[1]
    prow = cache_k.shape[1]
    pt = page_table.reshape(-1)

    def page_spec(pg):
        return pl.BlockSpec((None, prow, HEAD_DIM), lambda b, pt_ref: (pt_ref[b * n_pages + pg], 0, 0))

    const = lambda shape: pl.BlockSpec(shape, lambda b, pt_ref: (0,) * len(shape))
    grid_spec = pltpu.PrefetchScalarGridSpec(
        num_scalar_prefetch=1,
        grid=(db,),
        in_specs=[pl.BlockSpec((None, nrow, HEAD_DIM), lambda b, pt_ref: (b, 0, 0)),
                  pl.BlockSpec((None, t_new, a), lambda b, pt_ref: (b, 0, 0)),
                  pl.BlockSpec((None, t_new, a), lambda b, pt_ref: (b, 0, 0)),
                  const((2, QK_DIM)), const((2, QK_DIM)), const((1, HEAD_DIM))]
        + [page_spec(pg) for pg in range(n_pages)] + [page_spec(pg) for pg in range(n_pages)],
        out_specs=pl.BlockSpec((None, t_new, a), lambda b, pt_ref: (b, 0, 0)),
    )
    return pl.pallas_call(
        functools.partial(_attn_decode_kernel, n_pages=n_pages, t_new=t_new),
        grid_spec=grid_spec,
        out_shape=jax.ShapeDtypeStruct((db, t_new, a), BF16),
        compiler_params=pltpu.CompilerParams(
            dimension_semantics=("arbitrary",), vmem_limit_bytes=VMEM_LIMIT),
        name="attn_decode",
    )(pt, q_rep, kn, vn, p["lam_q"], p["lam_k"], p["sub_gain"],
      *([cache_k] * n_pages), *([cache_v] * n_pages))


def _ssm_param_kernel(ar_ref, ai_ref, ldt_ref, br_ref, bi_ref, abr_ref, abi_ref, bbr_ref, bbi_ref):
    ar, ai = ar_ref[...], ai_ref[...]
    dt = jnp.exp(ldt_ref[...])
    mag = jnp.exp(ar * dt)
    abr, abi = mag * jnp.cos(ai * dt), mag * jnp.sin(ai * dt)
    den = ar * ar + ai * ai
    nr, ni = abr - 1.0, abi
    gr, gi = (nr * ar + ni * ai) / den, (ni * ar - nr * ai) / den
    abr_ref[...] = abr
    abi_ref[...] = abi
    for c in range(SSM_GROUP):
        br, bi = br_ref[c], bi_ref[c]
        bbr_ref[c] = gr * br - gi * bi
        bbi_ref[c] = gr * bi + gi * br


def _ssm_params(a_re, a_im, log_dt, b_re, b_im):
    g, pdim = a_re.shape
    c = b_re.shape[-1]
    b_re_t = jnp.transpose(b_re, (2, 0, 1))
    b_im_t = jnp.transpose(b_im, (2, 0, 1))
    gp = jax.ShapeDtypeStruct((g, pdim), F32)
    cgp = jax.ShapeDtypeStruct((c, g, pdim), F32)
    return pl.pallas_call(_ssm_param_kernel, out_shape=(gp, gp, cgp, cgp), name="ssm_params")(
        a_re, a_im, log_dt.reshape(g, 1), b_re_t, b_im_t)


def _ssm_kernel(u_ref, h0r_ref, h0i_ref, abr_ref, abi_ref, bre_ref, bim_ref, crt_ref, cit_ref,
                d_ref, wg_ref, bg_ref, y_ref, hr_ref, hi_ref, xr_s, xi_s, *, tc, nb):
    i = pl.program_id(0)
    rows = tc * nb
    n_state = xr_s.shape[1]
    width = d_ref.shape[1]

    @pl.when(i == 0)
    def _():
        hr_ref[...] = h0r_ref[...]
        hi_ref[...] = h0i_ref[...]

    u = u_ref[...].reshape(rows, width)
    ub = u.astype(BF16)
    n_in = bre_ref.shape[0]
    kin, nout = bre_ref.shape[1], bre_ref.shape[2]
    for mblk in range(n_in):
        um = ub[:, mblk * kin:(mblk + 1) * kin]
        xr_s[:, mblk * nout:(mblk + 1) * nout] = _dot(um, bre_ref[mblk])
        xi_s[:, mblk * nout:(mblk + 1) * nout] = _dot(um, bim_ref[mblk])

    abr = jnp.broadcast_to(abr_ref[...], (nb, n_state))
    abi = jnp.broadcast_to(abi_ref[...], (nb, n_state))

    def step(t, carry):
        hr, hi = carry
        r0 = pl.multiple_of(t * nb, nb)
        xr = xr_s[pl.ds(r0, nb), :]
        xi = xi_s[pl.ds(r0, nb), :]
        nhr = abr * hr - abi * hi + xr
        nhi = abr * hi + abi * hr + xi
        xr_s[pl.ds(r0, nb), :] = nhr
        xi_s[pl.ds(r0, nb), :] = nhi
        return nhr, nhi

    hr, hi = lax.fori_loop(0, tc, step, (hr_ref[...], hi_ref[...]))
    hr_ref[...] = hr
    hi_ref[...] = hi

    n_out = crt_ref.shape[0]
    kout, wout = crt_ref.shape[1], crt_ref.shape[2]
    ys = []
    for j in range(n_out):
        hrj = xr_s[:, j * kout:(j + 1) * kout].astype(BF16)
        hij = xi_s[:, j * kout:(j + 1) * kout].astype(BF16)
        ys.append(_dot(hrj, crt_ref[j]) - _dot(hij, cit_ref[j]))
    y = jnp.concatenate(ys, axis=1) + d_ref[...] * u
    g = jax.nn.gelu(y)
    out = g * jax.nn.sigmoid(_dot(g.astype(BF16), wg_ref[...]) + bg_ref[...])
    y_ref[...] = out.reshape(tc, nb, width)


def _ssm(u3, h0r, h0i, p, tc):
    t_len, nb, width = u3.shape
    n_state = h0r.shape[1]
    assert t_len % tc == 0
    st = jax.ShapeDtypeStruct((nb, n_state), F32)
    st_spec = pl.BlockSpec((nb, n_state), lambda i: (0, 0))
    return pl.pallas_call(
        functools.partial(_ssm_kernel, tc=tc, nb=nb),
        grid=(t_len // tc,),
        in_specs=[pl.BlockSpec((tc, nb, width), lambda i: (i, 0, 0)),
                  _const_spec((nb, n_state)), _const_spec((nb, n_state)),
                  _const_spec((1, n_state)), _const_spec((1, n_state)),
                  _const_spec(p["bre"].shape), _const_spec(p["bim"].shape),
                  _const_spec(p["crt"].shape), _const_spec(p["cit"].shape),
                  _const_spec((1, width)), _const_spec((width, width)), _const_spec((1, width))],
        out_specs=(pl.BlockSpec((tc, nb, width), lambda i: (i, 0, 0)), st_spec, st_spec),
        out_shape=(jax.ShapeDtypeStruct((t_len, nb, width), F32), st, st),
        scratch_shapes=[pltpu.VMEM((tc * nb, n_state), F32), pltpu.VMEM((tc * nb, n_state), F32)],
        compiler_params=pltpu.CompilerParams(
            dimension_semantics=("arbitrary",), vmem_limit_bytes=VMEM_LIMIT),
        name="ssm",
    )(u3, h0r, h0i, p["abr"], p["abi"], p["bre"], p["bim"], p["crt"], p["cit"],
      p["d_skip"], p["w_glu"], p["b_glu"])


def _post_kernel(x_ref, att_ref, ssm_ref, hist_ref, wo_ref, g2_ref, wgate_ref, wup_ref, cw_ref, cb_ref,
                 wd_ref, y_ref, cst_ref, acc_s, ap_s, hist_s, *, shift, fc):
    t = pl.program_id(1)
    tm = x_ref.shape[0]
    d_ff = wgate_ref.shape[1]
    off = ap_s.shape[0] - tm

    @pl.when(t == 0)
    def _():
        hist_s[...] = hist_ref[...]

    mix = jnp.concatenate([att_ref[...], ssm_ref[...].astype(BF16)], axis=1)
    xm = x_ref[...] + _dot(mix, wo_ref[...])
    r = lax.rsqrt(jnp.mean(xm * xm, axis=-1, keepdims=True) + EPS)
    xn = ((xm * r) * g2_ref[...]).astype(BF16)
    acc_s[...] = xm
    for j in range(d_ff // fc):
        cs = slice(j * fc, (j + 1) * fc)
        a = _dot(xn, wgate_ref[:, cs])
        c = _dot(xn, wup_ref[:, cs])
        ap_s[off - 2 * shift:off, :] = hist_s[:, cs]
        ap_s[off:off + tm, :] = a
        conv = (cb_ref[:, cs] + cw_ref[0:1, cs] * ap_s[off - 2 * shift:off - 2 * shift + tm, :]
                + cw_ref[1:2, cs] * ap_s[off - shift:off - shift + tm, :] + cw_ref[2:3, cs] * a)
        hist_s[:, cs] = ap_s[off + tm - 2 * shift:off + tm, :]
        hcol = (jax.nn.gelu(conv) * c).astype(BF16)
        acc_s[...] += _dot(hcol, wd_ref[cs, :])
    y_ref[...] = acc_s[...]
    cst_ref[...] = hist_s[...]


def _post(x2d, att2d, ssm2d, hist, p, tm, nb, shift, ssm_time_major, fc=256):
    rows, d = x2d.shape
    a = att2d.shape[1]
    d_ff = p["w_gate"].shape[1]
    per_b = rows // nb
    nt = per_b // tm
    assert nt * tm == per_b and d_ff % fc == 0 and tm >= 2 * shift
    off = -(-2 * shift // 8) * 8
    row_spec = lambda w: pl.BlockSpec((tm, w), lambda b, t: (b * nt + t, 0))
    ssm_spec = pl.BlockSpec((tm, a), (lambda b, t: (t, b)) if ssm_time_major else (lambda b, t: (b * nt + t, 0)))
    hist_spec = pl.BlockSpec((None, 2 * shift, d_ff), lambda b, t: (b, 0, 0))
    return pl.pallas_call(
        functools.partial(_post_kernel, shift=shift, fc=fc),
        grid=(nb, nt),
        in_specs=[row_spec(d), row_spec(a), ssm_spec, hist_spec,
                  _const_spec(p["w_out"].shape), _const_spec((1, d)),
                  _const_spec(p["w_gate"].shape), _const_spec(p["w_up"].shape),
                  _const_spec((3, d_ff)), _const_spec((1, d_ff)), _const_spec(p["w_down"].shape)],
        out_specs=(row_spec(d), hist_spec),
        out_shape=(jax.ShapeDtypeStruct((rows, d), F32), jax.ShapeDtypeStruct((nb, 2 * shift, d_ff), F32)),
        scratch_shapes=[pltpu.VMEM((tm, d), F32), pltpu.VMEM((off + tm, fc), F32),
                        pltpu.VMEM((2 * shift, d_ff), F32)],
        compiler_params=pltpu.CompilerParams(
            dimension_semantics=("arbitrary", "arbitrary"), vmem_limit_bytes=VMEM_LIMIT),
        name="post",
    )(x2d, att2d, ssm2d, hist, p["w_out"], p["norm2"], p["w_gate"], p["w_up"], p["conv_w"], p["conv_b"],
      p["w_down"])


def _prepare_params(norm1, w_in, q_norm, k_norm, lam_q, lam_k, sub_norm, ssm_a_re, ssm_a_im, ssm_log_dt,
                    ssm_b_re, ssm_b_im, ssm_c_re, ssm_c_im, ssm_d, w_glu, b_glu, w_out, norm2, w_gate,
                    w_up, ffn_conv_w, ffn_conv_b, w_down):
    l = 0
    g, pdim = ssm_a_re[l].shape
    c = SSM_GROUP
    a = N_HEADS * HEAD_DIM
    abr, abi, bbr, bbi = _ssm_params(ssm_a_re[l], ssm_a_im[l], ssm_log_dt[l], ssm_b_re[l], ssm_b_im[l])
    gi = LANES // c
    eye_i = jnp.eye(gi, dtype=F32)

    def in_blocks(bb):
        bb = jnp.transpose(bb, (1, 0, 2)).reshape(g // gi, gi, c, pdim)
        return jnp.einsum("mgcp,gh->mgchp", bb, eye_i).reshape(g // gi, gi * c, gi * pdim).astype(BF16)

    go = MXU // c
    eye_o = jnp.eye(go, dtype=F32)

    def out_blocks(cc):
        cc = cc.reshape(g // go, go, c, pdim)
        return jnp.einsum("jgcp,gh->jgphc", cc, eye_o).reshape(g // go, go * pdim, go * c).astype(BF16)

    comp = jnp.arange(MXU) // QK_DIM
    ones_blk = (comp[:, None] == comp[None, :]).astype(BF16) * (1.0 / QK_DIM)
    return {
        "norm1": norm1[l].reshape(1, -1), "w_in": w_in[l].astype(BF16),
        "q_gain": jnp.tile(q_norm[l].reshape(-1), N_HEADS).reshape(1, a),
        "k_gain": jnp.tile(k_norm[l].reshape(-1), N_HEADS).reshape(1, a),
        "ones_blk": ones_blk.astype(BF16),
        "lam_q": lam_q[l], "lam_k": lam_k[l],
        "sub_gain": (sub_norm[l] * (1.0 - LAM_INIT)).reshape(1, HEAD_DIM),
        "abr": abr.reshape(1, g * pdim), "abi": abi.reshape(1, g * pdim),
        "bre": in_blocks(bbr), "bim": in_blocks(bbi),
        "crt": out_blocks(ssm_c_re[l]), "cit": out_blocks(ssm_c_im[l]),
        "d_skip": ssm_d[l].reshape(1, g * c), "w_glu": w_glu[l].astype(BF16), "b_glu": b_glu[l].reshape(1, -1),
        "w_out": w_out[l].astype(BF16), "norm2": norm2[l].reshape(1, -1),
        "w_gate": w_gate[l].astype(BF16), "w_up": w_up[l].astype(BF16),
        "conv_w": ffn_conv_w[l], "conv_b": ffn_conv_b[l].reshape(1, -1), "w_down": w_down[l].astype(BF16),
    }


def _row_tile(length, cap=768):
    best = None
    for t in range(16, cap + 1, 16):
        if length % t == 0:
            best = t
    assert best is not None
    return best


def kernel(x_prompt, x_sample, cache_k, cache_v, state_ssm_re, state_ssm_im, state_ffn_conv, page_table, meta_tokens, norm1, w_in, q_norm, k_norm, lam_q, lam_k, sub_norm, ssm_a_re, ssm_a_im, ssm_log_dt, ssm_b_re, ssm_b_im, ssm_c_re, ssm_c_im, ssm_d, w_glu, b_glu, w_out, norm2, w_gate, w_up, ffn_conv_w, ffn_conv_b, w_down):
    assert norm1.shape[0] == 1, "single-layer stack"
    p = _prepare_params(norm1, w_in, q_norm, k_norm, lam_q, lam_k, sub_norm, ssm_a_re, ssm_a_im, ssm_log_dt,
                        ssm_b_re, ssm_b_im, ssm_c_re, ssm_c_im, ssm_d, w_glu, b_glu, w_out, norm2, w_gate,
                        w_up, ffn_conv_w, ffn_conv_b, w_down)
    nb, seq, d = x_prompt.shape
    db, t_new, _ = x_sample.shape
    a = N_HEADS * HEAD_DIM
    g, pdim = ssm_a_re.shape[1:]
    n_state = g * pdim
    d_ff = w_gate.shape[-1]
    length = seq + N_META

    xp = jnp.concatenate([jnp.broadcast_to(meta_tokens.astype(x_prompt.dtype), (nb, N_META, d)), x_prompt], axis=1)
    xp2 = xp.reshape(nb * length, d)
    tm = _row_tile(length)
    qb, k, kb, v, vb, u = _project(xp2, tm, nb, p, u_time_major=True)
    att = _attn_prompt(qb.reshape(nb, length, a), kb.reshape(nb, length, a), vb.reshape(nb, length, a), p)
    zst = jnp.zeros((nb, n_state), F32)
    tc = _row_tile(length * nb, cap=768) // nb
    ys, hr, hi = _ssm(u.reshape(length, nb, a), zst, zst, p, tc)
    yp, cst_p = _post(xp2, att.reshape(nb * length, a), ys.reshape(length, nb * a),
                      jnp.zeros((nb, 2, d_ff), F32), p, tm, nb, shift=1, ssm_time_major=True)
    y_prompt = yp.reshape(nb, length, d)[:, N_META:]
    k_prompt = k.reshape(1, nb, length, N_HEADS, HEAD_DIM)
    v_prompt = v.reshape(1, nb, length, N_HEADS, HEAD_DIM)
    ssm_re_p = hr.reshape(1, nb, g, pdim)
    ssm_im_p = hi.reshape(1, nb, g, pdim)
    conv_p = cst_p[None]

    rows_s = db * t_new
    xs2 = jnp.transpose(x_sample, (1, 0, 2)).reshape(rows_s, d)
    qb, k, kb, v, vb, u = _project(xs2, rows_s, 1, p, u_time_major=False)
    to_bm = lambda z: jnp.transpose(z.reshape(t_new, db, a), (1, 0, 2))
    q_hct = jnp.transpose(qb.reshape(t_new, db, N_HEADS, 1, HEAD_DIM), (1, 2, 3, 0, 4))
    q_rep = jnp.broadcast_to(q_hct, (db, N_HEADS, 2, t_new, HEAD_DIM)).reshape(db, N_HEADS * 2 * t_new, HEAD_DIM)
    n_pool, page = cache_k.shape[1:3]
    att = _attn_decode(q_rep, to_bm(kb), to_bm(vb), cache_k[0].reshape(n_pool, page * N_HEADS, HEAD_DIM),
                       cache_v[0].reshape(n_pool, page * N_HEADS, HEAD_DIM), page_table, p)
    att_tm = jnp.transpose(att, (1, 0, 2)).reshape(rows_s, a)
    ys, hr, hi = _ssm(u.reshape(t_new, db, a), state_ssm_re[0].reshape(db, n_state),
                      state_ssm_im[0].reshape(db, n_state), p, t_new)
    hist = jnp.transpose(state_ffn_conv[0], (1, 0, 2)).reshape(1, 2 * db, d_ff)
    ysm, cst_s = _post(xs2, att_tm, ys.reshape(rows_s, a), hist, p, rows_s, 1, shift=db, ssm_time_major=False)
    y_sample = jnp.transpose(ysm.reshape(t_new, db, d), (1, 0, 2))
    k_sample = jnp.transpose(k.reshape(t_new, db, N_HEADS, HEAD_DIM), (1, 0, 2, 3))[None]
    v_sample = jnp.transpose(v.reshape(t_new, db, N_HEADS, HEAD_DIM), (1, 0, 2, 3))[None]
    ssm_re_s = hr.reshape(1, db, g, pdim)
    ssm_im_s = hi.reshape(1, db, g, pdim)
    conv_s = jnp.transpose(cst_s.reshape(2, db, d_ff), (1, 0, 2))[None]

    return (y_prompt, y_sample, k_prompt, v_prompt, k_sample, v_sample,
            ssm_re_p, ssm_im_p, ssm_re_s, ssm_im_s, conv_p, conv_s)
```

```python
import functools
import math

import jax
import jax.numpy as jnp
from jax import lax
from jax.experimental import pallas as pl
from jax.experimental.pallas import tpu as pltpu

N_META = 16
N_HEADS = 4
QK_DIM = 64
HEAD_DIM = 2 * QK_DIM
SSM_GROUP = 16
SSM_STATE = 64
EPS = 1e-6
NEG = -1e30
LAM_INIT = 0.8 - 0.6 * math.exp(-0.3 * 0)

LANES = 128
MXU = 256
VMEM_LIMIT = 56 * 1024 * 1024

F32 = jnp.float32
BF16 = jnp.bfloat16


def _dot(a, b):
    return jnp.dot(a, b, preferred_element_type=F32)


def _dot_nt(a, b):
    return lax.dot_general(a, b, (((1,), (1,)), ((), ())), preferred_element_type=F32)


def _const_spec(shape):
    nd = len(shape)
    return pl.BlockSpec(shape, lambda *_: (0,) * nd, pipeline_mode=pl.Buffered(1))


def _lam(lq_ref, lk_ref):
    e = jnp.exp(jnp.sum(lq_ref[...] * lk_ref[...], axis=1, keepdims=True))
    return e[0:1] - e[1:2] + LAM_INIT


def _proj_kernel(x_ref, meta_ref, g1_ref, w_ref, qg_ref, kg_ref, ones_ref,
                 qb_ref, k_ref, kb_ref, v_ref, vb_ref, u_ref, *, n_meta):
    outs = (qb_ref, k_ref, kb_ref, v_ref, vb_ref, u_ref)
    consts = (g1_ref, w_ref, qg_ref, kg_ref, ones_ref)
    tm = qb_ref.shape[0]
    if not n_meta:
        _proj_body(x_ref[...], consts, outs)
        return
    t = pl.program_id(1)

    @pl.when(t == 0)
    def _():
        _proj_body(jnp.concatenate([meta_ref[...], x_ref[0, 0:tm - n_meta, :]], axis=0), consts, outs)

    @pl.when(t != 0)
    def _():
        _proj_body(x_ref[0], consts, outs)


def _proj_body(x, consts, outs):
    g1_ref, w_ref, qg_ref, kg_ref, ones_ref = consts
    qb_ref, k_ref, kb_ref, v_ref, vb_ref, u_ref = outs
    a = N_HEADS * HEAD_DIM
    r = lax.rsqrt(jnp.mean(x * x, axis=-1, keepdims=True) + EPS)
    xn = ((x * r) * g1_ref[...]).astype(BF16)
    proj = _dot(xn, w_ref[...])

    def comp_norm(z, g):
        z2 = (z * z).astype(BF16)
        ms = jnp.concatenate(
            [_dot(z2[:, j * MXU:(j + 1) * MXU], ones_ref[...]) for j in range(a // MXU)], axis=1)
        return (z * lax.rsqrt(ms + EPS)) * g

    qn = comp_norm(proj[:, :a], qg_ref[...])
    kn = comp_norm(proj[:, a:2 * a], kg_ref[...])
    v = proj[:, 2 * a:3 * a]
    qb_ref[...] = (qn * (QK_DIM ** -0.5)).astype(BF16)
    kb_ref[...] = kn.astype(BF16)
    vb_ref[...] = v.astype(BF16)
    u_ref[...] = proj[:, 3 * a:]
    tm = x.shape[0]
    for h in range(N_HEADS):
        k_ref[pl.ds(h, tm, stride=N_HEADS), :] = kn[:, h * HEAD_DIM:(h + 1) * HEAD_DIM]
        v_ref[pl.ds(h, tm, stride=N_HEADS), :] = v[:, h * HEAD_DIM:(h + 1) * HEAD_DIM]


def _project(x, meta, tm, p, u_time_major):
    a = N_HEADS * HEAD_DIM
    d = x.shape[-1]
    if meta is None:
        nb, n_meta = 1, 0
        per_b = x.shape[0]
        x_spec = pl.BlockSpec((tm, d), lambda b, t: (t, 0))
        meta = jnp.zeros((8, d), x.dtype)
    else:
        nb, n_meta = x.shape[0], meta.shape[0]
        per_b = n_meta + x.shape[1]
        x_spec = pl.BlockSpec(
            (pl.Element(1), pl.Element(tm), pl.Element(d)),
            lambda b, t: (b, pl.multiple_of(jnp.maximum(t * tm - n_meta, 0), 8), 0))
    rows = nb * per_b
    nt = per_b // tm
    assert nt * tm == per_b and n_meta % 8 == 0
    grid = (nb, nt)
    row_spec = lambda w: pl.BlockSpec((tm, w), lambda b, t: (b * nt + t, 0))
    head_spec = pl.BlockSpec((tm * N_HEADS, HEAD_DIM), lambda b, t: (b * nt + t, 0))
    if u_time_major:
        u_shape = jax.ShapeDtypeStruct((per_b, nb * a), F32)
        u_spec = pl.BlockSpec((tm, a), lambda b, t: (t, b))
    else:
        u_shape = jax.ShapeDtypeStruct((rows, a), F32)
        u_spec = row_spec(a)
    out_shape = (jax.ShapeDtypeStruct((rows, a), BF16), jax.ShapeDtypeStruct((rows * N_HEADS, HEAD_DIM), F32),
                 jax.ShapeDtypeStruct((rows, a), BF16), jax.ShapeDtypeStruct((rows * N_HEADS, HEAD_DIM), F32),
                 jax.ShapeDtypeStruct((rows, a), BF16), u_shape)
    return pl.pallas_call(
        functools.partial(_proj_kernel, n_meta=n_meta),
        grid=grid,
        in_specs=[x_spec, _const_spec(meta.shape), _const_spec((1, d)), _const_spec(p["w_in"].shape),
                  _const_spec((1, a)), _const_spec((1, a)), _const_spec((MXU, MXU))],
        out_specs=(row_spec(a), head_spec, row_spec(a), head_spec, row_spec(a), u_spec),
        out_shape=out_shape,
        compiler_params=pltpu.CompilerParams(
            dimension_semantics=("arbitrary", "arbitrary"), vmem_limit_bytes=VMEM_LIMIT),
        name="proj",
    )(x, meta, p["norm1"], p["w_in"], p["q_gain"], p["k_gain"], p["ones_blk"])


def _attn_prompt_kernel(q_ref, k_ref, v_ref, lq_ref, lk_ref, sg_ref, o_ref, acc_s, m_s, l_s, *, tq, n_tiles):
    lam = _lam(lq_ref, lk_ref)
    sg = sg_ref[...]
    lo = lax.broadcasted_iota(jnp.int32, (1, HEAD_DIM), 1) < QK_DIM

    def hcols(h):
        return slice(h * HEAD_DIM, (h + 1) * HEAD_DIM)

    def stack_q(q):
        z = jnp.zeros_like(q)
        return jnp.concatenate([jnp.where(lo, q, z), jnp.where(lo, z, q)], axis=0)

    def init(h, t2):
        m_s[h, :, 0:t2] = jnp.full((1, t2), NEG, F32)
        l_s[h, :, 0:t2] = jnp.zeros((1, t2), F32)
        acc_s[h, :, 0:t2] = jnp.zeros((HEAD_DIM, t2), F32)

    def update(h, q2, kt, vt, mask):
        t2 = q2.shape[0]
        s = _dot_nt(kt, q2)
        if mask is not None:
            s = jnp.where(mask, s, NEG)
        m_prev = m_s[h, :, 0:t2]
        m_new = jnp.maximum(m_prev, jnp.max(s, axis=0, keepdims=True))
        alpha = jnp.exp(m_prev - m_new)
        pr = jnp.exp(s - m_new)
        l_s[h, :, 0:t2] = alpha * l_s[h, :, 0:t2] + jnp.sum(pr, axis=0, keepdims=True)
        pv = lax.dot_general(vt, pr.astype(BF16), (((0,), (0,)), ((), ())), preferred_element_type=F32)
        acc_s[h, :, 0:t2] = alpha * acc_s[h, :, 0:t2] + pv
        m_s[h, :, 0:t2] = m_new

    def finish(h, t):
        on = acc_s[h, :, 0:2 * t] / l_s[h, :, 0:2 * t]
        o = on[:, 0:t] - lam * on[:, t:2 * t]
        r = lax.rsqrt(jnp.mean(o * o, axis=0, keepdims=True) + EPS)
        return ((o * r).T * sg).astype(o_ref.dtype)

    def causal_mask(t, nk, offset):
        qi = lax.broadcasted_iota(jnp.int32, (nk, 2 * t), 1)
        qi = jnp.where(qi >= t, qi - t, qi)
        ki = lax.broadcasted_iota(jnp.int32, (nk, 2 * t), 0)
        return ki <= qi + offset

    tmq = LANES
    for h in range(N_HEADS):
        init(h, 2 * tmq)
        update(h, stack_q(q_ref[0:tmq, hcols(h)]), k_ref[0:tmq, hcols(h)], v_ref[0:tmq, hcols(h)],
               causal_mask(tmq, tmq, 0))
        o_ref[0:N_META, hcols(h)] = finish(h, tmq)[0:N_META]

    dk = tq + N_META

    def q_tile(j, carry):
        qs = pl.multiple_of(N_META + j * tq, N_META)
        q2 = [stack_q(q_ref[pl.ds(qs, tq), hcols(h)]) for h in range(N_HEADS)]
        for h in range(N_HEADS):
            init(h, 2 * tq)

        def k_tile(i, c):
            ks = pl.multiple_of(i * tq, tq)
            for h in range(N_HEADS):
                update(h, q2[h], k_ref[pl.ds(ks, tq), hcols(h)], v_ref[pl.ds(ks, tq), hcols(h)], None)
            return c

        lax.fori_loop(0, j, k_tile, 0)
        ds = pl.multiple_of(j * tq, tq)
        dmask = causal_mask(tq, dk, N_META)
        for h in range(N_HEADS):
            update(h, q2[h], k_ref[pl.ds(ds, dk), hcols(h)], v_ref[pl.ds(ds, dk), hcols(h)], dmask)
            o_ref[pl.ds(qs, tq), hcols(h)] = finish(h, tq)
        return carry

    lax.fori_loop(0, n_tiles, q_tile, 0)


def _attn_prompt(qb, kb, vb, p, tq=256):
    nb, length, a = qb.shape
    n_tiles = (length - N_META) // tq
    assert N_META + n_tiles * tq == length and tq % LANES == 0
    spec = pl.BlockSpec((None, length, a), lambda b: (b, 0, 0))
    return pl.pallas_call(
        functools.partial(_attn_prompt_kernel, tq=tq, n_tiles=n_tiles),
        grid=(nb,),
        in_specs=[spec, spec, spec, _const_spec((2, QK_DIM)), _const_spec((2, QK_DIM)),
                  _const_spec((1, HEAD_DIM))],
        out_specs=spec,
        out_shape=jax.ShapeDtypeStruct((nb, length, a), BF16),
        scratch_shapes=[pltpu.VMEM((N_HEADS, HEAD_DIM, 2 * tq), F32), pltpu.VMEM((N_HEADS, 1, 2 * tq), F32),
                        pltpu.VMEM((N_HEADS, 1, 2 * tq), F32)],
        compiler_params=pltpu.CompilerParams(
            dimension_semantics=("arbitrary",), vmem_limit_bytes=VMEM_LIMIT),
        name="attn_prompt",
    )(qb, kb, vb, p["lam_q"], p["lam_k"], p["sub_gain"])


def _attn_decode_kernel(pt_ref, q_ref, kn_ref, vn_ref, lq_ref, lk_ref, sg_ref, *refs, n_pages, t_new):
    del pt_ref
    k_refs, v_refs, o_ref = refs[:n_pages], refs[n_pages:2 * n_pages], refs[2 * n_pages]
    lam = _lam(lq_ref, lk_ref)
    grp = 2 * t_new
    nrow = N_HEADS * grp
    ncol = k_refs[0].shape[0]
    row = lax.broadcasted_iota(jnp.int32, (nrow, HEAD_DIM), 0)
    lane = lax.broadcasted_iota(jnp.int32, (nrow, HEAD_DIM), 1)
    comp_ok = (lane >= QK_DIM) == ((row % grp) >= t_new)
    q32 = jnp.where(comp_ok, q_ref[...].astype(F32), 0.0)
    qb = q32.astype(BF16)
    t_of_row = lax.broadcasted_iota(jnp.int32, (nrow, 1), 0) % t_new
    head_ok = (lax.broadcasted_iota(jnp.int32, (nrow, ncol), 1) % N_HEADS
               == lax.broadcasted_iota(jnp.int32, (nrow, ncol), 0) // grp)

    s_pages = [jnp.where(head_ok, _dot_nt(qb, k_refs[pg][...].astype(BF16)), NEG) for pg in range(n_pages)]

    def per_row_head(z_ref, t):
        z = z_ref[...].astype(F32)
        return jnp.concatenate(
            [jnp.broadcast_to(z[t:t + 1, h * HEAD_DIM:(h + 1) * HEAD_DIM], (grp, HEAD_DIM)) for h in range(N_HEADS)],
            axis=0)

    s_new = []
    for t in range(t_new):
        sc = jnp.sum(q32 * per_row_head(kn_ref, t), axis=1, keepdims=True)
        s_new.append(jnp.where(t_of_row >= t, sc, NEG))

    m = s_pages[0]
    for s in s_pages[1:]:
        m = jnp.maximum(m, s)
    m = jnp.max(m, axis=1, keepdims=True)
    for s in s_new:
        m = jnp.maximum(m, s)

    acc = jnp.zeros((nrow, HEAD_DIM), F32)
    psum = None
    for pg in range(n_pages):
        pr = jnp.exp(s_pages[pg] - m)
        psum = pr if psum is None else psum + pr
        acc = acc + _dot(pr.astype(BF16), v_refs[pg][...].astype(BF16))
    lsum = jnp.sum(psum, axis=1, keepdims=True)
    for t in range(t_new):
        pr = jnp.exp(s_new[t] - m)
        lsum = lsum + pr
        acc = acc + pr * per_row_head(vn_ref, t)
    on = acc / lsum
    sg = sg_ref[...]
    for h in range(N_HEADS):
        r0 = h * grp
        o = on[r0:r0 + t_new] - lam * on[r0 + t_new:r0 + grp]
        r = lax.rsqrt(jnp.mean(o * o, axis=-1, keepdims=True) + EPS)
        o_ref[:, h * HEAD_DIM:(h + 1) * HEAD_DIM] = ((o * r) * sg).astype(o_ref.dtype)


def _attn_decode(q_rep, kn, vn, cache_k, cache_v, page_table, p):
    db, nrow, _ = q_rep.shape
    t_new, a = kn.shape[1:]
    n_pages = page_table.shape[1]
    prow = cache_k.shape[1]
    pt = page_table.reshape(-1)

    def page_spec(pg):
        return pl.BlockSpec((None, prow, HEAD_DIM), lambda b, pt_ref: (pt_ref[b * n_pages + pg], 0, 0))

    const = lambda shape: pl.BlockSpec(shape, lambda b, pt_ref: (0,) * len(shape))
    grid_spec = pltpu.PrefetchScalarGridSpec(
        num_scalar_prefetch=1,
        grid=(db,),
        in_specs=[pl.BlockSpec((None, nrow, HEAD_DIM), lambda b, pt_ref: (b, 0, 0)),
                  pl.BlockSpec((None, t_new, a), lambda b, pt_ref: (b, 0, 0)),
                  pl.BlockSpec((None, t_new, a), lambda b, pt_ref: (b, 0, 0)),
                  const((2, QK_DIM)), const((2, QK_DIM)), const((1, HEAD_DIM))]
        + [page_spec(pg) for pg in range(n_pages)] + [page_spec(pg) for pg in range(n_pages)],
        out_specs=pl.BlockSpec((None, t_new, a), lambda b, pt_ref: (b, 0, 0)),
    )
    return pl.pallas_call(
        functools.partial(_attn_decode_kernel, n_pages=n_pages, t_new=t_new),
        grid_spec=grid_spec,
        out_shape=jax.ShapeDtypeStruct((db, t_new, a), BF16),
        compiler_params=pltpu.CompilerParams(
            dimension_semantics=("arbitrary",), vmem_limit_bytes=VMEM_LIMIT),
        name="attn_decode",
    )(pt, q_rep, kn, vn, p["lam_q"], p["lam_k"], p["sub_gain"],
      *([cache_k] * n_pages), *([cache_v] * n_pages))


def _ssm_param_kernel(ar_ref, ai_ref, ldt_ref, br_ref, bi_ref, abr_ref, abi_ref, bbr_ref, bbi_ref):
    ar, ai = ar_ref[...], ai_ref[...]
    dt = jnp.exp(ldt_ref[...])
    mag = jnp.exp(ar * dt)
    abr, abi = mag * jnp.cos(ai * dt), mag * jnp.sin(ai * dt)
    den = ar * ar + ai * ai
    nr, ni = abr - 1.0, abi
    gr, gi = (nr * ar + ni * ai) / den, (ni * ar - nr * ai) / den
    abr_ref[...] = abr
    abi_ref[...] = abi
    for c in range(SSM_GROUP):
        br, bi = br_ref[c], bi_ref[c]
        bbr_ref[c] = gr * br - gi * bi
        bbi_ref[c] = gr * bi + gi * br


def _ssm_params(a_re, a_im, log_dt, b_re, b_im):
    g, pdim = a_re.shape
    c = b_re.shape[-1]
    b_re_t = jnp.transpose(b_re, (2, 0, 1))
    b_im_t = jnp.transpose(b_im, (2, 0, 1))
    gp = jax.ShapeDtypeStruct((g, pdim), F32)
    cgp = jax.ShapeDtypeStruct((c, g, pdim), F32)
    return pl.pallas_call(_ssm_param_kernel, out_shape=(gp, gp, cgp, cgp), name="ssm_params")(
        a_re, a_im, log_dt.reshape(g, 1), b_re_t, b_im_t)


def _ssm_kernel(u_ref, h0r_ref, h0i_ref, abr_ref, abi_ref, bre_ref, bim_ref, crt_ref, cit_ref,
                d_ref, wg_ref, bg_ref, y_ref, hr_ref, hi_ref, xr_s, xi_s, *, tc, nb):
    i = pl.program_id(0)
    rows = tc * nb
    n_state = xr_s.shape[1]
    width = d_ref.shape[1]

    @pl.when(i == 0)
    def _():
        hr_ref[...] = h0r_ref[...]
        hi_ref[...] = h0i_ref[...]

    u = u_ref[...].reshape(rows, width)
    ub = u.astype(BF16)
    n_in = bre_ref.shape[0]
    kin, nout = bre_ref.shape[1], bre_ref.shape[2]
    for mblk in range(n_in):
        um = ub[:, mblk * kin:(mblk + 1) * kin]
        xr_s[:, mblk * nout:(mblk + 1) * nout] = _dot(um, bre_ref[mblk])
        xi_s[:, mblk * nout:(mblk + 1) * nout] = _dot(um, bim_ref[mblk])

    abr = jnp.broadcast_to(abr_ref[...], (nb, n_state))
    abi = jnp.broadcast_to(abi_ref[...], (nb, n_state))

    def step(t, carry):
        hr, hi = carry
        r0 = pl.multiple_of(t * nb, nb)
        xr = xr_s[pl.ds(r0, nb), :]
        xi = xi_s[pl.ds(r0, nb), :]
        nhr = abr * hr - abi * hi + xr
        nhi = abr * hi + abi * hr + xi
        xr_s[pl.ds(r0, nb), :] = nhr
        xi_s[pl.ds(r0, nb), :] = nhi
        return nhr, nhi

    hr, hi = lax.fori_loop(0, tc, step, (hr_ref[...], hi_ref[...]))
    hr_ref[...] = hr
    hi_ref[...] = hi

    n_out = crt_ref.shape[0]
    kout, wout = crt_ref.shape[1], crt_ref.shape[2]
    ys = []
    for j in range(n_out):
        hrj = xr_s[:, j * kout:(j + 1) * kout].astype(BF16)
        hij = xi_s[:, j * kout:(j + 1) * kout].astype(BF16)
        ys.append(_dot(hrj, crt_ref[j]) - _dot(hij, cit_ref[j]))
    y = jnp.concatenate(ys, axis=1) + d_ref[...] * u
    g = jax.nn.gelu(y)
    out = g * jax.nn.sigmoid(_dot(g.astype(BF16), wg_ref[...]) + bg_ref[...])
    y_ref[...] = out.reshape(tc, nb, width)


def _ssm(u3, h0r, h0i, p, tc):
    t_len, nb, width = u3.shape
    n_state = h0r.shape[1]
    assert t_len % tc == 0
    st = jax.ShapeDtypeStruct((nb, n_state), F32)
    st_spec = pl.BlockSpec((nb, n_state), lambda i: (0, 0))
    return pl.pallas_call(
        functools.partial(_ssm_kernel, tc=tc, nb=nb),
        grid=(t_len // tc,),
        in_specs=[pl.BlockSpec((tc, nb, width), lambda i: (i, 0, 0)),
                  _const_spec((nb, n_state)), _const_spec((nb, n_state)),
                  _const_spec((1, n_state)), _const_spec((1, n_state)),
                  _const_spec(p["bre"].shape), _const_spec(p["bim"].shape),
                  _const_spec(p["crt"].shape), _const_spec(p["cit"].shape),
                  _const_spec((1, width)), _const_spec((width, width)), _const_spec((1, width))],
        out_specs=(pl.BlockSpec((tc, nb, width), lambda i: (i, 0, 0)), st_spec, st_spec),
        out_shape=(jax.ShapeDtypeStruct((t_len, nb, width), F32), st, st),
        scratch_shapes=[pltpu.VMEM((tc * nb, n_state), F32), pltpu.VMEM((tc * nb, n_state), F32)],
        compiler_params=pltpu.CompilerParams(
            dimension_semantics=("arbitrary",), vmem_limit_bytes=VMEM_LIMIT),
        name="ssm",
    )(u3, h0r, h0i, p["abr"], p["abi"], p["bre"], p["bim"], p["crt"], p["cit"],
      p["d_skip"], p["w_glu"], p["b_glu"])


def _post_kernel(x_ref, att_ref, ssm_ref, hist_ref, wo_ref, g2_ref, wgate_ref, wup_ref, cw_ref, cb_ref,
                 wd_ref, y_ref, cst_ref, xn_s, h_s, ap_s, hist_s, *, shift, fc):
    t = pl.program_id(1)
    tm, a = y_ref.shape[0], wo_ref.shape[0] // 2
    d_ff = wgate_ref.shape[1]
    off = ap_s.shape[1] - tm

    @pl.when(t == 0)
    def _():
        hist_s[...] = hist_ref[...]

    mix = jnp.concatenate([att_ref[...].reshape(tm, a), ssm_ref[...].reshape(tm, a).astype(BF16)], axis=1)
    xm = x_ref[...] + _dot(mix, wo_ref[...])
    y_ref[...] = xm
    r = lax.rsqrt(jnp.mean(xm * xm, axis=-1, keepdims=True) + EPS)
    xn_s[...] = ((xm * r) * g2_ref[...]).astype(BF16)
    for j in range(d_ff // fc):
        cs = slice(j * fc, (j + 1) * fc)
        ap = ap_s.at[j % 2]
        gate = _dot(xn_s[...], wgate_ref[:, cs])
        up = _dot(xn_s[...], wup_ref[:, cs])
        ap[off - 2 * shift:off, :] = hist_s[:, cs]
        ap[off:off + tm, :] = gate
        conv = (cb_ref[:, cs] + cw_ref[0:1, cs] * ap[off - 2 * shift:off - 2 * shift + tm, :]
                + cw_ref[1:2, cs] * ap[off - shift:off - shift + tm, :] + cw_ref[2:3, cs] * gate)
        hist_s[:, cs] = ap[off + tm - 2 * shift:off + tm, :]
        h_s[:, cs] = (jax.nn.gelu(conv) * up).astype(BF16)
    y_ref[...] += _dot(h_s[...], wd_ref[...])
    cst_ref[...] = hist_s[...]


def _post(x2d, att, ssm, hist, p, tm, nb, shift, att_spec, ssm_spec, hist_shared=False, fc=256):
    rows, d = x2d.shape
    d_ff = p["w_gate"].shape[1]
    per_b = rows // nb
    nt = per_b // tm
    assert nt * tm == per_b and d_ff % fc == 0 and tm >= 2 * shift
    off = -(-2 * shift // 8) * 8
    row_spec = lambda w: pl.BlockSpec((tm, w), lambda b, t: (b * nt + t, 0))
    hist_in = pl.BlockSpec((None, 2 * shift, d_ff), (lambda b, t: (0, 0, 0)) if hist_shared else (lambda b, t: (b, 0, 0)))
    hist_out = pl.BlockSpec((None, 2 * shift, d_ff), lambda b, t: (b, 0, 0))
    return pl.pallas_call(
        functools.partial(_post_kernel, shift=shift, fc=fc),
        grid=(nb, nt),
        in_specs=[row_spec(d), att_spec, ssm_spec, hist_in,
                  _const_spec(p["w_out"].shape), _const_spec((1, d)),
                  _const_spec(p["w_gate"].shape), _const_spec(p["w_up"].shape),
                  _const_spec((3, d_ff)), _const_spec((1, d_ff)), _const_spec(p["w_down"].shape)],
        out_specs=(row_spec(d), hist_out),
        out_shape=(jax.ShapeDtypeStruct((rows, d), F32), jax.ShapeDtypeStruct((nb, 2 * shift, d_ff), F32)),
        scratch_shapes=[pltpu.VMEM((tm, d), BF16), pltpu.VMEM((tm, d_ff), BF16),
                        pltpu.VMEM((2, off + tm, fc), F32), pltpu.VMEM((2 * shift, d_ff), F32)],
        compiler_params=pltpu.CompilerParams(
            dimension_semantics=("arbitrary", "arbitrary"), vmem_limit_bytes=VMEM_LIMIT),
        name="post",
    )(x2d, att, ssm, hist, p["w_out"], p["norm2"], p["w_gate"], p["w_up"], p["conv_w"], p["conv_b"],
      p["w_down"])


def _prepare_params(norm1, w_in, q_norm, k_norm, lam_q, lam_k, sub_norm, ssm_a_re, ssm_a_im, ssm_log_dt,
                    ssm_b_re, ssm_b_im, ssm_c_re, ssm_c_im, ssm_d, w_glu, b_glu, w_out, norm2, w_gate,
                    w_up, ffn_conv_w, ffn_conv_b, w_down):
    l = 0
    g, pdim = ssm_a_re[l].shape
    c = SSM_GROUP
    a = N_HEADS * HEAD_DIM
    abr, abi, bbr, bbi = _ssm_params(ssm_a_re[l], ssm_a_im[l], ssm_log_dt[l], ssm_b_re[l], ssm_b_im[l])
    gi = LANES // c
    eye_i = jnp.eye(gi, dtype=F32)

    def in_blocks(bb):
        bb = jnp.transpose(bb, (1, 0, 2)).reshape(g // gi, gi, c, pdim)
        return jnp.einsum("mgcp,gh->mgchp", bb, eye_i).reshape(g // gi, gi * c, gi * pdim).astype(BF16)

    go = MXU // c
    eye_o = jnp.eye(go, dtype=F32)

    def out_blocks(cc):
        cc = cc.reshape(g // go, go, c, pdim)
        return jnp.einsum("jgcp,gh->jgphc", cc, eye_o).reshape(g // go, go * pdim, go * c).astype(BF16)

    comp = jnp.arange(MXU) // QK_DIM
    ones_blk = (comp[:, None] == comp[None, :]).astype(BF16) * (1.0 / QK_DIM)
    return {
        "norm1": norm1[l].reshape(1, -1), "w_in": w_in[l].astype(BF16),
        "q_gain": jnp.tile(q_norm[l].reshape(-1), N_HEADS).reshape(1, a),
        "k_gain": jnp.tile(k_norm[l].reshape(-1), N_HEADS).reshape(1, a),
        "ones_blk": ones_blk.astype(BF16),
        "lam_q": lam_q[l], "lam_k": lam_k[l],
        "sub_gain": (sub_norm[l] * (1.0 - LAM_INIT)).reshape(1, HEAD_DIM),
        "abr": abr.reshape(1, g * pdim), "abi": abi.reshape(1, g * pdim),
        "bre": in_blocks(bbr), "bim": in_blocks(bbi),
        "crt": out_blocks(ssm_c_re[l]), "cit": out_blocks(ssm_c_im[l]),
        "d_skip": ssm_d[l].reshape(1, g * c), "w_glu": w_glu[l].astype(BF16), "b_glu": b_glu[l].reshape(1, -1),
        "w_out": w_out[l].astype(BF16), "norm2": norm2[l].reshape(1, -1),
        "w_gate": w_gate[l].astype(BF16), "w_up": w_up[l].astype(BF16),
        "conv_w": ffn_conv_w[l], "conv_b": ffn_conv_b[l].reshape(1, -1), "w_down": w_down[l].astype(BF16),
    }


def _row_tile(length, cap=768):
    best = None
    for t in range(16, cap + 1, 16):
        if length % t == 0:
            best = t
    assert best is not None
    return best


def kernel(x_prompt, x_sample, cache_k, cache_v, state_ssm_re, state_ssm_im, state_ffn_conv, page_table, meta_tokens, norm1, w_in, q_norm, k_norm, lam_q, lam_k, sub_norm, ssm_a_re, ssm_a_im, ssm_log_dt, ssm_b_re, ssm_b_im, ssm_c_re, ssm_c_im, ssm_d, w_glu, b_glu, w_out, norm2, w_gate, w_up, ffn_conv_w, ffn_conv_b, w_down):
    assert norm1.shape[0] == 1, "single-layer stack"
    p = _prepare_params(norm1, w_in, q_norm, k_norm, lam_q, lam_k, sub_norm, ssm_a_re, ssm_a_im, ssm_log_dt,
                        ssm_b_re, ssm_b_im, ssm_c_re, ssm_c_im, ssm_d, w_glu, b_glu, w_out, norm2, w_gate,
                        w_up, ffn_conv_w, ffn_conv_b, w_down)
    nb, seq, d = x_prompt.shape
    db, t_new, _ = x_sample.shape
    a = N_HEADS * HEAD_DIM
    g, pdim = ssm_a_re.shape[1:]
    n_state = g * pdim
    d_ff = w_gate.shape[-1]
    length = seq + N_META

    meta = meta_tokens.astype(x_prompt.dtype)
    tm = _row_tile(length)
    qb, k, kb, v, vb, u = _project(x_prompt, meta, tm, p, u_time_major=True)
    att = _attn_prompt(qb.reshape(nb, length, a), kb.reshape(nb, length, a), vb.reshape(nb, length, a), p)
    zst = jnp.zeros((nb, n_state), F32)
    tc = _row_tile(length * nb, cap=768) // nb
    ys, hr, hi = _ssm(u.reshape(length, nb, a), zst, zst, p, tc)
    ys2 = ys.reshape(length, nb * a)
    plain = lambda w: pl.BlockSpec((N_META, w), lambda b, t: (0, 0))
    _, hist_meta = _post(meta, att[0, :N_META], ys2[:N_META, :a], jnp.zeros((1, 2, d_ff), F32), p, N_META, 1,
                         shift=1, att_spec=plain(a), ssm_spec=plain(a))
    tp = _row_tile(seq, cap=512)
    att_spec = pl.BlockSpec((pl.Element(1), pl.Element(tp), pl.Element(a)),
                            lambda b, t: (b, pl.multiple_of(N_META + t * tp, N_META), 0))
    ssm_spec = pl.BlockSpec((pl.Element(tp), pl.Element(a)),
                            lambda b, t: (pl.multiple_of(N_META + t * tp, N_META), pl.multiple_of(b * a, a)))
    yp, cst_p = _post(x_prompt.reshape(nb * seq, d), att, ys2, hist_meta, p, tp, nb, shift=1,
                      att_spec=att_spec, ssm_spec=ssm_spec, hist_shared=True)
    y_prompt = yp.reshape(nb, seq, d)
    k_prompt = k.reshape(1, nb, length, N_HEADS, HEAD_DIM)
    v_prompt = v.reshape(1, nb, length, N_HEADS, HEAD_DIM)
    ssm_re_p = hr.reshape(1, nb, g, pdim)
    ssm_im_p = hi.reshape(1, nb, g, pdim)
    conv_p = cst_p[None]

    rows_s = db * t_new
    xs2 = jnp.transpose(x_sample, (1, 0, 2)).reshape(rows_s, d)
    qb, k, kb, v, vb, u = _project(xs2, None, rows_s, p, u_time_major=False)
    to_bm = lambda z: jnp.transpose(z.reshape(t_new, db, a), (1, 0, 2))
    q_hct = jnp.transpose(qb.reshape(t_new, db, N_HEADS, 1, HEAD_DIM), (1, 2, 3, 0, 4))
    q_rep = jnp.broadcast_to(q_hct, (db, N_HEADS, 2, t_new, HEAD_DIM)).reshape(db, N_HEADS * 2 * t_new, HEAD_DIM)
    n_pool, page = cache_k.shape[1:3]
    att = _attn_decode(q_rep, to_bm(kb), to_bm(vb), cache_k[0].reshape(n_pool, page * N_HEADS, HEAD_DIM),
                       cache_v[0].reshape(n_pool, page * N_HEADS, HEAD_DIM), page_table, p)
    att_tm = jnp.transpose(att, (1, 0, 2)).reshape(rows_s, a)
    ys, hr, hi = _ssm(u.reshape(t_new, db, a), state_ssm_re[0].reshape(db, n_state),
                      state_ssm_im[0].reshape(db, n_state), p, t_new)
    hist = jnp.transpose(state_ffn_conv[0], (1, 0, 2)).reshape(1, 2 * db, d_ff)
    whole = pl.BlockSpec((rows_s, a), lambda b, t: (0, 0))
    ysm, cst_s = _post(xs2, att_tm, ys.reshape(rows_s, a), hist, p, rows_s, 1, shift=db,
                       att_spec=whole, ssm_spec=whole)
    y_sample = jnp.transpose(ysm.reshape(t_new, db, d), (1, 0, 2))
    k_sample = jnp.transpose(k.reshape(t_new, db, N_HEADS, HEAD_DIM), (1, 0, 2, 3))[None]
    v_sample = jnp.transpose(v.reshape(t_new, db, N_HEADS, HEAD_DIM), (1, 0, 2, 3))[None]
    ssm_re_s = hr.reshape(1, db, g, pdim)
    ssm_im_s = hi.reshape(1, db, g, pdim)
    conv_s = jnp.transpose(cst_s.reshape(2, db, d_ff), (1, 0, 2))[None]

    return (y_prompt, y_sample, k_prompt, v_prompt, k_sample, v_sample,
            ssm_re_p, ssm_im_p, ssm_re_s, ssm_im_s, conv_p, conv_s)
```

```python
import functools
import math

import jax
import jax.numpy as jnp
from jax import lax
from jax.experimental import pallas as pl
from jax.experimental.pallas import tpu as pltpu

N_META = 16
N_HEADS = 4
QK_DIM = 64
HEAD_DIM = 2 * QK_DIM
SSM_GROUP = 16
SSM_STATE = 64
EPS = 1e-6
NEG = -1e30
LAM_INIT = 0.8 - 0.6 * math.exp(-0.3 * 0)
LOG2E = math.log2(math.e)

LANES = 128
MXU = 256
VMEM_LIMIT = 56 * 1024 * 1024

F32 = jnp.float32
BF16 = jnp.bfloat16


def _dot(a, b):
    return jnp.dot(a, b, preferred_element_type=F32)


def _dot_nt(a, b):
    return lax.dot_general(a, b, (((1,), (1,)), ((), ())), preferred_element_type=F32)


def _const_spec(shape):
    nd = len(shape)
    return pl.BlockSpec(shape, lambda *_: (0,) * nd, pipeline_mode=pl.Buffered(1))


def _lam(lq_ref, lk_ref):
    e = jnp.exp(jnp.sum(lq_ref[...] * lk_ref[...], axis=1, keepdims=True))
    return e[0:1] - e[1:2] + LAM_INIT


def _proj_kernel(x_ref, meta_ref, g1_ref, w_ref, qg_ref, kg_ref, ones_ref,
                 qb_ref, k_ref, kb_ref, v_ref, vb_ref, u_ref, *, n_meta):
    outs = (qb_ref, k_ref, kb_ref, v_ref, vb_ref, u_ref)
    consts = (g1_ref, w_ref, qg_ref, kg_ref, ones_ref)
    tm = qb_ref.shape[0]
    if not n_meta:
        _proj_body(x_ref[...], consts, outs)
        return
    t = pl.program_id(1)

    @pl.when(t == 0)
    def _():
        _proj_body(jnp.concatenate([meta_ref[...], x_ref[0, 0:tm - n_meta, :]], axis=0), consts, outs)

    @pl.when(t != 0)
    def _():
        _proj_body(x_ref[0], consts, outs)


def _proj_body(x, consts, outs):
    g1_ref, w_ref, qg_ref, kg_ref, ones_ref = consts
    qb_ref, k_ref, kb_ref, v_ref, vb_ref, u_ref = outs
    a = N_HEADS * HEAD_DIM
    r = lax.rsqrt(jnp.mean(x * x, axis=-1, keepdims=True) + EPS)
    xn = ((x * r) * g1_ref[...]).astype(BF16)
    proj = _dot(xn, w_ref[...])

    def comp_norm(z, g):
        z2 = (z * z).astype(BF16)
        ms = jnp.concatenate(
            [_dot(z2[:, j * MXU:(j + 1) * MXU], ones_ref[...]) for j in range(a // MXU)], axis=1)
        return (z * lax.rsqrt(ms + EPS)) * g

    qn = comp_norm(proj[:, :a], qg_ref[...])
    kn = comp_norm(proj[:, a:2 * a], kg_ref[...])
    v = proj[:, 2 * a:3 * a]
    qb_ref[...] = (qn * (QK_DIM ** -0.5 * LOG2E)).astype(BF16)
    kb_ref[...] = kn.astype(BF16)
    vb_ref[...] = v.astype(BF16)
    u_ref[...] = proj[:, 3 * a:]
    tm = x.shape[0]
    for h in range(N_HEADS):
        k_ref[pl.ds(h, tm, stride=N_HEADS), :] = kn[:, h * HEAD_DIM:(h + 1) * HEAD_DIM]
        v_ref[pl.ds(h, tm, stride=N_HEADS), :] = v[:, h * HEAD_DIM:(h + 1) * HEAD_DIM]


def _project(x, meta, tm, p, u_time_major):
    a = N_HEADS * HEAD_DIM
    d = x.shape[-1]
    if meta is None:
        nb, n_meta = 1, 0
        per_b = x.shape[0]
        x_spec = pl.BlockSpec((tm, d), lambda b, t: (t, 0))
        meta = jnp.zeros((8, d), x.dtype)
    else:
        nb, n_meta = x.shape[0], meta.shape[0]
        per_b = n_meta + x.shape[1]
        x_spec = pl.BlockSpec(
            (pl.Element(1), pl.Element(tm), pl.Element(d)),
            lambda b, t: (b, pl.multiple_of(jnp.maximum(t * tm - n_meta, 0), 8), 0))
    rows = nb * per_b
    nt = per_b // tm
    assert nt * tm == per_b and n_meta % 8 == 0
    grid = (nb, nt)
    row_spec = lambda w: pl.BlockSpec((tm, w), lambda b, t: (b * nt + t, 0))
    head_spec = pl.BlockSpec((tm * N_HEADS, HEAD_DIM), lambda b, t: (b * nt + t, 0))
    if u_time_major:
        u_shape = jax.ShapeDtypeStruct((per_b, nb * a), F32)
        u_spec = pl.BlockSpec((tm, a), lambda b, t: (t, b))
    else:
        u_shape = jax.ShapeDtypeStruct((rows, a), F32)
        u_spec = row_spec(a)
    out_shape = (jax.ShapeDtypeStruct((rows, a), BF16), jax.ShapeDtypeStruct((rows * N_HEADS, HEAD_DIM), F32),
                 jax.ShapeDtypeStruct((rows, a), BF16), jax.ShapeDtypeStruct((rows * N_HEADS, HEAD_DIM), F32),
                 jax.ShapeDtypeStruct((rows, a), BF16), u_shape)
    return pl.pallas_call(
        functools.partial(_proj_kernel, n_meta=n_meta),
        grid=grid,
        in_specs=[x_spec, _const_spec(meta.shape), _const_spec((1, d)), _const_spec(p["w_in"].shape),
                  _const_spec((1, a)), _const_spec((1, a)), _const_spec((MXU, MXU))],
        out_specs=(row_spec(a), head_spec, row_spec(a), head_spec, row_spec(a), u_spec),
        out_shape=out_shape,
        compiler_params=pltpu.CompilerParams(
            dimension_semantics=("arbitrary", "arbitrary"), vmem_limit_bytes=VMEM_LIMIT),
        name="proj",
    )(x, meta, p["norm1"], p["w_in"], p["q_gain"], p["k_gain"], p["ones_blk"])


def _attn_prompt_kernel(q_ref, k_ref, v_ref, lq_ref, lk_ref, sg_ref, o_ref, acc_s, m_s, s_s, *, tq, n_tiles):
    lam = _lam(lq_ref, lk_ref)
    sg = sg_ref[...]
    lo = lax.broadcasted_iota(jnp.int32, (1, HEAD_DIM), 1) < QK_DIM
    n_ones = acc_s.shape[1] - HEAD_DIM

    def hcols(h):
        return slice(h * HEAD_DIM, (h + 1) * HEAD_DIM)

    def stack_q(q):
        z = jnp.zeros_like(q)
        return jnp.concatenate([jnp.where(lo, q, z), jnp.where(lo, z, q)], axis=0)

    def init(h, t2):
        m_s[h, :, 0:t2] = jnp.full((1, t2), NEG, F32)
        acc_s[h, :, 0:t2] = jnp.zeros((HEAD_DIM + n_ones, t2), F32)

    def scores(slot, h, q2, kt, mask):
        s = _dot_nt(kt, q2)
        if mask is not None:
            s = jnp.where(mask, s, NEG)
        s_s[slot, h, 0:kt.shape[0], 0:q2.shape[0]] = s

    def absorb(slot, h, nk, t2, vt):
        s = s_s[slot, h, 0:nk, 0:t2]
        m_prev = m_s[h, :, 0:t2]
        m_new = jnp.maximum(m_prev, jnp.max(s, axis=0, keepdims=True))
        alpha = jnp.exp2(m_prev - m_new)
        pr = jnp.exp2(s - m_new).astype(BF16)
        va = jnp.concatenate([vt.T, jnp.ones((n_ones, nk), BF16)], axis=0)
        acc_s[h, :, 0:t2] = alpha * acc_s[h, :, 0:t2] + _dot(va, pr)
        m_s[h, :, 0:t2] = m_new

    def finish(h, t):
        acc = acc_s[h, :, 0:2 * t]
        on = acc[0:HEAD_DIM] / acc[HEAD_DIM:HEAD_DIM + 1]
        o = on[:, 0:t] - lam * on[:, t:2 * t]
        r = lax.rsqrt(jnp.mean(o * o, axis=0, keepdims=True) + EPS)
        return ((o * r).T * sg).astype(o_ref.dtype)

    def causal_mask(t, nk, offset):
        qi = lax.broadcasted_iota(jnp.int32, (nk, 2 * t), 1)
        qi = jnp.where(qi >= t, qi - t, qi)
        ki = lax.broadcasted_iota(jnp.int32, (nk, 2 * t), 0)
        return ki <= qi + offset

    tmq = LANES
    for h in range(N_HEADS):
        init(h, 2 * tmq)
        scores(0, h, stack_q(q_ref[0:tmq, hcols(h)]), k_ref[0:tmq, hcols(h)], causal_mask(tmq, tmq, 0))
    for h in range(N_HEADS):
        absorb(0, h, tmq, 2 * tmq, v_ref[0:tmq, hcols(h)])
        o_ref[0:N_META, hcols(h)] = finish(h, tmq)[0:N_META]

    dk = tq + N_META

    def q_tile(j, carry):
        qs = pl.multiple_of(N_META + j * tq, N_META)
        q2 = [stack_q(q_ref[pl.ds(qs, tq), hcols(h)]) for h in range(N_HEADS)]
        ds = pl.multiple_of(j * tq, tq)
        dmask = causal_mask(tq, dk, N_META)
        for h in range(N_HEADS):
            init(h, 2 * tq)

        def plain_scores(slot, i):
            ks = pl.multiple_of(i * tq, tq)
            for h in range(N_HEADS):
                scores(slot, h, q2[h], k_ref[pl.ds(ks, tq), hcols(h)], None)

        def diag_scores(slot):
            for h in range(N_HEADS):
                scores(slot, h, q2[h], k_ref[pl.ds(ds, dk), hcols(h)], dmask)

        def plain_absorb(slot, i):
            ks = pl.multiple_of(i * tq, tq)
            for h in range(N_HEADS):
                absorb(slot, h, tq, 2 * tq, v_ref[pl.ds(ks, tq), hcols(h)])

        def diag_absorb(slot):
            for h in range(N_HEADS):
                absorb(slot, h, dk, 2 * tq, v_ref[pl.ds(ds, dk), hcols(h)])
                o_ref[pl.ds(qs, tq), hcols(h)] = finish(h, tq)

        @pl.when(j == 0)
        def _():
            diag_scores(0)
            diag_absorb(0)

        @pl.when(j > 0)
        def _():
            plain_scores(0, 0)

        n_pairs = (j - 1) // 2

        def k_pair(pi, c):
            i = 2 * pi
            plain_scores(1, i + 1)
            plain_absorb(0, i)
            plain_scores(0, i + 2)
            plain_absorb(1, i + 1)
            return c

        lax.fori_loop(0, n_pairs, k_pair, 0)

        @pl.when(jnp.logical_and(j > 0, j % 2 == 1))
        def _():
            diag_scores(1)
            plain_absorb(0, j - 1)
            diag_absorb(1)

        @pl.when(jnp.logical_and(j > 0, j % 2 == 0))
        def _():
            plain_scores(1, j - 1)
            plain_absorb(0, j - 2)
            diag_scores(0)
            plain_absorb(1, j - 1)
            diag_absorb(0)

        return carry

    lax.fori_loop(0, n_tiles, q_tile, 0)


def _attn_prompt(qb, kb, vb, p, tq=256):
    nb, length, a = qb.shape
    n_tiles = (length - N_META) // tq
    assert N_META + n_tiles * tq == length and tq % LANES == 0
    spec = pl.BlockSpec((None, length, a), lambda b: (b, 0, 0))
    return pl.pallas_call(
        functools.partial(_attn_prompt_kernel, tq=tq, n_tiles=n_tiles),
        grid=(nb,),
        in_specs=[spec, spec, spec, _const_spec((2, QK_DIM)), _const_spec((2, QK_DIM)),
                  _const_spec((1, HEAD_DIM))],
        out_specs=spec,
        out_shape=jax.ShapeDtypeStruct((nb, length, a), BF16),
        scratch_shapes=[pltpu.VMEM((N_HEADS, HEAD_DIM + 16, 2 * tq), F32), pltpu.VMEM((N_HEADS, 1, 2 * tq), F32),
                        pltpu.VMEM((2, N_HEADS, tq + N_META, 2 * tq), F32)],
        compiler_params=pltpu.CompilerParams(
            dimension_semantics=("arbitrary",), vmem_limit_bytes=VMEM_LIMIT),
        name="attn_prompt",
    )(qb, kb, vb, p["lam_q"], p["lam_k"], p["sub_gain"])


def _attn_decode_kernel(pt_ref, q_ref, kn_ref, vn_ref, lq_ref, lk_ref, sg_ref, *refs, n_pages, t_new):
    del pt_ref
    k_refs, v_refs, o_ref = refs[:n_pages], refs[n_pages:2 * n_pages], refs[2 * n_pages]
    lam = _lam(lq_ref, lk_ref)
    grp = 2 * t_new
    nrow = N_HEADS * grp
    ncol = k_refs[0].shape[0]
    row = lax.broadcasted_iota(jnp.int32, (nrow, HEAD_DIM), 0)
    lane = lax.broadcasted_iota(jnp.int32, (nrow, HEAD_DIM), 1)
    comp_ok = (lane >= QK_DIM) == ((row % grp) >= t_new)
    q32 = jnp.where(comp_ok, q_ref[...].astype(F32), 0.0)
    qb = q32.astype(BF16)
    t_of_row = lax.broadcasted_iota(jnp.int32, (nrow, 1), 0) % t_new
    head_ok = (lax.broadcasted_iota(jnp.int32, (nrow, ncol), 1) % N_HEADS
               == lax.broadcasted_iota(jnp.int32, (nrow, ncol), 0) // grp)

    s_pages = [jnp.where(head_ok, _dot_nt(qb, k_refs[pg][...].astype(BF16)), NEG) for pg in range(n_pages)]

    def per_row_head(z_ref, t):
        z = z_ref[...].astype(F32)
        return jnp.concatenate(
            [jnp.broadcast_to(z[t:t + 1, h * HEAD_DIM:(h + 1) * HEAD_DIM], (grp, HEAD_DIM)) for h in range(N_HEADS)],
            axis=0)

    s_new = []
    for t in range(t_new):
        sc = jnp.sum(q32 * per_row_head(kn_ref, t), axis=1, keepdims=True)
        s_new.append(jnp.where(t_of_row >= t, sc, NEG))

    m = s_pages[0]
    for s in s_pages[1:]:
        m = jnp.maximum(m, s)
    m = jnp.max(m, axis=1, keepdims=True)
    for s in s_new:
        m = jnp.maximum(m, s)

    acc = jnp.zeros((nrow, HEAD_DIM), F32)
    psum = None
    for pg in range(n_pages):
        pr = jnp.exp2(s_pages[pg] - m)
        psum = pr if psum is None else psum + pr
        acc = acc + _dot(pr.astype(BF16), v_refs[pg][...].astype(BF16))
    lsum = jnp.sum(psum, axis=1, keepdims=True)
    for t in range(t_new):
        pr = jnp.exp2(s_new[t] - m)
        lsum = lsum + pr
        acc = acc + pr * per_row_head(vn_ref, t)
    on = acc / lsum
    sg = sg_ref[...]
    for h in range(N_HEADS):
        r0 = h * grp
        o = on[r0:r0 + t_new] - lam * on[r0 + t_new:r0 + grp]
        r = lax.rsqrt(jnp.mean(o * o, axis=-1, keepdims=True) + EPS)
        o_ref[:, h * HEAD_DIM:(h + 1) * HEAD_DIM] = ((o * r) * sg).astype(o_ref.dtype)


def _attn_decode(q_rep, kn, vn, cache_k, cache_v, page_table, p):
    db, nrow, _ = q_rep.shape
    t_new, a = kn.shape[1:]
    n_pages = page_table.shape[1]
    prow = cache_k.shape[1]
    pt = page_table.reshape(-1)

    def page_spec(pg):
        return pl.BlockSpec((None, prow, HEAD_DIM), lambda b, pt_ref: (pt_ref[b * n_pages + pg], 0, 0))

    const = lambda shape: pl.BlockSpec(shape, lambda b, pt_ref: (0,) * len(shape))
    grid_spec = pltpu.PrefetchScalarGridSpec(
        num_scalar_prefetch=1,
        grid=(db,),
        in_specs=[pl.BlockSpec((None, nrow, HEAD_DIM), lambda b, pt_ref: (b, 0, 0)),
                  pl.BlockSpec((None, t_new, a), lambda b, pt_ref: (b, 0, 0)),
                  pl.BlockSpec((None, t_new, a), lambda b, pt_ref: (b, 0, 0)),
                  const((2, QK_DIM)), const((2, QK_DIM)), const((1, HEAD_DIM))]
        + [page_spec(pg) for pg in range(n_pages)] + [page_spec(pg) for pg in range(n_pages)],
        out_specs=pl.BlockSpec((None, t_new, a), lambda b, pt_ref: (b, 0, 0)),
    )
    return pl.pallas_call(
        functools.partial(_attn_decode_kernel, n_pages=n_pages, t_new=t_new),
        grid_spec=grid_spec,
        out_shape=jax.ShapeDtypeStruct((db, t_new, a), BF16),
        compiler_params=pltpu.CompilerParams(
            dimension_semantics=("arbitrary",), vmem_limit_bytes=VMEM_LIMIT),
        name="attn_decode",
    )(pt, q_rep, kn, vn, p["lam_q"], p["lam_k"], p["sub_gain"],
      *([cache_k] * n_pages), *([cache_v] * n_pages))


def _ssm_param_kernel(ar_ref, ai_ref, ldt_ref, br_ref, bi_ref, abr_ref, abi_ref, bbr_ref, bbi_ref):
    ar, ai = ar_ref[...], ai_ref[...]
    dt = jnp.exp(ldt_ref[...])
    mag = jnp.exp(ar * dt)
    abr, abi = mag * jnp.cos(ai * dt), mag * jnp.sin(ai * dt)
    den = ar * ar + ai * ai
    nr, ni = abr - 1.0, abi
    gr, gi = (nr * ar + ni * ai) / den, (ni * ar - nr * ai) / den
    abr_ref[...] = abr
    abi_ref[...] = abi
    for c in range(SSM_GROUP):
        br, bi = br_ref[c], bi_ref[c]
        bbr_ref[c] = gr * br - gi * bi
        bbi_ref[c] = gr * bi + gi * br


def _ssm_params(a_re, a_im, log_dt, b_re, b_im):
    g, pdim = a_re.shape
    c = b_re.shape[-1]
    b_re_t = jnp.transpose(b_re, (2, 0, 1))
    b_im_t = jnp.transpose(b_im, (2, 0, 1))
    gp = jax.ShapeDtypeStruct((g, pdim), F32)
    cgp = jax.ShapeDtypeStruct((c, g, pdim), F32)
    return pl.pallas_call(_ssm_param_kernel, out_shape=(gp, gp, cgp, cgp), name="ssm_params")(
        a_re, a_im, log_dt.reshape(g, 1), b_re_t, b_im_t)


def _ssm_kernel(u_ref, h0r_ref, h0i_ref, abr_ref, abi_ref, bre_ref, bim_ref, crt_ref, cit_ref,
                d_ref, wg_ref, bg_ref, y_ref, hr_ref, hi_ref, xr_s, xi_s, *, tc, nb):
    i = pl.program_id(0)
    rows = tc * nb
    n_state = xr_s.shape[1]
    width = d_ref.shape[1]

    @pl.when(i == 0)
    def _():
        hr_ref[...] = h0r_ref[...]
        hi_ref[...] = h0i_ref[...]

    u = u_ref[...].reshape(rows, width)
    ub = u.astype(BF16)
    n_in = bre_ref.shape[0]
    kin, nout = bre_ref.shape[1], bre_ref.shape[2]
    for mblk in range(n_in):
        um = ub[:, mblk * kin:(mblk + 1) * kin]
        xr_s[:, mblk * nout:(mblk + 1) * nout] = _dot(um, bre_ref[mblk])
        xi_s[:, mblk * nout:(mblk + 1) * nout] = _dot(um, bim_ref[mblk])

    def step(t, carry):
        hr, hi = carry
        r0 = pl.multiple_of(t * nb, nb)
        xr = xr_s[pl.ds(r0, nb), :]
        xi = xi_s[pl.ds(r0, nb), :]
        abr, abi = abr_ref[...], abi_ref[...]
        nhr = abr * hr - abi * hi + xr
        nhi = abr * hi + abi * hr + xi
        xr_s[pl.ds(r0, nb), :] = nhr
        xi_s[pl.ds(r0, nb), :] = nhi
        return nhr, nhi

    hr, hi = lax.fori_loop(0, tc, step, (hr_ref[...], hi_ref[...]), unroll=2 if tc % 2 == 0 else 1)
    hr_ref[...] = hr
    hi_ref[...] = hi

    n_out = crt_ref.shape[0]
    kout, wout = crt_ref.shape[1], crt_ref.shape[2]
    ys = []
    for j in range(n_out):
        hrj = xr_s[:, j * kout:(j + 1) * kout].astype(BF16)
        hij = xi_s[:, j * kout:(j + 1) * kout].astype(BF16)
        ys.append(_dot(hrj, crt_ref[j]) - _dot(hij, cit_ref[j]))
    y = jnp.concatenate(ys, axis=1) + d_ref[...] * u
    g = jax.nn.gelu(y)
    out = g * jax.nn.sigmoid(_dot(g.astype(BF16), wg_ref[...]) + bg_ref[...])
    y_ref[...] = out.reshape(tc, nb, width)


def _ssm(u3, h0r, h0i, p, tc):
    t_len, nb, width = u3.shape
    n_state = h0r.shape[1]
    assert t_len % tc == 0
    st = jax.ShapeDtypeStruct((nb, n_state), F32)
    st_spec = pl.BlockSpec((nb, n_state), lambda i: (0, 0))
    return pl.pallas_call(
        functools.partial(_ssm_kernel, tc=tc, nb=nb),
        grid=(t_len // tc,),
        in_specs=[pl.BlockSpec((tc, nb, width), lambda i: (i, 0, 0)),
                  _const_spec((nb, n_state)), _const_spec((nb, n_state)),
                  _const_spec((nb, n_state)), _const_spec((nb, n_state)),
                  _const_spec(p["bre"].shape), _const_spec(p["bim"].shape),
                  _const_spec(p["crt"].shape), _const_spec(p["cit"].shape),
                  _const_spec((1, width)), _const_spec((width, width)), _const_spec((1, width))],
        out_specs=(pl.BlockSpec((tc, nb, width), lambda i: (i, 0, 0)), st_spec, st_spec),
        out_shape=(jax.ShapeDtypeStruct((t_len, nb, width), F32), st, st),
        scratch_shapes=[pltpu.VMEM((tc * nb, n_state), F32), pltpu.VMEM((tc * nb, n_state), F32)],
        compiler_params=pltpu.CompilerParams(
            dimension_semantics=("arbitrary",), vmem_limit_bytes=VMEM_LIMIT),
        name="ssm",
    )(u3, h0r, h0i, jnp.broadcast_to(p["abr"], (nb, n_state)), jnp.broadcast_to(p["abi"], (nb, n_state)),
      p["bre"], p["bim"], p["crt"], p["cit"],
      p["d_skip"], p["w_glu"], p["b_glu"])


def _post_kernel(x_ref, att_ref, ssm_ref, hist_ref, wo_ref, g2_ref, wgate_ref, wup_ref, cw_ref, cb_ref,
                 wd_ref, y_ref, cst_ref, xn_s, h_s, ap_s, hist_s, *, shift, fc):
    t = pl.program_id(1)
    tm, a = y_ref.shape[0], wo_ref.shape[0] // 2
    d_ff = wgate_ref.shape[1]
    off = ap_s.shape[1] - tm

    @pl.when(t == 0)
    def _():
        hist_s[...] = hist_ref[...]

    mix = jnp.concatenate([att_ref[...].reshape(tm, a), ssm_ref[...].reshape(tm, a).astype(BF16)], axis=1)
    xm = x_ref[...] + _dot(mix, wo_ref[...])
    y_ref[...] = xm
    r = lax.rsqrt(jnp.mean(xm * xm, axis=-1, keepdims=True) + EPS)
    xn_s[...] = ((xm * r) * g2_ref[...]).astype(BF16)
    for j in range(d_ff // fc):
        cs = slice(j * fc, (j + 1) * fc)
        ap = ap_s.at[j % 2]
        gate = _dot(xn_s[...], wgate_ref[:, cs])
        up = _dot(xn_s[...], wup_ref[:, cs])
        ap[off - 2 * shift:off, :] = hist_s[:, cs]
        ap[off:off + tm, :] = gate
        conv = (cb_ref[:, cs] + cw_ref[0:1, cs] * ap[off - 2 * shift:off - 2 * shift + tm, :]
                + cw_ref[1:2, cs] * ap[off - shift:off - shift + tm, :] + cw_ref[2:3, cs] * gate)
        hist_s[:, cs] = ap[off + tm - 2 * shift:off + tm, :]
        h_s[:, cs] = (jax.nn.gelu(conv) * up).astype(BF16)
    y_ref[...] += _dot(h_s[...], wd_ref[...])
    cst_ref[...] = hist_s[...]


def _post(x2d, att, ssm, hist, p, tm, nb, shift, att_spec, ssm_spec, hist_shared=False, fc=256):
    rows, d = x2d.shape
    d_ff = p["w_gate"].shape[1]
    per_b = rows // nb
    nt = per_b // tm
    assert nt * tm == per_b and d_ff % fc == 0 and tm >= 2 * shift
    off = -(-2 * shift // 8) * 8
    row_spec = lambda w: pl.BlockSpec((tm, w), lambda b, t: (b * nt + t, 0))
    hist_in = pl.BlockSpec((None, 2 * shift, d_ff), (lambda b, t: (0, 0, 0)) if hist_shared else (lambda b, t: (b, 0, 0)))
    hist_out = pl.BlockSpec((None, 2 * shift, d_ff), lambda b, t: (b, 0, 0))
    return pl.pallas_call(
        functools.partial(_post_kernel, shift=shift, fc=fc),
        grid=(nb, nt),
        in_specs=[row_spec(d), att_spec, ssm_spec, hist_in,
                  _const_spec(p["w_out"].shape), _const_spec((1, d)),
                  _const_spec(p["w_gate"].shape), _const_spec(p["w_up"].shape),
                  _const_spec((3, d_ff)), _const_spec((1, d_ff)), _const_spec(p["w_down"].shape)],
        out_specs=(row_spec(d), hist_out),
        out_shape=(jax.ShapeDtypeStruct((rows, d), F32), jax.ShapeDtypeStruct((nb, 2 * shift, d_ff), F32)),
        scratch_shapes=[pltpu.VMEM((tm, d), BF16), pltpu.VMEM((tm, d_ff), BF16),
                        pltpu.VMEM((2, off + tm, fc), F32), pltpu.VMEM((2 * shift, d_ff), F32)],
        compiler_params=pltpu.CompilerParams(
            dimension_semantics=("arbitrary", "arbitrary"), vmem_limit_bytes=VMEM_LIMIT),
        name="post",
    )(x2d, att, ssm, hist, p["w_out"], p["norm2"], p["w_gate"], p["w_up"], p["conv_w"], p["conv_b"],
      p["w_down"])


def _prepare_params(norm1, w_in, q_norm, k_norm, lam_q, lam_k, sub_norm, ssm_a_re, ssm_a_im, ssm_log_dt,
                    ssm_b_re, ssm_b_im, ssm_c_re, ssm_c_im, ssm_d, w_glu, b_glu, w_out, norm2, w_gate,
                    w_up, ffn_conv_w, ffn_conv_b, w_down):
    l = 0
    g, pdim = ssm_a_re[l].shape
    c = SSM_GROUP
    a = N_HEADS * HEAD_DIM
    abr, abi, bbr, bbi = _ssm_params(ssm_a_re[l], ssm_a_im[l], ssm_log_dt[l], ssm_b_re[l], ssm_b_im[l])
    gi = LANES // c
    eye_i = jnp.eye(gi, dtype=F32)

    def in_blocks(bb):
        bb = jnp.transpose(bb, (1, 0, 2)).reshape(g // gi, gi, c, pdim)
        return jnp.einsum("mgcp,gh->mgchp", bb, eye_i).reshape(g // gi, gi * c, gi * pdim).astype(BF16)

    go = MXU // c
    eye_o = jnp.eye(go, dtype=F32)

    def out_blocks(cc):
        cc = cc.reshape(g // go, go, c, pdim)
        return jnp.einsum("jgcp,gh->jgphc", cc, eye_o).reshape(g // go, go * pdim, go * c).astype(BF16)

    comp = jnp.arange(MXU) // QK_DIM
    ones_blk = (comp[:, None] == comp[None, :]).astype(BF16) * (1.0 / QK_DIM)
    return {
        "norm1": norm1[l].reshape(1, -1), "w_in": w_in[l].astype(BF16),
        "q_gain": jnp.tile(q_norm[l].reshape(-1), N_HEADS).reshape(1, a),
        "k_gain": jnp.tile(k_norm[l].reshape(-1), N_HEADS).reshape(1, a),
        "ones_blk": ones_blk.astype(BF16),
        "lam_q": lam_q[l], "lam_k": lam_k[l],
        "sub_gain": (sub_norm[l] * (1.0 - LAM_INIT)).reshape(1, HEAD_DIM),
        "abr": abr.reshape(1, g * pdim), "abi": abi.reshape(1, g * pdim),
        "bre": in_blocks(bbr), "bim": in_blocks(bbi),
        "crt": out_blocks(ssm_c_re[l]), "cit": out_blocks(ssm_c_im[l]),
        "d_skip": ssm_d[l].reshape(1, g * c), "w_glu": w_glu[l].astype(BF16), "b_glu": b_glu[l].reshape(1, -1),
        "w_out": w_out[l].astype(BF16), "norm2": norm2[l].reshape(1, -1),
        "w_gate": w_gate[l].astype(BF16), "w_up": w_up[l].astype(BF16),
        "conv_w": ffn_conv_w[l], "conv_b": ffn_conv_b[l].reshape(1, -1), "w_down": w_down[l].astype(BF16),
    }


def _row_tile(length, cap=768):
    best = None
    for t in range(16, cap + 1, 16):
        if length % t == 0:
            best = t
    assert best is not None
    return best


def kernel(x_prompt, x_sample, cache_k, cache_v, state_ssm_re, state_ssm_im, state_ffn_conv, page_table, meta_tokens, norm1, w_in, q_norm, k_norm, lam_q, lam_k, sub_norm, ssm_a_re, ssm_a_im, ssm_log_dt, ssm_b_re, ssm_b_im, ssm_c_re, ssm_c_im, ssm_d, w_glu, b_glu, w_out, norm2, w_gate, w_up, ffn_conv_w, ffn_conv_b, w_down):
    assert norm1.shape[0] == 1, "single-layer stack"
    p = _prepare_params(norm1, w_in, q_norm, k_norm, lam_q, lam_k, sub_norm, ssm_a_re, ssm_a_im, ssm_log_dt,
                        ssm_b_re, ssm_b_im, ssm_c_re, ssm_c_im, ssm_d, w_glu, b_glu, w_out, norm2, w_gate,
                        w_up, ffn_conv_w, ffn_conv_b, w_down)
    nb, seq, d = x_prompt.shape
    db, t_new, _ = x_sample.shape
    a = N_HEADS * HEAD_DIM
    g, pdim = ssm_a_re.shape[1:]
    n_state = g * pdim
    d_ff = w_gate.shape[-1]
    length = seq + N_META

    meta = meta_tokens.astype(x_prompt.dtype)
    tm = _row_tile(length)
    qb, k, kb, v, vb, u = _project(x_prompt, meta, tm, p, u_time_major=True)
    att = _attn_prompt(qb.reshape(nb, length, a), kb.reshape(nb, length, a), vb.reshape(nb, length, a), p)
    zst = jnp.zeros((nb, n_state), F32)
    tc = _row_tile(length * nb, cap=768) // nb
    ys, hr, hi = _ssm(u.reshape(length, nb, a), zst, zst, p, tc)
    ys2 = ys.reshape(length, nb * a)
    plain = lambda w: pl.BlockSpec((N_META, w), lambda b, t: (0, 0))
    _, hist_meta = _post(meta, att[0, :N_META], ys2[:N_META, :a], jnp.zeros((1, 2, d_ff), F32), p, N_META, 1,
                         shift=1, att_spec=plain(a), ssm_spec=plain(a))
    tp = _row_tile(seq, cap=512)
    att_spec = pl.BlockSpec((pl.Element(1), pl.Element(tp), pl.Element(a)),
                            lambda b, t: (b, pl.multiple_of(N_META + t * tp, N_META), 0))
    ssm_spec = pl.BlockSpec((pl.Element(tp), pl.Element(a)),
                            lambda b, t: (pl.multiple_of(N_META + t * tp, N_META), pl.multiple_of(b * a, a)))
    yp, cst_p = _post(x_prompt.reshape(nb * seq, d), att, ys2, hist_meta, p, tp, nb, shift=1,
                      att_spec=att_spec, ssm_spec=ssm_spec, hist_shared=True)
    y_prompt = yp.reshape(nb, seq, d)
    k_prompt = k.reshape(1, nb, length, N_HEADS, HEAD_DIM)
    v_prompt = v.reshape(1, nb, length, N_HEADS, HEAD_DIM)
    ssm_re_p = hr.reshape(1, nb, g, pdim)
    ssm_im_p = hi.reshape(1, nb, g, pdim)
    conv_p = cst_p[None]

    rows_s = db * t_new
    xs2 = jnp.transpose(x_sample, (1, 0, 2)).reshape(rows_s, d)
    qb, k, kb, v, vb, u = _project(xs2, None, rows_s, p, u_time_major=False)
    to_bm = lambda z: jnp.transpose(z.reshape(t_new, db, a), (1, 0, 2))
    q_hct = jnp.transpose(qb.reshape(t_new, db, N_HEADS, 1, HEAD_DIM), (1, 2, 3, 0, 4))
    q_rep = jnp.broadcast_to(q_hct, (db, N_HEADS, 2, t_new, HEAD_DIM)).reshape(db, N_HEADS * 2 * t_new, HEAD_DIM)
    n_pool, page = cache_k.shape[1:3]
    att = _attn_decode(q_rep, to_bm(kb), to_bm(vb), cache_k[0].reshape(n_pool, page * N_HEADS, HEAD_DIM),
                       cache_v[0].reshape(n_pool, page * N_HEADS, HEAD_DIM), page_table, p)
    att_tm = jnp.transpose(att, (1, 0, 2)).reshape(rows_s, a)
    ys, hr, hi = _ssm(u.reshape(t_new, db, a), state_ssm_re[0].reshape(db, n_state),
                      state_ssm_im[0].reshape(db, n_state), p, t_new)
    hist = jnp.transpose(state_ffn_conv[0], (1, 0, 2)).reshape(1, 2 * db, d_ff)
    whole = pl.BlockSpec((rows_s, a), lambda b, t: (0, 0))
    ysm, cst_s = _post(xs2, att_tm, ys.reshape(rows_s, a), hist, p, rows_s, 1, shift=db,
                       att_spec=whole, ssm_spec=whole)
    y_sample = jnp.transpose(ysm.reshape(t_new, db, d), (1, 0, 2))
    k_sample = jnp.transpose(k.reshape(t_new, db, N_HEADS, HEAD_DIM), (1, 0, 2, 3))[None]
    v_sample = jnp.transpose(v.reshape(t_new, db, N_HEADS, HEAD_DIM), (1, 0, 2, 3))[None]
    ssm_re_s = hr.reshape(1, db, g, pdim)
    ssm_im_s = hi.reshape(1, db, g, pdim)
    conv_s = jnp.transpose(cst_s.reshape(2, db, d_ff), (1, 0, 2))[None]

    return (y_prompt, y_sample, k_prompt, v_prompt, k_sample, v_sample,
            ssm_re_p, ssm_im_p, ssm_re_s, ssm_im_s, conv_p, conv_s)
```

```python
import functools
import math

import jax
import jax.numpy as jnp
from jax import lax
from jax.experimental import pallas as pl
from jax.experimental.pallas import tpu as pltpu

N_META = 16
N_HEADS = 4
QK_DIM = 64
HEAD_DIM = 2 * QK_DIM
SSM_GROUP = 16
SSM_STATE = 64
EPS = 1e-6
NEG = -1e30
LAM_INIT = 0.8 - 0.6 * math.exp(-0.3 * 0)
LOG2E = math.log2(math.e)

LANES = 128
MXU = 256
VMEM_LIMIT = 56 * 1024 * 1024

F32 = jnp.float32
BF16 = jnp.bfloat16


def _dot(a, b):
    return jnp.dot(a, b, preferred_element_type=F32)


def _dot_nt(a, b):
    return lax.dot_general(a, b, (((1,), (1,)), ((), ())), preferred_element_type=F32)


def _const_spec(shape):
    nd = len(shape)
    return pl.BlockSpec(shape, lambda *_: (0,) * nd, pipeline_mode=pl.Buffered(1))


def _lam(lq_ref, lk_ref):
    e = jnp.exp(jnp.sum(lq_ref[...] * lk_ref[...], axis=1, keepdims=True))
    return e[0:1] - e[1:2] + LAM_INIT


def _proj_kernel(x_ref, meta_ref, g1_ref, w_ref, qg_ref, kg_ref, ones_ref,
                 qb_ref, k_ref, kb_ref, v_ref, vb_ref, u_ref, *, n_meta):
    outs = (qb_ref, k_ref, kb_ref, v_ref, vb_ref, u_ref)
    consts = (g1_ref, w_ref, qg_ref, kg_ref, ones_ref)
    tm = qb_ref.shape[0]
    if not n_meta:
        _proj_body(x_ref[...], consts, outs)
        return
    t = pl.program_id(1)

    @pl.when(t == 0)
    def _():
        _proj_body(jnp.concatenate([meta_ref[...], x_ref[0, 0:tm - n_meta, :]], axis=0), consts, outs)

    @pl.when(t != 0)
    def _():
        _proj_body(x_ref[0], consts, outs)


def _proj_body(x, consts, outs):
    g1_ref, w_ref, qg_ref, kg_ref, ones_ref = consts
    qb_ref, k_ref, kb_ref, v_ref, vb_ref, u_ref = outs
    a = N_HEADS * HEAD_DIM
    r = lax.rsqrt(jnp.mean(x * x, axis=-1, keepdims=True) + EPS)
    xn = ((x * r) * g1_ref[...]).astype(BF16)
    proj = _dot(xn, w_ref[...])

    def comp_norm(z, g):
        z2 = (z * z).astype(BF16)
        ms = jnp.concatenate(
            [_dot(z2[:, j * MXU:(j + 1) * MXU], ones_ref[...]) for j in range(a // MXU)], axis=1)
        return (z * lax.rsqrt(ms + EPS)) * g

    qn = comp_norm(proj[:, :a], qg_ref[...])
    kn = comp_norm(proj[:, a:2 * a], kg_ref[...])
    v = proj[:, 2 * a:3 * a]
    qb_ref[...] = (qn * (QK_DIM ** -0.5 * LOG2E)).astype(BF16)
    kb_ref[...] = kn.astype(BF16)
    vb_ref[...] = v.astype(BF16)
    u_ref[...] = proj[:, 3 * a:]
    tm = x.shape[0]
    for h in range(N_HEADS):
        k_ref[pl.ds(h, tm, stride=N_HEADS), :] = kn[:, h * HEAD_DIM:(h + 1) * HEAD_DIM]
        v_ref[pl.ds(h, tm, stride=N_HEADS), :] = v[:, h * HEAD_DIM:(h + 1) * HEAD_DIM]


def _project(x, meta, tm, p, u_time_major):
    a = N_HEADS * HEAD_DIM
    d = x.shape[-1]
    if meta is None:
        nb, n_meta = 1, 0
        per_b = x.shape[0]
        x_spec = pl.BlockSpec((tm, d), lambda b, t: (t, 0))
        meta = jnp.zeros((8, d), x.dtype)
    else:
        nb, n_meta = x.shape[0], meta.shape[0]
        per_b = n_meta + x.shape[1]
        x_spec = pl.BlockSpec(
            (pl.Element(1), pl.Element(tm), pl.Element(d)),
            lambda b, t: (b, pl.multiple_of(jnp.maximum(t * tm - n_meta, 0), 8), 0))
    rows = nb * per_b
    nt = per_b // tm
    assert nt * tm == per_b and n_meta % 8 == 0
    grid = (nb, nt)
    row_spec = lambda w: pl.BlockSpec((tm, w), lambda b, t: (b * nt + t, 0))
    head_spec = pl.BlockSpec((tm * N_HEADS, HEAD_DIM), lambda b, t: (b * nt + t, 0))
    if u_time_major:
        u_shape = jax.ShapeDtypeStruct((per_b, nb * a), F32)
        u_spec = pl.BlockSpec((tm, a), lambda b, t: (t, b))
    else:
        u_shape = jax.ShapeDtypeStruct((rows, a), F32)
        u_spec = row_spec(a)
    out_shape = (jax.ShapeDtypeStruct((rows, a), BF16), jax.ShapeDtypeStruct((rows * N_HEADS, HEAD_DIM), F32),
                 jax.ShapeDtypeStruct((rows, a), BF16), jax.ShapeDtypeStruct((rows * N_HEADS, HEAD_DIM), F32),
                 jax.ShapeDtypeStruct((rows, a), BF16), u_shape)
    return pl.pallas_call(
        functools.partial(_proj_kernel, n_meta=n_meta),
        grid=grid,
        in_specs=[x_spec, _const_spec(meta.shape), _const_spec((1, d)), _const_spec(p["w_in"].shape),
                  _const_spec((1, a)), _const_spec((1, a)), _const_spec((MXU, MXU))],
        out_specs=(row_spec(a), head_spec, row_spec(a), head_spec, row_spec(a), u_spec),
        out_shape=out_shape,
        compiler_params=pltpu.CompilerParams(
            dimension_semantics=("arbitrary", "arbitrary"), vmem_limit_bytes=VMEM_LIMIT),
        name="proj",
    )(x, meta, p["norm1"], p["w_in"], p["q_gain"], p["k_gain"], p["ones_blk"])


def _attn_prompt_kernel(q_ref, k_ref, v_ref, lq_ref, lk_ref, sg_ref, o_ref, acc_s, m_s, s_s, *, tq, n_tiles):
    lam = _lam(lq_ref, lk_ref)
    sg = sg_ref[...]
    lo = lax.broadcasted_iota(jnp.int32, (1, HEAD_DIM), 1) < QK_DIM
    n_ones = acc_s.shape[1] - HEAD_DIM

    def hcols(h):
        return slice(h * HEAD_DIM, (h + 1) * HEAD_DIM)

    def stack_q(q):
        z = jnp.zeros_like(q)
        return jnp.concatenate([jnp.where(lo, q, z), jnp.where(lo, z, q)], axis=0)

    def init(h, t2):
        m_s[h, :, 0:t2] = jnp.full((1, t2), NEG, F32)
        acc_s[h, :, 0:t2] = jnp.zeros((HEAD_DIM + n_ones, t2), F32)

    def scores(slot, h, q2, kt, mask):
        s = _dot_nt(kt, q2)
        if mask is not None:
            s = jnp.where(mask, s, NEG)
        s_s[slot, h, 0:kt.shape[0], 0:q2.shape[0]] = s

    def absorb(slot, h, nk, t2, vt):
        s = s_s[slot, h, 0:nk, 0:t2]
        m_prev = m_s[h, :, 0:t2]
        m_new = jnp.maximum(m_prev, jnp.max(s, axis=0, keepdims=True))
        alpha = jnp.exp2(m_prev - m_new)
        pr = jnp.exp2(s - m_new).astype(BF16)
        va = jnp.concatenate([vt.T, jnp.ones((n_ones, nk), BF16)], axis=0)
        acc_s[h, :, 0:t2] = alpha * acc_s[h, :, 0:t2] + _dot(va, pr)
        m_s[h, :, 0:t2] = m_new

    def finish(h, t):
        acc = acc_s[h, :, 0:2 * t]
        on = acc[0:HEAD_DIM] / acc[HEAD_DIM:HEAD_DIM + 1]
        o = on[:, 0:t] - lam * on[:, t:2 * t]
        r = lax.rsqrt(jnp.mean(o * o, axis=0, keepdims=True) + EPS)
        return ((o * r).T * sg).astype(o_ref.dtype)

    def causal_mask(t, nk, offset):
        qi = lax.broadcasted_iota(jnp.int32, (nk, 2 * t), 1)
        qi = jnp.where(qi >= t, qi - t, qi)
        ki = lax.broadcasted_iota(jnp.int32, (nk, 2 * t), 0)
        return ki <= qi + offset

    tmq = LANES
    for h in range(N_HEADS):
        init(h, 2 * tmq)
        scores(0, h, stack_q(q_ref[0:tmq, hcols(h)]), k_ref[0:tmq, hcols(h)], causal_mask(tmq, tmq, 0))
    for h in range(N_HEADS):
        absorb(0, h, tmq, 2 * tmq, v_ref[0:tmq, hcols(h)])
        o_ref[0:N_META, hcols(h)] = finish(h, tmq)[0:N_META]

    dk = tq + N_META

    def q_tile(j, carry):
        qs = pl.multiple_of(N_META + j * tq, N_META)
        q2 = [stack_q(q_ref[pl.ds(qs, tq), hcols(h)]) for h in range(N_HEADS)]
        ds = pl.multiple_of(j * tq, tq)
        dmask = causal_mask(tq, dk, N_META)
        for h in range(N_HEADS):
            init(h, 2 * tq)

        def plain_scores(slot, i):
            ks = pl.multiple_of(i * tq, tq)
            for h in range(N_HEADS):
                scores(slot, h, q2[h], k_ref[pl.ds(ks, tq), hcols(h)], None)

        def diag_scores(slot):
            for h in range(N_HEADS):
                scores(slot, h, q2[h], k_ref[pl.ds(ds, dk), hcols(h)], dmask)

        def plain_absorb(slot, i):
            ks = pl.multiple_of(i * tq, tq)
            for h in range(N_HEADS):
                absorb(slot, h, tq, 2 * tq, v_ref[pl.ds(ks, tq), hcols(h)])

        def diag_absorb(slot):
            for h in range(N_HEADS):
                absorb(slot, h, dk, 2 * tq, v_ref[pl.ds(ds, dk), hcols(h)])
                o_ref[pl.ds(qs, tq), hcols(h)] = finish(h, tq)

        @pl.when(j == 0)
        def _():
            diag_scores(0)
            diag_absorb(0)

        @pl.when(j > 0)
        def _():
            plain_scores(0, 0)

        n_pairs = (j - 1) // 2

        def k_pair(pi, c):
            i = 2 * pi
            plain_scores(1, i + 1)
            plain_absorb(0, i)
            plain_scores(0, i + 2)
            plain_absorb(1, i + 1)
            return c

        lax.fori_loop(0, n_pairs, k_pair, 0)

        @pl.when(jnp.logical_and(j > 0, j % 2 == 1))
        def _():
            diag_scores(1)
            plain_absorb(0, j - 1)
            diag_absorb(1)

        @pl.when(jnp.logical_and(j > 0, j % 2 == 0))
        def _():
            plain_scores(1, j - 1)
            plain_absorb(0, j - 2)
            diag_scores(0)
            plain_absorb(1, j - 1)
            diag_absorb(0)

        return carry

    lax.fori_loop(0, n_tiles, q_tile, 0)


def _attn_prompt(qb, kb, vb, p, tq=256):
    nb, length, a = qb.shape
    n_tiles = (length - N_META) // tq
    assert N_META + n_tiles * tq == length and tq % LANES == 0
    spec = pl.BlockSpec((None, length, a), lambda b: (b, 0, 0))
    return pl.pallas_call(
        functools.partial(_attn_prompt_kernel, tq=tq, n_tiles=n_tiles),
        grid=(nb,),
        in_specs=[spec, spec, spec, _const_spec((2, QK_DIM)), _const_spec((2, QK_DIM)),
                  _const_spec((1, HEAD_DIM))],
        out_specs=spec,
        out_shape=jax.ShapeDtypeStruct((nb, length, a), BF16),
        scratch_shapes=[pltpu.VMEM((N_HEADS, HEAD_DIM + 16, 2 * tq), F32), pltpu.VMEM((N_HEADS, 1, 2 * tq), F32),
                        pltpu.VMEM((2, N_HEADS, tq + N_META, 2 * tq), F32)],
        compiler_params=pltpu.CompilerParams(
            dimension_semantics=("arbitrary",), vmem_limit_bytes=VMEM_LIMIT),
        name="attn_prompt",
    )(qb, kb, vb, p["lam_q"], p["lam_k"], p["sub_gain"])


def _attn_decode_kernel(pt_ref, q_ref, kn_ref, vn_ref, lq_ref, lk_ref, sg_ref, *refs, n_pages, t_new, bps):
    del pt_ref
    n_kv = bps * n_pages
    k_refs, v_refs, o_ref = refs[:n_kv], refs[n_kv:2 * n_kv], refs[2 * n_kv]
    lam = _lam(lq_ref, lk_ref)
    sg = sg_ref[...]
    grp = 2 * t_new
    nrow = N_HEADS * grp
    ncol = k_refs[0].shape[0]
    row = lax.broadcasted_iota(jnp.int32, (nrow, HEAD_DIM), 0)
    lane = lax.broadcasted_iota(jnp.int32, (nrow, HEAD_DIM), 1)
    comp_ok = (lane >= QK_DIM) == ((row % grp) >= t_new)
    t_of_row = lax.broadcasted_iota(jnp.int32, (nrow, 1), 0) % t_new
    head_ok = (lax.broadcasted_iota(jnp.int32, (nrow, ncol), 1) % N_HEADS
               == lax.broadcasted_iota(jnp.int32, (nrow, ncol), 0) // grp)

    def per_row_head(z, t):
        return jnp.concatenate(
            [jnp.broadcast_to(z[t:t + 1, h * HEAD_DIM:(h + 1) * HEAD_DIM], (grp, HEAD_DIM)) for h in range(N_HEADS)],
            axis=0)

    for e in range(bps):
        kp, vp = k_refs[e * n_pages:(e + 1) * n_pages], v_refs[e * n_pages:(e + 1) * n_pages]
        q32 = jnp.where(comp_ok, q_ref[e].astype(F32), 0.0)
        qb = q32.astype(BF16)
        kn, vn = kn_ref[e].astype(F32), vn_ref[e].astype(F32)
        s_pages = [jnp.where(head_ok, _dot_nt(qb, kp[pg][...].astype(BF16)), NEG) for pg in range(n_pages)]
        s_new = []
        for t in range(t_new):
            sc = jnp.sum(q32 * per_row_head(kn, t), axis=1, keepdims=True)
            s_new.append(jnp.where(t_of_row >= t, sc, NEG))

        m = s_pages[0]
        for s in s_pages[1:]:
            m = jnp.maximum(m, s)
        m = jnp.max(m, axis=1, keepdims=True)
        for s in s_new:
            m = jnp.maximum(m, s)

        acc = jnp.zeros((nrow, HEAD_DIM), F32)
        psum = None
        for pg in range(n_pages):
            pr = jnp.exp2(s_pages[pg] - m)
            psum = pr if psum is None else psum + pr
            acc = acc + _dot(pr.astype(BF16), vp[pg][...].astype(BF16))
        lsum = jnp.sum(psum, axis=1, keepdims=True)
        for t in range(t_new):
            pr = jnp.exp2(s_new[t] - m)
            lsum = lsum + pr
            acc = acc + pr * per_row_head(vn, t)
        on = acc / lsum
        for h in range(N_HEADS):
            r0 = h * grp
            o = on[r0:r0 + t_new] - lam * on[r0 + t_new:r0 + grp]
            r = lax.rsqrt(jnp.mean(o * o, axis=-1, keepdims=True) + EPS)
            o_ref[e, :, h * HEAD_DIM:(h + 1) * HEAD_DIM] = ((o * r) * sg).astype(o_ref.dtype)


def _attn_decode(q_rep, kn, vn, cache_k, cache_v, page_table, p, bps=2):
    db, nrow, _ = q_rep.shape
    t_new, a = kn.shape[1:]
    n_pages = page_table.shape[1]
    prow = cache_k.shape[1]
    pt = page_table.reshape(-1)
    assert db % bps == 0

    def page_spec(e, pg):
        return pl.BlockSpec((None, prow, HEAD_DIM),
                            lambda b, pt_ref: (pt_ref[(b * bps + e) * n_pages + pg], 0, 0))

    page_specs = lambda: [page_spec(e, pg) for e in range(bps) for pg in range(n_pages)]
    const = lambda shape: pl.BlockSpec(shape, lambda b, pt_ref: (0,) * len(shape))
    per_step = lambda r, w: pl.BlockSpec((bps, r, w), lambda b, pt_ref: (b, 0, 0))
    grid_spec = pltpu.PrefetchScalarGridSpec(
        num_scalar_prefetch=1,
        grid=(db // bps,),
        in_specs=[per_step(nrow, HEAD_DIM), per_step(t_new, a), per_step(t_new, a),
                  const((2, QK_DIM)), const((2, QK_DIM)), const((1, HEAD_DIM))] + page_specs() + page_specs(),
        out_specs=per_step(t_new, a),
    )
    return pl.pallas_call(
        functools.partial(_attn_decode_kernel, n_pages=n_pages, t_new=t_new, bps=bps),
        grid_spec=grid_spec,
        out_shape=jax.ShapeDtypeStruct((db, t_new, a), BF16),
        compiler_params=pltpu.CompilerParams(
            dimension_semantics=("arbitrary",), vmem_limit_bytes=VMEM_LIMIT),
        name="attn_decode",
    )(pt, q_rep, kn, vn, p["lam_q"], p["lam_k"], p["sub_gain"],
      *([cache_k] * (bps * n_pages)), *([cache_v] * (bps * n_pages)))


def _ssm_param_kernel(ar_ref, ai_ref, ldt_ref, br_ref, bi_ref, abr_ref, abi_ref, bbr_ref, bbi_ref):
    ar, ai = ar_ref[...], ai_ref[...]
    dt = jnp.exp(ldt_ref[...])
    mag = jnp.exp(ar * dt)
    abr, abi = mag * jnp.cos(ai * dt), mag * jnp.sin(ai * dt)
    den = ar * ar + ai * ai
    nr, ni = abr - 1.0, abi
    gr, gi = (nr * ar + ni * ai) / den, (ni * ar - nr * ai) / den
    abr_ref[...] = abr
    abi_ref[...] = abi
    for c in range(SSM_GROUP):
        br, bi = br_ref[c], bi_ref[c]
        bbr_ref[c] = gr * br - gi * bi
        bbi_ref[c] = gr * bi + gi * br


def _ssm_params(a_re, a_im, log_dt, b_re, b_im):
    g, pdim = a_re.shape
    c = b_re.shape[-1]
    b_re_t = jnp.transpose(b_re, (2, 0, 1))
    b_im_t = jnp.transpose(b_im, (2, 0, 1))
    gp = jax.ShapeDtypeStruct((g, pdim), F32)
    cgp = jax.ShapeDtypeStruct((c, g, pdim), F32)
    return pl.pallas_call(_ssm_param_kernel, out_shape=(gp, gp, cgp, cgp), name="ssm_params")(
        a_re, a_im, log_dt.reshape(g, 1), b_re_t, b_im_t)


def _ssm_kernel(u_ref, up_ref, h0r_ref, h0i_ref, abr_ref, abi_ref, bre_ref, bim_ref, crt_ref, cit_ref,
                d_ref, wg_ref, bg_ref, y_ref, hr_ref, hi_ref, xr_s, xi_s, *, tc, nb, n_chunks):
    i = pl.program_id(0)
    rows = tc * nb
    width = d_ref.shape[1]

    @pl.when(i == 0)
    def _():
        hr_ref[...] = h0r_ref[...]
        hi_ref[...] = h0i_ref[...]
        xr_s[1] = jnp.zeros(xr_s.shape[1:], F32)
        xi_s[1] = jnp.zeros(xi_s.shape[1:], F32)

    def body(cur):
        prev = 1 - cur
        ub = u_ref[...].reshape(rows, width).astype(BF16)
        kin, nout = bre_ref.shape[1], bre_ref.shape[2]
        for mblk in range(bre_ref.shape[0]):
            um = ub[:, mblk * kin:(mblk + 1) * kin]
            xr_s[cur, :, mblk * nout:(mblk + 1) * nout] = _dot(um, bre_ref[mblk])
            xi_s[cur, :, mblk * nout:(mblk + 1) * nout] = _dot(um, bim_ref[mblk])

        hr, hi = hr_ref[...], hi_ref[...]
        for t in range(tc):
            r = slice(t * nb, (t + 1) * nb)
            abr, abi = abr_ref[...], abi_ref[...]
            nhr = abr * hr - abi * hi + xr_s[cur, r, :]
            nhi = abr * hi + abi * hr + xi_s[cur, r, :]
            xr_s[cur, r, :] = nhr
            xi_s[cur, r, :] = nhi
            hr, hi = nhr, nhi
        live = i < n_chunks
        hr_ref[...] = jnp.where(live, hr, hr_ref[...])
        hi_ref[...] = jnp.where(live, hi, hi_ref[...])

        kout = crt_ref.shape[1]
        ys = []
        for j in range(crt_ref.shape[0]):
            hrj = xr_s[prev, :, j * kout:(j + 1) * kout].astype(BF16)
            hij = xi_s[prev, :, j * kout:(j + 1) * kout].astype(BF16)
            ys.append(_dot(hrj, crt_ref[j]) - _dot(hij, cit_ref[j]))
        y = jnp.concatenate(ys, axis=1) + d_ref[...] * up_ref[...].reshape(rows, width)
        g = jax.nn.gelu(y)
        out = g * jax.nn.sigmoid(_dot(g.astype(BF16), wg_ref[...]) + bg_ref[...])
        y_ref[...] = out.reshape(tc, nb, width)

    for par in (0, 1):
        pl.when(i % 2 == par)(functools.partial(body, par))


def _ssm(u3, h0r, h0i, p, tc):
    t_len, nb, width = u3.shape
    n_state = h0r.shape[1]
    n_chunks = t_len // tc
    assert n_chunks * tc == t_len
    st = jax.ShapeDtypeStruct((nb, n_state), F32)
    st_spec = pl.BlockSpec((nb, n_state), lambda i: (0, 0))
    chunk = lambda index: pl.BlockSpec((tc, nb, width), lambda i: (index(i), 0, 0))
    return pl.pallas_call(
        functools.partial(_ssm_kernel, tc=tc, nb=nb, n_chunks=n_chunks),
        grid=(n_chunks + 1,),
        in_specs=[chunk(lambda i: jnp.minimum(i, n_chunks - 1)), chunk(lambda i: jnp.maximum(i - 1, 0)),
                  _const_spec((nb, n_state)), _const_spec((nb, n_state)),
                  _const_spec((nb, n_state)), _const_spec((nb, n_state)),
                  _const_spec(p["bre"].shape), _const_spec(p["bim"].shape),
                  _const_spec(p["crt"].shape), _const_spec(p["cit"].shape),
                  _const_spec((1, width)), _const_spec((width, width)), _const_spec((1, width))],
        out_specs=(chunk(lambda i: jnp.maximum(i - 1, 0)), st_spec, st_spec),
        out_shape=(jax.ShapeDtypeStruct((t_len, nb, width), F32), st, st),
        scratch_shapes=[pltpu.VMEM((2, tc * nb, n_state), F32), pltpu.VMEM((2, tc * nb, n_state), F32)],
        compiler_params=pltpu.CompilerParams(
            dimension_semantics=("arbitrary",), vmem_limit_bytes=VMEM_LIMIT),
        name="ssm",
    )(u3, u3, h0r, h0i, jnp.broadcast_to(p["abr"], (nb, n_state)), jnp.broadcast_to(p["abi"], (nb, n_state)),
      p["bre"], p["bim"], p["crt"], p["cit"],
      p["d_skip"], p["w_glu"], p["b_glu"])


def _post_kernel(x_ref, att_ref, ssm_ref, hist_ref, wo_ref, g2_ref, wgate_ref, wup_ref, cw_ref, cb_ref,
                 wd_ref, y_ref, cst_ref, xn_s, h_s, ap_s, hist_s, *, shift, fc):
    t = pl.program_id(1)
    tm, a = y_ref.shape[0], wo_ref.shape[0] // 2
    d_ff = wgate_ref.shape[1]
    off = ap_s.shape[1] - tm

    @pl.when(t == 0)
    def _():
        hist_s[...] = hist_ref[...]

    mix = jnp.concatenate([att_ref[...].reshape(tm, a), ssm_ref[...].reshape(tm, a).astype(BF16)], axis=1)
    xm = x_ref[...] + _dot(mix, wo_ref[...])
    y_ref[...] = xm
    r = lax.rsqrt(jnp.mean(xm * xm, axis=-1, keepdims=True) + EPS)
    xn_s[...] = ((xm * r) * g2_ref[...]).astype(BF16)
    for j in range(d_ff // fc):
        cs = slice(j * fc, (j + 1) * fc)
        ap = ap_s.at[j % 2]
        gate = _dot(xn_s[...], wgate_ref[:, cs])
        up = _dot(xn_s[...], wup_ref[:, cs])
        ap[off - 2 * shift:off, :] = hist_s[:, cs]
        ap[off:off + tm, :] = gate
        conv = (cb_ref[:, cs] + cw_ref[0:1, cs] * ap[off - 2 * shift:off - 2 * shift + tm, :]
                + cw_ref[1:2, cs] * ap[off - shift:off - shift + tm, :] + cw_ref[2:3, cs] * gate)
        hist_s[:, cs] = ap[off + tm - 2 * shift:off + tm, :]
        h_s[:, cs] = (jax.nn.gelu(conv) * up).astype(BF16)
    y_ref[...] += _dot(h_s[...], wd_ref[...])
    cst_ref[...] = hist_s[...]


def _post(x2d, att, ssm, hist, p, tm, nb, shift, att_spec, ssm_spec, hist_shared=False, fc=256):
    rows, d = x2d.shape
    d_ff = p["w_gate"].shape[1]
    per_b = rows // nb
    nt = per_b // tm
    assert nt * tm == per_b and d_ff % fc == 0 and tm >= 2 * shift
    off = -(-2 * shift // 8) * 8
    row_spec = lambda w: pl.BlockSpec((tm, w), lambda b, t: (b * nt + t, 0))
    hist_in = pl.BlockSpec((None, 2 * shift, d_ff), (lambda b, t: (0, 0, 0)) if hist_shared else (lambda b, t: (b, 0, 0)))
    hist_out = pl.BlockSpec((None, 2 * shift, d_ff), lambda b, t: (b, 0, 0))
    return pl.pallas_call(
        functools.partial(_post_kernel, shift=shift, fc=fc),
        grid=(nb, nt),
        in_specs=[row_spec(d), att_spec, ssm_spec, hist_in,
                  _const_spec(p["w_out"].shape), _const_spec((1, d)),
                  _const_spec(p["w_gate"].shape), _const_spec(p["w_up"].shape),
                  _const_spec((3, d_ff)), _const_spec((1, d_ff)), _const_spec(p["w_down"].shape)],
        out_specs=(row_spec(d), hist_out),
        out_shape=(jax.ShapeDtypeStruct((rows, d), F32), jax.ShapeDtypeStruct((nb, 2 * shift, d_ff), F32)),
        scratch_shapes=[pltpu.VMEM((tm, d), BF16), pltpu.VMEM((tm, d_ff), BF16),
                        pltpu.VMEM((2, off + tm, fc), F32), pltpu.VMEM((2 * shift, d_ff), F32)],
        compiler_params=pltpu.CompilerParams(
            dimension_semantics=("arbitrary", "arbitrary"), vmem_limit_bytes=VMEM_LIMIT),
        name="post",
    )(x2d, att, ssm, hist, p["w_out"], p["norm2"], p["w_gate"], p["w_up"], p["conv_w"], p["conv_b"],
      p["w_down"])


def _prepare_params(norm1, w_in, q_norm, k_norm, lam_q, lam_k, sub_norm, ssm_a_re, ssm_a_im, ssm_log_dt,
                    ssm_b_re, ssm_b_im, ssm_c_re, ssm_c_im, ssm_d, w_glu, b_glu, w_out, norm2, w_gate,
                    w_up, ffn_conv_w, ffn_conv_b, w_down):
    l = 0
    g, pdim = ssm_a_re[l].shape
    c = SSM_GROUP
    a = N_HEADS * HEAD_DIM
    abr, abi, bbr, bbi = _ssm_params(ssm_a_re[l], ssm_a_im[l], ssm_log_dt[l], ssm_b_re[l], ssm_b_im[l])
    gi = LANES // c
    eye_i = jnp.eye(gi, dtype=F32)

    def in_blocks(bb):
        bb = jnp.transpose(bb, (1, 0, 2)).reshape(g // gi, gi, c, pdim)
        return jnp.einsum("mgcp,gh->mgchp", bb, eye_i).reshape(g // gi, gi * c, gi * pdim).astype(BF16)

    go = MXU // c
    eye_o = jnp.eye(go, dtype=F32)

    def out_blocks(cc):
        cc = cc.reshape(g // go, go, c, pdim)
        return jnp.einsum("jgcp,gh->jgphc", cc, eye_o).reshape(g // go, go * pdim, go * c).astype(BF16)

    comp = jnp.arange(MXU) // QK_DIM
    ones_blk = (comp[:, None] == comp[None, :]).astype(BF16) * (1.0 / QK_DIM)
    return {
        "norm1": norm1[l].reshape(1, -1), "w_in": w_in[l].astype(BF16),
        "q_gain": jnp.tile(q_norm[l].reshape(-1), N_HEADS).reshape(1, a),
        "k_gain": jnp.tile(k_norm[l].reshape(-1), N_HEADS).reshape(1, a),
        "ones_blk": ones_blk.astype(BF16),
        "lam_q": lam_q[l], "lam_k": lam_k[l],
        "sub_gain": (sub_norm[l] * (1.0 - LAM_INIT)).reshape(1, HEAD_DIM),
        "abr": abr.reshape(1, g * pdim), "abi": abi.reshape(1, g * pdim),
        "bre": in_blocks(bbr), "bim": in_blocks(bbi),
        "crt": out_blocks(ssm_c_re[l]), "cit": out_blocks(ssm_c_im[l]),
        "d_skip": ssm_d[l].reshape(1, g * c), "w_glu": w_glu[l].astype(BF16), "b_glu": b_glu[l].reshape(1, -1),
        "w_out": w_out[l].astype(BF16), "norm2": norm2[l].reshape(1, -1),
        "w_gate": w_gate[l].astype(BF16), "w_up": w_up[l].astype(BF16),
        "conv_w": ffn_conv_w[l], "conv_b": ffn_conv_b[l].reshape(1, -1), "w_down": w_down[l].astype(BF16),
    }


def _row_tile(length, cap=768):
    best = None
    for t in range(16, cap + 1, 16):
        if length % t == 0:
            best = t
    assert best is not None
    return best


def kernel(x_prompt, x_sample, cache_k, cache_v, state_ssm_re, state_ssm_im, state_ffn_conv, page_table, meta_tokens, norm1, w_in, q_norm, k_norm, lam_q, lam_k, sub_norm, ssm_a_re, ssm_a_im, ssm_log_dt, ssm_b_re, ssm_b_im, ssm_c_re, ssm_c_im, ssm_d, w_glu, b_glu, w_out, norm2, w_gate, w_up, ffn_conv_w, ffn_conv_b, w_down):
    assert norm1.shape[0] == 1, "single-layer stack"
    p = _prepare_params(norm1, w_in, q_norm, k_norm, lam_q, lam_k, sub_norm, ssm_a_re, ssm_a_im, ssm_log_dt,
                        ssm_b_re, ssm_b_im, ssm_c_re, ssm_c_im, ssm_d, w_glu, b_glu, w_out, norm2, w_gate,
                        w_up, ffn_conv_w, ffn_conv_b, w_down)
    nb, seq, d = x_prompt.shape
    db, t_new, _ = x_sample.shape
    a = N_HEADS * HEAD_DIM
    g, pdim = ssm_a_re.shape[1:]
    n_state = g * pdim
    d_ff = w_gate.shape[-1]
    length = seq + N_META

    meta = meta_tokens.astype(x_prompt.dtype)
    tm = _row_tile(length)
    qb, k, kb, v, vb, u = _project(x_prompt, meta, tm, p, u_time_major=True)
    att = _attn_prompt(qb.reshape(nb, length, a), kb.reshape(nb, length, a), vb.reshape(nb, length, a), p)
    zst = jnp.zeros((nb, n_state), F32)
    tc = _row_tile(length * nb, cap=768) // nb
    ys, hr, hi = _ssm(u.reshape(length, nb, a), zst, zst, p, tc)
    ys2 = ys.reshape(length, nb * a)
    plain = lambda w: pl.BlockSpec((N_META, w), lambda b, t: (0, 0))
    _, hist_meta = _post(meta, att[0, :N_META], ys2[:N_META, :a], jnp.zeros((1, 2, d_ff), F32), p, N_META, 1,
                         shift=1, att_spec=plain(a), ssm_spec=plain(a))
    tp = _row_tile(seq, cap=512)
    att_spec = pl.BlockSpec((pl.Element(1), pl.Element(tp), pl.Element(a)),
                            lambda b, t: (b, pl.multiple_of(N_META + t * tp, N_META), 0))
    ssm_spec = pl.BlockSpec((pl.Element(tp), pl.Element(a)),
                            lambda b, t: (pl.multiple_of(N_META + t * tp, N_META), pl.multiple_of(b * a, a)))
    yp, cst_p = _post(x_prompt.reshape(nb * seq, d), att, ys2, hist_meta, p, tp, nb, shift=1,
                      att_spec=att_spec, ssm_spec=ssm_spec, hist_shared=True)
    y_prompt = yp.reshape(nb, seq, d)
    k_prompt = k.reshape(1, nb, length, N_HEADS, HEAD_DIM)
    v_prompt = v.reshape(1, nb, length, N_HEADS, HEAD_DIM)
    ssm_re_p = hr.reshape(1, nb, g, pdim)
    ssm_im_p = hi.reshape(1, nb, g, pdim)
    conv_p = cst_p[None]

    rows_s = db * t_new
    xs2 = jnp.transpose(x_sample, (1, 0, 2)).reshape(rows_s, d)
    qb, k, kb, v, vb, u = _project(xs2, None, rows_s, p, u_time_major=False)
    to_bm = lambda z: jnp.transpose(z.reshape(t_new, db, a), (1, 0, 2))
    q_hct = jnp.transpose(qb.reshape(t_new, db, N_HEADS, 1, HEAD_DIM), (1, 2, 3, 0, 4))
    q_rep = jnp.broadcast_to(q_hct, (db, N_HEADS, 2, t_new, HEAD_DIM)).reshape(db, N_HEADS * 2 * t_new, HEAD_DIM)
    n_pool, page = cache_k.shape[1:3]
    att = _attn_decode(q_rep, to_bm(kb), to_bm(vb), cache_k[0].reshape(n_pool, page * N_HEADS, HEAD_DIM),
                       cache_v[0].reshape(n_pool, page * N_HEADS, HEAD_DIM), page_table, p)
    att_tm = jnp.transpose(att, (1, 0, 2)).reshape(rows_s, a)
    ys, hr, hi = _ssm(u.reshape(t_new, db, a), state_ssm_re[0].reshape(db, n_state),
                      state_ssm_im[0].reshape(db, n_state), p, t_new)
    hist = jnp.transpose(state_ffn_conv[0], (1, 0, 2)).reshape(1, 2 * db, d_ff)
    whole = pl.BlockSpec((rows_s, a), lambda b, t: (0, 0))
    ysm, cst_s = _post(xs2, att_tm, ys.reshape(rows_s, a), hist, p, rows_s, 1, shift=db,
                       att_spec=whole, ssm_spec=whole)
    y_sample = jnp.transpose(ysm.reshape(t_new, db, d), (1, 0, 2))
    k_sample = jnp.transpose(k.reshape(t_new, db, N_HEADS, HEAD_DIM), (1, 0, 2, 3))[None]
    v_sample = jnp.transpose(v.reshape(t_new, db, N_HEADS, HEAD_DIM), (1, 0, 2, 3))[None]
    ssm_re_s = hr.reshape(1, db, g, pdim)
    ssm_im_s = hi.reshape(1, db, g, pdim)
    conv_s = jnp.transpose(cst_s.reshape(2, db, d_ff), (1, 0, 2))[None]

    return (y_prompt, y_sample, k_prompt, v_prompt, k_sample, v_sample,
            ssm_re_p, ssm_im_p, ssm_re_s, ssm_im_s, conv_p, conv_s)
```

```python
import functools
import math

import jax
import jax.numpy as jnp
from jax import lax
from jax.experimental import pallas as pl
from jax.experimental.pallas import tpu as pltpu

N_META = 16
N_HEADS = 4
QK_DIM = 64
HEAD_DIM = 2 * QK_DIM
SSM_GROUP = 16
SSM_STATE = 64
EPS = 1e-6
NEG = -1e30
LAM_INIT = 0.8 - 0.6 * math.exp(-0.3 * 0)
LOG2E = math.log2(math.e)

LANES = 128
MXU = 256
VMEM_LIMIT = 56 * 1024 * 1024

F32 = jnp.float32
BF16 = jnp.bfloat16


def _dot(a, b):
    return jnp.dot(a, b, preferred_element_type=F32)


def _dot_nt(a, b):
    return lax.dot_general(a, b, (((1,), (1,)), ((), ())), preferred_element_type=F32)


def _const_spec(shape):
    nd = len(shape)
    return pl.BlockSpec(shape, lambda *_: (0,) * nd, pipeline_mode=pl.Buffered(1))


def _lam(lq_ref, lk_ref):
    e = jnp.exp(jnp.sum(lq_ref[...] * lk_ref[...], axis=1, keepdims=True))
    return e[0:1] - e[1:2] + LAM_INIT


def _proj_kernel(x_ref, meta_ref, g1_ref, w_ref, qg_ref, kg_ref, ones_ref,
                 qb_ref, k_ref, kb_ref, v_ref, vb_ref, u_ref, wb_s, *, n_meta):
    outs = (qb_ref, k_ref, kb_ref, v_ref, vb_ref, u_ref)
    consts = (g1_ref, wb_s, qg_ref, kg_ref, ones_ref)
    tm = qb_ref.shape[0]

    @pl.when(jnp.logical_and(pl.program_id(0) == 0, pl.program_id(1) == 0))
    def _():
        wb_s[...] = w_ref[...].astype(BF16)

    if not n_meta:
        _proj_body(x_ref[...], consts, outs)
        return
    t = pl.program_id(1)

    @pl.when(t == 0)
    def _():
        _proj_body(jnp.concatenate([meta_ref[...], x_ref[0, 0:tm - n_meta, :]], axis=0), consts, outs)

    @pl.when(t != 0)
    def _():
        _proj_body(x_ref[0], consts, outs)


def _proj_body(x, consts, outs):
    g1_ref, w_ref, qg_ref, kg_ref, ones_ref = consts
    qb_ref, k_ref, kb_ref, v_ref, vb_ref, u_ref = outs
    a = N_HEADS * HEAD_DIM
    r = lax.rsqrt(jnp.mean(x * x, axis=-1, keepdims=True) + EPS)
    xn = ((x * r) * g1_ref[...]).astype(BF16)
    proj = _dot(xn, w_ref[...])

    def comp_norm(z, g):
        z2 = (z * z).astype(BF16)
        ms = jnp.concatenate(
            [_dot(z2[:, j * MXU:(j + 1) * MXU], ones_ref[...]) for j in range(a // MXU)], axis=1)
        return (z * lax.rsqrt(ms + EPS)) * g

    qn = comp_norm(proj[:, :a], qg_ref[...])
    kn = comp_norm(proj[:, a:2 * a], kg_ref[...])
    v = proj[:, 2 * a:3 * a]
    qb_ref[...] = (qn * (QK_DIM ** -0.5 * LOG2E)).astype(BF16)
    kb_ref[...] = kn.astype(BF16)
    vb_ref[...] = v.astype(BF16)
    u_ref[...] = proj[:, 3 * a:]
    tm = x.shape[0]
    for h in range(N_HEADS):
        k_ref[pl.ds(h, tm, stride=N_HEADS), :] = kn[:, h * HEAD_DIM:(h + 1) * HEAD_DIM]
        v_ref[pl.ds(h, tm, stride=N_HEADS), :] = v[:, h * HEAD_DIM:(h + 1) * HEAD_DIM]


def _project(x, meta, tm, p, u_time_major):
    a = N_HEADS * HEAD_DIM
    d = x.shape[-1]
    if meta is None:
        nb, n_meta = 1, 0
        per_b = x.shape[0]
        x_spec = pl.BlockSpec((tm, d), lambda b, t: (t, 0))
        meta = jnp.zeros((8, d), x.dtype)
    else:
        nb, n_meta = x.shape[0], meta.shape[0]
        per_b = n_meta + x.shape[1]
        x_spec = pl.BlockSpec(
            (pl.Element(1), pl.Element(tm), pl.Element(d)),
            lambda b, t: (b, pl.multiple_of(jnp.maximum(t * tm - n_meta, 0), 8), 0))
    rows = nb * per_b
    nt = per_b // tm
    assert nt * tm == per_b and n_meta % 8 == 0
    grid = (nb, nt)
    row_spec = lambda w: pl.BlockSpec((tm, w), lambda b, t: (b * nt + t, 0))
    head_spec = pl.BlockSpec((tm * N_HEADS, HEAD_DIM), lambda b, t: (b * nt + t, 0))
    if u_time_major:
        u_shape = jax.ShapeDtypeStruct((per_b, nb * a), F32)
        u_spec = pl.BlockSpec((tm, a), lambda b, t: (t, b))
    else:
        u_shape = jax.ShapeDtypeStruct((rows, a), F32)
        u_spec = row_spec(a)
    out_shape = (jax.ShapeDtypeStruct((rows, a), BF16), jax.ShapeDtypeStruct((rows * N_HEADS, HEAD_DIM), F32),
                 jax.ShapeDtypeStruct((rows, a), BF16), jax.ShapeDtypeStruct((rows * N_HEADS, HEAD_DIM), F32),
                 jax.ShapeDtypeStruct((rows, a), BF16), u_shape)
    return pl.pallas_call(
        functools.partial(_proj_kernel, n_meta=n_meta),
        grid=grid,
        in_specs=[x_spec, _const_spec(meta.shape), _const_spec((1, d)), _const_spec(p["w_in"].shape),
                  _const_spec((1, a)), _const_spec((1, a)), _const_spec((MXU, MXU))],
        out_specs=(row_spec(a), head_spec, row_spec(a), head_spec, row_spec(a), u_spec),
        out_shape=out_shape,
        scratch_shapes=[pltpu.VMEM(p["w_in"].shape, BF16)],
        compiler_params=pltpu.CompilerParams(
            dimension_semantics=("arbitrary", "arbitrary"), vmem_limit_bytes=VMEM_LIMIT),
        name="proj",
    )(x, meta, p["norm1"], p["w_in"], p["q_gain"], p["k_gain"], p["ones_blk"])


def _attn_prompt_kernel(q_ref, k_ref, v_ref, lq_ref, lk_ref, sg_ref, o_ref, acc_s, m_s, s_s, *, tq, n_tiles):
    lam = _lam(lq_ref, lk_ref)
    sg = sg_ref[...]
    lo = lax.broadcasted_iota(jnp.int32, (1, HEAD_DIM), 1) < QK_DIM
    n_ones = acc_s.shape[1] - HEAD_DIM

    def hcols(h):
        return slice(h * HEAD_DIM, (h + 1) * HEAD_DIM)

    def stack_q(q):
        z = jnp.zeros_like(q)
        return jnp.concatenate([jnp.where(lo, q, z), jnp.where(lo, z, q)], axis=0)

    def init(h, t2):
        m_s[h, :, 0:t2] = jnp.full((1, t2), NEG, F32)
        acc_s[h, :, 0:t2] = jnp.zeros((HEAD_DIM + n_ones, t2), F32)

    def scores(slot, h, q2, kt, mask):
        s = _dot_nt(kt, q2)
        if mask is not None:
            s = jnp.where(mask, s, NEG)
        s_s[slot, h, 0:kt.shape[0], 0:q2.shape[0]] = s

    def absorb(slot, h, nk, t2, vt):
        s = s_s[slot, h, 0:nk, 0:t2]
        m_prev = m_s[h, :, 0:t2]
        m_new = jnp.maximum(m_prev, jnp.max(s, axis=0, keepdims=True))
        alpha = jnp.exp2(m_prev - m_new)
        pr = jnp.exp2(s - m_new).astype(BF16)
        va = jnp.concatenate([vt.T, jnp.ones((n_ones, nk), BF16)], axis=0)
        acc_s[h, :, 0:t2] = alpha * acc_s[h, :, 0:t2] + _dot(va, pr)
        m_s[h, :, 0:t2] = m_new

    def finish(h, t):
        acc = acc_s[h, :, 0:2 * t]
        on = acc[0:HEAD_DIM] / acc[HEAD_DIM:HEAD_DIM + 1]
        o = on[:, 0:t] - lam * on[:, t:2 * t]
        r = lax.rsqrt(jnp.mean(o * o, axis=0, keepdims=True) + EPS)
        return ((o * r).T * sg).astype(o_ref.dtype)

    def causal_mask(t, nk, offset):
        qi = lax.broadcasted_iota(jnp.int32, (nk, 2 * t), 1)
        qi = jnp.where(qi >= t, qi - t, qi)
        ki = lax.broadcasted_iota(jnp.int32, (nk, 2 * t), 0)
        return ki <= qi + offset

    tmq = LANES
    for h in range(N_HEADS):
        init(h, 2 * tmq)
        scores(0, h, stack_q(q_ref[0:tmq, hcols(h)]), k_ref[0:tmq, hcols(h)], causal_mask(tmq, tmq, 0))
    for h in range(N_HEADS):
        absorb(0, h, tmq, 2 * tmq, v_ref[0:tmq, hcols(h)])
        o_ref[0:N_META, hcols(h)] = finish(h, tmq)[0:N_META]

    dk = tq + N_META

    def q_tile(j, carry):
        qs = pl.multiple_of(N_META + j * tq, N_META)
        q2 = [stack_q(q_ref[pl.ds(qs, tq), hcols(h)]) for h in range(N_HEADS)]
        ds = pl.multiple_of(j * tq, tq)
        dmask = causal_mask(tq, dk, N_META)
        for h in range(N_HEADS):
            init(h, 2 * tq)

        def plain_scores(slot, i):
            ks = pl.multiple_of(i * tq, tq)
            for h in range(N_HEADS):
                scores(slot, h, q2[h], k_ref[pl.ds(ks, tq), hcols(h)], None)

        def diag_scores(slot):
            for h in range(N_HEADS):
                scores(slot, h, q2[h], k_ref[pl.ds(ds, dk), hcols(h)], dmask)

        def plain_absorb(slot, i):
            ks = pl.multiple_of(i * tq, tq)
            for h in range(N_HEADS):
                absorb(slot, h, tq, 2 * tq, v_ref[pl.ds(ks, tq), hcols(h)])

        def diag_absorb(slot):
            for h in range(N_HEADS):
                absorb(slot, h, dk, 2 * tq, v_ref[pl.ds(ds, dk), hcols(h)])
                o_ref[pl.ds(qs, tq), hcols(h)] = finish(h, tq)

        @pl.when(j == 0)
        def _():
            diag_scores(0)
            diag_absorb(0)

        @pl.when(j > 0)
        def _():
            plain_scores(0, 0)

        n_pairs = (j - 1) // 2

        def k_pair(pi, c):
            i = 2 * pi
            plain_scores(1, i + 1)
            plain_absorb(0, i)
            plain_scores(0, i + 2)
            plain_absorb(1, i + 1)
            return c

        lax.fori_loop(0, n_pairs, k_pair, 0)

        @pl.when(jnp.logical_and(j > 0, j % 2 == 1))
        def _():
            diag_scores(1)
            plain_absorb(0, j - 1)
            diag_absorb(1)

        @pl.when(jnp.logical_and(j > 0, j % 2 == 0))
        def _():
            plain_scores(1, j - 1)
            plain_absorb(0, j - 2)
            diag_scores(0)
            plain_absorb(1, j - 1)
            diag_absorb(0)

        return carry

    lax.fori_loop(0, n_tiles, q_tile, 0)


def _attn_prompt(qb, kb, vb, p, tq=256):
    nb, length, a = qb.shape
    n_tiles = (length - N_META) // tq
    assert N_META + n_tiles * tq == length and tq % LANES == 0
    spec = pl.BlockSpec((None, length, a), lambda b: (b, 0, 0))
    return pl.pallas_call(
        functools.partial(_attn_prompt_kernel, tq=tq, n_tiles=n_tiles),
        grid=(nb,),
        in_specs=[spec, spec, spec, _const_spec((2, QK_DIM)), _const_spec((2, QK_DIM)),
                  _const_spec((1, HEAD_DIM))],
        out_specs=spec,
        out_shape=jax.ShapeDtypeStruct((nb, length, a), BF16),
        scratch_shapes=[pltpu.VMEM((N_HEADS, HEAD_DIM + 16, 2 * tq), F32), pltpu.VMEM((N_HEADS, 1, 2 * tq), F32),
                        pltpu.VMEM((2, N_HEADS, tq + N_META, 2 * tq), F32)],
        compiler_params=pltpu.CompilerParams(
            dimension_semantics=("arbitrary",), vmem_limit_bytes=VMEM_LIMIT),
        name="attn_prompt",
    )(qb, kb, vb, p["lam_q"], p["lam_k"], p["sub_gain"])


def _attn_decode_kernel(pt_ref, q_ref, kn_ref, vn_ref, lq_ref, lk_ref, sg_ref, *refs, n_pages, t_new, bps):
    del pt_ref
    n_kv = bps * n_pages
    k_refs, v_refs, o_ref = refs[:n_kv], refs[n_kv:2 * n_kv], refs[2 * n_kv]
    lam = _lam(lq_ref, lk_ref)
    sg = sg_ref[...]
    grp = 2 * t_new
    nrow = N_HEADS * grp
    ncol = k_refs[0].shape[0]
    row = lax.broadcasted_iota(jnp.int32, (nrow, HEAD_DIM), 0)
    lane = lax.broadcasted_iota(jnp.int32, (nrow, HEAD_DIM), 1)
    comp_ok = (lane >= QK_DIM) == ((row % grp) >= t_new)
    t_of_row = lax.broadcasted_iota(jnp.int32, (nrow, 1), 0) % t_new
    head_ok = (lax.broadcasted_iota(jnp.int32, (nrow, ncol), 1) % N_HEADS
               == lax.broadcasted_iota(jnp.int32, (nrow, ncol), 0) // grp)

    def per_row_head(z, t):
        return jnp.concatenate(
            [jnp.broadcast_to(z[t:t + 1, h * HEAD_DIM:(h + 1) * HEAD_DIM], (grp, HEAD_DIM)) for h in range(N_HEADS)],
            axis=0)

    for e in range(bps):
        kp, vp = k_refs[e * n_pages:(e + 1) * n_pages], v_refs[e * n_pages:(e + 1) * n_pages]
        q32 = jnp.where(comp_ok, q_ref[e].astype(F32), 0.0)
        qb = q32.astype(BF16)
        kn, vn = kn_ref[e].astype(F32), vn_ref[e].astype(F32)
        s_pages = [jnp.where(head_ok, _dot_nt(qb, kp[pg][...].astype(BF16)), NEG) for pg in range(n_pages)]
        s_new = []
        for t in range(t_new):
            sc = jnp.sum(q32 * per_row_head(kn, t), axis=1, keepdims=True)
            s_new.append(jnp.where(t_of_row >= t, sc, NEG))

        m = s_pages[0]
        for s in s_pages[1:]:
            m = jnp.maximum(m, s)
        m = jnp.max(m, axis=1, keepdims=True)
        for s in s_new:
            m = jnp.maximum(m, s)

        acc = jnp.zeros((nrow, HEAD_DIM), F32)
        psum = None
        for pg in range(n_pages):
            pr = jnp.exp2(s_pages[pg] - m)
            psum = pr if psum is None else psum + pr
            acc = acc + _dot(pr.astype(BF16), vp[pg][...].astype(BF16))
        lsum = jnp.sum(psum, axis=1, keepdims=True)
        for t in range(t_new):
            pr = jnp.exp2(s_new[t] - m)
            lsum = lsum + pr
            acc = acc + pr * per_row_head(vn, t)
        on = acc / lsum
        for h in range(N_HEADS):
            r0 = h * grp
            o = on[r0:r0 + t_new] - lam * on[r0 + t_new:r0 + grp]
            r = lax.rsqrt(jnp.mean(o * o, axis=-1, keepdims=True) + EPS)
            o_ref[e, :, h * HEAD_DIM:(h + 1) * HEAD_DIM] = ((o * r) * sg).astype(o_ref.dtype)


def _attn_decode(q_rep, kn, vn, cache_k, cache_v, page_table, p, bps=2):
    db, nrow, _ = q_rep.shape
    t_new, a = kn.shape[1:]
    n_pages = page_table.shape[1]
    prow = cache_k.shape[1]
    pt = page_table.reshape(-1)
    assert db % bps == 0

    def page_spec(e, pg):
        return pl.BlockSpec((None, prow, HEAD_DIM),
                            lambda b, pt_ref: (pt_ref[(b * bps + e) * n_pages + pg], 0, 0))

    page_specs = lambda: [page_spec(e, pg) for e in range(bps) for pg in range(n_pages)]
    const = lambda shape: pl.BlockSpec(shape, lambda b, pt_ref: (0,) * len(shape))
    per_step = lambda r, w: pl.BlockSpec((bps, r, w), lambda b, pt_ref: (b, 0, 0))
    grid_spec = pltpu.PrefetchScalarGridSpec(
        num_scalar_prefetch=1,
        grid=(db // bps,),
        in_specs=[per_step(nrow, HEAD_DIM), per_step(t_new, a), per_step(t_new, a),
                  const((2, QK_DIM)), const((2, QK_DIM)), const((1, HEAD_DIM))] + page_specs() + page_specs(),
        out_specs=per_step(t_new, a),
    )
    return pl.pallas_call(
        functools.partial(_attn_decode_kernel, n_pages=n_pages, t_new=t_new, bps=bps),
        grid_spec=grid_spec,
        out_shape=jax.ShapeDtypeStruct((db, t_new, a), BF16),
        compiler_params=pltpu.CompilerParams(
            dimension_semantics=("arbitrary",), vmem_limit_bytes=VMEM_LIMIT),
        name="attn_decode",
    )(pt, q_rep, kn, vn, p["lam_q"], p["lam_k"], p["sub_gain"],
      *([cache_k] * (bps * n_pages)), *([cache_v] * (bps * n_pages)))


def _ssm_param_kernel(ar_ref, ai_ref, ldt_ref, br_ref, bi_ref, abr_ref, abi_ref, bbr_ref, bbi_ref):
    ar, ai = ar_ref[...], ai_ref[...]
    dt = jnp.exp(ldt_ref[...])
    mag = jnp.exp(ar * dt)
    abr, abi = mag * jnp.cos(ai * dt), mag * jnp.sin(ai * dt)
    den = ar * ar + ai * ai
    nr, ni = abr - 1.0, abi
    gr, gi = (nr * ar + ni * ai) / den, (ni * ar - nr * ai) / den
    abr_ref[...] = abr
    abi_ref[...] = abi
    for c in range(SSM_GROUP):
        br, bi = br_ref[c], bi_ref[c]
        bbr_ref[c] = gr * br - gi * bi
        bbi_ref[c] = gr * bi + gi * br


def _ssm_params(a_re, a_im, log_dt, b_re, b_im):
    g, pdim = a_re.shape
    c = b_re.shape[-1]
    b_re_t = jnp.transpose(b_re, (2, 0, 1))
    b_im_t = jnp.transpose(b_im, (2, 0, 1))
    gp = jax.ShapeDtypeStruct((g, pdim), F32)
    cgp = jax.ShapeDtypeStruct((c, g, pdim), F32)
    return pl.pallas_call(_ssm_param_kernel, out_shape=(gp, gp, cgp, cgp), name="ssm_params")(
        a_re, a_im, log_dt.reshape(g, 1), b_re_t, b_im_t)


def _ssm_kernel(u_ref, up_ref, h0r_ref, h0i_ref, abr_ref, abi_ref, bre_ref, bim_ref, crt_ref, cit_ref,
                d_ref, wg_ref, bg_ref, y_ref, hr_ref, hi_ref, xr_s, xi_s, *, tc, nb, n_chunks):
    i = pl.program_id(0)

    @pl.when(i == 0)
    def _():
        hr_ref[...] = h0r_ref[...]
        hi_ref[...] = h0i_ref[...]
        xr_s[1] = jnp.zeros(xr_s.shape[1:], F32)
        xi_s[1] = jnp.zeros(xi_s.shape[1:], F32)

    def body(cur):
        prev = 1 - cur
        ub = u_ref[...].astype(BF16)
        kin, nout = bre_ref.shape[1], bre_ref.shape[2]
        for mblk in range(bre_ref.shape[0]):
            um = ub[:, mblk * kin:(mblk + 1) * kin]
            xr_s[cur, :, mblk * nout:(mblk + 1) * nout] = _dot(um, bre_ref[mblk])
            xi_s[cur, :, mblk * nout:(mblk + 1) * nout] = _dot(um, bim_ref[mblk])

        hr, hi = hr_ref[...], hi_ref[...]
        for t in range(tc):
            r = slice(t * nb, (t + 1) * nb)
            abr, abi = abr_ref[...], abi_ref[...]
            nhr = abr * hr - abi * hi + xr_s[cur, r, :]
            nhi = abr * hi + abi * hr + xi_s[cur, r, :]
            xr_s[cur, r, :] = nhr
            xi_s[cur, r, :] = nhi
            hr, hi = nhr, nhi
        live = i < n_chunks
        hr_ref[...] = jnp.where(live, hr, hr_ref[...])
        hi_ref[...] = jnp.where(live, hi, hi_ref[...])

        kout = crt_ref.shape[1]
        ys = []
        for j in range(crt_ref.shape[0]):
            hrj = xr_s[prev, :, j * kout:(j + 1) * kout].astype(BF16)
            hij = xi_s[prev, :, j * kout:(j + 1) * kout].astype(BF16)
            ys.append(_dot(hrj, crt_ref[j]) - _dot(hij, cit_ref[j]))
        y = jnp.concatenate(ys, axis=1) + d_ref[...] * up_ref[...]
        g = jax.nn.gelu(y)
        out = g * jax.nn.sigmoid(_dot(g.astype(BF16), wg_ref[...]) + bg_ref[...])
        y_ref[...] = out

    for par in (0, 1):
        pl.when(i % 2 == par)(functools.partial(body, par))


def _ssm(u3, h0r, h0i, p, tc):
    n_chunks, rows, width = u3.shape
    nb, n_state = h0r.shape
    assert rows == tc * nb
    st = jax.ShapeDtypeStruct((nb, n_state), F32)
    st_spec = pl.BlockSpec((nb, n_state), lambda i: (0, 0))
    chunk = lambda index: pl.BlockSpec((None, rows, width), lambda i: (index(i), 0, 0))
    return pl.pallas_call(
        functools.partial(_ssm_kernel, tc=tc, nb=nb, n_chunks=n_chunks),
        grid=(n_chunks + 1,),
        in_specs=[chunk(lambda i: jnp.minimum(i, n_chunks - 1)), chunk(lambda i: jnp.maximum(i - 1, 0)),
                  _const_spec((nb, n_state)), _const_spec((nb, n_state)),
                  _const_spec((nb, n_state)), _const_spec((nb, n_state)),
                  _const_spec(p["bre"].shape), _const_spec(p["bim"].shape),
                  _const_spec(p["crt"].shape), _const_spec(p["cit"].shape),
                  _const_spec((1, width)), _const_spec((width, width)), _const_spec((1, width))],
        out_specs=(chunk(lambda i: jnp.maximum(i - 1, 0)), st_spec, st_spec),
        out_shape=(jax.ShapeDtypeStruct((n_chunks, rows, width), F32), st, st),
        scratch_shapes=[pltpu.VMEM((2, tc * nb, n_state), F32), pltpu.VMEM((2, tc * nb, n_state), F32)],
        compiler_params=pltpu.CompilerParams(
            dimension_semantics=("arbitrary",), vmem_limit_bytes=VMEM_LIMIT),
        name="ssm",
    )(u3, u3, h0r, h0i, jnp.broadcast_to(p["abr"], (nb, n_state)), jnp.broadcast_to(p["abi"], (nb, n_state)),
      p["bre"], p["bim"], p["crt"], p["cit"],
      p["d_skip"], p["w_glu"], p["b_glu"])


HIST = 2
AP_OFF = 8


def _post_kernel(x_ref, att_ref, ssm_ref, hist_ref, wo_ref, g2_ref, wgate_ref, wup_ref, cw_ref, cb_ref,
                 wd_ref, y_ref, cst_ref, xn_s, h_s, ap_s, hist_s, *, seq_t, fc):
    t = pl.program_id(1)
    tm, a = y_ref.shape[0], wo_ref.shape[0] // 2
    d_ff = wgate_ref.shape[1]
    off = AP_OFF

    if seq_t is None:
        @pl.when(t == 0)
        def _():
            hist_s[...] = hist_ref[...]
    else:
        t_in_seq = lax.broadcasted_iota(jnp.int32, (tm, 1), 0) % seq_t

    mix = jnp.concatenate([att_ref[...].reshape(tm, a), ssm_ref[...].reshape(tm, a).astype(BF16)], axis=1)
    xm = x_ref[...] + _dot(mix, wo_ref[...])
    y_ref[...] = xm
    r = lax.rsqrt(jnp.mean(xm * xm, axis=-1, keepdims=True) + EPS)
    xn_s[...] = ((xm * r) * g2_ref[...]).astype(BF16)
    for j in range(d_ff // fc):
        cs = slice(j * fc, (j + 1) * fc)
        ap = ap_s.at[j % 2]
        gate = _dot(xn_s[...], wgate_ref[:, cs])
        up = _dot(xn_s[...], wup_ref[:, cs])
        ap[off:off + tm, :] = gate
        if seq_t is None:
            ap[off - HIST:off, :] = hist_s[:, cs]
            prev2, prev1 = ap[off - 2:off - 2 + tm, :], ap[off - 1:off - 1 + tm, :]
            hist_s[:, cs] = ap[off + tm - HIST:off + tm, :]
        else:
            ap[off - HIST:off, :] = jnp.zeros((HIST, fc), F32)
            prev2 = jnp.where(t_in_seq < 2, hist_ref[0:tm, cs], ap[off - 2:off - 2 + tm, :])
            prev1 = jnp.where(t_in_seq < 1, hist_ref[1:tm + 1, cs], ap[off - 1:off - 1 + tm, :])
            cst_ref[:, cs] = gate
        conv = cb_ref[:, cs] + cw_ref[0:1, cs] * prev2 + cw_ref[1:2, cs] * prev1 + cw_ref[2:3, cs] * gate
        h_s[:, cs] = (jax.nn.gelu(conv) * up).astype(BF16)
    y_ref[...] += _dot(h_s[...], wd_ref[...])
    if seq_t is None:
        cst_ref[...] = hist_s[...]


def _post(x2d, att, ssm, hist, p, tm, nb, att_spec, ssm_spec, hist_shared=False, seq_t=None, fc=256):
    rows, d = x2d.shape
    d_ff = p["w_gate"].shape[1]
    per_b = rows // nb
    nt = per_b // tm
    assert nt * tm == per_b and d_ff % fc == 0
    assert seq_t is None or (nb == 1 and nt == 1 and tm % seq_t == 0 and seq_t >= HIST)
    n_hist_in = HIST if seq_t is None else tm + AP_OFF
    n_hist_out = HIST if seq_t is None else tm
    row_spec = lambda w: pl.BlockSpec((tm, w), lambda b, t: (b * nt + t, 0))
    hist_in = pl.BlockSpec((None, n_hist_in, d_ff), (lambda b, t: (0, 0, 0)) if hist_shared else (lambda b, t: (b, 0, 0)))
    hist_out = pl.BlockSpec((None, n_hist_out, d_ff), lambda b, t: (b, 0, 0))
    return pl.pallas_call(
        functools.partial(_post_kernel, seq_t=seq_t, fc=fc),
        grid=(nb, nt),
        in_specs=[row_spec(d), att_spec, ssm_spec, hist_in,
                  _const_spec(p["w_out"].shape), _const_spec((1, d)),
                  _const_spec(p["w_gate"].shape), _const_spec(p["w_up"].shape),
                  _const_spec((3, d_ff)), _const_spec((1, d_ff)), _const_spec(p["w_down"].shape)],
        out_specs=(row_spec(d), hist_out),
        out_shape=(jax.ShapeDtypeStruct((rows, d), F32), jax.ShapeDtypeStruct((nb, n_hist_out, d_ff), F32)),
        scratch_shapes=[pltpu.VMEM((tm, d), BF16), pltpu.VMEM((tm, d_ff), BF16),
                        pltpu.VMEM((2, AP_OFF + tm, fc), F32), pltpu.VMEM((HIST, d_ff), F32)],
        compiler_params=pltpu.CompilerParams(
            dimension_semantics=("arbitrary", "arbitrary"), vmem_limit_bytes=VMEM_LIMIT),
        name="post",
    )(x2d, att, ssm, hist, p["w_out"], p["norm2"], p["w_gate"], p["w_up"], p["conv_w"], p["conv_b"],
      p["w_down"])


def _prepare_params(norm1, w_in, q_norm, k_norm, lam_q, lam_k, sub_norm, ssm_a_re, ssm_a_im, ssm_log_dt,
                    ssm_b_re, ssm_b_im, ssm_c_re, ssm_c_im, ssm_d, w_glu, b_glu, w_out, norm2, w_gate,
                    w_up, ffn_conv_w, ffn_conv_b, w_down):
    l = 0
    g, pdim = ssm_a_re[l].shape
    c = SSM_GROUP
    a = N_HEADS * HEAD_DIM
    abr, abi, bbr, bbi = _ssm_params(ssm_a_re[l], ssm_a_im[l], ssm_log_dt[l], ssm_b_re[l], ssm_b_im[l])
    gi = LANES // c

    def in_blocks(bb):
        bb = jnp.transpose(bb, (1, 0, 2)).reshape(g // gi, gi, c, 1, pdim)
        same = jnp.eye(gi, dtype=bool)[None, :, None, :, None]
        return jnp.where(same, bb, 0.0).astype(BF16).reshape(g // gi, gi * c, gi * pdim)

    go = MXU // c

    def out_blocks(cc):
        cc = jnp.transpose(cc.reshape(g // go, go, c, pdim), (0, 1, 3, 2))[:, :, :, None, :]
        same = jnp.eye(go, dtype=bool)[None, :, None, :, None]
        return jnp.where(same, cc, 0.0).astype(BF16).reshape(g // go, go * pdim, go * c)

    comp = jnp.arange(MXU) // QK_DIM
    ones_blk = (comp[:, None] == comp[None, :]).astype(BF16) * (1.0 / QK_DIM)
    return {
        "norm1": norm1[l].reshape(1, -1), "w_in": w_in[l],
        "q_gain": jnp.tile(q_norm[l].reshape(-1), N_HEADS).reshape(1, a),
        "k_gain": jnp.tile(k_norm[l].reshape(-1), N_HEADS).reshape(1, a),
        "ones_blk": ones_blk.astype(BF16),
        "lam_q": lam_q[l], "lam_k": lam_k[l],
        "sub_gain": (sub_norm[l] * (1.0 - LAM_INIT)).reshape(1, HEAD_DIM),
        "abr": abr.reshape(1, g * pdim), "abi": abi.reshape(1, g * pdim),
        "bre": in_blocks(bbr), "bim": in_blocks(bbi),
        "crt": out_blocks(ssm_c_re[l]), "cit": out_blocks(ssm_c_im[l]),
        "d_skip": ssm_d[l].reshape(1, g * c), "w_glu": w_glu[l].astype(BF16), "b_glu": b_glu[l].reshape(1, -1),
        "w_out": w_out[l].astype(BF16), "norm2": norm2[l].reshape(1, -1),
        "w_gate": w_gate[l].astype(BF16), "w_up": w_up[l].astype(BF16),
        "conv_w": ffn_conv_w[l], "conv_b": ffn_conv_b[l].reshape(1, -1), "w_down": w_down[l].astype(BF16),
    }


def _row_tile(length, cap=768):
    best = None
    for t in range(16, cap + 1, 16):
        if length % t == 0:
            best = t
    assert best is not None
    return best


def kernel(x_prompt, x_sample, cache_k, cache_v, state_ssm_re, state_ssm_im, state_ffn_conv, page_table, meta_tokens, norm1, w_in, q_norm, k_norm, lam_q, lam_k, sub_norm, ssm_a_re, ssm_a_im, ssm_log_dt, ssm_b_re, ssm_b_im, ssm_c_re, ssm_c_im, ssm_d, w_glu, b_glu, w_out, norm2, w_gate, w_up, ffn_conv_w, ffn_conv_b, w_down):
    assert norm1.shape[0] == 1, "single-layer stack"
    p = _prepare_params(norm1, w_in, q_norm, k_norm, lam_q, lam_k, sub_norm, ssm_a_re, ssm_a_im, ssm_log_dt,
                        ssm_b_re, ssm_b_im, ssm_c_re, ssm_c_im, ssm_d, w_glu, b_glu, w_out, norm2, w_gate,
                        w_up, ffn_conv_w, ffn_conv_b, w_down)
    nb, seq, d = x_prompt.shape
    db, t_new, _ = x_sample.shape
    a = N_HEADS * HEAD_DIM
    g, pdim = ssm_a_re.shape[1:]
    n_state = g * pdim
    d_ff = w_gate.shape[-1]
    length = seq + N_META

    meta = meta_tokens.astype(x_prompt.dtype)
    tm = _row_tile(length)
    qb, k, kb, v, vb, u = _project(x_prompt, meta, tm, p, u_time_major=True)
    att = _attn_prompt(qb.reshape(nb, length, a), kb.reshape(nb, length, a), vb.reshape(nb, length, a), p)
    zst = jnp.zeros((nb, n_state), F32)
    tc = _row_tile(length * nb, cap=768) // nb
    ys, hr, hi = _ssm(u.reshape(length // tc, tc * nb, a), zst, zst, p, tc)
    ys2 = ys.reshape(length, nb * a)
    plain = lambda w: pl.BlockSpec((N_META, w), lambda b, t: (0, 0))
    _, hist_meta = _post(meta, att[0, :N_META], ys2[:N_META, :a], jnp.zeros((1, HIST, d_ff), F32), p, N_META, 1,
                         att_spec=plain(a), ssm_spec=plain(a))
    tp = _row_tile(seq, cap=512)
    att_spec = pl.BlockSpec((pl.Element(1), pl.Element(tp), pl.Element(a)),
                            lambda b, t: (b, pl.multiple_of(N_META + t * tp, N_META), 0))
    ssm_spec = pl.BlockSpec((pl.Element(tp), pl.Element(a)),
                            lambda b, t: (pl.multiple_of(N_META + t * tp, N_META), pl.multiple_of(b * a, a)))
    yp, cst_p = _post(x_prompt.reshape(nb * seq, d), att, ys2, hist_meta, p, tp, nb,
                      att_spec=att_spec, ssm_spec=ssm_spec, hist_shared=True)
    y_prompt = yp.reshape(nb, seq, d)
    k_prompt = k.reshape(1, nb, length, N_HEADS, HEAD_DIM)
    v_prompt = v.reshape(1, nb, length, N_HEADS, HEAD_DIM)
    ssm_re_p = hr.reshape(1, nb, g, pdim)
    ssm_im_p = hi.reshape(1, nb, g, pdim)
    conv_p = cst_p[None]

    rows_s = db * t_new
    xs2 = x_sample.reshape(rows_s, d)
    qb, k, kb, v, vb, u = _project(xs2, None, rows_s, p, u_time_major=False)
    q_hct = jnp.transpose(qb.reshape(db, t_new, N_HEADS, 1, HEAD_DIM), (0, 2, 3, 1, 4))
    q_rep = jnp.broadcast_to(q_hct, (db, N_HEADS, 2, t_new, HEAD_DIM)).reshape(db, N_HEADS * 2 * t_new, HEAD_DIM)
    n_pool, page = cache_k.shape[1:3]
    att = _attn_decode(q_rep, kb.reshape(db, t_new, a), vb.reshape(db, t_new, a),
                       cache_k[0].reshape(n_pool, page * N_HEADS, HEAD_DIM),
                       cache_v[0].reshape(n_pool, page * N_HEADS, HEAD_DIM), page_table, p)
    u_tm = jnp.transpose(u.reshape(db, t_new, a), (1, 0, 2)).reshape(1, rows_s, a)
    ys, hr, hi = _ssm(u_tm, state_ssm_re[0].reshape(db, n_state), state_ssm_im[0].reshape(db, n_state), p, t_new)
    ys = jnp.transpose(ys.reshape(t_new, db, a), (1, 0, 2))
    hist = jnp.pad(state_ffn_conv[0], ((0, 0), (0, t_new - HIST), (0, 0))).reshape(rows_s, d_ff)
    hist = jnp.pad(hist, ((0, AP_OFF), (0, 0)))[None]
    whole = pl.BlockSpec((rows_s, a), lambda b, t: (0, 0))
    ysm, cst_s = _post(xs2, att.reshape(rows_s, a), ys.reshape(rows_s, a), hist, p, rows_s, 1,
                       att_spec=whole, ssm_spec=whole, seq_t=t_new)
    y_sample = ysm.reshape(db, t_new, d)
    k_sample = k.reshape(1, db, t_new, N_HEADS, HEAD_DIM)
    v_sample = v.reshape(1, db, t_new, N_HEADS, HEAD_DIM)
    ssm_re_s = hr.reshape(1, db, g, pdim)
    ssm_im_s = hi.reshape(1, db, g, pdim)
    conv_s = cst_s.reshape(1, db, t_new, d_ff)[:, :, t_new - HIST:]

    return (y_prompt, y_sample, k_prompt, v_prompt, k_sample, v_sample,
            ssm_re_p, ssm_im_p, ssm_re_s, ssm_im_s, conv_p, conv_s)
```

```python
import functools
import math

import jax
import jax.numpy as jnp
from jax import lax
from jax.experimental import pallas as pl
from jax.experimental.pallas import tpu as pltpu

N_META = 16
N_HEADS = 4
QK_DIM = 64
HEAD_DIM = 2 * QK_DIM
SSM_GROUP = 16
SSM_STATE = 64
EPS = 1e-6
NEG = -1e30
LAM_INIT = 0.8 - 0.6 * math.exp(-0.3 * 0)
LOG2E = math.log2(math.e)

LANES = 128
MXU = 256
VMEM_LIMIT = 56 * 1024 * 1024

F32 = jnp.float32
BF16 = jnp.bfloat16


def _dot(a, b):
    return jnp.dot(a, b, preferred_element_type=F32)


def _dot_nt(a, b):
    return lax.dot_general(a, b, (((1,), (1,)), ((), ())), preferred_element_type=F32)


def _const_spec(shape):
    nd = len(shape)
    return pl.BlockSpec(shape, lambda *_: (0,) * nd, pipeline_mode=pl.Buffered(1))


def _lam(lq_ref, lk_ref):
    e = jnp.exp(jnp.sum(lq_ref[...] * lk_ref[...], axis=1, keepdims=True))
    return e[0:1] - e[1:2] + LAM_INIT


def _proj_kernel(x_ref, meta_ref, g1_ref, w_ref, qg_ref, kg_ref, ones_ref,
                 qb_ref, k_ref, kb_ref, v_ref, vb_ref, u_ref, wb_s, *, n_meta):
    outs = (qb_ref, k_ref, kb_ref, v_ref, vb_ref, u_ref)
    consts = (g1_ref, wb_s, qg_ref, kg_ref, ones_ref)
    tm = qb_ref.shape[0]

    @pl.when(jnp.logical_and(pl.program_id(0) == 0, pl.program_id(1) == 0))
    def _():
        wb_s[...] = w_ref[...].astype(BF16)

    if not n_meta:
        _proj_body(x_ref[...], consts, outs)
        return
    t = pl.program_id(1)

    @pl.when(t == 0)
    def _():
        _proj_body(jnp.concatenate([meta_ref[...], x_ref[0, 0:tm - n_meta, :]], axis=0), consts, outs)

    @pl.when(t != 0)
    def _():
        _proj_body(x_ref[0], consts, outs)


def _proj_body(x, consts, outs):
    g1_ref, w_ref, qg_ref, kg_ref, ones_ref = consts
    qb_ref, k_ref, kb_ref, v_ref, vb_ref, u_ref = outs
    a = N_HEADS * HEAD_DIM
    r = lax.rsqrt(jnp.mean(x * x, axis=-1, keepdims=True) + EPS)
    xn = ((x * r) * g1_ref[...]).astype(BF16)
    proj = _dot(xn, w_ref[...])

    def comp_norm(z, g):
        z2 = (z * z).astype(BF16)
        ms = jnp.concatenate(
            [_dot(z2[:, j * MXU:(j + 1) * MXU], ones_ref[...]) for j in range(a // MXU)], axis=1)
        return (z * lax.rsqrt(ms + EPS)) * g

    qn = comp_norm(proj[:, :a], qg_ref[...])
    kn = comp_norm(proj[:, a:2 * a], kg_ref[...])
    v = proj[:, 2 * a:3 * a]
    qb_ref[...] = (qn * (QK_DIM ** -0.5 * LOG2E)).astype(BF16)
    kb_ref[...] = kn.astype(BF16)
    vb_ref[...] = v.astype(BF16)
    u_ref[...] = proj[:, 3 * a:]
    tm = x.shape[0]
    for h in range(N_HEADS):
        k_ref[pl.ds(h, tm, stride=N_HEADS), :] = kn[:, h * HEAD_DIM:(h + 1) * HEAD_DIM]
        v_ref[pl.ds(h, tm, stride=N_HEADS), :] = v[:, h * HEAD_DIM:(h + 1) * HEAD_DIM]


def _project(x, meta, tm, p, u_time_major):
    a = N_HEADS * HEAD_DIM
    d = x.shape[-1]
    if meta is None:
        nb, n_meta = 1, 0
        per_b = x.shape[0]
        x_spec = pl.BlockSpec((tm, d), lambda b, t: (t, 0))
        meta = jnp.zeros((8, d), x.dtype)
    else:
        nb, n_meta = x.shape[0], meta.shape[0]
        per_b = n_meta + x.shape[1]
        x_spec = pl.BlockSpec(
            (pl.Element(1), pl.Element(tm), pl.Element(d)),
            lambda b, t: (b, pl.multiple_of(jnp.maximum(t * tm - n_meta, 0), 8), 0))
    rows = nb * per_b
    nt = per_b // tm
    assert nt * tm == per_b and n_meta % 8 == 0
    grid = (nb, nt)
    row_spec = lambda w: pl.BlockSpec((tm, w), lambda b, t: (b * nt + t, 0))
    head_spec = pl.BlockSpec((tm * N_HEADS, HEAD_DIM), lambda b, t: (b * nt + t, 0))
    if u_time_major:
        u_shape = jax.ShapeDtypeStruct((per_b, nb * a), F32)
        u_spec = pl.BlockSpec((tm, a), lambda b, t: (t, b))
    else:
        u_shape = jax.ShapeDtypeStruct((rows, a), F32)
        u_spec = row_spec(a)
    out_shape = (jax.ShapeDtypeStruct((rows, a), BF16), jax.ShapeDtypeStruct((rows * N_HEADS, HEAD_DIM), F32),
                 jax.ShapeDtypeStruct((rows, a), BF16), jax.ShapeDtypeStruct((rows * N_HEADS, HEAD_DIM), F32),
                 jax.ShapeDtypeStruct((rows, a), BF16), u_shape)
    return pl.pallas_call(
        functools.partial(_proj_kernel, n_meta=n_meta),
        grid=grid,
        in_specs=[x_spec, _const_spec(meta.shape), _const_spec((1, d)), _const_spec(p["w_in"].shape),
                  _const_spec((1, a)), _const_spec((1, a)), _const_spec((MXU, MXU))],
        out_specs=(row_spec(a), head_spec, row_spec(a), head_spec, row_spec(a), u_spec),
        out_shape=out_shape,
        scratch_shapes=[pltpu.VMEM(p["w_in"].shape, BF16)],
        compiler_params=pltpu.CompilerParams(
            dimension_semantics=("arbitrary", "arbitrary"), vmem_limit_bytes=VMEM_LIMIT),
        name="proj",
    )(x, meta, p["norm1"], p["w_in"], p["q_gain"], p["k_gain"], p["ones_blk"])


def _attn_prompt_kernel(q_ref, k_ref, v_ref, lq_ref, lk_ref, sg_ref, o_ref, acc_s, m_s, s_s, *, tq, n_tiles):
    lam = _lam(lq_ref, lk_ref)
    sg = sg_ref[...]
    lo = lax.broadcasted_iota(jnp.int32, (1, HEAD_DIM), 1) < QK_DIM
    n_ones = acc_s.shape[1] - HEAD_DIM

    def hcols(h):
        return slice(h * HEAD_DIM, (h + 1) * HEAD_DIM)

    def stack_q(q):
        z = jnp.zeros_like(q)
        return jnp.concatenate([jnp.where(lo, q, z), jnp.where(lo, z, q)], axis=0)

    def init(h, t2):
        m_s[h, :, 0:t2] = jnp.full((1, t2), NEG, F32)
        acc_s[h, :, 0:t2] = jnp.zeros((HEAD_DIM + n_ones, t2), F32)

    def scores(slot, h, q2, kt, mask):
        s = _dot_nt(kt, q2)
        if mask is not None:
            s = jnp.where(mask, s, NEG)
        s_s[slot, h, 0:kt.shape[0], 0:q2.shape[0]] = s

    def absorb(slot, h, nk, t2, vt):
        s = s_s[slot, h, 0:nk, 0:t2]
        m_prev = m_s[h, :, 0:t2]
        m_new = jnp.maximum(m_prev, jnp.max(s, axis=0, keepdims=True))
        alpha = jnp.exp2(m_prev - m_new)
        pr = jnp.exp2(s - m_new).astype(BF16)
        va = jnp.concatenate([vt.T, jnp.ones((n_ones, nk), BF16)], axis=0)
        acc_s[h, :, 0:t2] = alpha * acc_s[h, :, 0:t2] + _dot(va, pr)
        m_s[h, :, 0:t2] = m_new

    def finish(h, t):
        acc = acc_s[h, :, 0:2 * t]
        on = acc[0:HEAD_DIM] / acc[HEAD_DIM:HEAD_DIM + 1]
        o = on[:, 0:t] - lam * on[:, t:2 * t]
        r = lax.rsqrt(jnp.mean(o * o, axis=0, keepdims=True) + EPS)
        return ((o * r).T * sg).astype(o_ref.dtype)

    def causal_mask(t, nk, offset):
        qi = lax.broadcasted_iota(jnp.int32, (nk, 2 * t), 1)
        qi = jnp.where(qi >= t, qi - t, qi)
        ki = lax.broadcasted_iota(jnp.int32, (nk, 2 * t), 0)
        return ki <= qi + offset

    tmq = LANES
    for h in range(N_HEADS):
        init(h, 2 * tmq)
        scores(0, h, stack_q(q_ref[0:tmq, hcols(h)]), k_ref[0:tmq, hcols(h)], causal_mask(tmq, tmq, 0))
    for h in range(N_HEADS):
        absorb(0, h, tmq, 2 * tmq, v_ref[0:tmq, hcols(h)])
        o_ref[0:N_META, hcols(h)] = finish(h, tmq)[0:N_META]

    dk = tq + N_META

    def q_tile(j, carry):
        qs = pl.multiple_of(N_META + j * tq, N_META)
        q2 = [stack_q(q_ref[pl.ds(qs, tq), hcols(h)]) for h in range(N_HEADS)]
        ds = pl.multiple_of(j * tq, tq)
        dmask = causal_mask(tq, dk, N_META)
        for h in range(N_HEADS):
            init(h, 2 * tq)

        def plain_scores(slot, i):
            ks = pl.multiple_of(i * tq, tq)
            for h in range(N_HEADS):
                scores(slot, h, q2[h], k_ref[pl.ds(ks, tq), hcols(h)], None)

        def diag_scores(slot):
            for h in range(N_HEADS):
                scores(slot, h, q2[h], k_ref[pl.ds(ds, dk), hcols(h)], dmask)

        def plain_absorb(slot, i):
            ks = pl.multiple_of(i * tq, tq)
            for h in range(N_HEADS):
                absorb(slot, h, tq, 2 * tq, v_ref[pl.ds(ks, tq), hcols(h)])

        def diag_absorb(slot):
            for h in range(N_HEADS):
                absorb(slot, h, dk, 2 * tq, v_ref[pl.ds(ds, dk), hcols(h)])
                o_ref[pl.ds(qs, tq), hcols(h)] = finish(h, tq)

        @pl.when(j == 0)
        def _():
            diag_scores(0)
            diag_absorb(0)

        @pl.when(j > 0)
        def _():
            plain_scores(0, 0)

        n_pairs = (j - 1) // 2

        def k_pair(pi, c):
            i = 2 * pi
            plain_scores(1, i + 1)
            plain_absorb(0, i)
            plain_scores(0, i + 2)
            plain_absorb(1, i + 1)
            return c

        lax.fori_loop(0, n_pairs, k_pair, 0)

        @pl.when(jnp.logical_and(j > 0, j % 2 == 1))
        def _():
            diag_scores(1)
            plain_absorb(0, j - 1)
            diag_absorb(1)

        @pl.when(jnp.logical_and(j > 0, j % 2 == 0))
        def _():
            plain_scores(1, j - 1)
            plain_absorb(0, j - 2)
            diag_scores(0)
            plain_absorb(1, j - 1)
            diag_absorb(0)

        return carry

    lax.fori_loop(0, n_tiles, q_tile, 0)


def _attn_prompt(qb, kb, vb, p, tq=256):
    nb, length, a = qb.shape
    n_tiles = (length - N_META) // tq
    assert N_META + n_tiles * tq == length and tq % LANES == 0
    spec = pl.BlockSpec((None, length, a), lambda b: (b, 0, 0))
    return pl.pallas_call(
        functools.partial(_attn_prompt_kernel, tq=tq, n_tiles=n_tiles),
        grid=(nb,),
        in_specs=[spec, spec, spec, _const_spec((2, QK_DIM)), _const_spec((2, QK_DIM)),
                  _const_spec((1, HEAD_DIM))],
        out_specs=spec,
        out_shape=jax.ShapeDtypeStruct((nb, length, a), BF16),
        scratch_shapes=[pltpu.VMEM((N_HEADS, HEAD_DIM + 16, 2 * tq), F32), pltpu.VMEM((N_HEADS, 1, 2 * tq), F32),
                        pltpu.VMEM((2, N_HEADS, tq + N_META, 2 * tq), F32)],
        compiler_params=pltpu.CompilerParams(
            dimension_semantics=("arbitrary",), vmem_limit_bytes=VMEM_LIMIT),
        name="attn_prompt",
    )(qb, kb, vb, p["lam_q"], p["lam_k"], p["sub_gain"])


def _attn_decode_kernel(pt_ref, q_ref, kn_ref, vn_ref, lq_ref, lk_ref, sg_ref, *refs, n_pages, t_new, bps):
    del pt_ref
    n_kv = bps * n_pages
    k_refs, v_refs, o_ref = refs[:n_kv], refs[n_kv:2 * n_kv], refs[2 * n_kv]
    lam = _lam(lq_ref, lk_ref)
    sg = sg_ref[...]
    grp = 2 * t_new
    nrow = N_HEADS * grp
    ncol = k_refs[0].shape[0]
    row = lax.broadcasted_iota(jnp.int32, (nrow, HEAD_DIM), 0)
    lane = lax.broadcasted_iota(jnp.int32, (nrow, HEAD_DIM), 1)
    comp_ok = (lane >= QK_DIM) == ((row % grp) >= t_new)
    t_of_row = lax.broadcasted_iota(jnp.int32, (nrow, 1), 0) % t_new
    head_ok = (lax.broadcasted_iota(jnp.int32, (nrow, ncol), 1) % N_HEADS
               == lax.broadcasted_iota(jnp.int32, (nrow, ncol), 0) // grp)

    def per_row_head(z, t):
        return jnp.concatenate(
            [jnp.broadcast_to(z[t:t + 1, h * HEAD_DIM:(h + 1) * HEAD_DIM], (grp, HEAD_DIM)) for h in range(N_HEADS)],
            axis=0)

    for e in range(bps):
        kp, vp = k_refs[e * n_pages:(e + 1) * n_pages], v_refs[e * n_pages:(e + 1) * n_pages]
        q32 = jnp.where(comp_ok, q_ref[e].astype(F32), 0.0)
        qb = q32.astype(BF16)
        kn, vn = kn_ref[e].astype(F32), vn_ref[e].astype(F32)
        s_pages = [jnp.where(head_ok, _dot_nt(qb, kp[pg][...].astype(BF16)), NEG) for pg in range(n_pages)]
        s_new = []
        for t in range(t_new):
            sc = jnp.sum(q32 * per_row_head(kn, t), axis=1, keepdims=True)
            s_new.append(jnp.where(t_of_row >= t, sc, NEG))

        m = s_pages[0]
        for s in s_pages[1:]:
            m = jnp.maximum(m, s)
        m = jnp.max(m, axis=1, keepdims=True)
        for s in s_new:
            m = jnp.maximum(m, s)

        acc = jnp.zeros((nrow, HEAD_DIM), F32)
        psum = None
        for pg in range(n_pages):
            pr = jnp.exp2(s_pages[pg] - m)
            psum = pr if psum is None else psum + pr
            acc = acc + _dot(pr.astype(BF16), vp[pg][...].astype(BF16))
        lsum = jnp.sum(psum, axis=1, keepdims=True)
        for t in range(t_new):
            pr = jnp.exp2(s_new[t] - m)
            lsum = lsum + pr
            acc = acc + pr * per_row_head(vn, t)
        on = acc / lsum
        for h in range(N_HEADS):
            r0 = h * grp
            o = on[r0:r0 + t_new] - lam * on[r0 + t_new:r0 + grp]
            r = lax.rsqrt(jnp.mean(o * o, axis=-1, keepdims=True) + EPS)
            o_ref[e, :, h * HEAD_DIM:(h + 1) * HEAD_DIM] = ((o * r) * sg).astype(o_ref.dtype)


def _attn_decode(q_rep, kn, vn, cache_k, cache_v, page_table, p, bps=2):
    db, nrow, _ = q_rep.shape
    t_new, a = kn.shape[1:]
    n_pages = page_table.shape[1]
    prow = cache_k.shape[1]
    pt = page_table.reshape(-1)
    assert db % bps == 0

    def page_spec(e, pg):
        return pl.BlockSpec((None, prow, HEAD_DIM),
                            lambda b, pt_ref: (pt_ref[(b * bps + e) * n_pages + pg], 0, 0))

    page_specs = lambda: [page_spec(e, pg) for e in range(bps) for pg in range(n_pages)]
    const = lambda shape: pl.BlockSpec(shape, lambda b, pt_ref: (0,) * len(shape))
    per_step = lambda r, w: pl.BlockSpec((bps, r, w), lambda b, pt_ref: (b, 0, 0))
    grid_spec = pltpu.PrefetchScalarGridSpec(
        num_scalar_prefetch=1,
        grid=(db // bps,),
        in_specs=[per_step(nrow, HEAD_DIM), per_step(t_new, a), per_step(t_new, a),
                  const((2, QK_DIM)), const((2, QK_DIM)), const((1, HEAD_DIM))] + page_specs() + page_specs(),
        out_specs=per_step(t_new, a),
    )
    return pl.pallas_call(
        functools.partial(_attn_decode_kernel, n_pages=n_pages, t_new=t_new, bps=bps),
        grid_spec=grid_spec,
        out_shape=jax.ShapeDtypeStruct((db, t_new, a), BF16),
        compiler_params=pltpu.CompilerParams(
            dimension_semantics=("arbitrary",), vmem_limit_bytes=VMEM_LIMIT),
        name="attn_decode",
    )(pt, q_rep, kn, vn, p["lam_q"], p["lam_k"], p["sub_gain"],
      *([cache_k] * (bps * n_pages)), *([cache_v] * (bps * n_pages)))


def _ssm_param_kernel(ar_ref, ai_ref, ldt_ref, br_ref, bi_ref, abr_ref, abi_ref, bbr_ref, bbi_ref):
    ar, ai = ar_ref[...], ai_ref[...]
    dt = jnp.exp(ldt_ref[...])
    mag = jnp.exp(ar * dt)
    abr, abi = mag * jnp.cos(ai * dt), mag * jnp.sin(ai * dt)
    den = ar * ar + ai * ai
    nr, ni = abr - 1.0, abi
    gr, gi = (nr * ar + ni * ai) / den, (ni * ar - nr * ai) / den
    abr_ref[...] = abr
    abi_ref[...] = abi
    for c in range(SSM_GROUP):
        br, bi = br_ref[c], bi_ref[c]
        bbr_ref[c] = gr * br - gi * bi
        bbi_ref[c] = gr * bi + gi * br


def _ssm_params(a_re, a_im, log_dt, b_re, b_im):
    g, pdim = a_re.shape
    c = b_re.shape[-1]
    b_re_t = jnp.transpose(b_re, (2, 0, 1))
    b_im_t = jnp.transpose(b_im, (2, 0, 1))
    gp = jax.ShapeDtypeStruct((g, pdim), F32)
    cgp = jax.ShapeDtypeStruct((c, g, pdim), F32)
    return pl.pallas_call(_ssm_param_kernel, out_shape=(gp, gp, cgp, cgp), name="ssm_params")(
        a_re, a_im, log_dt.reshape(g, 1), b_re_t, b_im_t)


def _ssm_kernel(u_ref, up_ref, h0r_ref, h0i_ref, abr_ref, abi_ref, bre_ref, bim_ref, crt_ref, cit_ref,
                d_ref, wg_ref, bg_ref, y_ref, hr_ref, hi_ref, xr_s, xi_s, *, tc, nb, n_chunks):
    i = pl.program_id(0)

    @pl.when(i == 0)
    def _():
        hr_ref[...] = h0r_ref[...]
        hi_ref[...] = h0i_ref[...]
        xr_s[1] = jnp.zeros(xr_s.shape[1:], F32)
        xi_s[1] = jnp.zeros(xi_s.shape[1:], F32)

    def body(cur):
        prev = 1 - cur
        rows, width = tc * nb, d_ref.shape[1]
        ub = u_ref[...].reshape(rows, width).astype(BF16)
        kin, nout = bre_ref.shape[1], bre_ref.shape[2]
        for mblk in range(bre_ref.shape[0]):
            um = ub[:, mblk * kin:(mblk + 1) * kin]
            xr_s[cur, :, mblk * nout:(mblk + 1) * nout] = _dot(um, bre_ref[mblk])
            xi_s[cur, :, mblk * nout:(mblk + 1) * nout] = _dot(um, bim_ref[mblk])

        hr, hi = hr_ref[...], hi_ref[...]
        for t in range(tc):
            r = slice(t * nb, (t + 1) * nb)
            abr, abi = abr_ref[...], abi_ref[...]
            nhr = abr * hr - abi * hi + xr_s[cur, r, :]
            nhi = abr * hi + abi * hr + xi_s[cur, r, :]
            xr_s[cur, r, :] = nhr
            xi_s[cur, r, :] = nhi
            hr, hi = nhr, nhi
        live = i < n_chunks
        hr_ref[...] = jnp.where(live, hr, hr_ref[...])
        hi_ref[...] = jnp.where(live, hi, hi_ref[...])

        kout = crt_ref.shape[1]
        ys = []
        for j in range(crt_ref.shape[0]):
            hrj = xr_s[prev, :, j * kout:(j + 1) * kout].astype(BF16)
            hij = xi_s[prev, :, j * kout:(j + 1) * kout].astype(BF16)
            ys.append(_dot(hrj, crt_ref[j]) - _dot(hij, cit_ref[j]))
        y = jnp.concatenate(ys, axis=1) + d_ref[...] * up_ref[...].reshape(rows, width)
        g = jax.nn.gelu(y)
        out = g * jax.nn.sigmoid(_dot(g.astype(BF16), wg_ref[...]) + bg_ref[...])
        y_ref[...] = out.reshape(tc, nb, width)

    for par in (0, 1):
        pl.when(i % 2 == par)(functools.partial(body, par))


def _ssm(u3, h0r, h0i, p, tc):
    t_len, nb, width = u3.shape
    n_state = h0r.shape[1]
    n_chunks = t_len // tc
    assert n_chunks * tc == t_len
    st = jax.ShapeDtypeStruct((nb, n_state), F32)
    st_spec = pl.BlockSpec((nb, n_state), lambda i: (0, 0))
    chunk = lambda index: pl.BlockSpec((tc, nb, width), lambda i: (index(i), 0, 0))
    return pl.pallas_call(
        functools.partial(_ssm_kernel, tc=tc, nb=nb, n_chunks=n_chunks),
        grid=(n_chunks + 1,),
        in_specs=[chunk(lambda i: jnp.minimum(i, n_chunks - 1)), chunk(lambda i: jnp.maximum(i - 1, 0)),
                  _const_spec((nb, n_state)), _const_spec((nb, n_state)),
                  _const_spec((nb, n_state)), _const_spec((nb, n_state)),
                  _const_spec(p["bre"].shape), _const_spec(p["bim"].shape),
                  _const_spec(p["crt"].shape), _const_spec(p["cit"].shape),
                  _const_spec((1, width)), _const_spec((width, width)), _const_spec((1, width))],
        out_specs=(chunk(lambda i: jnp.maximum(i - 1, 0)), st_spec, st_spec),
        out_shape=(jax.ShapeDtypeStruct((t_len, nb, width), F32), st, st),
        scratch_shapes=[pltpu.VMEM((2, tc * nb, n_state), F32), pltpu.VMEM((2, tc * nb, n_state), F32)],
        compiler_params=pltpu.CompilerParams(
            dimension_semantics=("arbitrary",), vmem_limit_bytes=VMEM_LIMIT),
        name="ssm",
    )(u3, u3, h0r, h0i, jnp.broadcast_to(p["abr"], (nb, n_state)), jnp.broadcast_to(p["abi"], (nb, n_state)),
      p["bre"], p["bim"], p["crt"], p["cit"],
      p["d_skip"], p["w_glu"], p["b_glu"])


HIST = 2
AP_OFF = 8


def _post_kernel(x_ref, att_ref, ssm_ref, hist_ref, wo_ref, g2_ref, wgate_ref, wup_ref, cw_ref, cb_ref,
                 wd_ref, y_ref, cst_ref, xn_s, h_s, ap_s, hist_s, *, seq_t, fc):
    t = pl.program_id(1)
    tm, a = y_ref.shape[0], wo_ref.shape[0] // 2
    d_ff = wgate_ref.shape[1]
    off = AP_OFF

    if seq_t is None:
        @pl.when(t == 0)
        def _():
            hist_s[...] = hist_ref[...]
    else:
        t_in_seq = lax.broadcasted_iota(jnp.int32, (tm, 1), 0) % seq_t

    mix = jnp.concatenate([att_ref[...].reshape(tm, a), ssm_ref[...].reshape(tm, a).astype(BF16)], axis=1)
    xm = x_ref[...] + _dot(mix, wo_ref[...])
    y_ref[...] = xm
    r = lax.rsqrt(jnp.mean(xm * xm, axis=-1, keepdims=True) + EPS)
    xn_s[...] = ((xm * r) * g2_ref[...]).astype(BF16)
    for j in range(d_ff // fc):
        cs = slice(j * fc, (j + 1) * fc)
        ap = ap_s.at[j % 2]
        gate = _dot(xn_s[...], wgate_ref[:, cs])
        up = _dot(xn_s[...], wup_ref[:, cs])
        ap[off:off + tm, :] = gate
        if seq_t is None:
            ap[off - HIST:off, :] = hist_s[:, cs]
            prev2, prev1 = ap[off - 2:off - 2 + tm, :], ap[off - 1:off - 1 + tm, :]
            hist_s[:, cs] = ap[off + tm - HIST:off + tm, :]
        else:
            ap[off - HIST:off, :] = jnp.zeros((HIST, fc), F32)
            prev2 = jnp.where(t_in_seq < 2, hist_ref[0:tm, cs], ap[off - 2:off - 2 + tm, :])
            prev1 = jnp.where(t_in_seq < 1, hist_ref[1:tm + 1, cs], ap[off - 1:off - 1 + tm, :])
            cst_ref[:, cs] = gate
        conv = cb_ref[:, cs] + cw_ref[0:1, cs] * prev2 + cw_ref[1:2, cs] * prev1 + cw_ref[2:3, cs] * gate
        h_s[:, cs] = (jax.nn.gelu(conv) * up).astype(BF16)
    y_ref[...] += _dot(h_s[...], wd_ref[...])
    if seq_t is None:
        cst_ref[...] = hist_s[...]


def _post(x2d, att, ssm, hist, p, tm, nb, att_spec, ssm_spec, hist_shared=False, seq_t=None, fc=256):
    rows, d = x2d.shape
    d_ff = p["w_gate"].shape[1]
    per_b = rows // nb
    nt = per_b // tm
    assert nt * tm == per_b and d_ff % fc == 0
    assert seq_t is None or (nb == 1 and nt == 1 and tm % seq_t == 0 and seq_t >= HIST)
    n_hist_in = HIST if seq_t is None else tm + AP_OFF
    n_hist_out = HIST if seq_t is None else tm
    row_spec = lambda w: pl.BlockSpec((tm, w), lambda b, t: (b * nt + t, 0))
    hist_in = pl.BlockSpec((None, n_hist_in, d_ff), (lambda b, t: (0, 0, 0)) if hist_shared else (lambda b, t: (b, 0, 0)))
    hist_out = pl.BlockSpec((None, n_hist_out, d_ff), lambda b, t: (b, 0, 0))
    return pl.pallas_call(
        functools.partial(_post_kernel, seq_t=seq_t, fc=fc),
        grid=(nb, nt),
        in_specs=[row_spec(d), att_spec, ssm_spec, hist_in,
                  _const_spec(p["w_out"].shape), _const_spec((1, d)),
                  _const_spec(p["w_gate"].shape), _const_spec(p["w_up"].shape),
                  _const_spec((3, d_ff)), _const_spec((1, d_ff)), _const_spec(p["w_down"].shape)],
        out_specs=(row_spec(d), hist_out),
        out_shape=(jax.ShapeDtypeStruct((rows, d), F32), jax.ShapeDtypeStruct((nb, n_hist_out, d_ff), F32)),
        scratch_shapes=[pltpu.VMEM((tm, d), BF16), pltpu.VMEM((tm, d_ff), BF16),
                        pltpu.VMEM((2, AP_OFF + tm, fc), F32), pltpu.VMEM((HIST, d_ff), F32)],
        compiler_params=pltpu.CompilerParams(
            dimension_semantics=("arbitrary", "arbitrary"), vmem_limit_bytes=VMEM_LIMIT),
        name="post",
    )(x2d, att, ssm, hist, p["w_out"], p["norm2"], p["w_gate"], p["w_up"], p["conv_w"], p["conv_b"],
      p["w_down"])


def _prepare_params(norm1, w_in, q_norm, k_norm, lam_q, lam_k, sub_norm, ssm_a_re, ssm_a_im, ssm_log_dt,
                    ssm_b_re, ssm_b_im, ssm_c_re, ssm_c_im, ssm_d, w_glu, b_glu, w_out, norm2, w_gate,
                    w_up, ffn_conv_w, ffn_conv_b, w_down):
    l = 0
    g, pdim = ssm_a_re[l].shape
    c = SSM_GROUP
    a = N_HEADS * HEAD_DIM
    abr, abi, bbr, bbi = _ssm_params(ssm_a_re[l], ssm_a_im[l], ssm_log_dt[l], ssm_b_re[l], ssm_b_im[l])
    gi = LANES // c

    def in_blocks(bb):
        bb = jnp.transpose(bb, (1, 0, 2)).reshape(g // gi, gi, c, 1, pdim)
        same = jnp.eye(gi, dtype=bool)[None, :, None, :, None]
        return jnp.where(same, bb, 0.0).astype(BF16).reshape(g // gi, gi * c, gi * pdim)

    go = MXU // c

    def out_blocks(cc):
        cc = jnp.transpose(cc.reshape(g // go, go, c, pdim), (0, 1, 3, 2))[:, :, :, None, :]
        same = jnp.eye(go, dtype=bool)[None, :, None, :, None]
        return jnp.where(same, cc, 0.0).astype(BF16).reshape(g // go, go * pdim, go * c)

    comp = jnp.arange(MXU) // QK_DIM
    ones_blk = (comp[:, None] == comp[None, :]).astype(BF16) * (1.0 / QK_DIM)
    return {
        "norm1": norm1[l].reshape(1, -1), "w_in": w_in[l],
        "q_gain": jnp.tile(q_norm[l].reshape(-1), N_HEADS).reshape(1, a),
        "k_gain": jnp.tile(k_norm[l].reshape(-1), N_HEADS).reshape(1, a),
        "ones_blk": ones_blk.astype(BF16),
        "lam_q": lam_q[l], "lam_k": lam_k[l],
        "sub_gain": (sub_norm[l] * (1.0 - LAM_INIT)).reshape(1, HEAD_DIM),
        "abr": abr.reshape(1, g * pdim), "abi": abi.reshape(1, g * pdim),
        "bre": in_blocks(bbr), "bim": in_blocks(bbi),
        "crt": out_blocks(ssm_c_re[l]), "cit": out_blocks(ssm_c_im[l]),
        "d_skip": ssm_d[l].reshape(1, g * c), "w_glu": w_glu[l].astype(BF16), "b_glu": b_glu[l].reshape(1, -1),
        "w_out": w_out[l].astype(BF16), "norm2": norm2[l].reshape(1, -1),
        "w_gate": w_gate[l].astype(BF16), "w_up": w_up[l].astype(BF16),
        "conv_w": ffn_conv_w[l], "conv_b": ffn_conv_b[l].reshape(1, -1), "w_down": w_down[l].astype(BF16),
    }


def _row_tile(length, cap=768):
    best = None
    for t in range(16, cap + 1, 16):
        if length % t == 0:
            best = t
    assert best is not None
    return best


def kernel(x_prompt, x_sample, cache_k, cache_v, state_ssm_re, state_ssm_im, state_ffn_conv, page_table, meta_tokens, norm1, w_in, q_norm, k_norm, lam_q, lam_k, sub_norm, ssm_a_re, ssm_a_im, ssm_log_dt, ssm_b_re, ssm_b_im, ssm_c_re, ssm_c_im, ssm_d, w_glu, b_glu, w_out, norm2, w_gate, w_up, ffn_conv_w, ffn_conv_b, w_down):
    assert norm1.shape[0] == 1, "single-layer stack"
    p = _prepare_params(norm1, w_in, q_norm, k_norm, lam_q, lam_k, sub_norm, ssm_a_re, ssm_a_im, ssm_log_dt,
                        ssm_b_re, ssm_b_im, ssm_c_re, ssm_c_im, ssm_d, w_glu, b_glu, w_out, norm2, w_gate,
                        w_up, ffn_conv_w, ffn_conv_b, w_down)
    nb, seq, d = x_prompt.shape
    db, t_new, _ = x_sample.shape
    a = N_HEADS * HEAD_DIM
    g, pdim = ssm_a_re.shape[1:]
    n_state = g * pdim
    d_ff = w_gate.shape[-1]
    length = seq + N_META

    meta = meta_tokens.astype(x_prompt.dtype)
    tm = _row_tile(length)
    qb, k, kb, v, vb, u = _project(x_prompt, meta, tm, p, u_time_major=True)
    att = _attn_prompt(qb.reshape(nb, length, a), kb.reshape(nb, length, a), vb.reshape(nb, length, a), p)
    zst = jnp.zeros((nb, n_state), F32)
    tc = _row_tile(length * nb, cap=768) // nb
    ys, hr, hi = _ssm(u.reshape(length, nb, a), zst, zst, p, tc)
    ys2 = ys.reshape(length, nb * a)
    plain = lambda w: pl.BlockSpec((N_META, w), lambda b, t: (0, 0))
    _, hist_meta = _post(meta, att[0, :N_META], ys2[:N_META, :a], jnp.zeros((1, HIST, d_ff), F32), p, N_META, 1,
                         att_spec=plain(a), ssm_spec=plain(a))
    tp = _row_tile(seq, cap=512)
    att_spec = pl.BlockSpec((pl.Element(1), pl.Element(tp), pl.Element(a)),
                            lambda b, t: (b, pl.multiple_of(N_META + t * tp, N_META), 0))
    ssm_spec = pl.BlockSpec((pl.Element(tp), pl.Element(a)),
                            lambda b, t: (pl.multiple_of(N_META + t * tp, N_META), pl.multiple_of(b * a, a)))
    yp, cst_p = _post(x_prompt.reshape(nb * seq, d), att, ys2, hist_meta, p, tp, nb,
                      att_spec=att_spec, ssm_spec=ssm_spec, hist_shared=True)
    y_prompt = yp.reshape(nb, seq, d)
    k_prompt = k.reshape(1, nb, length, N_HEADS, HEAD_DIM)
    v_prompt = v.reshape(1, nb, length, N_HEADS, HEAD_DIM)
    ssm_re_p = hr.reshape(1, nb, g, pdim)
    ssm_im_p = hi.reshape(1, nb, g, pdim)
    conv_p = cst_p[None]

    rows_s = db * t_new
    xs2 = x_sample.reshape(rows_s, d)
    qb, k, kb, v, vb, u = _project(xs2, None, rows_s, p, u_time_major=False)
    q_hct = jnp.transpose(qb.reshape(db, t_new, N_HEADS, 1, HEAD_DIM), (0, 2, 3, 1, 4))
    q_rep = jnp.broadcast_to(q_hct, (db, N_HEADS, 2, t_new, HEAD_DIM)).reshape(db, N_HEADS * 2 * t_new, HEAD_DIM)
    n_pool, page = cache_k.shape[1:3]
    att = _attn_decode(q_rep, kb.reshape(db, t_new, a), vb.reshape(db, t_new, a),
                       cache_k[0].reshape(n_pool, page * N_HEADS, HEAD_DIM),
                       cache_v[0].reshape(n_pool, page * N_HEADS, HEAD_DIM), page_table, p)
    u_tm = jnp.transpose(u.reshape(db, t_new, a), (1, 0, 2))
    ys, hr, hi = _ssm(u_tm, state_ssm_re[0].reshape(db, n_state), state_ssm_im[0].reshape(db, n_state), p, t_new)
    ys = jnp.transpose(ys, (1, 0, 2))
    hist = jnp.pad(state_ffn_conv[0], ((0, 0), (0, t_new - HIST), (0, 0))).reshape(rows_s, d_ff)
    hist = jnp.pad(hist, ((0, AP_OFF), (0, 0)))[None]
    whole = pl.BlockSpec((rows_s, a), lambda b, t: (0, 0))
    ysm, cst_s = _post(xs2, att.reshape(rows_s, a), ys.reshape(rows_s, a), hist, p, rows_s, 1,
                       att_spec=whole, ssm_spec=whole, seq_t=t_new)
    y_sample = ysm.reshape(db, t_new, d)
    k_sample = k.reshape(1, db, t_new, N_HEADS, HEAD_DIM)
    v_sample = v.reshape(1, db, t_new, N_HEADS, HEAD_DIM)
    ssm_re_s = hr.reshape(1, db, g, pdim)
    ssm_im_s = hi.reshape(1, db, g, pdim)
    conv_s = cst_s.reshape(1, db, t_new, d_ff)[:, :, t_new - HIST:]

    return (y_prompt, y_sample, k_prompt, v_prompt, k_sample, v_sample,
            ssm_re_p, ssm_im_p, ssm_re_s, ssm_im_s, conv_p, conv_s)
```

```python
import functools
import math

import jax
import jax.numpy as jnp
from jax import lax
from jax.experimental import pallas as pl
from jax.experimental.pallas import tpu as pltpu

N_META = 16
N_HEADS = 4
QK_DIM = 64
HEAD_DIM = 2 * QK_DIM
SSM_GROUP = 16
SSM_STATE = 64
EPS = 1e-6
NEG = -1e30
LAM_INIT = 0.8 - 0.6 * math.exp(-0.3 * 0)
LOG2E = math.log2(math.e)

LANES = 128
MXU = 256
VMEM_LIMIT = 56 * 1024 * 1024

F32 = jnp.float32
BF16 = jnp.bfloat16


def _dot(a, b):
    return jnp.dot(a, b, preferred_element_type=F32)


def _dot_nt(a, b):
    return lax.dot_general(a, b, (((1,), (1,)), ((), ())), preferred_element_type=F32)


def _const_spec(shape):
    nd = len(shape)
    return pl.BlockSpec(shape, lambda *_: (0,) * nd, pipeline_mode=pl.Buffered(1))


def _lam(lq_ref, lk_ref):
    e = jnp.exp(jnp.sum(lq_ref[...] * lk_ref[...], axis=1, keepdims=True))
    return e[0:1] - e[1:2] + LAM_INIT


def _proj_kernel(x_ref, meta_ref, g1_ref, w_ref, qg_ref, kg_ref, ones_ref,
                 qb_ref, k_ref, kb_ref, v_ref, vb_ref, u_ref, wb_s, *, n_meta):
    outs = (qb_ref, k_ref, kb_ref, v_ref, vb_ref, u_ref)
    consts = (g1_ref, wb_s, qg_ref, kg_ref, ones_ref)
    tm = qb_ref.shape[0]

    @pl.when(jnp.logical_and(pl.program_id(0) == 0, pl.program_id(1) == 0))
    def _():
        wb_s[...] = w_ref[...].astype(BF16)

    if not n_meta:
        _proj_body(x_ref[...], consts, outs)
        return
    t = pl.program_id(1)

    @pl.when(t == 0)
    def _():
        _proj_body(jnp.concatenate([meta_ref[...], x_ref[0, 0:tm - n_meta, :]], axis=0), consts, outs)

    @pl.when(t != 0)
    def _():
        _proj_body(x_ref[0], consts, outs)


def _proj_body(x, consts, outs):
    g1_ref, w_ref, qg_ref, kg_ref, ones_ref = consts
    qb_ref, k_ref, kb_ref, v_ref, vb_ref, u_ref = outs
    a = N_HEADS * HEAD_DIM
    r = lax.rsqrt(jnp.mean(x * x, axis=-1, keepdims=True) + EPS)
    xn = ((x * r) * g1_ref[...]).astype(BF16)
    proj = _dot(xn, w_ref[...])

    def comp_norm(z, g):
        z2 = (z * z).astype(BF16)
        ms = jnp.concatenate(
            [_dot(z2[:, j * MXU:(j + 1) * MXU], ones_ref[...]) for j in range(a // MXU)], axis=1)
        return (z * lax.rsqrt(ms + EPS)) * g

    qn = comp_norm(proj[:, :a], qg_ref[...])
    kn = comp_norm(proj[:, a:2 * a], kg_ref[...])
    v = proj[:, 2 * a:3 * a]
    qb_ref[...] = (qn * (QK_DIM ** -0.5 * LOG2E)).astype(BF16)
    kb_ref[...] = kn.astype(BF16)
    vb_ref[...] = v.astype(BF16)
    u_ref[...] = proj[:, 3 * a:]
    tm = x.shape[0]
    for h in range(N_HEADS):
        k_ref[pl.ds(h, tm, stride=N_HEADS), :] = kn[:, h * HEAD_DIM:(h + 1) * HEAD_DIM]
        v_ref[pl.ds(h, tm, stride=N_HEADS), :] = v[:, h * HEAD_DIM:(h + 1) * HEAD_DIM]


def _project(x, meta, tm, p, u_time_major):
    a = N_HEADS * HEAD_DIM
    d = x.shape[-1]
    if meta is None:
        nb, n_meta = 1, 0
        per_b = x.shape[0]
        x_spec = pl.BlockSpec((tm, d), lambda b, t: (t, 0))
        meta = jnp.zeros((8, d), x.dtype)
    else:
        nb, n_meta = x.shape[0], meta.shape[0]
        per_b = n_meta + x.shape[1]
        x_spec = pl.BlockSpec(
            (pl.Element(1), pl.Element(tm), pl.Element(d)),
            lambda b, t: (b, pl.multiple_of(jnp.maximum(t * tm - n_meta, 0), 8), 0))
    rows = nb * per_b
    nt = per_b // tm
    assert nt * tm == per_b and n_meta % 8 == 0
    grid = (nb, nt)
    row_spec = lambda w: pl.BlockSpec((tm, w), lambda b, t: (b * nt + t, 0))
    head_spec = pl.BlockSpec((tm * N_HEADS, HEAD_DIM), lambda b, t: (b * nt + t, 0))
    if u_time_major:
        u_shape = jax.ShapeDtypeStruct((per_b, nb * a), F32)
        u_spec = pl.BlockSpec((tm, a), lambda b, t: (t, b))
    else:
        u_shape = jax.ShapeDtypeStruct((rows, a), F32)
        u_spec = row_spec(a)
    out_shape = (jax.ShapeDtypeStruct((rows, a), BF16), jax.ShapeDtypeStruct((rows * N_HEADS, HEAD_DIM), F32),
                 jax.ShapeDtypeStruct((rows, a), BF16), jax.ShapeDtypeStruct((rows * N_HEADS, HEAD_DIM), F32),
                 jax.ShapeDtypeStruct((rows, a), BF16), u_shape)
    return pl.pallas_call(
        functools.partial(_proj_kernel, n_meta=n_meta),
        grid=grid,
        in_specs=[x_spec, _const_spec(meta.shape), _const_spec((1, d)), _const_spec(p["w_in"].shape),
                  _const_spec((1, a)), _const_spec((1, a)), _const_spec((MXU, MXU))],
        out_specs=(row_spec(a), head_spec, row_spec(a), head_spec, row_spec(a), u_spec),
        out_shape=out_shape,
        scratch_shapes=[pltpu.VMEM(p["w_in"].shape, BF16)],
        compiler_params=pltpu.CompilerParams(
            dimension_semantics=("arbitrary", "arbitrary"), vmem_limit_bytes=VMEM_LIMIT),
        name="proj",
    )(x, meta, p["norm1"], p["w_in"], p["q_gain"], p["k_gain"], p["ones_blk"])


def _attn_prompt_kernel(q_ref, k_ref, v_ref, lq_ref, lk_ref, sg_ref, o_ref, acc_s, m_s, s_s, *, tq, n_tiles):
    lam = _lam(lq_ref, lk_ref)
    sg = sg_ref[...]
    lo = lax.broadcasted_iota(jnp.int32, (1, HEAD_DIM), 1) < QK_DIM
    n_ones = acc_s.shape[1] - HEAD_DIM

    def hcols(h):
        return slice(h * HEAD_DIM, (h + 1) * HEAD_DIM)

    def stack_q(q):
        z = jnp.zeros_like(q)
        return jnp.concatenate([jnp.where(lo, q, z), jnp.where(lo, z, q)], axis=0)

    def init(h, t2):
        m_s[h, :, 0:t2] = jnp.full((1, t2), NEG, F32)
        acc_s[h, :, 0:t2] = jnp.zeros((HEAD_DIM + n_ones, t2), F32)

    def scores(slot, h, q2, kt, mask):
        s = _dot_nt(kt, q2)
        if mask is not None:
            s = jnp.where(mask, s, NEG)
        s_s[slot, h, 0:kt.shape[0], 0:q2.shape[0]] = s

    def absorb(slot, h, nk, t2, vt):
        s = s_s[slot, h, 0:nk, 0:t2]
        m_prev = m_s[h, :, 0:t2]
        m_new = jnp.maximum(m_prev, jnp.max(s, axis=0, keepdims=True))
        alpha = jnp.exp2(m_prev - m_new)
        pr = jnp.exp2(s - m_new).astype(BF16)
        va = jnp.concatenate([vt.T, jnp.ones((n_ones, nk), BF16)], axis=0)
        acc_s[h, :, 0:t2] = alpha * acc_s[h, :, 0:t2] + _dot(va, pr)
        m_s[h, :, 0:t2] = m_new

    def finish(h, t):
        acc = acc_s[h, :, 0:2 * t]
        on = acc[0:HEAD_DIM] / acc[HEAD_DIM:HEAD_DIM + 1]
        o = on[:, 0:t] - lam * on[:, t:2 * t]
        r = lax.rsqrt(jnp.mean(o * o, axis=0, keepdims=True) + EPS)
        return ((o * r).T * sg).astype(o_ref.dtype)

    def causal_mask(t, nk, offset):
        qi = lax.broadcasted_iota(jnp.int32, (nk, 2 * t), 1)
        qi = jnp.where(qi >= t, qi - t, qi)
        ki = lax.broadcasted_iota(jnp.int32, (nk, 2 * t), 0)
        return ki <= qi + offset

    tmq = LANES
    for h in range(N_HEADS):
        init(h, 2 * tmq)
        scores(0, h, stack_q(q_ref[0:tmq, hcols(h)]), k_ref[0:tmq, hcols(h)], causal_mask(tmq, tmq, 0))
    for h in range(N_HEADS):
        absorb(0, h, tmq, 2 * tmq, v_ref[0:tmq, hcols(h)])
        o_ref[0:N_META, hcols(h)] = finish(h, tmq)[0:N_META]

    dk = tq + N_META

    def q_tile(j, carry):
        qs = pl.multiple_of(N_META + j * tq, N_META)
        q2 = [stack_q(q_ref[pl.ds(qs, tq), hcols(h)]) for h in range(N_HEADS)]
        ds = pl.multiple_of(j * tq, tq)
        dmask = causal_mask(tq, dk, N_META)
        for h in range(N_HEADS):
            init(h, 2 * tq)

        def plain_scores(slot, i):
            ks = pl.multiple_of(i * tq, tq)
            for h in range(N_HEADS):
                scores(slot, h, q2[h], k_ref[pl.ds(ks, tq), hcols(h)], None)

        def diag_scores(slot):
            for h in range(N_HEADS):
                scores(slot, h, q2[h], k_ref[pl.ds(ds, dk), hcols(h)], dmask)

        def plain_absorb(slot, i):
            ks = pl.multiple_of(i * tq, tq)
            for h in range(N_HEADS):
                absorb(slot, h, tq, 2 * tq, v_ref[pl.ds(ks, tq), hcols(h)])

        def diag_absorb(slot):
            for h in range(N_HEADS):
                absorb(slot, h, dk, 2 * tq, v_ref[pl.ds(ds, dk), hcols(h)])
                o_ref[pl.ds(qs, tq), hcols(h)] = finish(h, tq)

        @pl.when(j == 0)
        def _():
            diag_scores(0)
            diag_absorb(0)

        @pl.when(j > 0)
        def _():
            plain_scores(0, 0)

        n_pairs = (j - 1) // 2

        def k_pair(pi, c):
            i = 2 * pi
            plain_scores(1, i + 1)
            plain_absorb(0, i)
            plain_scores(0, i + 2)
            plain_absorb(1, i + 1)
            return c

        lax.fori_loop(0, n_pairs, k_pair, 0)

        @pl.when(jnp.logical_and(j > 0, j % 2 == 1))
        def _():
            diag_scores(1)
            plain_absorb(0, j - 1)
            diag_absorb(1)

        @pl.when(jnp.logical_and(j > 0, j % 2 == 0))
        def _():
            plain_scores(1, j - 1)
            plain_absorb(0, j - 2)
            diag_scores(0)
            plain_absorb(1, j - 1)
            diag_absorb(0)

        return carry

    lax.fori_loop(0, n_tiles, q_tile, 0)


def _attn_prompt(qb, kb, vb, p, tq=256):
    nb, length, a = qb.shape
    n_tiles = (length - N_META) // tq
    assert N_META + n_tiles * tq == length and tq % LANES == 0
    spec = pl.BlockSpec((None, length, a), lambda b: (b, 0, 0))
    return pl.pallas_call(
        functools.partial(_attn_prompt_kernel, tq=tq, n_tiles=n_tiles),
        grid=(nb,),
        in_specs=[spec, spec, spec, _const_spec((2, QK_DIM)), _const_spec((2, QK_DIM)),
                  _const_spec((1, HEAD_DIM))],
        out_specs=spec,
        out_shape=jax.ShapeDtypeStruct((nb, length, a), BF16),
        scratch_shapes=[pltpu.VMEM((N_HEADS, HEAD_DIM + 16, 2 * tq), F32), pltpu.VMEM((N_HEADS, 1, 2 * tq), F32),
                        pltpu.VMEM((2, N_HEADS, tq + N_META, 2 * tq), F32)],
        compiler_params=pltpu.CompilerParams(
            dimension_semantics=("arbitrary",), vmem_limit_bytes=VMEM_LIMIT),
        name="attn_prompt",
    )(qb, kb, vb, p["lam_q"], p["lam_k"], p["sub_gain"])


def _attn_decode_kernel(pt_ref, q_ref, kn_ref, vn_ref, lq_ref, lk_ref, sg_ref, *refs, n_pages, t_new, bps):
    del pt_ref
    n_kv = bps * n_pages
    k_refs, v_refs, o_ref = refs[:n_kv], refs[n_kv:2 * n_kv], refs[2 * n_kv]
    lam = _lam(lq_ref, lk_ref)
    sg = sg_ref[...]
    grp = 2 * t_new
    nrow = N_HEADS * grp
    ncol = k_refs[0].shape[0]
    row = lax.broadcasted_iota(jnp.int32, (nrow, HEAD_DIM), 0)
    lane = lax.broadcasted_iota(jnp.int32, (nrow, HEAD_DIM), 1)
    comp_ok = (lane >= QK_DIM) == ((row % grp) >= t_new)
    t_of_row = lax.broadcasted_iota(jnp.int32, (nrow, 1), 0) % t_new
    head_ok = (lax.broadcasted_iota(jnp.int32, (nrow, ncol), 1) % N_HEADS
               == lax.broadcasted_iota(jnp.int32, (nrow, ncol), 0) // grp)

    def per_row_head(z, t):
        return jnp.concatenate(
            [jnp.broadcast_to(z[t:t + 1, h * HEAD_DIM:(h + 1) * HEAD_DIM], (grp, HEAD_DIM)) for h in range(N_HEADS)],
            axis=0)

    for e in range(bps):
        kp, vp = k_refs[e * n_pages:(e + 1) * n_pages], v_refs[e * n_pages:(e + 1) * n_pages]
        q32 = jnp.where(comp_ok, q_ref[e].astype(F32), 0.0)
        qb = q32.astype(BF16)
        kn, vn = kn_ref[e].astype(F32), vn_ref[e].astype(F32)
        s_pages = [jnp.where(head_ok, _dot_nt(qb, kp[pg][...].astype(BF16)), NEG) for pg in range(n_pages)]
        s_new = []
        for t in range(t_new):
            sc = jnp.sum(q32 * per_row_head(kn, t), axis=1, keepdims=True)
            s_new.append(jnp.where(t_of_row >= t, sc, NEG))

        m = s_pages[0]
        for s in s_pages[1:]:
            m = jnp.maximum(m, s)
        m = jnp.max(m, axis=1, keepdims=True)
        for s in s_new:
            m = jnp.maximum(m, s)

        acc = jnp.zeros((nrow, HEAD_DIM), F32)
        psum = None
        for pg in range(n_pages):
            pr = jnp.exp2(s_pages[pg] - m)
            psum = pr if psum is None else psum + pr
            acc = acc + _dot(pr.astype(BF16), vp[pg][...].astype(BF16))
        lsum = jnp.sum(psum, axis=1, keepdims=True)
        for t in range(t_new):
            pr = jnp.exp2(s_new[t] - m)
            lsum = lsum + pr
            acc = acc + pr * per_row_head(vn, t)
        on = acc / lsum
        for h in range(N_HEADS):
            r0 = h * grp
            o = on[r0:r0 + t_new] - lam * on[r0 + t_new:r0 + grp]
            r = lax.rsqrt(jnp.mean(o * o, axis=-1, keepdims=True) + EPS)
            o_ref[e, :, h * HEAD_DIM:(h + 1) * HEAD_DIM] = ((o * r) * sg).astype(o_ref.dtype)


def _attn_decode(q_rep, kn, vn, cache_k, cache_v, page_table, p, bps=2):
    db, nrow, _ = q_rep.shape
    t_new, a = kn.shape[1:]
    n_pages = page_table.shape[1]
    prow = cache_k.shape[1]
    pt = page_table.reshape(-1)
    assert db % bps == 0

    def page_spec(e, pg):
        return pl.BlockSpec((None, prow, HEAD_DIM),
                            lambda b, pt_ref: (pt_ref[(b * bps + e) * n_pages + pg], 0, 0))

    page_specs = lambda: [page_spec(e, pg) for e in range(bps) for pg in range(n_pages)]
    const = lambda shape: pl.BlockSpec(shape, lambda b, pt_ref: (0,) * len(shape))
    per_step = lambda r, w: pl.BlockSpec((bps, r, w), lambda b, pt_ref: (b, 0, 0))
    grid_spec = pltpu.PrefetchScalarGridSpec(
        num_scalar_prefetch=1,
        grid=(db // bps,),
        in_specs=[per_step(nrow, HEAD_DIM), per_step(t_new, a), per_step(t_new, a),
                  const((2, QK_DIM)), const((2, QK_DIM)), const((1, HEAD_DIM))] + page_specs() + page_specs(),
        out_specs=per_step(t_new, a),
    )
    return pl.pallas_call(
        functools.partial(_attn_decode_kernel, n_pages=n_pages, t_new=t_new, bps=bps),
        grid_spec=grid_spec,
        out_shape=jax.ShapeDtypeStruct((db, t_new, a), BF16),
        compiler_params=pltpu.CompilerParams(
            dimension_semantics=("arbitrary",), vmem_limit_bytes=VMEM_LIMIT),
        name="attn_decode",
    )(pt, q_rep, kn, vn, p["lam_q"], p["lam_k"], p["sub_gain"],
      *([cache_k] * (bps * n_pages)), *([cache_v] * (bps * n_pages)))


def _ssm_param_kernel(ar_ref, ai_ref, ldt_ref, br_ref, bi_ref, abr_ref, abi_ref, bbr_ref, bbi_ref):
    ar, ai = ar_ref[...], ai_ref[...]
    dt = jnp.exp(ldt_ref[...])
    mag = jnp.exp(ar * dt)
    abr, abi = mag * jnp.cos(ai * dt), mag * jnp.sin(ai * dt)
    den = ar * ar + ai * ai
    nr, ni = abr - 1.0, abi
    gr, gi = (nr * ar + ni * ai) / den, (ni * ar - nr * ai) / den
    abr_ref[...] = abr
    abi_ref[...] = abi
    for c in range(SSM_GROUP):
        br, bi = br_ref[c], bi_ref[c]
        bbr_ref[c] = gr * br - gi * bi
        bbi_ref[c] = gr * bi + gi * br


def _ssm_params(a_re, a_im, log_dt, b_re, b_im):
    g, pdim = a_re.shape
    c = b_re.shape[-1]
    b_re_t = jnp.transpose(b_re, (2, 0, 1))
    b_im_t = jnp.transpose(b_im, (2, 0, 1))
    gp = jax.ShapeDtypeStruct((g, pdim), F32)
    cgp = jax.ShapeDtypeStruct((c, g, pdim), F32)
    return pl.pallas_call(_ssm_param_kernel, out_shape=(gp, gp, cgp, cgp), name="ssm_params")(
        a_re, a_im, log_dt.reshape(g, 1), b_re_t, b_im_t)


def _ssm_kernel(u_ref, up_ref, h0r_ref, h0i_ref, abr_ref, abi_ref, bre_ref, bim_ref, crt_ref, cit_ref,
                d_ref, wg_ref, bg_ref, y_ref, hr_ref, hi_ref, xr_s, xi_s, *, tc, nb, n_chunks):
    i = pl.program_id(0)

    @pl.when(i == 0)
    def _():
        hr_ref[...] = h0r_ref[...]
        hi_ref[...] = h0i_ref[...]
        xr_s[1] = jnp.zeros(xr_s.shape[1:], F32)
        xi_s[1] = jnp.zeros(xi_s.shape[1:], F32)

    def body(cur):
        prev = 1 - cur
        rows, width = tc * nb, d_ref.shape[1]
        ub = u_ref[...].reshape(rows, width).astype(BF16)
        kin, nout = bre_ref.shape[1], bre_ref.shape[2]
        for mblk in range(bre_ref.shape[0]):
            um = ub[:, mblk * kin:(mblk + 1) * kin]
            xr_s[cur, :, mblk * nout:(mblk + 1) * nout] = _dot(um, bre_ref[mblk])
            xi_s[cur, :, mblk * nout:(mblk + 1) * nout] = _dot(um, bim_ref[mblk])

        hr, hi = hr_ref[...], hi_ref[...]
        for t in range(tc):
            r = slice(t * nb, (t + 1) * nb)
            abr, abi = abr_ref[...], abi_ref[...]
            nhr = abr * hr - abi * hi + xr_s[cur, r, :]
            nhi = abr * hi + abi * hr + xi_s[cur, r, :]
            xr_s[cur, r, :] = nhr
            xi_s[cur, r, :] = nhi
            hr, hi = nhr, nhi
        live = i < n_chunks
        hr_ref[...] = jnp.where(live, hr, hr_ref[...])
        hi_ref[...] = jnp.where(live, hi, hi_ref[...])

        kout = crt_ref.shape[1]
        ys = []
        for j in range(crt_ref.shape[0]):
            hrj = xr_s[prev, :, j * kout:(j + 1) * kout].astype(BF16)
            hij = xi_s[prev, :, j * kout:(j + 1) * kout].astype(BF16)
            ys.append(_dot(hrj, crt_ref[j]) - _dot(hij, cit_ref[j]))
        y = jnp.concatenate(ys, axis=1) + d_ref[...] * up_ref[...].reshape(rows, width)
        g = jax.nn.gelu(y)
        out = g * jax.nn.sigmoid(_dot(g.astype(BF16), wg_ref[...]) + bg_ref[...])
        y_ref[...] = out.reshape(tc, nb, width)

    for par in (0, 1):
        pl.when(i % 2 == par)(functools.partial(body, par))


def _ssm(u3, h0r, h0i, p, tc):
    t_len, nb, width = u3.shape
    n_state = h0r.shape[1]
    n_chunks = t_len // tc
    assert n_chunks * tc == t_len
    st = jax.ShapeDtypeStruct((nb, n_state), F32)
    st_spec = pl.BlockSpec((nb, n_state), lambda i: (0, 0))
    chunk = lambda index: pl.BlockSpec((tc, nb, width), lambda i: (index(i), 0, 0))
    return pl.pallas_call(
        functools.partial(_ssm_kernel, tc=tc, nb=nb, n_chunks=n_chunks),
        grid=(n_chunks + 1,),
        in_specs=[chunk(lambda i: jnp.minimum(i, n_chunks - 1)), chunk(lambda i: jnp.maximum(i - 1, 0)),
                  _const_spec((nb, n_state)), _const_spec((nb, n_state)),
                  _const_spec((nb, n_state)), _const_spec((nb, n_state)),
                  _const_spec(p["bre"].shape), _const_spec(p["bim"].shape),
                  _const_spec(p["crt"].shape), _const_spec(p["cit"].shape),
                  _const_spec((1, width)), _const_spec((width, width)), _const_spec((1, width))],
        out_specs=(chunk(lambda i: jnp.maximum(i - 1, 0)), st_spec, st_spec),
        out_shape=(jax.ShapeDtypeStruct((t_len, nb, width), F32), st, st),
        scratch_shapes=[pltpu.VMEM((2, tc * nb, n_state), F32), pltpu.VMEM((2, tc * nb, n_state), F32)],
        compiler_params=pltpu.CompilerParams(
            dimension_semantics=("arbitrary",), vmem_limit_bytes=VMEM_LIMIT),
        name="ssm",
    )(u3, u3, h0r, h0i, jnp.broadcast_to(p["abr"], (nb, n_state)), jnp.broadcast_to(p["abi"], (nb, n_state)),
      p["bre"], p["bim"], p["crt"], p["cit"],
      p["d_skip"], p["w_glu"], p["b_glu"])


def _post_kernel(x_ref, att_ref, ssm_ref, hist_ref, wo_ref, g2_ref, wgate_ref, wup_ref, cw_ref, cb_ref,
                 wd_ref, y_ref, cst_ref, xn_s, h_s, ap_s, hist_s, *, shift, fc):
    t = pl.program_id(1)
    tm, a = y_ref.shape[0], wo_ref.shape[0] // 2
    d_ff = wgate_ref.shape[1]
    off = ap_s.shape[1] - tm

    @pl.when(t == 0)
    def _():
        hist_s[...] = hist_ref[...]

    mix = jnp.concatenate([att_ref[...].reshape(tm, a), ssm_ref[...].reshape(tm, a).astype(BF16)], axis=1)
    xm = x_ref[...] + _dot(mix, wo_ref[...])
    y_ref[...] = xm
    r = lax.rsqrt(jnp.mean(xm * xm, axis=-1, keepdims=True) + EPS)
    xn_s[...] = ((xm * r) * g2_ref[...]).astype(BF16)
    for j in range(d_ff // fc):
        cs = slice(j * fc, (j + 1) * fc)
        ap = ap_s.at[j % 2]
        gate = _dot(xn_s[...], wgate_ref[:, cs])
        up = _dot(xn_s[...], wup_ref[:, cs])
        ap[off - 2 * shift:off, :] = hist_s[:, cs]
        ap[off:off + tm, :] = gate
        conv = (cb_ref[:, cs] + cw_ref[0:1, cs] * ap[off - 2 * shift:off - 2 * shift + tm, :]
                + cw_ref[1:2, cs] * ap[off - shift:off - shift + tm, :] + cw_ref[2:3, cs] * gate)
        hist_s[:, cs] = ap[off + tm - 2 * shift:off + tm, :]
        h_s[:, cs] = (jax.nn.gelu(conv) * up).astype(BF16)
    y_ref[...] += _dot(h_s[...], wd_ref[...])
    cst_ref[...] = hist_s[...]


def _post(x2d, att, ssm, hist, p, tm, nb, shift, att_spec, ssm_spec, hist_shared=False, fc=256):
    rows, d = x2d.shape
    d_ff = p["w_gate"].shape[1]
    per_b = rows // nb
    nt = per_b // tm
    assert nt * tm == per_b and d_ff % fc == 0 and tm >= 2 * shift
    off = -(-2 * shift // 8) * 8
    row_spec = lambda w: pl.BlockSpec((tm, w), lambda b, t: (b * nt + t, 0))
    hist_in = pl.BlockSpec((None, 2 * shift, d_ff), (lambda b, t: (0, 0, 0)) if hist_shared else (lambda b, t: (b, 0, 0)))
    hist_out = pl.BlockSpec((None, 2 * shift, d_ff), lambda b, t: (b, 0, 0))
    return pl.pallas_call(
        functools.partial(_post_kernel, shift=shift, fc=fc),
        grid=(nb, nt),
        in_specs=[row_spec(d), att_spec, ssm_spec, hist_in,
                  _const_spec(p["w_out"].shape), _const_spec((1, d)),
                  _const_spec(p["w_gate"].shape), _const_spec(p["w_up"].shape),
                  _const_spec((3, d_ff)), _const_spec((1, d_ff)), _const_spec(p["w_down"].shape)],
        out_specs=(row_spec(d), hist_out),
        out_shape=(jax.ShapeDtypeStruct((rows, d), F32), jax.ShapeDtypeStruct((nb, 2 * shift, d_ff), F32)),
        scratch_shapes=[pltpu.VMEM((tm, d), BF16), pltpu.VMEM((tm, d_ff), BF16),
                        pltpu.VMEM((2, off + tm, fc), F32), pltpu.VMEM((2 * shift, d_ff), F32)],
        compiler_params=pltpu.CompilerParams(
            dimension_semantics=("arbitrary", "arbitrary"), vmem_limit_bytes=VMEM_LIMIT),
        name="post",
    )(x2d, att, ssm, hist, p["w_out"], p["norm2"], p["w_gate"], p["w_up"], p["conv_w"], p["conv_b"],
      p["w_down"])


def _prepare_params(norm1, w_in, q_norm, k_norm, lam_q, lam_k, sub_norm, ssm_a_re, ssm_a_im, ssm_log_dt,
                    ssm_b_re, ssm_b_im, ssm_c_re, ssm_c_im, ssm_d, w_glu, b_glu, w_out, norm2, w_gate,
                    w_up, ffn_conv_w, ffn_conv_b, w_down):
    l = 0
    g, pdim = ssm_a_re[l].shape
    c = SSM_GROUP
    a = N_HEADS * HEAD_DIM
    abr, abi, bbr, bbi = _ssm_params(ssm_a_re[l], ssm_a_im[l], ssm_log_dt[l], ssm_b_re[l], ssm_b_im[l])
    gi = LANES // c

    def in_blocks(bb):
        bb = jnp.transpose(bb, (1, 0, 2)).reshape(g // gi, gi, c, 1, pdim)
        same = jnp.eye(gi, dtype=bool)[None, :, None, :, None]
        return jnp.where(same, bb, 0.0).astype(BF16).reshape(g // gi, gi * c, gi * pdim)

    go = MXU // c

    def out_blocks(cc):
        cc = jnp.transpose(cc.reshape(g // go, go, c, pdim), (0, 1, 3, 2))[:, :, :, None, :]
        same = jnp.eye(go, dtype=bool)[None, :, None, :, None]
        return jnp.where(same, cc, 0.0).astype(BF16).reshape(g // go, go * pdim, go * c)

    comp = jnp.arange(MXU) // QK_DIM
    ones_blk = (comp[:, None] == comp[None, :]).astype(BF16) * (1.0 / QK_DIM)
    return {
        "norm1": norm1[l].reshape(1, -1), "w_in": w_in[l],
        "q_gain": jnp.tile(q_norm[l].reshape(-1), N_HEADS).reshape(1, a),
        "k_gain": jnp.tile(k_norm[l].reshape(-1), N_HEADS).reshape(1, a),
        "ones_blk": ones_blk.astype(BF16),
        "lam_q": lam_q[l], "lam_k": lam_k[l],
        "sub_gain": (sub_norm[l] * (1.0 - LAM_INIT)).reshape(1, HEAD_DIM),
        "abr": abr.reshape(1, g * pdim), "abi": abi.reshape(1, g * pdim),
        "bre": in_blocks(bbr), "bim": in_blocks(bbi),
        "crt": out_blocks(ssm_c_re[l]), "cit": out_blocks(ssm_c_im[l]),
        "d_skip": ssm_d[l].reshape(1, g * c), "w_glu": w_glu[l].astype(BF16), "b_glu": b_glu[l].reshape(1, -1),
        "w_out": w_out[l].astype(BF16), "norm2": norm2[l].reshape(1, -1),
        "w_gate": w_gate[l].astype(BF16), "w_up": w_up[l].astype(BF16),
        "conv_w": ffn_conv_w[l], "conv_b": ffn_conv_b[l].reshape(1, -1), "w_down": w_down[l].astype(BF16),
    }


def _row_tile(length, cap=768):
    best = None
    for t in range(16, cap + 1, 16):
        if length % t == 0:
            best = t
    assert best is not None
    return best


def kernel(x_prompt, x_sample, cache_k, cache_v, state_ssm_re, state_ssm_im, state_ffn_conv, page_table, meta_tokens, norm1, w_in, q_norm, k_norm, lam_q, lam_k, sub_norm, ssm_a_re, ssm_a_im, ssm_log_dt, ssm_b_re, ssm_b_im, ssm_c_re, ssm_c_im, ssm_d, w_glu, b_glu, w_out, norm2, w_gate, w_up, ffn_conv_w, ffn_conv_b, w_down):
    assert norm1.shape[0] == 1, "single-layer stack"
    p = _prepare_params(norm1, w_in, q_norm, k_norm, lam_q, lam_k, sub_norm, ssm_a_re, ssm_a_im, ssm_log_dt,
                        ssm_b_re, ssm_b_im, ssm_c_re, ssm_c_im, ssm_d, w_glu, b_glu, w_out, norm2, w_gate,
                        w_up, ffn_conv_w, ffn_conv_b, w_down)
    nb, seq, d = x_prompt.shape
    db, t_new, _ = x_sample.shape
    a = N_HEADS * HEAD_DIM
    g, pdim = ssm_a_re.shape[1:]
    n_state = g * pdim
    d_ff = w_gate.shape[-1]
    length = seq + N_META

    meta = meta_tokens.astype(x_prompt.dtype)
    tm = _row_tile(length)
    qb, k, kb, v, vb, u = _project(x_prompt, meta, tm, p, u_time_major=True)
    att = _attn_prompt(qb.reshape(nb, length, a), kb.reshape(nb, length, a), vb.reshape(nb, length, a), p)
    zst = jnp.zeros((nb, n_state), F32)
    tc = _row_tile(length * nb, cap=768) // nb
    ys, hr, hi = _ssm(u.reshape(length, nb, a), zst, zst, p, tc)
    ys2 = ys.reshape(length, nb * a)
    plain = lambda w: pl.BlockSpec((N_META, w), lambda b, t: (0, 0))
    _, hist_meta = _post(meta, att[0, :N_META], ys2[:N_META, :a], jnp.zeros((1, 2, d_ff), F32), p, N_META, 1,
                         shift=1, att_spec=plain(a), ssm_spec=plain(a))
    tp = _row_tile(seq, cap=512)
    att_spec = pl.BlockSpec((pl.Element(1), pl.Element(tp), pl.Element(a)),
                            lambda b, t: (b, pl.multiple_of(N_META + t * tp, N_META), 0))
    ssm_spec = pl.BlockSpec((pl.Element(tp), pl.Element(a)),
                            lambda b, t: (pl.multiple_of(N_META + t * tp, N_META), pl.multiple_of(b * a, a)))
    yp, cst_p = _post(x_prompt.reshape(nb * seq, d), att, ys2, hist_meta, p, tp, nb, shift=1,
                      att_spec=att_spec, ssm_spec=ssm_spec, hist_shared=True)
    y_prompt = yp.reshape(nb, seq, d)
    k_prompt = k.reshape(1, nb, length, N_HEADS, HEAD_DIM)
    v_prompt = v.reshape(1, nb, length, N_HEADS, HEAD_DIM)
    ssm_re_p = hr.reshape(1, nb, g, pdim)
    ssm_im_p = hi.reshape(1, nb, g, pdim)
    conv_p = cst_p[None]

    rows_s = db * t_new
    xs2 = jnp.transpose(x_sample, (1, 0, 2)).reshape(rows_s, d)
    qb, k, kb, v, vb, u = _project(xs2, None, rows_s, p, u_time_major=False)
    to_bm = lambda z: jnp.transpose(z.reshape(t_new, db, a), (1, 0, 2))
    q_hct = jnp.transpose(qb.reshape(t_new, db, N_HEADS, 1, HEAD_DIM), (1, 2, 3, 0, 4))
    q_rep = jnp.broadcast_to(q_hct, (db, N_HEADS, 2, t_new, HEAD_DIM)).reshape(db, N_HEADS * 2 * t_new, HEAD_DIM)
    n_pool, page = cache_k.shape[1:3]
    att = _attn_decode(q_rep, to_bm(kb), to_bm(vb), cache_k[0].reshape(n_pool, page * N_HEADS, HEAD_DIM),
                       cache_v[0].reshape(n_pool, page * N_HEADS, HEAD_DIM), page_table, p)
    att_tm = jnp.transpose(att, (1, 0, 2)).reshape(rows_s, a)
    ys, hr, hi = _ssm(u.reshape(t_new, db, a), state_ssm_re[0].reshape(db, n_state),
                      state_ssm_im[0].reshape(db, n_state), p, t_new)
    hist = jnp.transpose(state_ffn_conv[0], (1, 0, 2)).reshape(1, 2 * db, d_ff)
    whole = pl.BlockSpec((rows_s, a), lambda b, t: (0, 0))
    ysm, cst_s = _post(xs2, att_tm, ys.reshape(rows_s, a), hist, p, rows_s, 1, shift=db,
                       att_spec=whole, ssm_spec=whole)
    y_sample = jnp.transpose(ysm.reshape(t_new, db, d), (1, 0, 2))
    k_sample = jnp.transpose(k.reshape(t_new, db, N_HEADS, HEAD_DIM), (1, 0, 2, 3))[None]
    v_sample = jnp.transpose(v.reshape(t_new, db, N_HEADS, HEAD_DIM), (1, 0, 2, 3))[None]
    ssm_re_s = hr.reshape(1, db, g, pdim)
    ssm_im_s = hi.reshape(1, db, g, pdim)
    conv_s = jnp.transpose(cst_s.reshape(2, db, d_ff), (1, 0, 2))[None]

    return (y_prompt, y_sample, k_prompt, v_prompt, k_sample, v_sample,
            ssm_re_p, ssm_im_p, ssm_re_s, ssm_im_s, conv_p, conv_s)
```

```python
import functools
import math

import jax
import jax.numpy as jnp
from jax import lax
from jax.experimental import pallas as pl
from jax.experimental.pallas import tpu as pltpu

N_META = 16
N_HEADS = 4
QK_DIM = 64
HEAD_DIM = 2 * QK_DIM
SSM_GROUP = 16
SSM_STATE = 64
EPS = 1e-6
NEG = -1e30
LAM_INIT = 0.8 - 0.6 * math.exp(-0.3 * 0)
LOG2E = math.log2(math.e)

LANES = 128
MXU = 256
VMEM_LIMIT = 56 * 1024 * 1024

F32 = jnp.float32
BF16 = jnp.bfloat16


def _dot(a, b):
    return jnp.dot(a, b, preferred_element_type=F32)


def _dot_nt(a, b):
    return lax.dot_general(a, b, (((1,), (1,)), ((), ())), preferred_element_type=F32)


def _const_spec(shape):
    nd = len(shape)
    return pl.BlockSpec(shape, lambda *_: (0,) * nd, pipeline_mode=pl.Buffered(1))


def _lam(lq_ref, lk_ref):
    e = jnp.exp(jnp.sum(lq_ref[...] * lk_ref[...], axis=1, keepdims=True))
    return e[0:1] - e[1:2] + LAM_INIT


def _proj_kernel(x_ref, meta_ref, g1_ref, w_ref, qg_ref, kg_ref, ones_ref,
                 qb_ref, k_ref, kb_ref, v_ref, vb_ref, u_ref, wb_s, *, n_meta):
    outs = (qb_ref, k_ref, kb_ref, v_ref, vb_ref, u_ref)
    consts = (g1_ref, wb_s, qg_ref, kg_ref, ones_ref)
    tm = qb_ref.shape[0]

    @pl.when(jnp.logical_and(pl.program_id(0) == 0, pl.program_id(1) == 0))
    def _():
        wb_s[...] = w_ref[...].astype(BF16)

    if not n_meta:
        _proj_body(x_ref[...], consts, outs)
        return
    t = pl.program_id(1)

    @pl.when(t == 0)
    def _():
        _proj_body(jnp.concatenate([meta_ref[...], x_ref[0, 0:tm - n_meta, :]], axis=0), consts, outs)

    @pl.when(t != 0)
    def _():
        _proj_body(x_ref[0], consts, outs)


def _proj_body(x, consts, outs):
    g1_ref, w_ref, qg_ref, kg_ref, ones_ref = consts
    qb_ref, k_ref, kb_ref, v_ref, vb_ref, u_ref = outs
    a = N_HEADS * HEAD_DIM
    r = lax.rsqrt(jnp.mean(x * x, axis=-1, keepdims=True) + EPS)
    xn = ((x * r) * g1_ref[...]).astype(BF16)
    proj = _dot(xn, w_ref[...])

    def comp_norm(z, g):
        z2 = (z * z).astype(BF16)
        ms = jnp.concatenate(
            [_dot(z2[:, j * MXU:(j + 1) * MXU], ones_ref[...]) for j in range(a // MXU)], axis=1)
        return (z * lax.rsqrt(ms + EPS)) * g

    qn = comp_norm(proj[:, :a], qg_ref[...])
    kn = comp_norm(proj[:, a:2 * a], kg_ref[...])
    v = proj[:, 2 * a:3 * a]
    qb_ref[...] = (qn * (QK_DIM ** -0.5 * LOG2E)).astype(BF16)
    kb_ref[...] = kn.astype(BF16)
    vb_ref[...] = v.astype(BF16)
    u_ref[...] = proj[:, 3 * a:]
    tm = x.shape[0]
    for h in range(N_HEADS):
        k_ref[pl.ds(h, tm, stride=N_HEADS), :] = kn[:, h * HEAD_DIM:(h + 1) * HEAD_DIM]
        v_ref[pl.ds(h, tm, stride=N_HEADS), :] = v[:, h * HEAD_DIM:(h + 1) * HEAD_DIM]


def _project(x, meta, tm, p, u_time_major):
    a = N_HEADS * HEAD_DIM
    d = x.shape[-1]
    if meta is None:
        nb, n_meta = 1, 0
        per_b = x.shape[0]
        x_spec = pl.BlockSpec((tm, d), lambda b, t: (t, 0))
        meta = jnp.zeros((8, d), x.dtype)
    else:
        nb, n_meta = x.shape[0], meta.shape[0]
        per_b = n_meta + x.shape[1]
        x_spec = pl.BlockSpec(
            (pl.Element(1), pl.Element(tm), pl.Element(d)),
            lambda b, t: (b, pl.multiple_of(jnp.maximum(t * tm - n_meta, 0), 8), 0))
    rows = nb * per_b
    nt = per_b // tm
    assert nt * tm == per_b and n_meta % 8 == 0
    grid = (nb, nt)
    row_spec = lambda w: pl.BlockSpec((tm, w), lambda b, t: (b * nt + t, 0))
    head_spec = pl.BlockSpec((tm * N_HEADS, HEAD_DIM), lambda b, t: (b * nt + t, 0))
    if u_time_major:
        u_shape = jax.ShapeDtypeStruct((per_b, nb * a), F32)
        u_spec = pl.BlockSpec((tm, a), lambda b, t: (t, b))
    else:
        u_shape = jax.ShapeDtypeStruct((rows, a), F32)
        u_spec = row_spec(a)
    out_shape = (jax.ShapeDtypeStruct((rows, a), BF16), jax.ShapeDtypeStruct((rows * N_HEADS, HEAD_DIM), F32),
                 jax.ShapeDtypeStruct((rows, a), BF16), jax.ShapeDtypeStruct((rows * N_HEADS, HEAD_DIM), F32),
                 jax.ShapeDtypeStruct((rows, a), BF16), u_shape)
    return pl.pallas_call(
        functools.partial(_proj_kernel, n_meta=n_meta),
        grid=grid,
        in_specs=[x_spec, _const_spec(meta.shape), _const_spec((1, d)), _const_spec(p["w_in"].shape),
                  _const_spec((1, a)), _const_spec((1, a)), _const_spec((MXU, MXU))],
        out_specs=(row_spec(a), head_spec, row_spec(a), head_spec, row_spec(a), u_spec),
        out_shape=out_shape,
        scratch_shapes=[pltpu.VMEM(p["w_in"].shape, BF16)],
        compiler_params=pltpu.CompilerParams(
            dimension_semantics=("arbitrary", "arbitrary"), vmem_limit_bytes=VMEM_LIMIT),
        name="proj",
    )(x, meta, p["norm1"], p["w_in"], p["q_gain"], p["k_gain"], p["ones_blk"])


def _attn_prompt_kernel(q_ref, k_ref, v_ref, lq_ref, lk_ref, sg_ref, o_ref, acc_s, m_s, s_s, *, tq, n_tiles):
    lam = _lam(lq_ref, lk_ref)
    sg = sg_ref[...]
    lo = lax.broadcasted_iota(jnp.int32, (1, HEAD_DIM), 1) < QK_DIM
    n_ones = acc_s.shape[1] - HEAD_DIM

    def hcols(h):
        return slice(h * HEAD_DIM, (h + 1) * HEAD_DIM)

    def stack_q(q):
        z = jnp.zeros_like(q)
        return jnp.concatenate([jnp.where(lo, q, z), jnp.where(lo, z, q)], axis=0)

    def init(h, t2):
        m_s[h, :, 0:t2] = jnp.full((1, t2), NEG, F32)
        acc_s[h, :, 0:t2] = jnp.zeros((HEAD_DIM + n_ones, t2), F32)

    def scores(slot, h, q2, kt, mask):
        s = _dot_nt(kt, q2)
        if mask is not None:
            s = jnp.where(mask, s, NEG)
        s_s[slot, h, 0:kt.shape[0], 0:q2.shape[0]] = s

    def absorb(slot, h, nk, t2, vt):
        s = s_s[slot, h, 0:nk, 0:t2]
        m_prev = m_s[h, :, 0:t2]
        m_new = jnp.maximum(m_prev, jnp.max(s, axis=0, keepdims=True))
        alpha = jnp.exp2(m_prev - m_new)
        pr = jnp.exp2(s - m_new).astype(BF16)
        va = jnp.concatenate([vt.T, jnp.ones((n_ones, nk), BF16)], axis=0)
        acc_s[h, :, 0:t2] = alpha * acc_s[h, :, 0:t2] + _dot(va, pr)
        m_s[h, :, 0:t2] = m_new

    def finish(h, t):
        acc = acc_s[h, :, 0:2 * t]
        on = acc[0:HEAD_DIM] / acc[HEAD_DIM:HEAD_DIM + 1]
        o = on[:, 0:t] - lam * on[:, t:2 * t]
        r = lax.rsqrt(jnp.mean(o * o, axis=0, keepdims=True) + EPS)
        return ((o * r).T * sg).astype(o_ref.dtype)

    def causal_mask(t, nk, offset):
        qi = lax.broadcasted_iota(jnp.int32, (nk, 2 * t), 1)
        qi = jnp.where(qi >= t, qi - t, qi)
        ki = lax.broadcasted_iota(jnp.int32, (nk, 2 * t), 0)
        return ki <= qi + offset

    tmq = LANES
    for h in range(N_HEADS):
        init(h, 2 * tmq)
        scores(0, h, stack_q(q_ref[0:tmq, hcols(h)]), k_ref[0:tmq, hcols(h)], causal_mask(tmq, tmq, 0))
    for h in range(N_HEADS):
        absorb(0, h, tmq, 2 * tmq, v_ref[0:tmq, hcols(h)])
        o_ref[0:N_META, hcols(h)] = finish(h, tmq)[0:N_META]

    dk = tq + N_META

    def q_tile(j, carry):
        qs = pl.multiple_of(N_META + j * tq, N_META)
        q2 = [stack_q(q_ref[pl.ds(qs, tq), hcols(h)]) for h in range(N_HEADS)]
        ds = pl.multiple_of(j * tq, tq)
        dmask = causal_mask(tq, dk, N_META)
        for h in range(N_HEADS):
            init(h, 2 * tq)

        def plain_scores(slot, i):
            ks = pl.multiple_of(i * tq, tq)
            for h in range(N_HEADS):
                scores(slot, h, q2[h], k_ref[pl.ds(ks, tq), hcols(h)], None)

        def diag_scores(slot):
            for h in range(N_HEADS):
                scores(slot, h, q2[h], k_ref[pl.ds(ds, dk), hcols(h)], dmask)

        def plain_absorb(slot, i):
            ks = pl.multiple_of(i * tq, tq)
            for h in range(N_HEADS):
                absorb(slot, h, tq, 2 * tq, v_ref[pl.ds(ks, tq), hcols(h)])

        def diag_absorb(slot):
            for h in range(N_HEADS):
                absorb(slot, h, dk, 2 * tq, v_ref[pl.ds(ds, dk), hcols(h)])
                o_ref[pl.ds(qs, tq), hcols(h)] = finish(h, tq)

        @pl.when(j == 0)
        def _():
            diag_scores(0)
            diag_absorb(0)

        @pl.when(j > 0)
        def _():
            plain_scores(0, 0)

        n_pairs = (j - 1) // 2

        def k_pair(pi, c):
            i = 2 * pi
            plain_scores(1, i + 1)
            plain_absorb(0, i)
            plain_scores(0, i + 2)
            plain_absorb(1, i + 1)
            return c

        lax.fori_loop(0, n_pairs, k_pair, 0)

        @pl.when(jnp.logical_and(j > 0, j % 2 == 1))
        def _():
            diag_scores(1)
            plain_absorb(0, j - 1)
            diag_absorb(1)

        @pl.when(jnp.logical_and(j > 0, j % 2 == 0))
        def _():
            plain_scores(1, j - 1)
            plain_absorb(0, j - 2)
            diag_scores(0)
            plain_absorb(1, j - 1)
            diag_absorb(0)

        return carry

    lax.fori_loop(0, n_tiles, q_tile, 0)


def _attn_prompt(qb, kb, vb, p, tq=256):
    nb, length, a = qb.shape
    n_tiles = (length - N_META) // tq
    assert N_META + n_tiles * tq == length and tq % LANES == 0
    spec = pl.BlockSpec((None, length, a), lambda b: (b, 0, 0))
    return pl.pallas_call(
        functools.partial(_attn_prompt_kernel, tq=tq, n_tiles=n_tiles),
        grid=(nb,),
        in_specs=[spec, spec, spec, _const_spec((2, QK_DIM)), _const_spec((2, QK_DIM)),
                  _const_spec((1, HEAD_DIM))],
        out_specs=spec,
        out_shape=jax.ShapeDtypeStruct((nb, length, a), BF16),
        scratch_shapes=[pltpu.VMEM((N_HEADS, HEAD_DIM + 16, 2 * tq), F32), pltpu.VMEM((N_HEADS, 1, 2 * tq), F32),
                        pltpu.VMEM((2, N_HEADS, tq + N_META, 2 * tq), F32)],
        compiler_params=pltpu.CompilerParams(
            dimension_semantics=("arbitrary",), vmem_limit_bytes=VMEM_LIMIT),
        name="attn_prompt",
    )(qb, kb, vb, p["lam_q"], p["lam_k"], p["sub_gain"])


def _decode_entry(q, kn, vn, kp, vp, lam, sg, o_ref):
    t_new = kn.shape[0]
    grp = 2 * t_new
    nrow = N_HEADS * grp
    ncol = kp[0].shape[0]
    row = lax.broadcasted_iota(jnp.int32, (nrow, HEAD_DIM), 0)
    lane = lax.broadcasted_iota(jnp.int32, (nrow, HEAD_DIM), 1)
    comp_ok = (lane >= QK_DIM) == ((row % grp) >= t_new)
    t_of_row = lax.broadcasted_iota(jnp.int32, (nrow, 1), 0) % t_new
    head_ok = (lax.broadcasted_iota(jnp.int32, (nrow, ncol), 1) % N_HEADS
               == lax.broadcasted_iota(jnp.int32, (nrow, ncol), 0) // grp)

    def per_row_head(z, t):
        return jnp.concatenate(
            [jnp.broadcast_to(z[t:t + 1, h * HEAD_DIM:(h + 1) * HEAD_DIM], (grp, HEAD_DIM)) for h in range(N_HEADS)],
            axis=0)

    q32 = jnp.where(comp_ok, q.astype(F32), 0.0)
    qb = q32.astype(BF16)
    kn, vn = kn.astype(F32), vn.astype(F32)
    s_pages = [jnp.where(head_ok, _dot_nt(qb, k_ref[...].astype(BF16)), NEG) for k_ref in kp]
    s_new = []
    for t in range(t_new):
        sc = jnp.sum(q32 * per_row_head(kn, t), axis=1, keepdims=True)
        s_new.append(jnp.where(t_of_row >= t, sc, NEG))

    m = s_pages[0]
    for s in s_pages[1:]:
        m = jnp.maximum(m, s)
    m = jnp.max(m, axis=1, keepdims=True)
    for s in s_new:
        m = jnp.maximum(m, s)

    acc = jnp.zeros((nrow, HEAD_DIM), F32)
    psum = None
    for s, v_ref in zip(s_pages, vp):
        pr = jnp.exp2(s - m)
        psum = pr if psum is None else psum + pr
        acc = acc + _dot(pr.astype(BF16), v_ref[...].astype(BF16))
    lsum = jnp.sum(psum, axis=1, keepdims=True)
    for t in range(t_new):
        pr = jnp.exp2(s_new[t] - m)
        lsum = lsum + pr
        acc = acc + pr * per_row_head(vn, t)
    on = acc / lsum
    for h in range(N_HEADS):
        r0 = h * grp
        o = on[r0:r0 + t_new] - lam * on[r0 + t_new:r0 + grp]
        r = lax.rsqrt(jnp.mean(o * o, axis=-1, keepdims=True) + EPS)
        o_ref[:, h * HEAD_DIM:(h + 1) * HEAD_DIM] = ((o * r) * sg).astype(o_ref.dtype)


def _ssm_param_kernel(ar_ref, ai_ref, ldt_ref, br_ref, bi_ref, abr_ref, abi_ref, bbr_ref, bbi_ref):
    ar, ai = ar_ref[...], ai_ref[...]
    dt = jnp.exp(ldt_ref[...])
    mag = jnp.exp(ar * dt)
    abr, abi = mag * jnp.cos(ai * dt), mag * jnp.sin(ai * dt)
    den = ar * ar + ai * ai
    nr, ni = abr - 1.0, abi
    gr, gi = (nr * ar + ni * ai) / den, (ni * ar - nr * ai) / den
    abr_ref[...] = abr
    abi_ref[...] = abi
    for c in range(SSM_GROUP):
        br, bi = br_ref[c], bi_ref[c]
        bbr_ref[c] = gr * br - gi * bi
        bbi_ref[c] = gr * bi + gi * br


def _ssm_params(a_re, a_im, log_dt, b_re, b_im):
    g, pdim = a_re.shape
    c = b_re.shape[-1]
    b_re_t = jnp.transpose(b_re, (2, 0, 1))
    b_im_t = jnp.transpose(b_im, (2, 0, 1))
    gp = jax.ShapeDtypeStruct((g, pdim), F32)
    cgp = jax.ShapeDtypeStruct((c, g, pdim), F32)
    return pl.pallas_call(_ssm_param_kernel, out_shape=(gp, gp, cgp, cgp), name="ssm_params")(
        a_re, a_im, log_dt.reshape(g, 1), b_re_t, b_im_t)


def _ssm_kernel(u_ref, up_ref, h0r_ref, h0i_ref, abr_ref, abi_ref, bre_ref, bim_ref, crt_ref, cit_ref,
                d_ref, wg_ref, bg_ref, y_ref, hr_ref, hi_ref, xr_s, xi_s, *, tc, nb, n_chunks):
    i = pl.program_id(0)

    @pl.when(i == 0)
    def _():
        hr_ref[...] = h0r_ref[...]
        hi_ref[...] = h0i_ref[...]
        xr_s[1] = jnp.zeros(xr_s.shape[1:], F32)
        xi_s[1] = jnp.zeros(xi_s.shape[1:], F32)

    def body(cur):
        prev = 1 - cur
        rows, width = tc * nb, d_ref.shape[1]
        ub = u_ref[...].reshape(rows, width).astype(BF16)
        kin, nout = bre_ref.shape[1], bre_ref.shape[2]
        for mblk in range(bre_ref.shape[0]):
            um = ub[:, mblk * kin:(mblk + 1) * kin]
            xr_s[cur, :, mblk * nout:(mblk + 1) * nout] = _dot(um, bre_ref[mblk])
            xi_s[cur, :, mblk * nout:(mblk + 1) * nout] = _dot(um, bim_ref[mblk])

        hr, hi = hr_ref[...], hi_ref[...]
        for t in range(tc):
            r = slice(t * nb, (t + 1) * nb)
            abr, abi = abr_ref[...], abi_ref[...]
            nhr = abr * hr - abi * hi + xr_s[cur, r, :]
            nhi = abr * hi + abi * hr + xi_s[cur, r, :]
            xr_s[cur, r, :] = nhr
            xi_s[cur, r, :] = nhi
            hr, hi = nhr, nhi
        live = i < n_chunks
        hr_ref[...] = jnp.where(live, hr, hr_ref[...])
        hi_ref[...] = jnp.where(live, hi, hi_ref[...])

        kout = crt_ref.shape[1]
        ys = []
        for j in range(crt_ref.shape[0]):
            hrj = xr_s[prev, :, j * kout:(j + 1) * kout].astype(BF16)
            hij = xi_s[prev, :, j * kout:(j + 1) * kout].astype(BF16)
            ys.append(_dot(hrj, crt_ref[j]) - _dot(hij, cit_ref[j]))
        y = jnp.concatenate(ys, axis=1) + d_ref[...] * up_ref[...].reshape(rows, width)
        g = jax.nn.gelu(y)
        out = g * jax.nn.sigmoid(_dot(g.astype(BF16), wg_ref[...]) + bg_ref[...])
        y_ref[...] = out.reshape(tc, nb, width)

    for par in (0, 1):
        pl.when(i % 2 == par)(functools.partial(body, par))


def _ssm(u3, h0r, h0i, p, tc):
    t_len, nb, width = u3.shape
    n_state = h0r.shape[1]
    n_chunks = t_len // tc
    assert n_chunks * tc == t_len
    st = jax.ShapeDtypeStruct((nb, n_state), F32)
    st_spec = pl.BlockSpec((nb, n_state), lambda i: (0, 0))
    chunk = lambda index: pl.BlockSpec((tc, nb, width), lambda i: (index(i), 0, 0))
    return pl.pallas_call(
        functools.partial(_ssm_kernel, tc=tc, nb=nb, n_chunks=n_chunks),
        grid=(n_chunks + 1,),
        in_specs=[chunk(lambda i: jnp.minimum(i, n_chunks - 1)), chunk(lambda i: jnp.maximum(i - 1, 0)),
                  _const_spec((nb, n_state)), _const_spec((nb, n_state)),
                  _const_spec((nb, n_state)), _const_spec((nb, n_state)),
                  _const_spec(p["bre"].shape), _const_spec(p["bim"].shape),
                  _const_spec(p["crt"].shape), _const_spec(p["cit"].shape),
                  _const_spec((1, width)), _const_spec((width, width)), _const_spec((1, width))],
        out_specs=(chunk(lambda i: jnp.maximum(i - 1, 0)), st_spec, st_spec),
        out_shape=(jax.ShapeDtypeStruct((t_len, nb, width), F32), st, st),
        scratch_shapes=[pltpu.VMEM((2, tc * nb, n_state), F32), pltpu.VMEM((2, tc * nb, n_state), F32)],
        compiler_params=pltpu.CompilerParams(
            dimension_semantics=("arbitrary",), vmem_limit_bytes=VMEM_LIMIT),
        name="ssm",
    )(u3, u3, h0r, h0i, jnp.broadcast_to(p["abr"], (nb, n_state)), jnp.broadcast_to(p["abi"], (nb, n_state)),
      p["bre"], p["bim"], p["crt"], p["cit"],
      p["d_skip"], p["w_glu"], p["b_glu"])


def _post_head(x_ref, att_ref, ssm_ref, wo_ref, g2_ref, y_ref, xn_s):
    tm, a = y_ref.shape[0], wo_ref.shape[0] // 2
    mix = jnp.concatenate([att_ref[...].reshape(tm, a), ssm_ref[...].reshape(tm, a).astype(BF16)], axis=1)
    xm = x_ref[...] + _dot(mix, wo_ref[...])
    y_ref[...] = xm
    r = lax.rsqrt(jnp.mean(xm * xm, axis=-1, keepdims=True) + EPS)
    xn_s[...] = ((xm * r) * g2_ref[...]).astype(BF16)


def _post_chunks(j0, j1, wgate_ref, wup_ref, cw_ref, cb_ref, xn_s, h_s, ap_s, hist_s, *, shift, fc):
    tm = xn_s.shape[0]
    off = ap_s.shape[1] - tm
    for j in range(j0, j1):
        cs = slice(j * fc, (j + 1) * fc)
        ap = ap_s.at[j % 2]
        gate = _dot(xn_s[...], wgate_ref[:, cs])
        up = _dot(xn_s[...], wup_ref[:, cs])
        ap[off - 2 * shift:off, :] = hist_s[:, cs]
        ap[off:off + tm, :] = gate
        conv = (cb_ref[:, cs] + cw_ref[0:1, cs] * ap[off - 2 * shift:off - 2 * shift + tm, :]
                + cw_ref[1:2, cs] * ap[off - shift:off - shift + tm, :] + cw_ref[2:3, cs] * gate)
        hist_s[:, cs] = ap[off + tm - 2 * shift:off + tm, :]
        h_s[:, cs] = (jax.nn.gelu(conv) * up).astype(BF16)


def _post_kernel(x_ref, att_ref, ssm_ref, hist_ref, wo_ref, g2_ref, wgate_ref, wup_ref, cw_ref, cb_ref,
                 wd_ref, y_ref, cst_ref, xn_s, h_s, ap_s, hist_s, *, shift, fc):
    @pl.when(pl.program_id(1) == 0)
    def _():
        hist_s[...] = hist_ref[...]

    _post_head(x_ref, att_ref, ssm_ref, wo_ref, g2_ref, y_ref, xn_s)
    _post_chunks(0, wgate_ref.shape[1] // fc, wgate_ref, wup_ref, cw_ref, cb_ref, xn_s, h_s, ap_s, hist_s,
                 shift=shift, fc=fc)
    y_ref[...] += _dot(h_s[...], wd_ref[...])
    cst_ref[...] = hist_s[...]


def _post_decode_kernel(pt_ref, x_ref, att_ref, ssm_ref, hist_ref, wo_ref, g2_ref, wgate_ref, wup_ref, cw_ref,
                        cb_ref, wd_ref, q_ref, kn_ref, vn_ref, lq_ref, lk_ref, sg_ref, *refs,
                        shift, fc, n_pages, stages):
    del pt_ref
    kp, vp = refs[:n_pages], refs[n_pages:2 * n_pages]
    y_ref, cst_ref, o_ref, xn_s, h_s, ap_s, hist_s = refs[2 * n_pages:]
    t, e = pl.program_id(1), pl.program_id(2)

    def stage(k):
        j0, j1 = stages[k]
        if k == 0:
            @pl.when(t == 0)
            def _():
                hist_s[...] = hist_ref[...]

            _post_head(x_ref, att_ref, ssm_ref, wo_ref, g2_ref, y_ref, xn_s)
        _post_chunks(j0, j1, wgate_ref, wup_ref, cw_ref, cb_ref, xn_s, h_s, ap_s, hist_s, shift=shift, fc=fc)
        if k == len(stages) - 1:
            y_ref[...] += _dot(h_s[...], wd_ref[...])
            cst_ref[...] = hist_s[...]
        _decode_entry(q_ref[...], kn_ref[...], vn_ref[...], kp, vp, _lam(lq_ref, lk_ref), sg_ref[...], o_ref)

    for k in range(len(stages)):
        pl.when(e == k)(functools.partial(stage, k))


def _post(x2d, att, ssm, hist, p, tm, nb, shift, att_spec, ssm_spec, hist_shared=False, fc=256):
    rows, d = x2d.shape
    d_ff = p["w_gate"].shape[1]
    per_b = rows // nb
    nt = per_b // tm
    assert nt * tm == per_b and d_ff % fc == 0 and tm >= 2 * shift
    off = -(-2 * shift // 8) * 8
    row_spec = lambda w: pl.BlockSpec((tm, w), lambda b, t: (b * nt + t, 0))
    hist_in = pl.BlockSpec((None, 2 * shift, d_ff), (lambda b, t: (0, 0, 0)) if hist_shared else (lambda b, t: (b, 0, 0)))
    hist_out = pl.BlockSpec((None, 2 * shift, d_ff), lambda b, t: (b, 0, 0))
    return pl.pallas_call(
        functools.partial(_post_kernel, shift=shift, fc=fc),
        grid=(nb, nt),
        in_specs=[row_spec(d), att_spec, ssm_spec, hist_in,
                  _const_spec(p["w_out"].shape), _const_spec((1, d)),
                  _const_spec(p["w_gate"].shape), _const_spec(p["w_up"].shape),
                  _const_spec((3, d_ff)), _const_spec((1, d_ff)), _const_spec(p["w_down"].shape)],
        out_specs=(row_spec(d), hist_out),
        out_shape=(jax.ShapeDtypeStruct((rows, d), F32), jax.ShapeDtypeStruct((nb, 2 * shift, d_ff), F32)),
        scratch_shapes=[pltpu.VMEM((tm, d), BF16), pltpu.VMEM((tm, d_ff), BF16),
                        pltpu.VMEM((2, off + tm, fc), F32), pltpu.VMEM((2 * shift, d_ff), F32)],
        compiler_params=pltpu.CompilerParams(
            dimension_semantics=("arbitrary", "arbitrary"), vmem_limit_bytes=VMEM_LIMIT),
        name="post",
    )(x2d, att, ssm, hist, p["w_out"], p["norm2"], p["w_gate"], p["w_up"], p["conv_w"], p["conv_b"],
      p["w_down"])


def _post_decode(x2d, att, ssm, hist, p, tm, nb, att_block, ssm_block, q_rep, kn, vn, cache_k, cache_v,
                 page_table, fc=256):
    rows, d = x2d.shape
    d_ff = p["w_gate"].shape[1]
    nt = rows // nb // tm
    db, nrow, _ = q_rep.shape
    t_new, a = kn.shape[1:]
    n_pages = page_table.shape[1]
    prow = cache_k.shape[1]
    n_sub = db // (nb * nt)
    n_chunks = d_ff // fc
    assert nt * tm * nb == rows and n_sub * nb * nt == db and n_sub >= 2 and n_chunks * fc == d_ff
    base, rem = divmod(n_chunks, n_sub - 1)
    sizes = [base + (1 if i >= n_sub - 1 - rem else 0) for i in range(n_sub - 1)] + [0]
    stages = tuple((sum(sizes[:i]), sum(sizes[:i + 1])) for i in range(n_sub))
    shift, off = 1, 8
    entry = lambda b, t, e: (b * nt + t) * n_sub + e
    tile = lambda block, index: pl.BlockSpec(block, lambda b, t, e, pt_ref: index(b, t))
    const = lambda shape: pl.BlockSpec(shape, lambda b, t, e, pt_ref: (0,) * len(shape), pipeline_mode=pl.Buffered(1))
    per_entry = lambda r, w: pl.BlockSpec((None, r, w), lambda b, t, e, pt_ref: (entry(b, t, e), 0, 0))

    def page_spec(pg):
        return pl.BlockSpec((None, prow, HEAD_DIM),
                            lambda b, t, e, pt_ref: (pt_ref[entry(b, t, e) * n_pages + pg], 0, 0))

    grid_spec = pltpu.PrefetchScalarGridSpec(
        num_scalar_prefetch=1,
        grid=(nb, nt, n_sub),
        in_specs=[tile((tm, d), lambda b, t: (b * nt + t, 0)), tile(*att_block), tile(*ssm_block),
                  const((None, 2 * shift, d_ff)),
                  const(p["w_out"].shape), const((1, d)), const(p["w_gate"].shape), const(p["w_up"].shape),
                  const((3, d_ff)), const((1, d_ff)), const(p["w_down"].shape),
                  per_entry(nrow, HEAD_DIM), per_entry(t_new, a), per_entry(t_new, a),
                  const((2, QK_DIM)), const((2, QK_DIM)), const((1, HEAD_DIM))]
        + [page_spec(pg) for pg in range(n_pages)] + [page_spec(pg) for pg in range(n_pages)],
        out_specs=(tile((tm, d), lambda b, t: (b * nt + t, 0)),
                   tile((None, 2 * shift, d_ff), lambda b, t: (b, 0, 0)), per_entry(t_new, a)),
        scratch_shapes=[pltpu.VMEM((tm, d), BF16), pltpu.VMEM((tm, d_ff), BF16),
                        pltpu.VMEM((2, off + tm, fc), F32), pltpu.VMEM((2 * shift, d_ff), F32)],
    )
    return pl.pallas_call(
        functools.partial(_post_decode_kernel, shift=shift, fc=fc, n_pages=n_pages, stages=stages),
        grid_spec=grid_spec,
        out_shape=(jax.ShapeDtypeStruct((rows, d), F32), jax.ShapeDtypeStruct((nb, 2 * shift, d_ff), F32),
                   jax.ShapeDtypeStruct((db, t_new, a), BF16)),
        compiler_params=pltpu.CompilerParams(
            dimension_semantics=("arbitrary", "arbitrary", "arbitrary"), vmem_limit_bytes=VMEM_LIMIT),
        name="post_decode",
    )(page_table.reshape(-1), x2d, att, ssm, hist, p["w_out"], p["norm2"], p["w_gate"], p["w_up"], p["conv_w"],
      p["conv_b"], p["w_down"], q_rep, kn, vn, p["lam_q"], p["lam_k"], p["sub_gain"],
      *([cache_k] * n_pages), *([cache_v] * n_pages))


def _prepare_params(norm1, w_in, q_norm, k_norm, lam_q, lam_k, sub_norm, ssm_a_re, ssm_a_im, ssm_log_dt,
                    ssm_b_re, ssm_b_im, ssm_c_re, ssm_c_im, ssm_d, w_glu, b_glu, w_out, norm2, w_gate,
                    w_up, ffn_conv_w, ffn_conv_b, w_down):
    l = 0
    g, pdim = ssm_a_re[l].shape
    c = SSM_GROUP
    a = N_HEADS * HEAD_DIM
    abr, abi, bbr, bbi = _ssm_params(ssm_a_re[l], ssm_a_im[l], ssm_log_dt[l], ssm_b_re[l], ssm_b_im[l])
    gi = LANES // c

    def in_blocks(bb):
        bb = jnp.transpose(bb, (1, 0, 2)).reshape(g // gi, gi, c, 1, pdim)
        same = jnp.eye(gi, dtype=bool)[None, :, None, :, None]
        return jnp.where(same, bb, 0.0).astype(BF16).reshape(g // gi, gi * c, gi * pdim)

    go = MXU // c

    def out_blocks(cc):
        cc = jnp.transpose(cc.reshape(g // go, go, c, pdim), (0, 1, 3, 2))[:, :, :, None, :]
        same = jnp.eye(go, dtype=bool)[None, :, None, :, None]
        return jnp.where(same, cc, 0.0).astype(BF16).reshape(g // go, go * pdim, go * c)

    comp = jnp.arange(MXU) // QK_DIM
    ones_blk = (comp[:, None] == comp[None, :]).astype(BF16) * (1.0 / QK_DIM)
    return {
        "norm1": norm1[l].reshape(1, -1), "w_in": w_in[l],
        "q_gain": jnp.tile(q_norm[l].reshape(-1), N_HEADS).reshape(1, a),
        "k_gain": jnp.tile(k_norm[l].reshape(-1), N_HEADS).reshape(1, a),
        "ones_blk": ones_blk.astype(BF16),
        "lam_q": lam_q[l], "lam_k": lam_k[l],
        "sub_gain": (sub_norm[l] * (1.0 - LAM_INIT)).reshape(1, HEAD_DIM),
        "abr": abr.reshape(1, g * pdim), "abi": abi.reshape(1, g * pdim),
        "bre": in_blocks(bbr), "bim": in_blocks(bbi),
        "crt": out_blocks(ssm_c_re[l]), "cit": out_blocks(ssm_c_im[l]),
        "d_skip": ssm_d[l].reshape(1, g * c), "w_glu": w_glu[l].astype(BF16), "b_glu": b_glu[l].reshape(1, -1),
        "w_out": w_out[l].astype(BF16), "norm2": norm2[l].reshape(1, -1),
        "w_gate": w_gate[l].astype(BF16), "w_up": w_up[l].astype(BF16),
        "conv_w": ffn_conv_w[l], "conv_b": ffn_conv_b[l].reshape(1, -1), "w_down": w_down[l].astype(BF16),
    }


def _row_tile(length, cap=768):
    best = None
    for t in range(16, cap + 1, 16):
        if length % t == 0:
            best = t
    assert best is not None
    return best


def kernel(x_prompt, x_sample, cache_k, cache_v, state_ssm_re, state_ssm_im, state_ffn_conv, page_table, meta_tokens, norm1, w_in, q_norm, k_norm, lam_q, lam_k, sub_norm, ssm_a_re, ssm_a_im, ssm_log_dt, ssm_b_re, ssm_b_im, ssm_c_re, ssm_c_im, ssm_d, w_glu, b_glu, w_out, norm2, w_gate, w_up, ffn_conv_w, ffn_conv_b, w_down):
    assert norm1.shape[0] == 1, "single-layer stack"
    p = _prepare_params(norm1, w_in, q_norm, k_norm, lam_q, lam_k, sub_norm, ssm_a_re, ssm_a_im, ssm_log_dt,
                        ssm_b_re, ssm_b_im, ssm_c_re, ssm_c_im, ssm_d, w_glu, b_glu, w_out, norm2, w_gate,
                        w_up, ffn_conv_w, ffn_conv_b, w_down)
    nb, seq, d = x_prompt.shape
    db, t_new, _ = x_sample.shape
    a = N_HEADS * HEAD_DIM
    g, pdim = ssm_a_re.shape[1:]
    n_state = g * pdim
    d_ff = w_gate.shape[-1]
    length = seq + N_META

    meta = meta_tokens.astype(x_prompt.dtype)
    tm = _row_tile(length)
    qb, k, kb, v, vb, u = _project(x_prompt, meta, tm, p, u_time_major=True)
    att = _attn_prompt(qb.reshape(nb, length, a), kb.reshape(nb, length, a), vb.reshape(nb, length, a), p)
    zst = jnp.zeros((nb, n_state), F32)
    tc = _row_tile(length * nb, cap=768) // nb
    ys, hr, hi = _ssm(u.reshape(length, nb, a), zst, zst, p, tc)
    ys2 = ys.reshape(length, nb * a)
    plain = lambda w: pl.BlockSpec((N_META, w), lambda b, t: (0, 0))
    _, hist_meta = _post(meta, att[0, :N_META], ys2[:N_META, :a], jnp.zeros((1, 2, d_ff), F32), p, N_META, 1,
                         shift=1, att_spec=plain(a), ssm_spec=plain(a))
    k_prompt = k.reshape(1, nb, length, N_HEADS, HEAD_DIM)
    v_prompt = v.reshape(1, nb, length, N_HEADS, HEAD_DIM)
    ssm_re_p = hr.reshape(1, nb, g, pdim)
    ssm_im_p = hi.reshape(1, nb, g, pdim)

    rows_s = db * t_new
    xs2 = jnp.transpose(x_sample, (1, 0, 2)).reshape(rows_s, d)
    qb, k, kb, v, vb, u = _project(xs2, None, rows_s, p, u_time_major=False)
    to_bm = lambda z: jnp.transpose(z.reshape(t_new, db, a), (1, 0, 2))
    q_hct = jnp.transpose(qb.reshape(t_new, db, N_HEADS, 1, HEAD_DIM), (1, 2, 3, 0, 4))
    q_rep = jnp.broadcast_to(q_hct, (db, N_HEADS, 2, t_new, HEAD_DIM)).reshape(db, N_HEADS * 2 * t_new, HEAD_DIM)
    n_pool, page = cache_k.shape[1:3]

    tp = _row_tile(seq, cap=512)
    att_block = ((pl.Element(1), pl.Element(tp), pl.Element(a)),
                 lambda b, t: (b, pl.multiple_of(N_META + t * tp, N_META), 0))
    ssm_block = ((pl.Element(tp), pl.Element(a)),
                 lambda b, t: (pl.multiple_of(N_META + t * tp, N_META), pl.multiple_of(b * a, a)))
    yp, cst_p, att = _post_decode(
        x_prompt.reshape(nb * seq, d), att, ys2, hist_meta, p, tp, nb, att_block, ssm_block,
        q_rep, to_bm(kb), to_bm(vb), cache_k[0].reshape(n_pool, page * N_HEADS, HEAD_DIM),
        cache_v[0].reshape(n_pool, page * N_HEADS, HEAD_DIM), page_table)
    y_prompt = yp.reshape(nb, seq, d)
    conv_p = cst_p[None]
    att_tm = jnp.transpose(att, (1, 0, 2)).reshape(rows_s, a)
    ys, hr, hi = _ssm(u.reshape(t_new, db, a), state_ssm_re[0].reshape(db, n_state),
                      state_ssm_im[0].reshape(db, n_state), p, t_new)
    hist = jnp.transpose(state_ffn_conv[0], (1, 0, 2)).reshape(1, 2 * db, d_ff)
    whole = pl.BlockSpec((rows_s, a), lambda b, t: (0, 0))
    ysm, cst_s = _post(xs2, att_tm, ys.reshape(rows_s, a), hist, p, rows_s, 1, shift=db,
                       att_spec=whole, ssm_spec=whole)
    y_sample = jnp.transpose(ysm.reshape(t_new, db, d), (1, 0, 2))
    k_sample = jnp.transpose(k.reshape(t_new, db, N_HEADS, HEAD_DIM), (1, 0, 2, 3))[None]
    v_sample = jnp.transpose(v.reshape(t_new, db, N_HEADS, HEAD_DIM), (1, 0, 2, 3))[None]
    ssm_re_s = hr.reshape(1, db, g, pdim)
    ssm_im_s = hi.reshape(1, db, g, pdim)
    conv_s = jnp.transpose(cst_s.reshape(2, db, d_ff), (1, 0, 2))[None]

    return (y_prompt, y_sample, k_prompt, v_prompt, k_sample, v_sample,
            ssm_re_p, ssm_im_p, ssm_re_s, ssm_im_s, conv_p, conv_s)
```

```python
import functools
import math

import jax
import jax.numpy as jnp
from jax import lax
from jax.experimental import pallas as pl
from jax.experimental.pallas import tpu as pltpu

N_META = 16
N_HEADS = 4
QK_DIM = 64
HEAD_DIM = 2 * QK_DIM
SSM_GROUP = 16
SSM_STATE = 64
EPS = 1e-6
NEG = -1e30
LAM_INIT = 0.8 - 0.6 * math.exp(-0.3 * 0)
LOG2E = math.log2(math.e)

LANES = 128
MXU = 256
VMEM_LIMIT = 56 * 1024 * 1024

F32 = jnp.float32
BF16 = jnp.bfloat16


def _dot(a, b):
    return jnp.dot(a, b, preferred_element_type=F32)


def _dot_nt(a, b):
    return lax.dot_general(a, b, (((1,), (1,)), ((), ())), preferred_element_type=F32)


def _const_spec(shape):
    nd = len(shape)
    return pl.BlockSpec(shape, lambda *_: (0,) * nd, pipeline_mode=pl.Buffered(1))


def _lam(lq_ref, lk_ref):
    e = jnp.exp(jnp.sum(lq_ref[...] * lk_ref[...], axis=1, keepdims=True))
    return e[0:1] - e[1:2] + LAM_INIT


def _proj_kernel(x_ref, meta_ref, g1_ref, w_ref, qg_ref, kg_ref, ones_ref,
                 qb_ref, k_ref, kb_ref, v_ref, vb_ref, u_ref, wb_s, *, n_meta):
    outs = (qb_ref, k_ref, kb_ref, v_ref, vb_ref, u_ref)
    consts = (g1_ref, wb_s, qg_ref, kg_ref, ones_ref)
    tm = qb_ref.shape[0]

    @pl.when(jnp.logical_and(pl.program_id(0) == 0, pl.program_id(1) == 0))
    def _():
        wb_s[...] = w_ref[...].astype(BF16)

    if not n_meta:
        _proj_body(x_ref[...], consts, outs)
        return
    t = pl.program_id(1)

    @pl.when(t == 0)
    def _():
        _proj_body(jnp.concatenate([meta_ref[...], x_ref[0, 0:tm - n_meta, :]], axis=0), consts, outs)

    @pl.when(t != 0)
    def _():
        _proj_body(x_ref[0], consts, outs)


def _proj_body(x, consts, outs):
    g1_ref, w_ref, qg_ref, kg_ref, ones_ref = consts
    qb_ref, k_ref, kb_ref, v_ref, vb_ref, u_ref = outs
    a = N_HEADS * HEAD_DIM
    r = lax.rsqrt(jnp.mean(x * x, axis=-1, keepdims=True) + EPS)
    xn = ((x * r) * g1_ref[...]).astype(BF16)
    proj = _dot(xn, w_ref[...])

    def comp_norm(z, g):
        z2 = (z * z).astype(BF16)
        ms = jnp.concatenate(
            [_dot(z2[:, j * MXU:(j + 1) * MXU], ones_ref[...]) for j in range(a // MXU)], axis=1)
        return (z * lax.rsqrt(ms + EPS)) * g

    qn = comp_norm(proj[:, :a], qg_ref[...])
    kn = comp_norm(proj[:, a:2 * a], kg_ref[...])
    v = proj[:, 2 * a:3 * a]
    qb_ref[...] = (qn * (QK_DIM ** -0.5 * LOG2E)).astype(BF16)
    kb_ref[...] = kn.astype(BF16)
    vb_ref[...] = v.astype(BF16)
    u_ref[...] = proj[:, 3 * a:]
    tm = x.shape[0]
    for h in range(N_HEADS):
        k_ref[pl.ds(h, tm, stride=N_HEADS), :] = kn[:, h * HEAD_DIM:(h + 1) * HEAD_DIM]
        v_ref[pl.ds(h, tm, stride=N_HEADS), :] = v[:, h * HEAD_DIM:(h + 1) * HEAD_DIM]


def _project(x, meta, tm, p, u_time_major):
    a = N_HEADS * HEAD_DIM
    d = x.shape[-1]
    if meta is None:
        nb, n_meta = 1, 0
        per_b = x.shape[0]
        x_spec = pl.BlockSpec((tm, d), lambda b, t: (t, 0))
        meta = jnp.zeros((8, d), x.dtype)
    else:
        nb, n_meta = x.shape[0], meta.shape[0]
        per_b = n_meta + x.shape[1]
        x_spec = pl.BlockSpec(
            (pl.Element(1), pl.Element(tm), pl.Element(d)),
            lambda b, t: (b, pl.multiple_of(jnp.maximum(t * tm - n_meta, 0), 8), 0))
    rows = nb * per_b
    nt = per_b // tm
    assert nt * tm == per_b and n_meta % 8 == 0
    grid = (nb, nt)
    row_spec = lambda w: pl.BlockSpec((tm, w), lambda b, t: (b * nt + t, 0))
    head_spec = pl.BlockSpec((tm * N_HEADS, HEAD_DIM), lambda b, t: (b * nt + t, 0))
    if u_time_major:
        u_shape = jax.ShapeDtypeStruct((per_b, nb * a), F32)
        u_spec = pl.BlockSpec((tm, a), lambda b, t: (t, b))
    else:
        u_shape = jax.ShapeDtypeStruct((rows, a), F32)
        u_spec = row_spec(a)
    out_shape = (jax.ShapeDtypeStruct((rows, a), BF16), jax.ShapeDtypeStruct((rows * N_HEADS, HEAD_DIM), F32),
                 jax.ShapeDtypeStruct((rows, a), BF16), jax.ShapeDtypeStruct((rows * N_HEADS, HEAD_DIM), F32),
                 jax.ShapeDtypeStruct((rows, a), BF16), u_shape)
    return pl.pallas_call(
        functools.partial(_proj_kernel, n_meta=n_meta),
        grid=grid,
        in_specs=[x_spec, _const_spec(meta.shape), _const_spec((1, d)), _const_spec(p["w_in"].shape),
                  _const_spec((1, a)), _const_spec((1, a)), _const_spec((MXU, MXU))],
        out_specs=(row_spec(a), head_spec, row_spec(a), head_spec, row_spec(a), u_spec),
        out_shape=out_shape,
        scratch_shapes=[pltpu.VMEM(p["w_in"].shape, BF16)],
        compiler_params=pltpu.CompilerParams(
            dimension_semantics=("arbitrary", "arbitrary"), vmem_limit_bytes=VMEM_LIMIT),
        name="proj",
    )(x, meta, p["norm1"], p["w_in"], p["q_gain"], p["k_gain"], p["ones_blk"])


def _attn_prompt_kernel(q_ref, k_ref, v_ref, lq_ref, lk_ref, sg_ref, o_ref, acc_s, m_s, s_s, *, tq, n_tiles):
    lam = _lam(lq_ref, lk_ref)
    sg = sg_ref[...]
    lo = lax.broadcasted_iota(jnp.int32, (1, HEAD_DIM), 1) < QK_DIM
    n_ones = acc_s.shape[1] - HEAD_DIM

    def hcols(h):
        return slice(h * HEAD_DIM, (h + 1) * HEAD_DIM)

    def stack_q(q):
        z = jnp.zeros_like(q)
        return jnp.concatenate([jnp.where(lo, q, z), jnp.where(lo, z, q)], axis=0)

    def init(h, t2):
        m_s[h, :, 0:t2] = jnp.full((1, t2), NEG, F32)
        acc_s[h, :, 0:t2] = jnp.zeros((HEAD_DIM + n_ones, t2), F32)

    def scores(slot, h, q2, kt, mask):
        s = _dot_nt(kt, q2)
        if mask is not None:
            s = jnp.where(mask, s, NEG)
        s_s[slot, h, 0:kt.shape[0], 0:q2.shape[0]] = s

    def absorb(slot, h, nk, t2, vt):
        s = s_s[slot, h, 0:nk, 0:t2]
        m_prev = m_s[h, :, 0:t2]
        m_new = jnp.maximum(m_prev, jnp.max(s, axis=0, keepdims=True))
        alpha = jnp.exp2(m_prev - m_new)
        pr = jnp.exp2(s - m_new).astype(BF16)
        va = jnp.concatenate([vt.T, jnp.ones((n_ones, nk), BF16)], axis=0)
        acc_s[h, :, 0:t2] = alpha * acc_s[h, :, 0:t2] + _dot(va, pr)
        m_s[h, :, 0:t2] = m_new

    def finish(h, t):
        acc = acc_s[h, :, 0:2 * t]
        on = acc[0:HEAD_DIM] / acc[HEAD_DIM:HEAD_DIM + 1]
        o = on[:, 0:t] - lam * on[:, t:2 * t]
        r = lax.rsqrt(jnp.mean(o * o, axis=0, keepdims=True) + EPS)
        return ((o * r).T * sg).astype(o_ref.dtype)

    def causal_mask(t, nk, offset):
        qi = lax.broadcasted_iota(jnp.int32, (nk, 2 * t), 1)
        qi = jnp.where(qi >= t, qi - t, qi)
        ki = lax.broadcasted_iota(jnp.int32, (nk, 2 * t), 0)
        return ki <= qi + offset

    tmq = LANES
    for h in range(N_HEADS):
        init(h, 2 * tmq)
        scores(0, h, stack_q(q_ref[0:tmq, hcols(h)]), k_ref[0:tmq, hcols(h)], causal_mask(tmq, tmq, 0))
    for h in range(N_HEADS):
        absorb(0, h, tmq, 2 * tmq, v_ref[0:tmq, hcols(h)])
        o_ref[0:N_META, hcols(h)] = finish(h, tmq)[0:N_META]

    dk = tq + N_META

    def q_tile(j, carry):
        qs = pl.multiple_of(N_META + j * tq, N_META)
        q2 = [stack_q(q_ref[pl.ds(qs, tq), hcols(h)]) for h in range(N_HEADS)]
        ds = pl.multiple_of(j * tq, tq)
        dmask = causal_mask(tq, dk, N_META)
        for h in range(N_HEADS):
            init(h, 2 * tq)

        def plain_scores(slot, i):
            ks = pl.multiple_of(i * tq, tq)
            for h in range(N_HEADS):
                scores(slot, h, q2[h], k_ref[pl.ds(ks, tq), hcols(h)], None)

        def diag_scores(slot):
            for h in range(N_HEADS):
                scores(slot, h, q2[h], k_ref[pl.ds(ds, dk), hcols(h)], dmask)

        def plain_absorb(slot, i):
            ks = pl.multiple_of(i * tq, tq)
            for h in range(N_HEADS):
                absorb(slot, h, tq, 2 * tq, v_ref[pl.ds(ks, tq), hcols(h)])

        def diag_absorb(slot):
            for h in range(N_HEADS):
                absorb(slot, h, dk, 2 * tq, v_ref[pl.ds(ds, dk), hcols(h)])
                o_ref[pl.ds(qs, tq), hcols(h)] = finish(h, tq)

        @pl.when(j == 0)
        def _():
            diag_scores(0)
            diag_absorb(0)

        @pl.when(j > 0)
        def _():
            plain_scores(0, 0)

        n_pairs = (j - 1) // 2

        def k_pair(pi, c):
            i = 2 * pi
            plain_scores(1, i + 1)
            plain_absorb(0, i)
            plain_scores(0, i + 2)
            plain_absorb(1, i + 1)
            return c

        lax.fori_loop(0, n_pairs, k_pair, 0)

        @pl.when(jnp.logical_and(j > 0, j % 2 == 1))
        def _():
            diag_scores(1)
            plain_absorb(0, j - 1)
            diag_absorb(1)

        @pl.when(jnp.logical_and(j > 0, j % 2 == 0))
        def _():
            plain_scores(1, j - 1)
            plain_absorb(0, j - 2)
            diag_scores(0)
            plain_absorb(1, j - 1)
            diag_absorb(0)

        return carry

    lax.fori_loop(0, n_tiles, q_tile, 0)


def _attn_prompt(qb, kb, vb, p, tq=256):
    nb, length, a = qb.shape
    n_tiles = (length - N_META) // tq
    assert N_META + n_tiles * tq == length and tq % LANES == 0
    spec = pl.BlockSpec((None, length, a), lambda b: (b, 0, 0))
    return pl.pallas_call(
        functools.partial(_attn_prompt_kernel, tq=tq, n_tiles=n_tiles),
        grid=(nb,),
        in_specs=[spec, spec, spec, _const_spec((2, QK_DIM)), _const_spec((2, QK_DIM)),
                  _const_spec((1, HEAD_DIM))],
        out_specs=spec,
        out_shape=jax.ShapeDtypeStruct((nb, length, a), BF16),
        scratch_shapes=[pltpu.VMEM((N_HEADS, HEAD_DIM + 16, 2 * tq), F32), pltpu.VMEM((N_HEADS, 1, 2 * tq), F32),
                        pltpu.VMEM((2, N_HEADS, tq + N_META, 2 * tq), F32)],
        compiler_params=pltpu.CompilerParams(
            dimension_semantics=("arbitrary",), vmem_limit_bytes=VMEM_LIMIT),
        name="attn_prompt",
    )(qb, kb, vb, p["lam_q"], p["lam_k"], p["sub_gain"])


def _decode_entry(q, kn, vn, kp, vp, lam, sg, o_ref):
    t_new = kn.shape[0]
    grp = 2 * t_new
    nrow = N_HEADS * grp
    ncol = kp[0].shape[0]
    row = lax.broadcasted_iota(jnp.int32, (nrow, HEAD_DIM), 0)
    lane = lax.broadcasted_iota(jnp.int32, (nrow, HEAD_DIM), 1)
    comp_ok = (lane >= QK_DIM) == ((row % grp) >= t_new)
    t_of_row = lax.broadcasted_iota(jnp.int32, (nrow, 1), 0) % t_new
    head_ok = (lax.broadcasted_iota(jnp.int32, (nrow, ncol), 1) % N_HEADS
               == lax.broadcasted_iota(jnp.int32, (nrow, ncol), 0) // grp)

    def per_row_head(z, t):
        return jnp.concatenate(
            [jnp.broadcast_to(z[t:t + 1, h * HEAD_DIM:(h + 1) * HEAD_DIM], (grp, HEAD_DIM)) for h in range(N_HEADS)],
            axis=0)

    q32 = jnp.where(comp_ok, q.astype(F32), 0.0)
    qb = q32.astype(BF16)
    kn, vn = kn.astype(F32), vn.astype(F32)
    s_pages = [jnp.where(head_ok, _dot_nt(qb, k_ref[...].astype(BF16)), NEG) for k_ref in kp]
    s_new = []
    for t in range(t_new):
        sc = jnp.sum(q32 * per_row_head(kn, t), axis=1, keepdims=True)
        s_new.append(jnp.where(t_of_row >= t, sc, NEG))

    m = s_pages[0]
    for s in s_pages[1:]:
        m = jnp.maximum(m, s)
    m = jnp.max(m, axis=1, keepdims=True)
    for s in s_new:
        m = jnp.maximum(m, s)

    acc = jnp.zeros((nrow, HEAD_DIM), F32)
    psum = None
    for s, v_ref in zip(s_pages, vp):
        pr = jnp.exp2(s - m)
        psum = pr if psum is None else psum + pr
        acc = acc + _dot(pr.astype(BF16), v_ref[...].astype(BF16))
    lsum = jnp.sum(psum, axis=1, keepdims=True)
    for t in range(t_new):
        pr = jnp.exp2(s_new[t] - m)
        lsum = lsum + pr
        acc = acc + pr * per_row_head(vn, t)
    on = acc / lsum
    for h in range(N_HEADS):
        r0 = h * grp
        o = on[r0:r0 + t_new] - lam * on[r0 + t_new:r0 + grp]
        r = lax.rsqrt(jnp.mean(o * o, axis=-1, keepdims=True) + EPS)
        o_ref[:, h * HEAD_DIM:(h + 1) * HEAD_DIM] = ((o * r) * sg).astype(o_ref.dtype)


def _attn_decode_kernel(pt_ref, q_ref, kn_ref, vn_ref, lq_ref, lk_ref, sg_ref, *refs, n_pages, bps):
    del pt_ref
    n_kv = bps * n_pages
    k_refs, v_refs, o_ref = refs[:n_kv], refs[n_kv:2 * n_kv], refs[2 * n_kv]
    lam, sg = _lam(lq_ref, lk_ref), sg_ref[...]
    for e in range(bps):
        _decode_entry(q_ref[e], kn_ref[e], vn_ref[e], k_refs[e * n_pages:(e + 1) * n_pages],
                      v_refs[e * n_pages:(e + 1) * n_pages], lam, sg, o_ref.at[e])


def _attn_decode(q_rep, kn, vn, cache_k, cache_v, page_table, p, bps=2):
    db, nrow, _ = q_rep.shape
    t_new, a = kn.shape[1:]
    n_pages = page_table.shape[1]
    prow = cache_k.shape[1]
    pt = page_table.reshape(-1)
    assert db % bps == 0

    def page_spec(e, pg):
        return pl.BlockSpec((None, prow, HEAD_DIM),
                            lambda b, pt_ref: (pt_ref[(b * bps + e) * n_pages + pg], 0, 0))

    page_specs = lambda: [page_spec(e, pg) for e in range(bps) for pg in range(n_pages)]
    const = lambda shape: pl.BlockSpec(shape, lambda b, pt_ref: (0,) * len(shape))
    per_step = lambda r, w: pl.BlockSpec((bps, r, w), lambda b, pt_ref: (b, 0, 0))
    grid_spec = pltpu.PrefetchScalarGridSpec(
        num_scalar_prefetch=1,
        grid=(db // bps,),
        in_specs=[per_step(nrow, HEAD_DIM), per_step(t_new, a), per_step(t_new, a),
                  const((2, QK_DIM)), const((2, QK_DIM)), const((1, HEAD_DIM))] + page_specs() + page_specs(),
        out_specs=per_step(t_new, a),
    )
    return pl.pallas_call(
        functools.partial(_attn_decode_kernel, n_pages=n_pages, bps=bps),
        grid_spec=grid_spec,
        out_shape=jax.ShapeDtypeStruct((db, t_new, a), BF16),
        compiler_params=pltpu.CompilerParams(
            dimension_semantics=("arbitrary",), vmem_limit_bytes=VMEM_LIMIT),
        name="attn_decode",
    )(pt, q_rep, kn, vn, p["lam_q"], p["lam_k"], p["sub_gain"],
      *([cache_k] * (bps * n_pages)), *([cache_v] * (bps * n_pages)))


def _ssm_param_kernel(ar_ref, ai_ref, ldt_ref, br_ref, bi_ref, abr_ref, abi_ref, bbr_ref, bbi_ref):
    ar, ai = ar_ref[...], ai_ref[...]
    dt = jnp.exp(ldt_ref[...])
    mag = jnp.exp(ar * dt)
    abr, abi = mag * jnp.cos(ai * dt), mag * jnp.sin(ai * dt)
    den = ar * ar + ai * ai
    nr, ni = abr - 1.0, abi
    gr, gi = (nr * ar + ni * ai) / den, (ni * ar - nr * ai) / den
    abr_ref[...] = abr
    abi_ref[...] = abi
    for c in range(SSM_GROUP):
        br, bi = br_ref[c], bi_ref[c]
        bbr_ref[c] = gr * br - gi * bi
        bbi_ref[c] = gr * bi + gi * br


def _ssm_params(a_re, a_im, log_dt, b_re, b_im):
    g, pdim = a_re.shape
    c = b_re.shape[-1]
    b_re_t = jnp.transpose(b_re, (2, 0, 1))
    b_im_t = jnp.transpose(b_im, (2, 0, 1))
    gp = jax.ShapeDtypeStruct((g, pdim), F32)
    cgp = jax.ShapeDtypeStruct((c, g, pdim), F32)
    return pl.pallas_call(_ssm_param_kernel, out_shape=(gp, gp, cgp, cgp), name="ssm_params")(
        a_re, a_im, log_dt.reshape(g, 1), b_re_t, b_im_t)


def _ssm_kernel(u_ref, up_ref, h0r_ref, h0i_ref, abr_ref, abi_ref, bre_ref, bim_ref, crt_ref, cit_ref,
                d_ref, wg_ref, bg_ref, y_ref, hr_ref, hi_ref, xr_s, xi_s, *, tc, nb, n_chunks):
    i = pl.program_id(0)

    @pl.when(i == 0)
    def _():
        hr_ref[...] = h0r_ref[...]
        hi_ref[...] = h0i_ref[...]
        xr_s[1] = jnp.zeros(xr_s.shape[1:], F32)
        xi_s[1] = jnp.zeros(xi_s.shape[1:], F32)

    def body(cur):
        prev = 1 - cur
        rows, width = tc * nb, d_ref.shape[1]
        ub = u_ref[...].reshape(rows, width).astype(BF16)
        kin, nout = bre_ref.shape[1], bre_ref.shape[2]
        for mblk in range(bre_ref.shape[0]):
            um = ub[:, mblk * kin:(mblk + 1) * kin]
            xr_s[cur, :, mblk * nout:(mblk + 1) * nout] = _dot(um, bre_ref[mblk])
            xi_s[cur, :, mblk * nout:(mblk + 1) * nout] = _dot(um, bim_ref[mblk])

        hr, hi = hr_ref[...], hi_ref[...]
        for t in range(tc):
            r = slice(t * nb, (t + 1) * nb)
            abr, abi = abr_ref[...], abi_ref[...]
            nhr = abr * hr - abi * hi + xr_s[cur, r, :]
            nhi = abr * hi + abi * hr + xi_s[cur, r, :]
            xr_s[cur, r, :] = nhr
            xi_s[cur, r, :] = nhi
            hr, hi = nhr, nhi
        live = i < n_chunks
        hr_ref[...] = jnp.where(live, hr, hr_ref[...])
        hi_ref[...] = jnp.where(live, hi, hi_ref[...])

        kout = crt_ref.shape[1]
        ys = []
        for j in range(crt_ref.shape[0]):
            hrj = xr_s[prev, :, j * kout:(j + 1) * kout].astype(BF16)
            hij = xi_s[prev, :, j * kout:(j + 1) * kout].astype(BF16)
            ys.append(_dot(hrj, crt_ref[j]) - _dot(hij, cit_ref[j]))
        y = jnp.concatenate(ys, axis=1) + d_ref[...] * up_ref[...].reshape(rows, width)
        g = jax.nn.gelu(y)
        out = g * jax.nn.sigmoid(_dot(g.astype(BF16), wg_ref[...]) + bg_ref[...])
        y_ref[...] = out.reshape(tc, nb, width)

    for par in (0, 1):
        pl.when(i % 2 == par)(functools.partial(body, par))


def _ssm(u3, h0r, h0i, p, tc):
    t_len, nb, width = u3.shape
    n_state = h0r.shape[1]
    n_chunks = t_len // tc
    assert n_chunks * tc == t_len
    st = jax.ShapeDtypeStruct((nb, n_state), F32)
    st_spec = pl.BlockSpec((nb, n_state), lambda i: (0, 0))
    chunk = lambda index: pl.BlockSpec((tc, nb, width), lambda i: (index(i), 0, 0))
    return pl.pallas_call(
        functools.partial(_ssm_kernel, tc=tc, nb=nb, n_chunks=n_chunks),
        grid=(n_chunks + 1,),
        in_specs=[chunk(lambda i: jnp.minimum(i, n_chunks - 1)), chunk(lambda i: jnp.maximum(i - 1, 0)),
                  _const_spec((nb, n_state)), _const_spec((nb, n_state)),
                  _const_spec((nb, n_state)), _const_spec((nb, n_state)),
                  _const_spec(p["bre"].shape), _const_spec(p["bim"].shape),
                  _const_spec(p["crt"].shape), _const_spec(p["cit"].shape),
                  _const_spec((1, width)), _const_spec((width, width)), _const_spec((1, width))],
        out_specs=(chunk(lambda i: jnp.maximum(i - 1, 0)), st_spec, st_spec),
        out_shape=(jax.ShapeDtypeStruct((t_len, nb, width), F32), st, st),
        scratch_shapes=[pltpu.VMEM((2, tc * nb, n_state), F32), pltpu.VMEM((2, tc * nb, n_state), F32)],
        compiler_params=pltpu.CompilerParams(
            dimension_semantics=("arbitrary",), vmem_limit_bytes=VMEM_LIMIT),
        name="ssm",
    )(u3, u3, h0r, h0i, jnp.broadcast_to(p["abr"], (nb, n_state)), jnp.broadcast_to(p["abi"], (nb, n_state)),
      p["bre"], p["bim"], p["crt"], p["cit"],
      p["d_skip"], p["w_glu"], p["b_glu"])


def _post_head(x_ref, att_ref, ssm_ref, wo_ref, g2_ref, y_ref, xn_s):
    tm, a = y_ref.shape[0], wo_ref.shape[0] // 2
    mix = jnp.concatenate([att_ref[...].reshape(tm, a), ssm_ref[...].reshape(tm, a).astype(BF16)], axis=1)
    xm = x_ref[...] + _dot(mix, wo_ref[...])
    y_ref[...] = xm
    r = lax.rsqrt(jnp.mean(xm * xm, axis=-1, keepdims=True) + EPS)
    xn_s[...] = ((xm * r) * g2_ref[...]).astype(BF16)


def _post_chunks(j0, j1, wgate_ref, wup_ref, cw_ref, cb_ref, xn_s, h_s, ap_s, hist_s, *, shift, fc):
    tm = xn_s.shape[0]
    off = ap_s.shape[1] - tm
    for j in range(j0, j1):
        cs = slice(j * fc, (j + 1) * fc)
        ap = ap_s.at[j % 2]
        gate = _dot(xn_s[...], wgate_ref[:, cs])
        up = _dot(xn_s[...], wup_ref[:, cs])
        ap[off - 2 * shift:off, :] = hist_s[:, cs]
        ap[off:off + tm, :] = gate
        conv = (cb_ref[:, cs] + cw_ref[0:1, cs] * ap[off - 2 * shift:off - 2 * shift + tm, :]
                + cw_ref[1:2, cs] * ap[off - shift:off - shift + tm, :] + cw_ref[2:3, cs] * gate)
        hist_s[:, cs] = ap[off + tm - 2 * shift:off + tm, :]
        h_s[:, cs] = (jax.nn.gelu(conv) * up).astype(BF16)


def _post_kernel(x_ref, att_ref, ssm_ref, *refs, shift, fc, n_lead):
    hist_src, (wo_ref, g2_ref, wgate_ref, wup_ref, cw_ref, cb_ref, wd_ref, y_ref, cst_ref,
               xn_s, h_s, ap_s, hist_s) = refs[:-13], refs[-13:]
    if n_lead:
        xl_ref, al_ref, sl_ref = hist_src
        lead_s = hist_s.at[1]
        hist_s = hist_s.at[0]

        @pl.when(jnp.logical_and(pl.program_id(0) == 0, pl.program_id(1) == 0))
        def _():
            mix = jnp.concatenate([al_ref[...], sl_ref[...].astype(BF16)], axis=1)
            xm = xl_ref[...] + _dot(mix, wo_ref[...])
            r = lax.rsqrt(jnp.mean(xm * xm, axis=-1, keepdims=True) + EPS)
            gate = _dot(((xm * r) * g2_ref[...]).astype(BF16), wgate_ref[...])
            lead_s[...] = gate[n_lead - 2 * shift:n_lead, :]

        first_hist = lead_s
    else:
        first_hist, = hist_src
        hist_s = hist_s.at[0]

    @pl.when(pl.program_id(1) == 0)
    def _():
        hist_s[...] = first_hist[...]

    _post_head(x_ref, att_ref, ssm_ref, wo_ref, g2_ref, y_ref, xn_s)
    _post_chunks(0, wgate_ref.shape[1] // fc, wgate_ref, wup_ref, cw_ref, cb_ref, xn_s, h_s, ap_s, hist_s,
                 shift=shift, fc=fc)
    y_ref[...] += _dot(h_s[...], wd_ref[...])
    cst_ref[...] = hist_s[...]


def _post(x2d, att, ssm, hist, p, tm, nb, shift, att_spec, ssm_spec, lead=None, fc=256):
    rows, d = x2d.shape
    d_ff = p["w_gate"].shape[1]
    per_b = rows // nb
    nt = per_b // tm
    assert nt * tm == per_b and d_ff % fc == 0 and tm >= 2 * shift
    off = -(-2 * shift // 8) * 8
    row_spec = lambda w: pl.BlockSpec((tm, w), lambda b, t: (b * nt + t, 0))
    hist_spec = pl.BlockSpec((None, 2 * shift, d_ff), lambda b, t: (b, 0, 0))
    if lead is None:
        hist_src, hist_specs, n_lead = (hist,), [hist_spec], 0
    else:
        hist_src, hist_specs, n_lead = lead, [_const_spec(z.shape) for z in lead], lead[0].shape[0]
        assert n_lead >= 2 * shift
    return pl.pallas_call(
        functools.partial(_post_kernel, shift=shift, fc=fc, n_lead=n_lead),
        grid=(nb, nt),
        in_specs=[row_spec(d), att_spec, ssm_spec, *hist_specs,
                  _const_spec(p["w_out"].shape), _const_spec((1, d)),
                  _const_spec(p["w_gate"].shape), _const_spec(p["w_up"].shape),
                  _const_spec((3, d_ff)), _const_spec((1, d_ff)), _const_spec(p["w_down"].shape)],
        out_specs=(row_spec(d), hist_spec),
        out_shape=(jax.ShapeDtypeStruct((rows, d), F32), jax.ShapeDtypeStruct((nb, 2 * shift, d_ff), F32)),
        scratch_shapes=[pltpu.VMEM((tm, d), BF16), pltpu.VMEM((tm, d_ff), BF16),
                        pltpu.VMEM((2, off + tm, fc), F32), pltpu.VMEM((2, 2 * shift, d_ff), F32)],
        compiler_params=pltpu.CompilerParams(
            dimension_semantics=("arbitrary", "arbitrary"), vmem_limit_bytes=VMEM_LIMIT),
        name="post",
    )(x2d, att, ssm, *hist_src, p["w_out"], p["norm2"], p["w_gate"], p["w_up"], p["conv_w"], p["conv_b"],
      p["w_down"])


def _prepare_params(norm1, w_in, q_norm, k_norm, lam_q, lam_k, sub_norm, ssm_a_re, ssm_a_im, ssm_log_dt,
                    ssm_b_re, ssm_b_im, ssm_c_re, ssm_c_im, ssm_d, w_glu, b_glu, w_out, norm2, w_gate,
                    w_up, ffn_conv_w, ffn_conv_b, w_down):
    l = 0
    g, pdim = ssm_a_re[l].shape
    c = SSM_GROUP
    a = N_HEADS * HEAD_DIM
    abr, abi, bbr, bbi = _ssm_params(ssm_a_re[l], ssm_a_im[l], ssm_log_dt[l], ssm_b_re[l], ssm_b_im[l])
    gi = LANES // c

    def in_blocks(bb):
        bb = jnp.transpose(bb, (1, 0, 2)).reshape(g // gi, gi, c, 1, pdim)
        same = jnp.eye(gi, dtype=bool)[None, :, None, :, None]
        return jnp.where(same, bb, 0.0).astype(BF16).reshape(g // gi, gi * c, gi * pdim)

    go = MXU // c

    def out_blocks(cc):
        cc = jnp.transpose(cc.reshape(g // go, go, c, pdim), (0, 1, 3, 2))[:, :, :, None, :]
        same = jnp.eye(go, dtype=bool)[None, :, None, :, None]
        return jnp.where(same, cc, 0.0).astype(BF16).reshape(g // go, go * pdim, go * c)

    comp = jnp.arange(MXU) // QK_DIM
    ones_blk = (comp[:, None] == comp[None, :]).astype(BF16) * (1.0 / QK_DIM)
    return {
        "norm1": norm1[l].reshape(1, -1), "w_in": w_in[l],
        "q_gain": jnp.tile(q_norm[l].reshape(-1), N_HEADS).reshape(1, a),
        "k_gain": jnp.tile(k_norm[l].reshape(-1), N_HEADS).reshape(1, a),
        "ones_blk": ones_blk.astype(BF16),
        "lam_q": lam_q[l], "lam_k": lam_k[l],
        "sub_gain": (sub_norm[l] * (1.0 - LAM_INIT)).reshape(1, HEAD_DIM),
        "abr": abr.reshape(1, g * pdim), "abi": abi.reshape(1, g * pdim),
        "bre": in_blocks(bbr), "bim": in_blocks(bbi),
        "crt": out_blocks(ssm_c_re[l]), "cit": out_blocks(ssm_c_im[l]),
        "d_skip": ssm_d[l].reshape(1, g * c), "w_glu": w_glu[l].astype(BF16), "b_glu": b_glu[l].reshape(1, -1),
        "w_out": w_out[l].astype(BF16), "norm2": norm2[l].reshape(1, -1),
        "w_gate": w_gate[l].astype(BF16), "w_up": w_up[l].astype(BF16),
        "conv_w": ffn_conv_w[l], "conv_b": ffn_conv_b[l].reshape(1, -1), "w_down": w_down[l].astype(BF16),
    }


def _row_tile(length, cap=768):
    best = None
    for t in range(16, cap + 1, 16):
        if length % t == 0:
            best = t
    assert best is not None
    return best


def kernel(x_prompt, x_sample, cache_k, cache_v, state_ssm_re, state_ssm_im, state_ffn_conv, page_table, meta_tokens, norm1, w_in, q_norm, k_norm, lam_q, lam_k, sub_norm, ssm_a_re, ssm_a_im, ssm_log_dt, ssm_b_re, ssm_b_im, ssm_c_re, ssm_c_im, ssm_d, w_glu, b_glu, w_out, norm2, w_gate, w_up, ffn_conv_w, ffn_conv_b, w_down):
    assert norm1.shape[0] == 1, "single-layer stack"
    p = _prepare_params(norm1, w_in, q_norm, k_norm, lam_q, lam_k, sub_norm, ssm_a_re, ssm_a_im, ssm_log_dt,
                        ssm_b_re, ssm_b_im, ssm_c_re, ssm_c_im, ssm_d, w_glu, b_glu, w_out, norm2, w_gate,
                        w_up, ffn_conv_w, ffn_conv_b, w_down)
    nb, seq, d = x_prompt.shape
    db, t_new, _ = x_sample.shape
    a = N_HEADS * HEAD_DIM
    g, pdim = ssm_a_re.shape[1:]
    n_state = g * pdim
    d_ff = w_gate.shape[-1]
    length = seq + N_META

    meta = meta_tokens.astype(x_prompt.dtype)
    tm = _row_tile(length)
    qb, k, kb, v, vb, u = _project(x_prompt, meta, tm, p, u_time_major=True)
    att = _attn_prompt(qb.reshape(nb, length, a), kb.reshape(nb, length, a), vb.reshape(nb, length, a), p)
    zst = jnp.zeros((nb, n_state), F32)
    tc = _row_tile(length * nb, cap=768) // nb
    ys, hr, hi = _ssm(u.reshape(length, nb, a), zst, zst, p, tc)
    ys2 = ys.reshape(length, nb * a)
    tp = _row_tile(seq, cap=512)
    att_spec = pl.BlockSpec((pl.Element(1), pl.Element(tp), pl.Element(a)),
                            lambda b, t: (b, pl.multiple_of(N_META + t * tp, N_META), 0))
    ssm_spec = pl.BlockSpec((pl.Element(tp), pl.Element(a)),
                            lambda b, t: (pl.multiple_of(N_META + t * tp, N_META), pl.multiple_of(b * a, a)))
    yp, cst_p = _post(x_prompt.reshape(nb * seq, d), att, ys2, None, p, tp, nb, shift=1,
                      att_spec=att_spec, ssm_spec=ssm_spec, lead=(meta, att[0, :N_META], ys2[:N_META, :a]))
    y_prompt = yp.reshape(nb, seq, d)
    k_prompt = k.reshape(1, nb, length, N_HEADS, HEAD_DIM)
    v_prompt = v.reshape(1, nb, length, N_HEADS, HEAD_DIM)
    ssm_re_p = hr.reshape(1, nb, g, pdim)
    ssm_im_p = hi.reshape(1, nb, g, pdim)
    conv_p = cst_p[None]

    rows_s = db * t_new
    xs2 = jnp.transpose(x_sample, (1, 0, 2)).reshape(rows_s, d)
    qb, k, kb, v, vb, u = _project(xs2, None, rows_s, p, u_time_major=False)
    to_bm = lambda z: jnp.transpose(z.reshape(t_new, db, a), (1, 0, 2))
    q_hct = jnp.transpose(qb.reshape(t_new, db, N_HEADS, 1, HEAD_DIM), (1, 2, 3, 0, 4))
    q_rep = jnp.broadcast_to(q_hct, (db, N_HEADS, 2, t_new, HEAD_DIM)).reshape(db, N_HEADS * 2 * t_new, HEAD_DIM)
    n_pool, page = cache_k.shape[1:3]
    att = _attn_decode(q_rep, to_bm(kb), to_bm(vb), cache_k[0].reshape(n_pool, page * N_HEADS, HEAD_DIM),
                       cache_v[0].reshape(n_pool, page * N_HEADS, HEAD_DIM), page_table, p)
    att_tm = jnp.transpose(att, (1, 0, 2)).reshape(rows_s, a)
    ys, hr, hi = _ssm(u.reshape(t_new, db, a), state_ssm_re[0].reshape(db, n_state),
                      state_ssm_im[0].reshape(db, n_state), p, t_new)
    hist = jnp.transpose(state_ffn_conv[0], (1, 0, 2)).reshape(1, 2 * db, d_ff)
    whole = pl.BlockSpec((rows_s, a), lambda b, t: (0, 0))
    ysm, cst_s = _post(xs2, att_tm, ys.reshape(rows_s, a), hist, p, rows_s, 1, shift=db,
                       att_spec=whole, ssm_spec=whole)
    y_sample = jnp.transpose(ysm.reshape(t_new, db, d), (1, 0, 2))
    k_sample = jnp.transpose(k.reshape(t_new, db, N_HEADS, HEAD_DIM), (1, 0, 2, 3))[None]
    v_sample = jnp.transpose(v.reshape(t_new, db, N_HEADS, HEAD_DIM), (1, 0, 2, 3))[None]
    ssm_re_s = hr.reshape(1, db, g, pdim)
    ssm_im_s = hi.reshape(1, db, g, pdim)
    conv_s = jnp.transpose(cst_s.reshape(2, db, d_ff), (1, 0, 2))[None]

    return (y_prompt, y_sample, k_prompt, v_prompt, k_sample, v_sample,
            ssm_re_p, ssm_im_p, ssm_re_s, ssm_im_s, conv_p, conv_s)
```

```python
import functools
import math

import jax
import jax.numpy as jnp
from jax import lax
from jax.experimental import pallas as pl
from jax.experimental.pallas import tpu as pltpu

N_META = 16
N_HEADS = 4
QK_DIM = 64
HEAD_DIM = 2 * QK_DIM
SSM_GROUP = 16
SSM_STATE = 64
EPS = 1e-6
NEG = -1e30
LAM_INIT = 0.8 - 0.6 * math.exp(-0.3 * 0)
LOG2E = math.log2(math.e)

LANES = 128
MXU = 256
VMEM_LIMIT = 56 * 1024 * 1024

F32 = jnp.float32
BF16 = jnp.bfloat16


def _dot(a, b):
    return jnp.dot(a, b, preferred_element_type=F32)


def _dot_nt(a, b):
    return lax.dot_general(a, b, (((1,), (1,)), ((), ())), preferred_element_type=F32)


def _const_spec(shape):
    nd = len(shape)
    return pl.BlockSpec(shape, lambda *_: (0,) * nd, pipeline_mode=pl.Buffered(1))


def _lam(lq_ref, lk_ref):
    e = jnp.exp(jnp.sum(lq_ref[...] * lk_ref[...], axis=1, keepdims=True))
    return e[0:1] - e[1:2] + LAM_INIT


def _proj_kernel(x_ref, meta_ref, g1_ref, w_ref, qg_ref, kg_ref, ones_ref,
                 qb_ref, k_ref, kb_ref, v_ref, vb_ref, u_ref, wb_s, *, n_meta):
    outs = (qb_ref, k_ref, kb_ref, v_ref, vb_ref, u_ref)
    consts = (g1_ref, wb_s, qg_ref, kg_ref, ones_ref)
    tm = qb_ref.shape[0]

    @pl.when(jnp.logical_and(pl.program_id(0) == 0, pl.program_id(1) == 0))
    def _():
        wb_s[...] = w_ref[...].astype(BF16)

    if not n_meta:
        _proj_body(x_ref[...], consts, outs)
        return
    t = pl.program_id(1)

    @pl.when(t == 0)
    def _():
        _proj_body(jnp.concatenate([meta_ref[...], x_ref[0, 0:tm - n_meta, :]], axis=0), consts, outs)

    @pl.when(t != 0)
    def _():
        _proj_body(x_ref[0], consts, outs)


def _proj_body(x, consts, outs):
    g1_ref, w_ref, qg_ref, kg_ref, ones_ref = consts
    qb_ref, k_ref, kb_ref, v_ref, vb_ref, u_ref = outs
    a = N_HEADS * HEAD_DIM
    r = lax.rsqrt(jnp.mean(x * x, axis=-1, keepdims=True) + EPS)
    xn = ((x * r) * g1_ref[...]).astype(BF16)
    proj = _dot(xn, w_ref[...])

    def comp_norm(z, g):
        z2 = (z * z).astype(BF16)
        ms = jnp.concatenate(
            [_dot(z2[:, j * MXU:(j + 1) * MXU], ones_ref[...]) for j in range(a // MXU)], axis=1)
        return (z * lax.rsqrt(ms + EPS)) * g

    qn = comp_norm(proj[:, :a], qg_ref[...])
    kn = comp_norm(proj[:, a:2 * a], kg_ref[...])
    v = proj[:, 2 * a:3 * a]
    qb_ref[...] = (qn * (QK_DIM ** -0.5 * LOG2E)).astype(BF16)
    kb_ref[...] = kn.astype(BF16)
    vb_ref[...] = v.astype(BF16)
    u_ref[...] = proj[:, 3 * a:]
    tm = x.shape[0]
    for h in range(N_HEADS):
        k_ref[pl.ds(h, tm, stride=N_HEADS), :] = kn[:, h * HEAD_DIM:(h + 1) * HEAD_DIM]
        v_ref[pl.ds(h, tm, stride=N_HEADS), :] = v[:, h * HEAD_DIM:(h + 1) * HEAD_DIM]


def _project(x, meta, tm, p, u_time_major):
    a = N_HEADS * HEAD_DIM
    d = x.shape[-1]
    if meta is None:
        nb, n_meta = 1, 0
        per_b = x.shape[0]
        x_spec = pl.BlockSpec((tm, d), lambda b, t: (t, 0))
        meta = jnp.zeros((8, d), x.dtype)
    else:
        nb, n_meta = x.shape[0], meta.shape[0]
        per_b = n_meta + x.shape[1]
        x_spec = pl.BlockSpec(
            (pl.Element(1), pl.Element(tm), pl.Element(d)),
            lambda b, t: (b, pl.multiple_of(jnp.maximum(t * tm - n_meta, 0), 8), 0))
    rows = nb * per_b
    nt = per_b // tm
    assert nt * tm == per_b and n_meta % 8 == 0
    grid = (nb, nt)
    row_spec = lambda w: pl.BlockSpec((tm, w), lambda b, t: (b * nt + t, 0))
    head_spec = pl.BlockSpec((tm * N_HEADS, HEAD_DIM), lambda b, t: (b * nt + t, 0))
    if u_time_major:
        u_shape = jax.ShapeDtypeStruct((per_b, nb * a), F32)
        u_spec = pl.BlockSpec((tm, a), lambda b, t: (t, b))
    else:
        u_shape = jax.ShapeDtypeStruct((rows, a), F32)
        u_spec = row_spec(a)
    out_shape = (jax.ShapeDtypeStruct((rows, a), BF16), jax.ShapeDtypeStruct((rows * N_HEADS, HEAD_DIM), F32),
                 jax.ShapeDtypeStruct((rows, a), BF16), jax.ShapeDtypeStruct((rows * N_HEADS, HEAD_DIM), F32),
                 jax.ShapeDtypeStruct((rows, a), BF16), u_shape)
    return pl.pallas_call(
        functools.partial(_proj_kernel, n_meta=n_meta),
        grid=grid,
        in_specs=[x_spec, _const_spec(meta.shape), _const_spec((1, d)), _const_spec(p["w_in"].shape),
                  _const_spec((1, a)), _const_spec((1, a)), _const_spec((MXU, MXU))],
        out_specs=(row_spec(a), head_spec, row_spec(a), head_spec, row_spec(a), u_spec),
        out_shape=out_shape,
        scratch_shapes=[pltpu.VMEM(p["w_in"].shape, BF16)],
        compiler_params=pltpu.CompilerParams(
            dimension_semantics=("arbitrary", "arbitrary"), vmem_limit_bytes=VMEM_LIMIT),
        name="proj",
    )(x, meta, p["norm1"], p["w_in"], p["q_gain"], p["k_gain"], p["ones_blk"])


def _attn_prompt_kernel(q_ref, k_ref, v_ref, lq_ref, lk_ref, sg_ref, o_ref, acc_s, m_s, s_s, *, tq, n_tiles):
    lam = _lam(lq_ref, lk_ref)
    sg = sg_ref[...]
    lo = lax.broadcasted_iota(jnp.int32, (1, HEAD_DIM), 1) < QK_DIM
    n_ones = acc_s.shape[1] - HEAD_DIM

    def hcols(h):
        return slice(h * HEAD_DIM, (h + 1) * HEAD_DIM)

    def stack_q(q):
        z = jnp.zeros_like(q)
        return jnp.concatenate([jnp.where(lo, q, z), jnp.where(lo, z, q)], axis=0)

    def init(h, t2):
        m_s[h, :, 0:t2] = jnp.full((1, t2), NEG, F32)
        acc_s[h, :, 0:t2] = jnp.zeros((HEAD_DIM + n_ones, t2), F32)

    def scores(slot, h, q2, kt, mask):
        s = _dot_nt(kt, q2)
        if mask is not None:
            s = jnp.where(mask, s, NEG)
        s_s[slot, h, 0:kt.shape[0], 0:q2.shape[0]] = s

    def absorb(slot, h, nk, t2, vt):
        s = s_s[slot, h, 0:nk, 0:t2]
        m_prev = m_s[h, :, 0:t2]
        m_new = jnp.maximum(m_prev, jnp.max(s, axis=0, keepdims=True))
        alpha = jnp.exp2(m_prev - m_new)
        pr = jnp.exp2(s - m_new).astype(BF16)
        va = jnp.concatenate([vt.T, jnp.ones((n_ones, nk), BF16)], axis=0)
        acc_s[h, :, 0:t2] = alpha * acc_s[h, :, 0:t2] + _dot(va, pr)
        m_s[h, :, 0:t2] = m_new

    def finish(h, t):
        acc = acc_s[h, :, 0:2 * t]
        on = acc[0:HEAD_DIM] / acc[HEAD_DIM:HEAD_DIM + 1]
        o = on[:, 0:t] - lam * on[:, t:2 * t]
        r = lax.rsqrt(jnp.mean(o * o, axis=0, keepdims=True) + EPS)
        return ((o * r).T * sg).astype(o_ref.dtype)

    def causal_mask(t, nk, offset):
        qi = lax.broadcasted_iota(jnp.int32, (nk, 2 * t), 1)
        qi = jnp.where(qi >= t, qi - t, qi)
        ki = lax.broadcasted_iota(jnp.int32, (nk, 2 * t), 0)
        return ki <= qi + offset

    tmq = LANES
    for h in range(N_HEADS):
        init(h, 2 * tmq)
        scores(0, h, stack_q(q_ref[0:tmq, hcols(h)]), k_ref[0:tmq, hcols(h)], causal_mask(tmq, tmq, 0))
    for h in range(N_HEADS):
        absorb(0, h, tmq, 2 * tmq, v_ref[0:tmq, hcols(h)])
        o_ref[0:N_META, hcols(h)] = finish(h, tmq)[0:N_META]

    dk = tq + N_META

    def q_tile(j, carry):
        qs = pl.multiple_of(N_META + j * tq, N_META)
        q2 = [stack_q(q_ref[pl.ds(qs, tq), hcols(h)]) for h in range(N_HEADS)]
        ds = pl.multiple_of(j * tq, tq)
        dmask = causal_mask(tq, dk, N_META)
        for h in range(N_HEADS):
            init(h, 2 * tq)

        def plain_scores(slot, i):
            ks = pl.multiple_of(i * tq, tq)
            for h in range(N_HEADS):
                scores(slot, h, q2[h], k_ref[pl.ds(ks, tq), hcols(h)], None)

        def diag_scores(slot):
            for h in range(N_HEADS):
                scores(slot, h, q2[h], k_ref[pl.ds(ds, dk), hcols(h)], dmask)

        def plain_absorb(slot, i):
            ks = pl.multiple_of(i * tq, tq)
            for h in range(N_HEADS):
                absorb(slot, h, tq, 2 * tq, v_ref[pl.ds(ks, tq), hcols(h)])

        def diag_absorb(slot):
            for h in range(N_HEADS):
                absorb(slot, h, dk, 2 * tq, v_ref[pl.ds(ds, dk), hcols(h)])
                o_ref[pl.ds(qs, tq), hcols(h)] = finish(h, tq)

        @pl.when(j == 0)
        def _():
            diag_scores(0)
            diag_absorb(0)

        @pl.when(j > 0)
        def _():
            plain_scores(0, 0)

        n_pairs = (j - 1) // 2

        def k_pair(pi, c):
            i = 2 * pi
            plain_scores(1, i + 1)
            plain_absorb(0, i)
            plain_scores(0, i + 2)
            plain_absorb(1, i + 1)
            return c

        lax.fori_loop(0, n_pairs, k_pair, 0)

        @pl.when(jnp.logical_and(j > 0, j % 2 == 1))
        def _():
            diag_scores(1)
            plain_absorb(0, j - 1)
            diag_absorb(1)

        @pl.when(jnp.logical_and(j > 0, j % 2 == 0))
        def _():
            plain_scores(1, j - 1)
            plain_absorb(0, j - 2)
            diag_scores(0)
            plain_absorb(1, j - 1)
            diag_absorb(0)

        return carry

    lax.fori_loop(0, n_tiles, q_tile, 0)


def _attn_prompt(qb, kb, vb, p, tq=256):
    nb, length, a = qb.shape
    n_tiles = (length - N_META) // tq
    assert N_META + n_tiles * tq == length and tq % LANES == 0
    spec = pl.BlockSpec((None, length, a), lambda b: (b, 0, 0))
    return pl.pallas_call(
        functools.partial(_attn_prompt_kernel, tq=tq, n_tiles=n_tiles),
        grid=(nb,),
        in_specs=[spec, spec, spec, _const_spec((2, QK_DIM)), _const_spec((2, QK_DIM)),
                  _const_spec((1, HEAD_DIM))],
        out_specs=spec,
        out_shape=jax.ShapeDtypeStruct((nb, length, a), BF16),
        scratch_shapes=[pltpu.VMEM((N_HEADS, HEAD_DIM + 16, 2 * tq), F32), pltpu.VMEM((N_HEADS, 1, 2 * tq), F32),
                        pltpu.VMEM((2, N_HEADS, tq + N_META, 2 * tq), F32)],
        compiler_params=pltpu.CompilerParams(
            dimension_semantics=("arbitrary",), vmem_limit_bytes=VMEM_LIMIT),
        name="attn_prompt",
    )(qb, kb, vb, p["lam_q"], p["lam_k"], p["sub_gain"])


def _decode_entry(q, kn, vn, kp, vp, lam, sg, o_ref):
    t_new = kn.shape[0]
    grp = 2 * t_new
    nrow = N_HEADS * grp
    ncol = kp[0].shape[0]
    row = lax.broadcasted_iota(jnp.int32, (nrow, HEAD_DIM), 0)
    lane = lax.broadcasted_iota(jnp.int32, (nrow, HEAD_DIM), 1)
    comp_ok = (lane >= QK_DIM) == ((row % grp) >= t_new)
    t_of_row = lax.broadcasted_iota(jnp.int32, (nrow, 1), 0) % t_new
    head_ok = (lax.broadcasted_iota(jnp.int32, (nrow, ncol), 1) % N_HEADS
               == lax.broadcasted_iota(jnp.int32, (nrow, ncol), 0) // grp)

    def per_row_head(z, t):
        return jnp.concatenate(
            [jnp.broadcast_to(z[t:t + 1, h * HEAD_DIM:(h + 1) * HEAD_DIM], (grp, HEAD_DIM)) for h in range(N_HEADS)],
            axis=0)

    q32 = jnp.where(comp_ok, q.astype(F32), 0.0)
    qb = q32.astype(BF16)
    kn, vn = kn.astype(F32), vn.astype(F32)
    s_pages = [jnp.where(head_ok, _dot_nt(qb, k_ref[...].astype(BF16)), NEG) for k_ref in kp]
    s_new = []
    for t in range(t_new):
        sc = jnp.sum(q32 * per_row_head(kn, t), axis=1, keepdims=True)
        s_new.append(jnp.where(t_of_row >= t, sc, NEG))

    m = s_pages[0]
    for s in s_pages[1:]:
        m = jnp.maximum(m, s)
    m = jnp.max(m, axis=1, keepdims=True)
    for s in s_new:
        m = jnp.maximum(m, s)

    acc = jnp.zeros((nrow, HEAD_DIM), F32)
    psum = None
    for s, v_ref in zip(s_pages, vp):
        pr = jnp.exp2(s - m)
        psum = pr if psum is None else psum + pr
        acc = acc + _dot(pr.astype(BF16), v_ref[...].astype(BF16))
    lsum = jnp.sum(psum, axis=1, keepdims=True)
    for t in range(t_new):
        pr = jnp.exp2(s_new[t] - m)
        lsum = lsum + pr
        acc = acc + pr * per_row_head(vn, t)
    on = acc / lsum
    for h in range(N_HEADS):
        r0 = h * grp
        o = on[r0:r0 + t_new] - lam * on[r0 + t_new:r0 + grp]
        r = lax.rsqrt(jnp.mean(o * o, axis=-1, keepdims=True) + EPS)
        o_ref[:, h * HEAD_DIM:(h + 1) * HEAD_DIM] = ((o * r) * sg).astype(o_ref.dtype)


def _attn_decode_kernel(pt_ref, q_ref, kn_ref, vn_ref, lq_ref, lk_ref, sg_ref, ck_hbm, cv_hbm, o_ref,
                        kbuf, vbuf, sem, *, n_pages, ahead):
    g, n = pl.program_id(0), pl.num_programs(0)
    ring = kbuf.shape[0]

    def page_copies(slot, page_of):
        return [pltpu.make_async_copy(hbm.at[page_of(pg)], buf.at[slot, pg], sem.at[w, slot])
                for pg in range(n_pages) for w, (hbm, buf) in enumerate(((ck_hbm, kbuf), (cv_hbm, vbuf)))]

    def start(entry):
        for c in page_copies(entry % ring, lambda pg: pt_ref[entry * n_pages + pg]):
            c.start()

    @pl.when(g == 0)
    def _():
        for entry in range(ahead):
            start(entry)

    @pl.when(g + ahead < n)
    def _():
        start(g + ahead)

    slot = g % ring
    for c in page_copies(slot, lambda pg: 0):
        c.wait()
    _decode_entry(q_ref[...], kn_ref[...], vn_ref[...], [kbuf.at[slot, pg] for pg in range(n_pages)],
                  [vbuf.at[slot, pg] for pg in range(n_pages)], _lam(lq_ref, lk_ref), sg_ref[...], o_ref)


def _attn_decode(q_rep, kn, vn, cache_k, cache_v, page_table, p, ahead=2):
    db, nrow, _ = q_rep.shape
    t_new, a = kn.shape[1:]
    n_pages = page_table.shape[1]
    prow = cache_k.shape[1]
    assert db >= ahead
    const = lambda shape: pl.BlockSpec(shape, lambda g, pt_ref: (0,) * len(shape))
    per_entry = lambda r, w: pl.BlockSpec((None, r, w), lambda g, pt_ref: (g, 0, 0))
    grid_spec = pltpu.PrefetchScalarGridSpec(
        num_scalar_prefetch=1,
        grid=(db,),
        in_specs=[per_entry(nrow, HEAD_DIM), per_entry(t_new, a), per_entry(t_new, a),
                  const((2, QK_DIM)), const((2, QK_DIM)), const((1, HEAD_DIM)),
                  pl.BlockSpec(memory_space=pl.ANY), pl.BlockSpec(memory_space=pl.ANY)],
        out_specs=per_entry(t_new, a),
        scratch_shapes=[pltpu.VMEM((ahead + 1, n_pages, prow, HEAD_DIM), cache_k.dtype),
                        pltpu.VMEM((ahead + 1, n_pages, prow, HEAD_DIM), cache_v.dtype),
                        pltpu.SemaphoreType.DMA((2, ahead + 1))],
    )
    return pl.pallas_call(
        functools.partial(_attn_decode_kernel, n_pages=n_pages, ahead=ahead),
        grid_spec=grid_spec,
        out_shape=jax.ShapeDtypeStruct((db, t_new, a), BF16),
        compiler_params=pltpu.CompilerParams(
            dimension_semantics=("arbitrary",), vmem_limit_bytes=VMEM_LIMIT),
        name="attn_decode",
    )(page_table.reshape(-1), q_rep, kn, vn, p["lam_q"], p["lam_k"], p["sub_gain"], cache_k, cache_v)


def _ssm_param_kernel(ar_ref, ai_ref, ldt_ref, br_ref, bi_ref, abr_ref, abi_ref, bbr_ref, bbi_ref):
    ar, ai = ar_ref[...], ai_ref[...]
    dt = jnp.exp(ldt_ref[...])
    mag = jnp.exp(ar * dt)
    abr, abi = mag * jnp.cos(ai * dt), mag * jnp.sin(ai * dt)
    den = ar * ar + ai * ai
    nr, ni = abr - 1.0, abi
    gr, gi = (nr * ar + ni * ai) / den, (ni * ar - nr * ai) / den
    abr_ref[...] = abr
    abi_ref[...] = abi
    for c in range(SSM_GROUP):
        br, bi = br_ref[c], bi_ref[c]
        bbr_ref[c] = gr * br - gi * bi
        bbi_ref[c] = gr * bi + gi * br


def _ssm_params(a_re, a_im, log_dt, b_re, b_im):
    g, pdim = a_re.shape
    c = b_re.shape[-1]
    b_re_t = jnp.transpose(b_re, (2, 0, 1))
    b_im_t = jnp.transpose(b_im, (2, 0, 1))
    gp = jax.ShapeDtypeStruct((g, pdim), F32)
    cgp = jax.ShapeDtypeStruct((c, g, pdim), F32)
    return pl.pallas_call(_ssm_param_kernel, out_shape=(gp, gp, cgp, cgp), name="ssm_params")(
        a_re, a_im, log_dt.reshape(g, 1), b_re_t, b_im_t)


def _ssm_kernel(u_ref, up_ref, h0r_ref, h0i_ref, abr_ref, abi_ref, bre_ref, bim_ref, crt_ref, cit_ref,
                d_ref, wg_ref, bg_ref, y_ref, hr_ref, hi_ref, xr_s, xi_s, *, tc, nb, n_chunks):
    i = pl.program_id(0)

    @pl.when(i == 0)
    def _():
        hr_ref[...] = h0r_ref[...]
        hi_ref[...] = h0i_ref[...]
        if n_chunks > 1:
            xr_s[1] = jnp.zeros(xr_s.shape[1:], F32)
            xi_s[1] = jnp.zeros(xi_s.shape[1:], F32)

    def body(cur, prev):
        rows, width = tc * nb, d_ref.shape[1]
        ub = u_ref[...].reshape(rows, width).astype(BF16)
        kin, nout = bre_ref.shape[1], bre_ref.shape[2]
        for mblk in range(bre_ref.shape[0]):
            um = ub[:, mblk * kin:(mblk + 1) * kin]
            xr_s[cur, :, mblk * nout:(mblk + 1) * nout] = _dot(um, bre_ref[mblk])
            xi_s[cur, :, mblk * nout:(mblk + 1) * nout] = _dot(um, bim_ref[mblk])

        hr, hi = hr_ref[...], hi_ref[...]
        for t in range(tc):
            r = slice(t * nb, (t + 1) * nb)
            abr, abi = abr_ref[...], abi_ref[...]
            nhr = abr * hr - abi * hi + xr_s[cur, r, :]
            nhi = abr * hi + abi * hr + xi_s[cur, r, :]
            xr_s[cur, r, :] = nhr
            xi_s[cur, r, :] = nhi
            hr, hi = nhr, nhi
        live = i < n_chunks
        hr_ref[...] = jnp.where(live, hr, hr_ref[...])
        hi_ref[...] = jnp.where(live, hi, hi_ref[...])

        kout = crt_ref.shape[1]
        ys = []
        for j in range(crt_ref.shape[0]):
            hrj = xr_s[prev, :, j * kout:(j + 1) * kout].astype(BF16)
            hij = xi_s[prev, :, j * kout:(j + 1) * kout].astype(BF16)
            ys.append(_dot(hrj, crt_ref[j]) - _dot(hij, cit_ref[j]))
        y = jnp.concatenate(ys, axis=1) + d_ref[...] * up_ref[...].reshape(rows, width)
        g = jax.nn.gelu(y)
        out = g * jax.nn.sigmoid(_dot(g.astype(BF16), wg_ref[...]) + bg_ref[...])
        y_ref[...] = out.reshape(tc, nb, width)

    if n_chunks == 1:
        body(0, 0)
    else:
        for par in (0, 1):
            pl.when(i % 2 == par)(functools.partial(body, par, 1 - par))


def _ssm(u3, h0r, h0i, p, tc):
    t_len, nb, width = u3.shape
    n_state = h0r.shape[1]
    n_chunks = t_len // tc
    assert n_chunks * tc == t_len
    st = jax.ShapeDtypeStruct((nb, n_state), F32)
    st_spec = pl.BlockSpec((nb, n_state), lambda i: (0, 0))
    chunk = lambda index: pl.BlockSpec((tc, nb, width), lambda i: (index(i), 0, 0))
    return pl.pallas_call(
        functools.partial(_ssm_kernel, tc=tc, nb=nb, n_chunks=n_chunks),
        grid=(n_chunks + 1 if n_chunks > 1 else 1,),
        in_specs=[chunk(lambda i: jnp.minimum(i, n_chunks - 1)), chunk(lambda i: jnp.maximum(i - 1, 0)),
                  _const_spec((nb, n_state)), _const_spec((nb, n_state)),
                  _const_spec((nb, n_state)), _const_spec((nb, n_state)),
                  _const_spec(p["bre"].shape), _const_spec(p["bim"].shape),
                  _const_spec(p["crt"].shape), _const_spec(p["cit"].shape),
                  _const_spec((1, width)), _const_spec((width, width)), _const_spec((1, width))],
        out_specs=(chunk(lambda i: jnp.maximum(i - 1, 0)), st_spec, st_spec),
        out_shape=(jax.ShapeDtypeStruct((t_len, nb, width), F32), st, st),
        scratch_shapes=[pltpu.VMEM((min(n_chunks, 2), tc * nb, n_state), F32)] * 2,
        compiler_params=pltpu.CompilerParams(
            dimension_semantics=("arbitrary",), vmem_limit_bytes=VMEM_LIMIT),
        name="ssm",
    )(u3, u3, h0r, h0i, jnp.broadcast_to(p["abr"], (nb, n_state)), jnp.broadcast_to(p["abi"], (nb, n_state)),
      p["bre"], p["bim"], p["crt"], p["cit"],
      p["d_skip"], p["w_glu"], p["b_glu"])


def _post_head(x_ref, att_ref, ssm_ref, wo_ref, g2_ref, y_ref, xn_s):
    tm, a = y_ref.shape[0], wo_ref.shape[0] // 2
    mix = jnp.concatenate([att_ref[...].reshape(tm, a), ssm_ref[...].reshape(tm, a).astype(BF16)], axis=1)
    xm = x_ref[...] + _dot(mix, wo_ref[...])
    y_ref[...] = xm
    r = lax.rsqrt(jnp.mean(xm * xm, axis=-1, keepdims=True) + EPS)
    xn_s[...] = ((xm * r) * g2_ref[...]).astype(BF16)


def _post_chunks(j0, j1, wgate_ref, wup_ref, cw_ref, cb_ref, xn_s, h_s, ap_s, hist_s, *, shift, fc):
    tm = xn_s.shape[0]
    off = ap_s.shape[1] - tm
    for j in range(j0, j1):
        cs = slice(j * fc, (j + 1) * fc)
        ap = ap_s.at[j % 2]
        gate = _dot(xn_s[...], wgate_ref[:, cs])
        up = _dot(xn_s[...], wup_ref[:, cs])
        ap[off - 2 * shift:off, :] = hist_s[:, cs]
        ap[off:off + tm, :] = gate
        conv = (cb_ref[:, cs] + cw_ref[0:1, cs] * ap[off - 2 * shift:off - 2 * shift + tm, :]
                + cw_ref[1:2, cs] * ap[off - shift:off - shift + tm, :] + cw_ref[2:3, cs] * gate)
        hist_s[:, cs] = ap[off + tm - 2 * shift:off + tm, :]
        h_s[:, cs] = (jax.nn.gelu(conv) * up).astype(BF16)


def _post_kernel(x_ref, att_ref, ssm_ref, *refs, shift, fc, n_lead):
    hist_src, (wo_ref, g2_ref, wgate_ref, wup_ref, cw_ref, cb_ref, wd_ref, y_ref, cst_ref,
               xn_s, h_s, ap_s, hist_s) = refs[:-13], refs[-13:]
    if n_lead:
        xl_ref, al_ref, sl_ref = hist_src
        lead_s = hist_s.at[1]
        hist_s = hist_s.at[0]

        @pl.when(jnp.logical_and(pl.program_id(0) == 0, pl.program_id(1) == 0))
        def _():
            mix = jnp.concatenate([al_ref[...], sl_ref[...].astype(BF16)], axis=1)
            xm = xl_ref[...] + _dot(mix, wo_ref[...])
            r = lax.rsqrt(jnp.mean(xm * xm, axis=-1, keepdims=True) + EPS)
            gate = _dot(((xm * r) * g2_ref[...]).astype(BF16), wgate_ref[...])
            lead_s[...] = gate[n_lead - 2 * shift:n_lead, :]

        first_hist = lead_s
    else:
        first_hist, = hist_src
        hist_s = hist_s.at[0]

    @pl.when(pl.program_id(1) == 0)
    def _():
        hist_s[...] = first_hist[...]

    _post_head(x_ref, att_ref, ssm_ref, wo_ref, g2_ref, y_ref, xn_s)
    _post_chunks(0, wgate_ref.shape[1] // fc, wgate_ref, wup_ref, cw_ref, cb_ref, xn_s, h_s, ap_s, hist_s,
                 shift=shift, fc=fc)
    y_ref[...] += _dot(h_s[...], wd_ref[...])
    cst_ref[...] = hist_s[...]


def _post(x2d, att, ssm, hist, p, tm, nb, shift, att_spec, ssm_spec, lead=None, fc=256):
    rows, d = x2d.shape
    d_ff = p["w_gate"].shape[1]
    per_b = rows // nb
    nt = per_b // tm
    assert nt * tm == per_b and d_ff % fc == 0 and tm >= 2 * shift
    off = -(-2 * shift // 8) * 8
    row_spec = lambda w: pl.BlockSpec((tm, w), lambda b, t: (b * nt + t, 0))
    hist_spec = pl.BlockSpec((None, 2 * shift, d_ff), lambda b, t: (b, 0, 0))
    if lead is None:
        hist_src, hist_specs, n_lead = (hist,), [hist_spec], 0
    else:
        hist_src, hist_specs, n_lead = lead, [_const_spec(z.shape) for z in lead], lead[0].shape[0]
        assert n_lead >= 2 * shift
    return pl.pallas_call(
        functools.partial(_post_kernel, shift=shift, fc=fc, n_lead=n_lead),
        grid=(nb, nt),
        in_specs=[row_spec(d), att_spec, ssm_spec, *hist_specs,
                  _const_spec(p["w_out"].shape), _const_spec((1, d)),
                  _const_spec(p["w_gate"].shape), _const_spec(p["w_up"].shape),
                  _const_spec((3, d_ff)), _const_spec((1, d_ff)), _const_spec(p["w_down"].shape)],
        out_specs=(row_spec(d), hist_spec),
        out_shape=(jax.ShapeDtypeStruct((rows, d), F32), jax.ShapeDtypeStruct((nb, 2 * shift, d_ff), F32)),
        scratch_shapes=[pltpu.VMEM((tm, d), BF16), pltpu.VMEM((tm, d_ff), BF16),
                        pltpu.VMEM((2, off + tm, fc), F32), pltpu.VMEM((2, 2 * shift, d_ff), F32)],
        compiler_params=pltpu.CompilerParams(
            dimension_semantics=("arbitrary", "arbitrary"), vmem_limit_bytes=VMEM_LIMIT),
        name="post",
    )(x2d, att, ssm, *hist_src, p["w_out"], p["norm2"], p["w_gate"], p["w_up"], p["conv_w"], p["conv_b"],
      p["w_down"])


def _prepare_params(norm1, w_in, q_norm, k_norm, lam_q, lam_k, sub_norm, ssm_a_re, ssm_a_im, ssm_log_dt,
                    ssm_b_re, ssm_b_im, ssm_c_re, ssm_c_im, ssm_d, w_glu, b_glu, w_out, norm2, w_gate,
                    w_up, ffn_conv_w, ffn_conv_b, w_down):
    l = 0
    g, pdim = ssm_a_re[l].shape
    c = SSM_GROUP
    a = N_HEADS * HEAD_DIM
    abr, abi, bbr, bbi = _ssm_params(ssm_a_re[l], ssm_a_im[l], ssm_log_dt[l], ssm_b_re[l], ssm_b_im[l])
    gi = LANES // c

    def in_blocks(bb):
        bb = jnp.transpose(bb, (1, 0, 2)).reshape(g // gi, gi, c, 1, pdim)
        same = jnp.eye(gi, dtype=bool)[None, :, None, :, None]
        return jnp.where(same, bb, 0.0).astype(BF16).reshape(g // gi, gi * c, gi * pdim)

    go = MXU // c

    def out_blocks(cc):
        cc = jnp.transpose(cc.reshape(g // go, go, c, pdim), (0, 1, 3, 2))[:, :, :, None, :]
        same = jnp.eye(go, dtype=bool)[None, :, None, :, None]
        return jnp.where(same, cc, 0.0).astype(BF16).reshape(g // go, go * pdim, go * c)

    comp = jnp.arange(MXU) // QK_DIM
    ones_blk = (comp[:, None] == comp[None, :]).astype(BF16) * (1.0 / QK_DIM)
    return {
        "norm1": norm1[l].reshape(1, -1), "w_in": w_in[l],
        "q_gain": jnp.tile(q_norm[l].reshape(-1), N_HEADS).reshape(1, a),
        "k_gain": jnp.tile(k_norm[l].reshape(-1), N_HEADS).reshape(1, a),
        "ones_blk": ones_blk.astype(BF16),
        "lam_q": lam_q[l], "lam_k": lam_k[l],
        "sub_gain": (sub_norm[l] * (1.0 - LAM_INIT)).reshape(1, HEAD_DIM),
        "abr": abr.reshape(1, g * pdim), "abi": abi.reshape(1, g * pdim),
        "bre": in_blocks(bbr), "bim": in_blocks(bbi),
        "crt": out_blocks(ssm_c_re[l]), "cit": out_blocks(ssm_c_im[l]),
        "d_skip": ssm_d[l].reshape(1, g * c), "w_glu": w_glu[l].astype(BF16), "b_glu": b_glu[l].reshape(1, -1),
        "w_out": w_out[l].astype(BF16), "norm2": norm2[l].reshape(1, -1),
        "w_gate": w_gate[l].astype(BF16), "w_up": w_up[l].astype(BF16),
        "conv_w": ffn_conv_w[l], "conv_b": ffn_conv_b[l].reshape(1, -1), "w_down": w_down[l].astype(BF16),
    }


def _row_tile(length, cap=768):
    best = None
    for t in range(16, cap + 1, 16):
        if length % t == 0:
            best = t
    assert best is not None
    return best


def kernel(x_prompt, x_sample, cache_k, cache_v, state_ssm_re, state_ssm_im, state_ffn_conv, page_table, meta_tokens, norm1, w_in, q_norm, k_norm, lam_q, lam_k, sub_norm, ssm_a_re, ssm_a_im, ssm_log_dt, ssm_b_re, ssm_b_im, ssm_c_re, ssm_c_im, ssm_d, w_glu, b_glu, w_out, norm2, w_gate, w_up, ffn_conv_w, ffn_conv_b, w_down):
    assert norm1.shape[0] == 1, "single-layer stack"
    p = _prepare_params(norm1, w_in, q_norm, k_norm, lam_q, lam_k, sub_norm, ssm_a_re, ssm_a_im, ssm_log_dt,
                        ssm_b_re, ssm_b_im, ssm_c_re, ssm_c_im, ssm_d, w_glu, b_glu, w_out, norm2, w_gate,
                        w_up, ffn_conv_w, ffn_conv_b, w_down)
    nb, seq, d = x_prompt.shape
    db, t_new, _ = x_sample.shape
    a = N_HEADS * HEAD_DIM
    g, pdim = ssm_a_re.shape[1:]
    n_state = g * pdim
    d_ff = w_gate.shape[-1]
    length = seq + N_META

    meta = meta_tokens.astype(x_prompt.dtype)
    tm = _row_tile(length)
    qb, k, kb, v, vb, u = _project(x_prompt, meta, tm, p, u_time_major=True)
    att = _attn_prompt(qb.reshape(nb, length, a), kb.reshape(nb, length, a), vb.reshape(nb, length, a), p)
    zst = jnp.zeros((nb, n_state), F32)
    tc = _row_tile(length * nb, cap=768) // nb
    ys, hr, hi = _ssm(u.reshape(length, nb, a), zst, zst, p, tc)
    ys2 = ys.reshape(length, nb * a)
    tp = _row_tile(seq, cap=512)
    att_spec = pl.BlockSpec((pl.Element(1), pl.Element(tp), pl.Element(a)),
                            lambda b, t: (b, pl.multiple_of(N_META + t * tp, N_META), 0))
    ssm_spec = pl.BlockSpec((pl.Element(tp), pl.Element(a)),
                            lambda b, t: (pl.multiple_of(N_META + t * tp, N_META), pl.multiple_of(b * a, a)))
    yp, cst_p = _post(x_prompt.reshape(nb * seq, d), att, ys2, None, p, tp, nb, shift=1,
                      att_spec=att_spec, ssm_spec=ssm_spec, lead=(meta, att[0, :N_META], ys2[:N_META, :a]))
    y_prompt = yp.reshape(nb, seq, d)
    k_prompt = k.reshape(1, nb, length, N_HEADS, HEAD_DIM)
    v_prompt = v.reshape(1, nb, length, N_HEADS, HEAD_DIM)
    ssm_re_p = hr.reshape(1, nb, g, pdim)
    ssm_im_p = hi.reshape(1, nb, g, pdim)
    conv_p = cst_p[None]

    rows_s = db * t_new
    xs2 = jnp.transpose(x_sample, (1, 0, 2)).reshape(rows_s, d)
    qb, k, kb, v, vb, u = _project(xs2, None, rows_s, p, u_time_major=False)
    to_bm = lambda z: jnp.transpose(z.reshape(t_new, db, a), (1, 0, 2))
    q_hct = jnp.transpose(qb.reshape(t_new, db, N_HEADS, 1, HEAD_DIM), (1, 2, 3, 0, 4))
    q_rep = jnp.broadcast_to(q_hct, (db, N_HEADS, 2, t_new, HEAD_DIM)).reshape(db, N_HEADS * 2 * t_new, HEAD_DIM)
    n_pool, page = cache_k.shape[1:3]
    att = _attn_decode(q_rep, to_bm(kb), to_bm(vb), cache_k[0].reshape(n_pool, page * N_HEADS, HEAD_DIM),
                       cache_v[0].reshape(n_pool, page * N_HEADS, HEAD_DIM), page_table, p)
    att_tm = jnp.transpose(att, (1, 0, 2)).reshape(rows_s, a)
    ys, hr, hi = _ssm(u.reshape(t_new, db, a), state_ssm_re[0].reshape(db, n_state),
                      state_ssm_im[0].reshape(db, n_state), p, t_new)
    hist = jnp.transpose(state_ffn_conv[0], (1, 0, 2)).reshape(1, 2 * db, d_ff)
    whole = pl.BlockSpec((rows_s, a), lambda b, t: (0, 0))
    ysm, cst_s = _post(xs2, att_tm, ys.reshape(rows_s, a), hist, p, rows_s, 1, shift=db,
                       att_spec=whole, ssm_spec=whole)
    y_sample = jnp.transpose(ysm.reshape(t_new, db, d), (1, 0, 2))
    k_sample = jnp.transpose(k.reshape(t_new, db, N_HEADS, HEAD_DIM), (1, 0, 2, 3))[None]
    v_sample = jnp.transpose(v.reshape(t_new, db, N_HEADS, HEAD_DIM), (1, 0, 2, 3))[None]
    ssm_re_s = hr.reshape(1, db, g, pdim)
    ssm_im_s = hi.reshape(1, db, g, pdim)
    conv_s = jnp.transpose(cst_s.reshape(2, db, d_ff), (1, 0, 2))[None]

    return (y_prompt, y_sample, k_prompt, v_prompt, k_sample, v_sample,
            ssm_re_p, ssm_im_p, ssm_re_s, ssm_im_s, conv_p, conv_s)
```

```python
import functools
import math

import jax
import jax.numpy as jnp
from jax import lax
from jax.experimental import pallas as pl
from jax.experimental.pallas import tpu as pltpu

N_META = 16
N_HEADS = 4
QK_DIM = 64
HEAD_DIM = 2 * QK_DIM
SSM_GROUP = 16
SSM_STATE = 64
EPS = 1e-6
NEG = -1e30
LAM_INIT = 0.8 - 0.6 * math.exp(-0.3 * 0)
LOG2E = math.log2(math.e)

LANES = 128
MXU = 256
VMEM_LIMIT = 56 * 1024 * 1024

F32 = jnp.float32
BF16 = jnp.bfloat16


def _dot(a, b):
    return jnp.dot(a, b, preferred_element_type=F32)


def _dot_nt(a, b):
    return lax.dot_general(a, b, (((1,), (1,)), ((), ())), preferred_element_type=F32)


def _const_spec(shape):
    nd = len(shape)
    return pl.BlockSpec(shape, lambda *_: (0,) * nd, pipeline_mode=pl.Buffered(1))


def _lam(lq_ref, lk_ref):
    e = jnp.exp(jnp.sum(lq_ref[...] * lk_ref[...], axis=1, keepdims=True))
    return e[0:1] - e[1:2] + LAM_INIT


def _proj_kernel(x_ref, meta_ref, g1_ref, w_ref, qg_ref, kg_ref, ones_ref,
                 qb_ref, k_ref, kb_ref, v_ref, vb_ref, u_ref, wb_s, *, n_meta):
    outs = (qb_ref, k_ref, kb_ref, v_ref, vb_ref, u_ref)
    consts = (g1_ref, wb_s, qg_ref, kg_ref, ones_ref)
    tm = qb_ref.shape[0]

    @pl.when(jnp.logical_and(pl.program_id(0) == 0, pl.program_id(1) == 0))
    def _():
        wb_s[...] = w_ref[...].astype(BF16)

    if not n_meta:
        _proj_body(x_ref[...], consts, outs)
        return
    t = pl.program_id(1)

    @pl.when(t == 0)
    def _():
        _proj_body(jnp.concatenate([meta_ref[...], x_ref[0, 0:tm - n_meta, :]], axis=0), consts, outs)

    @pl.when(t != 0)
    def _():
        _proj_body(x_ref[0], consts, outs)


def _proj_body(x, consts, outs):
    g1_ref, w_ref, qg_ref, kg_ref, ones_ref = consts
    qb_ref, k_ref, kb_ref, v_ref, vb_ref, u_ref = outs
    a = N_HEADS * HEAD_DIM
    r = lax.rsqrt(jnp.mean(x * x, axis=-1, keepdims=True) + EPS)
    xn = ((x * r) * g1_ref[...]).astype(BF16)
    proj = _dot(xn, w_ref[...])

    def comp_norm(z, g):
        z2 = (z * z).astype(BF16)
        ms = jnp.concatenate(
            [_dot(z2[:, j * MXU:(j + 1) * MXU], ones_ref[...]) for j in range(a // MXU)], axis=1)
        return (z * lax.rsqrt(ms + EPS)) * g

    qn = comp_norm(proj[:, :a], qg_ref[...])
    kn = comp_norm(proj[:, a:2 * a], kg_ref[...])
    v = proj[:, 2 * a:3 * a]
    qb_ref[...] = (qn * (QK_DIM ** -0.5 * LOG2E)).astype(BF16)
    kb_ref[...] = kn.astype(BF16)
    vb_ref[...] = v.astype(BF16)
    u_ref[...] = proj[:, 3 * a:]
    tm = x.shape[0]
    for h in range(N_HEADS):
        k_ref[pl.ds(h, tm, stride=N_HEADS), :] = kn[:, h * HEAD_DIM:(h + 1) * HEAD_DIM]
        v_ref[pl.ds(h, tm, stride=N_HEADS), :] = v[:, h * HEAD_DIM:(h + 1) * HEAD_DIM]


def _project(x, meta, tm, p, u_time_major):
    a = N_HEADS * HEAD_DIM
    d = x.shape[-1]
    if meta is None:
        nb, n_meta = 1, 0
        per_b = x.shape[0]
        x_spec = pl.BlockSpec((tm, d), lambda b, t: (t, 0))
        meta = jnp.zeros((8, d), x.dtype)
    else:
        nb, n_meta = x.shape[0], meta.shape[0]
        per_b = n_meta + x.shape[1]
        x_spec = pl.BlockSpec(
            (pl.Element(1), pl.Element(tm), pl.Element(d)),
            lambda b, t: (b, pl.multiple_of(jnp.maximum(t * tm - n_meta, 0), 8), 0))
    rows = nb * per_b
    nt = per_b // tm
    assert nt * tm == per_b and n_meta % 8 == 0
    grid = (nb, nt)
    row_spec = lambda w: pl.BlockSpec((tm, w), lambda b, t: (b * nt + t, 0))
    head_spec = pl.BlockSpec((tm * N_HEADS, HEAD_DIM), lambda b, t: (b * nt + t, 0))
    if u_time_major:
        u_shape = jax.ShapeDtypeStruct((per_b, nb * a), F32)
        u_spec = pl.BlockSpec((tm, a), lambda b, t: (t, b))
    else:
        u_shape = jax.ShapeDtypeStruct((rows, a), F32)
        u_spec = row_spec(a)
    out_shape = (jax.ShapeDtypeStruct((rows, a), BF16), jax.ShapeDtypeStruct((rows * N_HEADS, HEAD_DIM), F32),
                 jax.ShapeDtypeStruct((rows, a), BF16), jax.ShapeDtypeStruct((rows * N_HEADS, HEAD_DIM), F32),
                 jax.ShapeDtypeStruct((rows, a), BF16), u_shape)
    return pl.pallas_call(
        functools.partial(_proj_kernel, n_meta=n_meta),
        grid=grid,
        in_specs=[x_spec, _const_spec(meta.shape), _const_spec((1, d)), _const_spec(p["w_in"].shape),
                  _const_spec((1, a)), _const_spec((1, a)), _const_spec((MXU, MXU))],
        out_specs=(row_spec(a), head_spec, row_spec(a), head_spec, row_spec(a), u_spec),
        out_shape=out_shape,
        scratch_shapes=[pltpu.VMEM(p["w_in"].shape, BF16)],
        compiler_params=pltpu.CompilerParams(
            dimension_semantics=("arbitrary", "arbitrary"), vmem_limit_bytes=VMEM_LIMIT),
        name="proj",
    )(x, meta, p["norm1"], p["w_in"], p["q_gain"], p["k_gain"], p["ones_blk"])


def _attn_prompt_kernel(q_ref, k_ref, v_ref, lq_ref, lk_ref, sg_ref, o_ref, acc_s, m_s, s_s, *, tq, n_tiles):
    lam = _lam(lq_ref, lk_ref)
    sg = sg_ref[...]
    lo = lax.broadcasted_iota(jnp.int32, (1, HEAD_DIM), 1) < QK_DIM
    n_ones = acc_s.shape[1] - HEAD_DIM

    def hcols(h):
        return slice(h * HEAD_DIM, (h + 1) * HEAD_DIM)

    def stack_q(q):
        z = jnp.zeros_like(q)
        return jnp.concatenate([jnp.where(lo, q, z), jnp.where(lo, z, q)], axis=0)

    def init(h, t2):
        m_s[h, :, 0:t2] = jnp.full((1, t2), NEG, F32)
        acc_s[h, :, 0:t2] = jnp.zeros((HEAD_DIM + n_ones, t2), F32)

    def scores(slot, h, q2, kt, mask):
        s = _dot_nt(kt, q2)
        if mask is not None:
            s = jnp.where(mask, s, NEG)
        s_s[slot, h, 0:kt.shape[0], 0:q2.shape[0]] = s

    def absorb(slot, h, nk, t2, vt):
        s = s_s[slot, h, 0:nk, 0:t2]
        m_prev = m_s[h, :, 0:t2]
        m_new = jnp.maximum(m_prev, jnp.max(s, axis=0, keepdims=True))
        alpha = jnp.exp2(m_prev - m_new)
        pr = jnp.exp2(s - m_new).astype(BF16)
        va = jnp.concatenate([vt.T, jnp.ones((n_ones, nk), BF16)], axis=0)
        acc_s[h, :, 0:t2] = alpha * acc_s[h, :, 0:t2] + _dot(va, pr)
        m_s[h, :, 0:t2] = m_new

    def finish(h, t):
        acc = acc_s[h, :, 0:2 * t]
        on = acc[0:HEAD_DIM] / acc[HEAD_DIM:HEAD_DIM + 1]
        o = on[:, 0:t] - lam * on[:, t:2 * t]
        r = lax.rsqrt(jnp.mean(o * o, axis=0, keepdims=True) + EPS)
        return ((o * r).T * sg).astype(o_ref.dtype)

    def causal_mask(t, nk, offset):
        qi = lax.broadcasted_iota(jnp.int32, (nk, 2 * t), 1)
        qi = jnp.where(qi >= t, qi - t, qi)
        ki = lax.broadcasted_iota(jnp.int32, (nk, 2 * t), 0)
        return ki <= qi + offset

    tmq = LANES
    for h in range(N_HEADS):
        init(h, 2 * tmq)
        scores(0, h, stack_q(q_ref[0:tmq, hcols(h)]), k_ref[0:tmq, hcols(h)], causal_mask(tmq, tmq, 0))
    for h in range(N_HEADS):
        absorb(0, h, tmq, 2 * tmq, v_ref[0:tmq, hcols(h)])
        o_ref[0:N_META, hcols(h)] = finish(h, tmq)[0:N_META]

    dk = tq + N_META

    def q_tile(j, carry):
        qs = pl.multiple_of(N_META + j * tq, N_META)
        q2 = [stack_q(q_ref[pl.ds(qs, tq), hcols(h)]) for h in range(N_HEADS)]
        ds = pl.multiple_of(j * tq, tq)
        dmask = causal_mask(tq, dk, N_META)
        for h in range(N_HEADS):
            init(h, 2 * tq)

        def plain_scores(slot, i):
            ks = pl.multiple_of(i * tq, tq)
            for h in range(N_HEADS):
                scores(slot, h, q2[h], k_ref[pl.ds(ks, tq), hcols(h)], None)

        def diag_scores(slot):
            for h in range(N_HEADS):
                scores(slot, h, q2[h], k_ref[pl.ds(ds, dk), hcols(h)], dmask)

        def plain_absorb(slot, i):
            ks = pl.multiple_of(i * tq, tq)
            for h in range(N_HEADS):
                absorb(slot, h, tq, 2 * tq, v_ref[pl.ds(ks, tq), hcols(h)])

        def diag_absorb(slot):
            for h in range(N_HEADS):
                absorb(slot, h, dk, 2 * tq, v_ref[pl.ds(ds, dk), hcols(h)])
                o_ref[pl.ds(qs, tq), hcols(h)] = finish(h, tq)

        @pl.when(j == 0)
        def _():
            diag_scores(0)
            diag_absorb(0)

        @pl.when(j > 0)
        def _():
            plain_scores(0, 0)

        n_pairs = (j - 1) // 2

        def k_pair(pi, c):
            i = 2 * pi
            plain_scores(1, i + 1)
            plain_absorb(0, i)
            plain_scores(0, i + 2)
            plain_absorb(1, i + 1)
            return c

        lax.fori_loop(0, n_pairs, k_pair, 0)

        @pl.when(jnp.logical_and(j > 0, j % 2 == 1))
        def _():
            diag_scores(1)
            plain_absorb(0, j - 1)
            diag_absorb(1)

        @pl.when(jnp.logical_and(j > 0, j % 2 == 0))
        def _():
            plain_scores(1, j - 1)
            plain_absorb(0, j - 2)
            diag_scores(0)
            plain_absorb(1, j - 1)
            diag_absorb(0)

        return carry

    lax.fori_loop(0, n_tiles, q_tile, 0)


def _attn_prompt(qb, kb, vb, p, tq=256):
    nb, length, a = qb.shape
    n_tiles = (length - N_META) // tq
    assert N_META + n_tiles * tq == length and tq % LANES == 0
    spec = pl.BlockSpec((None, length, a), lambda b: (b, 0, 0))
    return pl.pallas_call(
        functools.partial(_attn_prompt_kernel, tq=tq, n_tiles=n_tiles),
        grid=(nb,),
        in_specs=[spec, spec, spec, _const_spec((2, QK_DIM)), _const_spec((2, QK_DIM)),
                  _const_spec((1, HEAD_DIM))],
        out_specs=spec,
        out_shape=jax.ShapeDtypeStruct((nb, length, a), BF16),
        scratch_shapes=[pltpu.VMEM((N_HEADS, HEAD_DIM + 16, 2 * tq + LANES), F32),
                        pltpu.VMEM((N_HEADS, 1, 2 * tq), F32),
                        pltpu.VMEM((2, N_HEADS, tq + N_META, 2 * tq + LANES), F32)],
        compiler_params=pltpu.CompilerParams(
            dimension_semantics=("arbitrary",), vmem_limit_bytes=VMEM_LIMIT),
        name="attn_prompt",
    )(qb, kb, vb, p["lam_q"], p["lam_k"], p["sub_gain"])


def _decode_entry(q, kn, vn, kp, vp, lam, sg, o_ref):
    t_new = kn.shape[0]
    grp = 2 * t_new
    nrow = N_HEADS * grp
    ncol = kp[0].shape[0]
    row = lax.broadcasted_iota(jnp.int32, (nrow, HEAD_DIM), 0)
    lane = lax.broadcasted_iota(jnp.int32, (nrow, HEAD_DIM), 1)
    comp_ok = (lane >= QK_DIM) == ((row % grp) >= t_new)
    t_of_row = lax.broadcasted_iota(jnp.int32, (nrow, 1), 0) % t_new
    head_ok = (lax.broadcasted_iota(jnp.int32, (nrow, ncol), 1) % N_HEADS
               == lax.broadcasted_iota(jnp.int32, (nrow, ncol), 0) // grp)

    def per_row_head(z, t):
        return jnp.concatenate(
            [jnp.broadcast_to(z[t:t + 1, h * HEAD_DIM:(h + 1) * HEAD_DIM], (grp, HEAD_DIM)) for h in range(N_HEADS)],
            axis=0)

    q32 = jnp.where(comp_ok, q.astype(F32), 0.0)
    qb = q32.astype(BF16)
    kn, vn = kn.astype(F32), vn.astype(F32)
    s_pages = [jnp.where(head_ok, _dot_nt(qb, k_ref[...].astype(BF16)), NEG) for k_ref in kp]
    s_new = []
    for t in range(t_new):
        sc = jnp.sum(q32 * per_row_head(kn, t), axis=1, keepdims=True)
        s_new.append(jnp.where(t_of_row >= t, sc, NEG))

    m = s_pages[0]
    for s in s_pages[1:]:
        m = jnp.maximum(m, s)
    m = jnp.max(m, axis=1, keepdims=True)
    for s in s_new:
        m = jnp.maximum(m, s)

    acc = jnp.zeros((nrow, HEAD_DIM), F32)
    psum = None
    for s, v_ref in zip(s_pages, vp):
        pr = jnp.exp2(s - m)
        psum = pr if psum is None else psum + pr
        acc = acc + _dot(pr.astype(BF16), v_ref[...].astype(BF16))
    lsum = jnp.sum(psum, axis=1, keepdims=True)
    for t in range(t_new):
        pr = jnp.exp2(s_new[t] - m)
        lsum = lsum + pr
        acc = acc + pr * per_row_head(vn, t)
    on = acc / lsum
    for h in range(N_HEADS):
        r0 = h * grp
        o = on[r0:r0 + t_new] - lam * on[r0 + t_new:r0 + grp]
        r = lax.rsqrt(jnp.mean(o * o, axis=-1, keepdims=True) + EPS)
        o_ref[:, h * HEAD_DIM:(h + 1) * HEAD_DIM] = ((o * r) * sg).astype(o_ref.dtype)


def _attn_decode_kernel(pt_ref, q_ref, kn_ref, vn_ref, lq_ref, lk_ref, sg_ref, ck_hbm, cv_hbm, o_ref,
                        kbuf, vbuf, sem, *, n_pages, ahead):
    g, n = pl.program_id(0), pl.num_programs(0)
    ring = kbuf.shape[0]

    def page_copies(slot, page_of):
        return [pltpu.make_async_copy(hbm.at[page_of(pg)], buf.at[slot, pg], sem.at[w, slot])
                for pg in range(n_pages) for w, (hbm, buf) in enumerate(((ck_hbm, kbuf), (cv_hbm, vbuf)))]

    def start(entry):
        for c in page_copies(entry % ring, lambda pg: pt_ref[entry * n_pages + pg]):
            c.start()

    @pl.when(g == 0)
    def _():
        for entry in range(ahead):
            start(entry)

    @pl.when(g + ahead < n)
    def _():
        start(g + ahead)

    slot = g % ring
    for c in page_copies(slot, lambda pg: 0):
        c.wait()
    _decode_entry(q_ref[...], kn_ref[...], vn_ref[...], [kbuf.at[slot, pg] for pg in range(n_pages)],
                  [vbuf.at[slot, pg] for pg in range(n_pages)], _lam(lq_ref, lk_ref), sg_ref[...], o_ref)


def _attn_decode(q_rep, kn, vn, cache_k, cache_v, page_table, p, ahead=2):
    db, nrow, _ = q_rep.shape
    t_new, a = kn.shape[1:]
    n_pages = page_table.shape[1]
    prow = cache_k.shape[1]
    assert db >= ahead
    const = lambda shape: pl.BlockSpec(shape, lambda g, pt_ref: (0,) * len(shape))
    per_entry = lambda r, w: pl.BlockSpec((None, r, w), lambda g, pt_ref: (g, 0, 0))
    grid_spec = pltpu.PrefetchScalarGridSpec(
        num_scalar_prefetch=1,
        grid=(db,),
        in_specs=[per_entry(nrow, HEAD_DIM), per_entry(t_new, a), per_entry(t_new, a),
                  const((2, QK_DIM)), const((2, QK_DIM)), const((1, HEAD_DIM)),
                  pl.BlockSpec(memory_space=pl.ANY), pl.BlockSpec(memory_space=pl.ANY)],
        out_specs=per_entry(t_new, a),
        scratch_shapes=[pltpu.VMEM((ahead + 1, n_pages, prow, HEAD_DIM), cache_k.dtype),
                        pltpu.VMEM((ahead + 1, n_pages, prow, HEAD_DIM), cache_v.dtype),
                        pltpu.SemaphoreType.DMA((2, ahead + 1))],
    )
    return pl.pallas_call(
        functools.partial(_attn_decode_kernel, n_pages=n_pages, ahead=ahead),
        grid_spec=grid_spec,
        out_shape=jax.ShapeDtypeStruct((db, t_new, a), BF16),
        compiler_params=pltpu.CompilerParams(
            dimension_semantics=("arbitrary",), vmem_limit_bytes=VMEM_LIMIT),
        name="attn_decode",
    )(page_table.reshape(-1), q_rep, kn, vn, p["lam_q"], p["lam_k"], p["sub_gain"], cache_k, cache_v)


def _ssm_param_kernel(ar_ref, ai_ref, ldt_ref, br_ref, bi_ref, abr_ref, abi_ref, bbr_ref, bbi_ref):
    ar, ai = ar_ref[...], ai_ref[...]
    dt = jnp.exp(ldt_ref[...])
    mag = jnp.exp(ar * dt)
    abr, abi = mag * jnp.cos(ai * dt), mag * jnp.sin(ai * dt)
    den = ar * ar + ai * ai
    nr, ni = abr - 1.0, abi
    gr, gi = (nr * ar + ni * ai) / den, (ni * ar - nr * ai) / den
    abr_ref[...] = abr
    abi_ref[...] = abi
    for c in range(SSM_GROUP):
        br, bi = br_ref[c], bi_ref[c]
        bbr_ref[c] = gr * br - gi * bi
        bbi_ref[c] = gr * bi + gi * br


def _ssm_params(a_re, a_im, log_dt, b_re, b_im):
    g, pdim = a_re.shape
    c = b_re.shape[-1]
    b_re_t = jnp.transpose(b_re, (2, 0, 1))
    b_im_t = jnp.transpose(b_im, (2, 0, 1))
    gp = jax.ShapeDtypeStruct((g, pdim), F32)
    cgp = jax.ShapeDtypeStruct((c, g, pdim), F32)
    return pl.pallas_call(_ssm_param_kernel, out_shape=(gp, gp, cgp, cgp), name="ssm_params")(
        a_re, a_im, log_dt.reshape(g, 1), b_re_t, b_im_t)


def _ssm_kernel(u_ref, up_ref, h0r_ref, h0i_ref, abr_ref, abi_ref, bre_ref, bim_ref, crt_ref, cit_ref,
                d_ref, wg_ref, bg_ref, y_ref, hr_ref, hi_ref, xr_s, xi_s, *, tc, nb, n_chunks):
    i = pl.program_id(0)

    @pl.when(i == 0)
    def _():
        hr_ref[...] = h0r_ref[...]
        hi_ref[...] = h0i_ref[...]
        if n_chunks > 1:
            xr_s[1] = jnp.zeros(xr_s.shape[1:], F32)
            xi_s[1] = jnp.zeros(xi_s.shape[1:], F32)

    def body(cur, prev):
        rows, width = tc * nb, d_ref.shape[1]
        ub = u_ref[...].reshape(rows, width).astype(BF16)
        kin, nout = bre_ref.shape[1], bre_ref.shape[2]
        for mblk in range(bre_ref.shape[0]):
            um = ub[:, mblk * kin:(mblk + 1) * kin]
            xr_s[cur, :, mblk * nout:(mblk + 1) * nout] = _dot(um, bre_ref[mblk])
            xi_s[cur, :, mblk * nout:(mblk + 1) * nout] = _dot(um, bim_ref[mblk])

        hr, hi = hr_ref[...], hi_ref[...]
        for t in range(tc):
            r = slice(t * nb, (t + 1) * nb)
            abr, abi = abr_ref[...], abi_ref[...]
            nhr = abr * hr - abi * hi + xr_s[cur, r, :]
            nhi = abr * hi + abi * hr + xi_s[cur, r, :]
            xr_s[cur, r, :] = nhr
            xi_s[cur, r, :] = nhi
            hr, hi = nhr, nhi
        live = i < n_chunks
        hr_ref[...] = jnp.where(live, hr, hr_ref[...])
        hi_ref[...] = jnp.where(live, hi, hi_ref[...])

        kout = crt_ref.shape[1]
        ys = []
        for j in range(crt_ref.shape[0]):
            hrj = xr_s[prev, :, j * kout:(j + 1) * kout].astype(BF16)
            hij = xi_s[prev, :, j * kout:(j + 1) * kout].astype(BF16)
            ys.append(_dot(hrj, crt_ref[j]) - _dot(hij, cit_ref[j]))
        y = jnp.concatenate(ys, axis=1) + d_ref[...] * up_ref[...].reshape(rows, width)
        g = jax.nn.gelu(y)
        out = g * jax.nn.sigmoid(_dot(g.astype(BF16), wg_ref[...]) + bg_ref[...])
        y_ref[...] = out.reshape(tc, nb, width)

    if n_chunks == 1:
        body(0, 0)
    else:
        for par in (0, 1):
            pl.when(i % 2 == par)(functools.partial(body, par, 1 - par))


def _ssm(u3, h0r, h0i, p, tc):
    t_len, nb, width = u3.shape
    n_state = h0r.shape[1]
    n_chunks = t_len // tc
    assert n_chunks * tc == t_len
    st = jax.ShapeDtypeStruct((nb, n_state), F32)
    st_spec = pl.BlockSpec((nb, n_state), lambda i: (0, 0))
    chunk = lambda index: pl.BlockSpec((tc, nb, width), lambda i: (index(i), 0, 0))
    return pl.pallas_call(
        functools.partial(_ssm_kernel, tc=tc, nb=nb, n_chunks=n_chunks),
        grid=(n_chunks + 1 if n_chunks > 1 else 1,),
        in_specs=[chunk(lambda i: jnp.minimum(i, n_chunks - 1)), chunk(lambda i: jnp.maximum(i - 1, 0)),
                  _const_spec((nb, n_state)), _const_spec((nb, n_state)),
                  _const_spec((nb, n_state)), _const_spec((nb, n_state)),
                  _const_spec(p["bre"].shape), _const_spec(p["bim"].shape),
                  _const_spec(p["crt"].shape), _const_spec(p["cit"].shape),
                  _const_spec((1, width)), _const_spec((width, width)), _const_spec((1, width))],
        out_specs=(chunk(lambda i: jnp.maximum(i - 1, 0)), st_spec, st_spec),
        out_shape=(jax.ShapeDtypeStruct((t_len, nb, width), F32), st, st),
        scratch_shapes=[pltpu.VMEM((min(n_chunks, 2), tc * nb, n_state), F32)] * 2,
        compiler_params=pltpu.CompilerParams(
            dimension_semantics=("arbitrary",), vmem_limit_bytes=VMEM_LIMIT),
        name="ssm",
    )(u3, u3, h0r, h0i, jnp.broadcast_to(p["abr"], (nb, n_state)), jnp.broadcast_to(p["abi"], (nb, n_state)),
      p["bre"], p["bim"], p["crt"], p["cit"],
      p["d_skip"], p["w_glu"], p["b_glu"])


def _post_head(x_ref, att_ref, ssm_ref, wo_ref, g2_ref, y_ref, xn_s):
    tm, a = y_ref.shape[0], wo_ref.shape[0] // 2
    mix = jnp.concatenate([att_ref[...].reshape(tm, a), ssm_ref[...].reshape(tm, a).astype(BF16)], axis=1)
    xm = x_ref[...] + _dot(mix, wo_ref[...])
    y_ref[...] = xm
    r = lax.rsqrt(jnp.mean(xm * xm, axis=-1, keepdims=True) + EPS)
    xn_s[...] = ((xm * r) * g2_ref[...]).astype(BF16)


def _post_chunks(j0, j1, wgate_ref, wup_ref, cw_ref, cb_ref, xn_s, h_s, ap_s, hist_s, *, shift, fc):
    tm = xn_s.shape[0]
    off = ap_s.shape[1] - tm
    for j in range(j0, j1):
        cs = slice(j * fc, (j + 1) * fc)
        ap = ap_s.at[j % 2]
        gate = _dot(xn_s[...], wgate_ref[:, cs])
        up = _dot(xn_s[...], wup_ref[:, cs])
        ap[off - 2 * shift:off, :] = hist_s[:, cs]
        ap[off:off + tm, :] = gate
        conv = (cb_ref[:, cs] + cw_ref[0:1, cs] * ap[off - 2 * shift:off - 2 * shift + tm, :]
                + cw_ref[1:2, cs] * ap[off - shift:off - shift + tm, :] + cw_ref[2:3, cs] * gate)
        hist_s[:, cs] = ap[off + tm - 2 * shift:off + tm, :]
        h_s[:, cs] = (jax.nn.gelu(conv) * up).astype(BF16)


def _post_kernel(x_ref, att_ref, ssm_ref, *refs, shift, fc, n_lead):
    hist_src, (wo_ref, g2_ref, wgate_ref, wup_ref, cw_ref, cb_ref, wd_ref, y_ref, cst_ref,
               xn_s, h_s, ap_s, hist_s) = refs[:-13], refs[-13:]
    if n_lead:
        xl_ref, al_ref, sl_ref = hist_src
        lead_s = hist_s.at[1]
        hist_s = hist_s.at[0]

        @pl.when(jnp.logical_and(pl.program_id(0) == 0, pl.program_id(1) == 0))
        def _():
            mix = jnp.concatenate([al_ref[...], sl_ref[...].astype(BF16)], axis=1)
            xm = xl_ref[...] + _dot(mix, wo_ref[...])
            r = lax.rsqrt(jnp.mean(xm * xm, axis=-1, keepdims=True) + EPS)
            gate = _dot(((xm * r) * g2_ref[...]).astype(BF16), wgate_ref[...])
            lead_s[...] = gate[n_lead - 2 * shift:n_lead, :]

        first_hist = lead_s
    else:
        first_hist, = hist_src
        hist_s = hist_s.at[0]

    @pl.when(pl.program_id(1) == 0)
    def _():
        hist_s[...] = first_hist[...]

    _post_head(x_ref, att_ref, ssm_ref, wo_ref, g2_ref, y_ref, xn_s)
    _post_chunks(0, wgate_ref.shape[1] // fc, wgate_ref, wup_ref, cw_ref, cb_ref, xn_s, h_s, ap_s, hist_s,
                 shift=shift, fc=fc)
    y_ref[...] += _dot(h_s[...], wd_ref[...])
    cst_ref[...] = hist_s[...]


def _post(x2d, att, ssm, hist, p, tm, nb, shift, att_spec, ssm_spec, lead=None, fc=256):
    rows, d = x2d.shape
    d_ff = p["w_gate"].shape[1]
    per_b = rows // nb
    nt = per_b // tm
    assert nt * tm == per_b and d_ff % fc == 0 and tm >= 2 * shift
    off = -(-2 * shift // 8) * 8
    row_spec = lambda w: pl.BlockSpec((tm, w), lambda b, t: (b * nt + t, 0))
    hist_spec = pl.BlockSpec((None, 2 * shift, d_ff), lambda b, t: (b, 0, 0))
    if lead is None:
        hist_src, hist_specs, n_lead = (hist,), [hist_spec], 0
    else:
        hist_src, hist_specs, n_lead = lead, [_const_spec(z.shape) for z in lead], lead[0].shape[0]
        assert n_lead >= 2 * shift
    return pl.pallas_call(
        functools.partial(_post_kernel, shift=shift, fc=fc, n_lead=n_lead),
        grid=(nb, nt),
        in_specs=[row_spec(d), att_spec, ssm_spec, *hist_specs,
                  _const_spec(p["w_out"].shape), _const_spec((1, d)),
                  _const_spec(p["w_gate"].shape), _const_spec(p["w_up"].shape),
                  _const_spec((3, d_ff)), _const_spec((1, d_ff)), _const_spec(p["w_down"].shape)],
        out_specs=(row_spec(d), hist_spec),
        out_shape=(jax.ShapeDtypeStruct((rows, d), F32), jax.ShapeDtypeStruct((nb, 2 * shift, d_ff), F32)),
        scratch_shapes=[pltpu.VMEM((tm, d), BF16), pltpu.VMEM((tm, d_ff), BF16),
                        pltpu.VMEM((2, off + tm, fc), F32), pltpu.VMEM((2, 2 * shift, d_ff), F32)],
        compiler_params=pltpu.CompilerParams(
            dimension_semantics=("arbitrary", "arbitrary"), vmem_limit_bytes=VMEM_LIMIT),
        name="post",
    )(x2d, att, ssm, *hist_src, p["w_out"], p["norm2"], p["w_gate"], p["w_up"], p["conv_w"], p["conv_b"],
      p["w_down"])


def _prepare_params(norm1, w_in, q_norm, k_norm, lam_q, lam_k, sub_norm, ssm_a_re, ssm_a_im, ssm_log_dt,
                    ssm_b_re, ssm_b_im, ssm_c_re, ssm_c_im, ssm_d, w_glu, b_glu, w_out, norm2, w_gate,
                    w_up, ffn_conv_w, ffn_conv_b, w_down):
    l = 0
    g, pdim = ssm_a_re[l].shape
    c = SSM_GROUP
    a = N_HEADS * HEAD_DIM
    abr, abi, bbr, bbi = _ssm_params(ssm_a_re[l], ssm_a_im[l], ssm_log_dt[l], ssm_b_re[l], ssm_b_im[l])
    gi = LANES // c

    def in_blocks(bb):
        bb = jnp.transpose(bb, (1, 0, 2)).reshape(g // gi, gi, c, 1, pdim)
        same = jnp.eye(gi, dtype=bool)[None, :, None, :, None]
        return jnp.where(same, bb, 0.0).astype(BF16).reshape(g // gi, gi * c, gi * pdim)

    go = MXU // c

    def out_blocks(cc):
        cc = jnp.transpose(cc.reshape(g // go, go, c, pdim), (0, 1, 3, 2))[:, :, :, None, :]
        same = jnp.eye(go, dtype=bool)[None, :, None, :, None]
        return jnp.where(same, cc, 0.0).astype(BF16).reshape(g // go, go * pdim, go * c)

    comp = jnp.arange(MXU) // QK_DIM
    ones_blk = (comp[:, None] == comp[None, :]).astype(BF16) * (1.0 / QK_DIM)
    return {
        "norm1": norm1[l].reshape(1, -1), "w_in": w_in[l],
        "q_gain": jnp.tile(q_norm[l].reshape(-1), N_HEADS).reshape(1, a),
        "k_gain": jnp.tile(k_norm[l].reshape(-1), N_HEADS).reshape(1, a),
        "ones_blk": ones_blk.astype(BF16),
        "lam_q": lam_q[l], "lam_k": lam_k[l],
        "sub_gain": (sub_norm[l] * (1.0 - LAM_INIT)).reshape(1, HEAD_DIM),
        "abr": abr.reshape(1, g * pdim), "abi": abi.reshape(1, g * pdim),
        "bre": in_blocks(bbr), "bim": in_blocks(bbi),
        "crt": out_blocks(ssm_c_re[l]), "cit": out_blocks(ssm_c_im[l]),
        "d_skip": ssm_d[l].reshape(1, g * c), "w_glu": w_glu[l].astype(BF16), "b_glu": b_glu[l].reshape(1, -1),
        "w_out": w_out[l].astype(BF16), "norm2": norm2[l].reshape(1, -1),
        "w_gate": w_gate[l].astype(BF16), "w_up": w_up[l].astype(BF16),
        "conv_w": ffn_conv_w[l], "conv_b": ffn_conv_b[l].reshape(1, -1), "w_down": w_down[l].astype(BF16),
    }


def _row_tile(length, cap=768):
    best = None
    for t in range(16, cap + 1, 16):
        if length % t == 0:
            best = t
    assert best is not None
    return best


def kernel(x_prompt, x_sample, cache_k, cache_v, state_ssm_re, state_ssm_im, state_ffn_conv, page_table, meta_tokens, norm1, w_in, q_norm, k_norm, lam_q, lam_k, sub_norm, ssm_a_re, ssm_a_im, ssm_log_dt, ssm_b_re, ssm_b_im, ssm_c_re, ssm_c_im, ssm_d, w_glu, b_glu, w_out, norm2, w_gate, w_up, ffn_conv_w, ffn_conv_b, w_down):
    assert norm1.shape[0] == 1, "single-layer stack"
    p = _prepare_params(norm1, w_in, q_norm, k_norm, lam_q, lam_k, sub_norm, ssm_a_re, ssm_a_im, ssm_log_dt,
                        ssm_b_re, ssm_b_im, ssm_c_re, ssm_c_im, ssm_d, w_glu, b_glu, w_out, norm2, w_gate,
                        w_up, ffn_conv_w, ffn_conv_b, w_down)
    nb, seq, d = x_prompt.shape
    db, t_new, _ = x_sample.shape
    a = N_HEADS * HEAD_DIM
    g, pdim = ssm_a_re.shape[1:]
    n_state = g * pdim
    d_ff = w_gate.shape[-1]
    length = seq + N_META

    meta = meta_tokens.astype(x_prompt.dtype)
    tm = _row_tile(length)
    qb, k, kb, v, vb, u = _project(x_prompt, meta, tm, p, u_time_major=True)
    att = _attn_prompt(qb.reshape(nb, length, a), kb.reshape(nb, length, a), vb.reshape(nb, length, a), p)
    zst = jnp.zeros((nb, n_state), F32)
    tc = _row_tile(length * nb, cap=768) // nb
    ys, hr, hi = _ssm(u.reshape(length, nb, a), zst, zst, p, tc)
    ys2 = ys.reshape(length, nb * a)
    tp = _row_tile(seq, cap=512)
    att_spec = pl.BlockSpec((pl.Element(1), pl.Element(tp), pl.Element(a)),
                            lambda b, t: (b, pl.multiple_of(N_META + t * tp, N_META), 0))
    ssm_spec = pl.BlockSpec((pl.Element(tp), pl.Element(a)),
                            lambda b, t: (pl.multiple_of(N_META + t * tp, N_META), pl.multiple_of(b * a, a)))
    yp, cst_p = _post(x_prompt.reshape(nb * seq, d), att, ys2, None, p, tp, nb, shift=1,
                      att_spec=att_spec, ssm_spec=ssm_spec, lead=(meta, att[0, :N_META], ys2[:N_META, :a]))
    y_prompt = yp.reshape(nb, seq, d)
    k_prompt = k.reshape(1, nb, length, N_HEADS, HEAD_DIM)
    v_prompt = v.reshape(1, nb, length, N_HEADS, HEAD_DIM)
    ssm_re_p = hr.reshape(1, nb, g, pdim)
    ssm_im_p = hi.reshape(1, nb, g, pdim)
    conv_p = cst_p[None]

    rows_s = db * t_new
    xs2 = jnp.transpose(x_sample, (1, 0, 2)).reshape(rows_s, d)
    qb, k, kb, v, vb, u = _project(xs2, None, rows_s, p, u_time_major=False)
    to_bm = lambda z: jnp.transpose(z.reshape(t_new, db, a), (1, 0, 2))
    q_hct = jnp.transpose(qb.reshape(t_new, db, N_HEADS, 1, HEAD_DIM), (1, 2, 3, 0, 4))
    q_rep = jnp.broadcast_to(q_hct, (db, N_HEADS, 2, t_new, HEAD_DIM)).reshape(db, N_HEADS * 2 * t_new, HEAD_DIM)
    n_pool, page = cache_k.shape[1:3]
    att = _attn_decode(q_rep, to_bm(kb), to_bm(vb), cache_k[0].reshape(n_pool, page * N_HEADS, HEAD_DIM),
                       cache_v[0].reshape(n_pool, page * N_HEADS, HEAD_DIM), page_table, p)
    att_tm = jnp.transpose(att, (1, 0, 2)).reshape(rows_s, a)
    ys, hr, hi = _ssm(u.reshape(t_new, db, a), state_ssm_re[0].reshape(db, n_state),
                      state_ssm_im[0].reshape(db, n_state), p, t_new)
    hist = jnp.transpose(state_ffn_conv[0], (1, 0, 2)).reshape(1, 2 * db, d_ff)
    whole = pl.BlockSpec((rows_s, a), lambda b, t: (0, 0))
    ysm, cst_s = _post(xs2, att_tm, ys.reshape(rows_s, a), hist, p, rows_s, 1, shift=db,
                       att_spec=whole, ssm_spec=whole)
    y_sample = jnp.transpose(ysm.reshape(t_new, db, d), (1, 0, 2))
    k_sample = jnp.transpose(k.reshape(t_new, db, N_HEADS, HEAD_DIM), (1, 0, 2, 3))[None]
    v_sample = jnp.transpose(v.reshape(t_new, db, N_HEADS, HEAD_DIM), (1, 0, 2, 3))[None]
    ssm_re_s = hr.reshape(1, db, g, pdim)
    ssm_im_s = hi.reshape(1, db, g, pdim)
    conv_s = jnp.transpose(cst_s.reshape(2, db, d_ff), (1, 0, 2))[None]

    return (y_prompt, y_sample, k_prompt, v_prompt, k_sample, v_sample,
            ssm_re_p, ssm_im_p, ssm_re_s, ssm_im_s, conv_p, conv_s)
```

```python
import functools
import math

import jax
import jax.numpy as jnp
from jax import lax
from jax.experimental import pallas as pl
from jax.experimental.pallas import tpu as pltpu

N_META = 16
N_HEADS = 4
QK_DIM = 64
HEAD_DIM = 2 * QK_DIM
SSM_GROUP = 16
SSM_STATE = 64
EPS = 1e-6
NEG = -1e30
LAM_INIT = 0.8 - 0.6 * math.exp(-0.3 * 0)
LOG2E = math.log2(math.e)

LANES = 128
MXU = 256
VMEM_LIMIT = 56 * 1024 * 1024

F32 = jnp.float32
BF16 = jnp.bfloat16


def _dot(a, b):
    return jnp.dot(a, b, preferred_element_type=F32)


def _dot_nt(a, b):
    return lax.dot_general(a, b, (((1,), (1,)), ((), ())), preferred_element_type=F32)


def _const_spec(shape):
    nd = len(shape)
    return pl.BlockSpec(shape, lambda *_: (0,) * nd, pipeline_mode=pl.Buffered(1))


def _lam(lq_ref, lk_ref):
    e = jnp.exp(jnp.sum(lq_ref[...] * lk_ref[...], axis=1, keepdims=True))
    return e[0:1] - e[1:2] + LAM_INIT


def _proj_kernel(x_ref, meta_ref, g1_ref, w_ref, qg_ref, kg_ref, ones_ref,
                 qb_ref, k_ref, kb_ref, v_ref, vb_ref, u_ref, wb_s, *, n_meta):
    outs = (qb_ref, k_ref, kb_ref, v_ref, vb_ref, u_ref)
    consts = (g1_ref, wb_s, qg_ref, kg_ref, ones_ref)
    tm = qb_ref.shape[0]

    @pl.when(jnp.logical_and(pl.program_id(0) == 0, pl.program_id(1) == 0))
    def _():
        wb_s[...] = w_ref[...].astype(BF16)

    if not n_meta:
        _proj_body(x_ref[...], consts, outs)
        return
    t = pl.program_id(1)

    @pl.when(t == 0)
    def _():
        _proj_body(jnp.concatenate([meta_ref[...], x_ref[0, 0:tm - n_meta, :]], axis=0), consts, outs)

    @pl.when(t != 0)
    def _():
        _proj_body(x_ref[0], consts, outs)


def _proj_body(x, consts, outs):
    g1_ref, w_ref, qg_ref, kg_ref, ones_ref = consts
    qb_ref, k_ref, kb_ref, v_ref, vb_ref, u_ref = outs
    a = N_HEADS * HEAD_DIM
    r = lax.rsqrt(jnp.mean(x * x, axis=-1, keepdims=True) + EPS)
    xn = ((x * r) * g1_ref[...]).astype(BF16)
    proj = _dot(xn, w_ref[...])

    def comp_norm(z, g):
        z2 = (z * z).astype(BF16)
        ms = jnp.concatenate(
            [_dot(z2[:, j * MXU:(j + 1) * MXU], ones_ref[...]) for j in range(a // MXU)], axis=1)
        return (z * lax.rsqrt(ms + EPS)) * g

    qn = comp_norm(proj[:, :a], qg_ref[...])
    kn = comp_norm(proj[:, a:2 * a], kg_ref[...])
    v = proj[:, 2 * a:3 * a]
    qb_ref[...] = (qn * (QK_DIM ** -0.5 * LOG2E)).astype(BF16)
    kb_ref[...] = kn.astype(BF16)
    vb_ref[...] = v.astype(BF16)
    u_ref[...] = proj[:, 3 * a:]
    tm = x.shape[0]
    for h in range(N_HEADS):
        k_ref[pl.ds(h, tm, stride=N_HEADS), :] = kn[:, h * HEAD_DIM:(h + 1) * HEAD_DIM]
        v_ref[pl.ds(h, tm, stride=N_HEADS), :] = v[:, h * HEAD_DIM:(h + 1) * HEAD_DIM]


def _project(x, meta, tm, p, u_time_major):
    a = N_HEADS * HEAD_DIM
    d = x.shape[-1]
    if meta is None:
        nb, n_meta = 1, 0
        per_b = x.shape[0]
        x_spec = pl.BlockSpec((tm, d), lambda b, t: (t, 0))
        meta = jnp.zeros((8, d), x.dtype)
    else:
        nb, n_meta = x.shape[0], meta.shape[0]
        per_b = n_meta + x.shape[1]
        x_spec = pl.BlockSpec(
            (pl.Element(1), pl.Element(tm), pl.Element(d)),
            lambda b, t: (b, pl.multiple_of(jnp.maximum(t * tm - n_meta, 0), 8), 0))
    rows = nb * per_b
    nt = per_b // tm
    assert nt * tm == per_b and n_meta % 8 == 0
    grid = (nb, nt)
    row_spec = lambda w: pl.BlockSpec((tm, w), lambda b, t: (b * nt + t, 0))
    head_spec = pl.BlockSpec((tm * N_HEADS, HEAD_DIM), lambda b, t: (b * nt + t, 0))
    if u_time_major:
        u_shape = jax.ShapeDtypeStruct((per_b, nb * a), F32)
        u_spec = pl.BlockSpec((tm, a), lambda b, t: (t, b))
    else:
        u_shape = jax.ShapeDtypeStruct((rows, a), F32)
        u_spec = row_spec(a)
    out_shape = (jax.ShapeDtypeStruct((rows, a), BF16), jax.ShapeDtypeStruct((rows * N_HEADS, HEAD_DIM), F32),
                 jax.ShapeDtypeStruct((rows, a), BF16), jax.ShapeDtypeStruct((rows * N_HEADS, HEAD_DIM), F32),
                 jax.ShapeDtypeStruct((rows, a), BF16), u_shape)
    return pl.pallas_call(
        functools.partial(_proj_kernel, n_meta=n_meta),
        grid=grid,
        in_specs=[x_spec, _const_spec(meta.shape), _const_spec((1, d)), _const_spec(p["w_in"].shape),
                  _const_spec((1, a)), _const_spec((1, a)), _const_spec((MXU, MXU))],
        out_specs=(row_spec(a), head_spec, row_spec(a), head_spec, row_spec(a), u_spec),
        out_shape=out_shape,
        scratch_shapes=[pltpu.VMEM(p["w_in"].shape, BF16)],
        compiler_params=pltpu.CompilerParams(
            dimension_semantics=("arbitrary", "arbitrary"), vmem_limit_bytes=VMEM_LIMIT),
        name="proj",
    )(x, meta, p["norm1"], p["w_in"], p["q_gain"], p["k_gain"], p["ones_blk"])


def _attn_prompt_kernel(q_ref, k_ref, v_ref, lq_ref, lk_ref, sg_ref, o_ref, acc_s, m_s, s_s, *, tq, n_tiles):
    lam = _lam(lq_ref, lk_ref)
    sg = sg_ref[...]
    lo = lax.broadcasted_iota(jnp.int32, (1, HEAD_DIM), 1) < QK_DIM
    n_ones = acc_s.shape[1] - HEAD_DIM

    def hcols(h):
        return slice(h * HEAD_DIM, (h + 1) * HEAD_DIM)

    def stack_q(q):
        z = jnp.zeros_like(q)
        return jnp.concatenate([jnp.where(lo, q, z), jnp.where(lo, z, q)], axis=0)

    def init(h, t2):
        m_s[h, :, 0:t2] = jnp.full((1, t2), NEG, F32)
        acc_s[h, :, 0:t2] = jnp.zeros((HEAD_DIM + n_ones, t2), F32)

    def scores(slot, h, q2, kt, mask):
        s = _dot_nt(kt, q2)
        if mask is not None:
            s = jnp.where(mask, s, NEG)
        s_s[slot, h, 0:kt.shape[0], 0:q2.shape[0]] = s

    def absorb(slot, h, nk, t2, vt):
        s = s_s[slot, h, 0:nk, 0:t2]
        m_prev = m_s[h, :, 0:t2]
        m_new = jnp.maximum(m_prev, jnp.max(s, axis=0, keepdims=True))
        alpha = jnp.exp2(m_prev - m_new)
        pr = jnp.exp2(s - m_new).astype(BF16)
        va = jnp.concatenate([vt.T, jnp.ones((n_ones, nk), BF16)], axis=0)
        acc_s[h, :, 0:t2] = alpha * acc_s[h, :, 0:t2] + _dot(va, pr)
        m_s[h, :, 0:t2] = m_new

    def finish(h, t):
        acc = acc_s[h, :, 0:2 * t]
        on = acc[0:HEAD_DIM] / acc[HEAD_DIM:HEAD_DIM + 1]
        o = on[:, 0:t] - lam * on[:, t:2 * t]
        r = lax.rsqrt(jnp.mean(o * o, axis=0, keepdims=True) + EPS)
        return ((o * r).T * sg).astype(o_ref.dtype)

    def causal_mask(t, nk, offset):
        qi = lax.broadcasted_iota(jnp.int32, (nk, 2 * t), 1)
        qi = jnp.where(qi >= t, qi - t, qi)
        ki = lax.broadcasted_iota(jnp.int32, (nk, 2 * t), 0)
        return ki <= qi + offset

    tmq = LANES
    for h in range(N_HEADS):
        init(h, 2 * tmq)
        scores(0, h, stack_q(q_ref[0:tmq, hcols(h)]), k_ref[0:tmq, hcols(h)], causal_mask(tmq, tmq, 0))
    for h in range(N_HEADS):
        absorb(0, h, tmq, 2 * tmq, v_ref[0:tmq, hcols(h)])
        o_ref[0:N_META, hcols(h)] = finish(h, tmq)[0:N_META]

    dk = tq + N_META

    def q_tile(j, carry):
        qs = pl.multiple_of(N_META + j * tq, N_META)
        q2 = [stack_q(q_ref[pl.ds(qs, tq), hcols(h)]) for h in range(N_HEADS)]
        ds = pl.multiple_of(j * tq, tq)
        dmask = causal_mask(tq, dk, N_META)
        for h in range(N_HEADS):
            init(h, 2 * tq)

        def plain_scores(slot, i):
            ks = pl.multiple_of(i * tq, tq)
            for h in range(N_HEADS):
                scores(slot, h, q2[h], k_ref[pl.ds(ks, tq), hcols(h)], None)

        def diag_scores(slot):
            for h in range(N_HEADS):
                scores(slot, h, q2[h], k_ref[pl.ds(ds, dk), hcols(h)], dmask)

        def plain_absorb(slot, i):
            ks = pl.multiple_of(i * tq, tq)
            for h in range(N_HEADS):
                absorb(slot, h, tq, 2 * tq, v_ref[pl.ds(ks, tq), hcols(h)])

        def diag_absorb(slot):
            for h in range(N_HEADS):
                absorb(slot, h, dk, 2 * tq, v_ref[pl.ds(ds, dk), hcols(h)])
                o_ref[pl.ds(qs, tq), hcols(h)] = finish(h, tq)

        @pl.when(j == 0)
        def _():
            diag_scores(0)
            diag_absorb(0)

        @pl.when(j > 0)
        def _():
            plain_scores(0, 0)

        n_pairs = (j - 1) // 2

        def k_pair(pi, c):
            i = 2 * pi
            plain_scores(1, i + 1)
            plain_absorb(0, i)
            plain_scores(0, i + 2)
            plain_absorb(1, i + 1)
            return c

        lax.fori_loop(0, n_pairs, k_pair, 0)

        @pl.when(jnp.logical_and(j > 0, j % 2 == 1))
        def _():
            diag_scores(1)
            plain_absorb(0, j - 1)
            diag_absorb(1)

        @pl.when(jnp.logical_and(j > 0, j % 2 == 0))
        def _():
            plain_scores(1, j - 1)
            plain_absorb(0, j - 2)
            diag_scores(0)
            plain_absorb(1, j - 1)
            diag_absorb(0)

        return carry

    lax.fori_loop(0, n_tiles, q_tile, 0)


def _attn_prompt(qb, kb, vb, p, tq=256):
    nb, length, a = qb.shape
    n_tiles = (length - N_META) // tq
    assert N_META + n_tiles * tq == length and tq % LANES == 0
    spec = pl.BlockSpec((None, length, a), lambda b: (b, 0, 0))
    return pl.pallas_call(
        functools.partial(_attn_prompt_kernel, tq=tq, n_tiles=n_tiles),
        grid=(nb,),
        in_specs=[spec, spec, spec, _const_spec((2, QK_DIM)), _const_spec((2, QK_DIM)),
                  _const_spec((1, HEAD_DIM))],
        out_specs=spec,
        out_shape=jax.ShapeDtypeStruct((nb, length, a), BF16),
        scratch_shapes=[pltpu.VMEM((N_HEADS, HEAD_DIM + 16, 2 * tq), F32), pltpu.VMEM((N_HEADS, 1, 2 * tq), F32),
                        pltpu.VMEM((2, N_HEADS, tq + N_META, 2 * tq), F32)],
        compiler_params=pltpu.CompilerParams(
            dimension_semantics=("arbitrary",), vmem_limit_bytes=VMEM_LIMIT),
        name="attn_prompt",
    )(qb, kb, vb, p["lam_q"], p["lam_k"], p["sub_gain"])


def _decode_entry(q, kn, vn, kp, vp, lam, sg, o_ref):
    t_new = kn.shape[0]
    grp = 2 * t_new
    nrow = N_HEADS * grp
    ncol = kp[0].shape[0]
    row = lax.broadcasted_iota(jnp.int32, (nrow, HEAD_DIM), 0)
    lane = lax.broadcasted_iota(jnp.int32, (nrow, HEAD_DIM), 1)
    comp_ok = (lane >= QK_DIM) == ((row % grp) >= t_new)
    t_of_row = lax.broadcasted_iota(jnp.int32, (nrow, 1), 0) % t_new
    head_ok = (lax.broadcasted_iota(jnp.int32, (nrow, ncol), 1) % N_HEADS
               == lax.broadcasted_iota(jnp.int32, (nrow, ncol), 0) // grp)

    def per_row_head(z, t):
        return jnp.concatenate(
            [jnp.broadcast_to(z[t:t + 1, h * HEAD_DIM:(h + 1) * HEAD_DIM], (grp, HEAD_DIM)) for h in range(N_HEADS)],
            axis=0)

    q32 = jnp.where(comp_ok, q.astype(F32), 0.0)
    qb = q32.astype(BF16)
    kn, vn = kn.astype(F32), vn.astype(F32)
    s_pages = [jnp.where(head_ok, _dot_nt(qb, k_ref[...].astype(BF16)), NEG) for k_ref in kp]
    s_new = []
    for t in range(t_new):
        sc = jnp.sum(q32 * per_row_head(kn, t), axis=1, keepdims=True)
        s_new.append(jnp.where(t_of_row >= t, sc, NEG))

    m = s_pages[0]
    for s in s_pages[1:]:
        m = jnp.maximum(m, s)
    m = jnp.max(m, axis=1, keepdims=True)
    for s in s_new:
        m = jnp.maximum(m, s)

    acc = jnp.zeros((nrow, HEAD_DIM), F32)
    psum = None
    for s, v_ref in zip(s_pages, vp):
        pr = jnp.exp2(s - m)
        psum = pr if psum is None else psum + pr
        acc = acc + _dot(pr.astype(BF16), v_ref[...].astype(BF16))
    lsum = jnp.sum(psum, axis=1, keepdims=True)
    for t in range(t_new):
        pr = jnp.exp2(s_new[t] - m)
        lsum = lsum + pr
        acc = acc + pr * per_row_head(vn, t)
    on = acc / lsum
    for h in range(N_HEADS):
        r0 = h * grp
        o = on[r0:r0 + t_new] - lam * on[r0 + t_new:r0 + grp]
        r = lax.rsqrt(jnp.mean(o * o, axis=-1, keepdims=True) + EPS)
        o_ref[:, h * HEAD_DIM:(h + 1) * HEAD_DIM] = ((o * r) * sg).astype(o_ref.dtype)


def _attn_decode_kernel(pt_ref, q_ref, kn_ref, vn_ref, lq_ref, lk_ref, sg_ref, ck_hbm, cv_hbm, o_ref,
                        kbuf, vbuf, sem, *, n_pages, ahead):
    g, n = pl.program_id(0), pl.num_programs(0)
    ring = kbuf.shape[0]

    def page_copies(slot, page_of):
        return [pltpu.make_async_copy(hbm.at[page_of(pg)], buf.at[slot, pg], sem.at[w, slot])
                for pg in range(n_pages) for w, (hbm, buf) in enumerate(((ck_hbm, kbuf), (cv_hbm, vbuf)))]

    def start(entry):
        for c in page_copies(entry % ring, lambda pg: pt_ref[entry * n_pages + pg]):
            c.start()

    @pl.when(g == 0)
    def _():
        for entry in range(ahead):
            start(entry)

    @pl.when(g + ahead < n)
    def _():
        start(g + ahead)

    slot = g % ring
    for c in page_copies(slot, lambda pg: 0):
        c.wait()
    _decode_entry(q_ref[...], kn_ref[...], vn_ref[...], [kbuf.at[slot, pg] for pg in range(n_pages)],
                  [vbuf.at[slot, pg] for pg in range(n_pages)], _lam(lq_ref, lk_ref), sg_ref[...], o_ref)


def _attn_decode(q_rep, kn, vn, cache_k, cache_v, page_table, p, ahead=2):
    db, nrow, _ = q_rep.shape
    t_new, a = kn.shape[1:]
    n_pages = page_table.shape[1]
    prow = cache_k.shape[1]
    assert db >= ahead
    const = lambda shape: pl.BlockSpec(shape, lambda g, pt_ref: (0,) * len(shape))
    per_entry = lambda r, w: pl.BlockSpec((None, r, w), lambda g, pt_ref: (g, 0, 0))
    grid_spec = pltpu.PrefetchScalarGridSpec(
        num_scalar_prefetch=1,
        grid=(db,),
        in_specs=[per_entry(nrow, HEAD_DIM), per_entry(t_new, a), per_entry(t_new, a),
                  const((2, QK_DIM)), const((2, QK_DIM)), const((1, HEAD_DIM)),
                  pl.BlockSpec(memory_space=pl.ANY), pl.BlockSpec(memory_space=pl.ANY)],
        out_specs=per_entry(t_new, a),
        scratch_shapes=[pltpu.VMEM((ahead + 1, n_pages, prow, HEAD_DIM), cache_k.dtype),
                        pltpu.VMEM((ahead + 1, n_pages, prow, HEAD_DIM), cache_v.dtype),
                        pltpu.SemaphoreType.DMA((2, ahead + 1))],
    )
    return pl.pallas_call(
        functools.partial(_attn_decode_kernel, n_pages=n_pages, ahead=ahead),
        grid_spec=grid_spec,
        out_shape=jax.ShapeDtypeStruct((db, t_new, a), BF16),
        compiler_params=pltpu.CompilerParams(
            dimension_semantics=("arbitrary",), vmem_limit_bytes=VMEM_LIMIT),
        name="attn_decode",
    )(page_table.reshape(-1), q_rep, kn, vn, p["lam_q"], p["lam_k"], p["sub_gain"], cache_k, cache_v)


def _ssm_param_kernel(ar_ref, ai_ref, ldt_ref, br_ref, bi_ref, abr_ref, abi_ref, bbr_ref, bbi_ref):
    ar, ai = ar_ref[...], ai_ref[...]
    dt = jnp.exp(ldt_ref[...])
    mag = jnp.exp(ar * dt)
    abr, abi = mag * jnp.cos(ai * dt), mag * jnp.sin(ai * dt)
    den = ar * ar + ai * ai
    nr, ni = abr - 1.0, abi
    gr, gi = (nr * ar + ni * ai) / den, (ni * ar - nr * ai) / den
    abr_ref[...] = abr
    abi_ref[...] = abi
    for c in range(SSM_GROUP):
        br, bi = br_ref[c], bi_ref[c]
        bbr_ref[c] = gr * br - gi * bi
        bbi_ref[c] = gr * bi + gi * br


def _ssm_params(a_re, a_im, log_dt, b_re, b_im):
    g, pdim = a_re.shape
    c = b_re.shape[-1]
    b_re_t = jnp.transpose(b_re, (2, 0, 1))
    b_im_t = jnp.transpose(b_im, (2, 0, 1))
    gp = jax.ShapeDtypeStruct((g, pdim), F32)
    cgp = jax.ShapeDtypeStruct((c, g, pdim), F32)
    return pl.pallas_call(_ssm_param_kernel, out_shape=(gp, gp, cgp, cgp), name="ssm_params")(
        a_re, a_im, log_dt.reshape(g, 1), b_re_t, b_im_t)


def _ssm_kernel(u_ref, up_ref, h0r_ref, h0i_ref, abr_ref, abi_ref, bre_ref, bim_ref, crt_ref, cit_ref,
                d_ref, wg_ref, bg_ref, y_ref, hr_ref, hi_ref, xr_s, xi_s, ab_s, *, tc, nb, n_chunks):
    i = pl.program_id(0)

    @pl.when(i == 0)
    def _():
        hr_ref[...] = h0r_ref[...]
        hi_ref[...] = h0i_ref[...]
        ab_s[0] = jnp.broadcast_to(abr_ref[...], ab_s.shape[1:])
        ab_s[1] = jnp.broadcast_to(abi_ref[...], ab_s.shape[1:])
        if n_chunks > 1:
            xr_s[1] = jnp.zeros(xr_s.shape[1:], F32)
            xi_s[1] = jnp.zeros(xi_s.shape[1:], F32)

    def body(cur, prev):
        rows, width = tc * nb, d_ref.shape[1]
        ub = u_ref[...].reshape(rows, width).astype(BF16)
        kin, nout = bre_ref.shape[1], bre_ref.shape[2]
        for mblk in range(bre_ref.shape[0]):
            um = ub[:, mblk * kin:(mblk + 1) * kin]
            xr_s[cur, :, mblk * nout:(mblk + 1) * nout] = _dot(um, bre_ref[mblk])
            xi_s[cur, :, mblk * nout:(mblk + 1) * nout] = _dot(um, bim_ref[mblk])

        hr, hi = hr_ref[...], hi_ref[...]
        for t in range(tc):
            r = slice(t * nb, (t + 1) * nb)
            abr, abi = ab_s[0], ab_s[1]
            nhr = abr * hr - abi * hi + xr_s[cur, r, :]
            nhi = abr * hi + abi * hr + xi_s[cur, r, :]
            xr_s[cur, r, :] = nhr
            xi_s[cur, r, :] = nhi
            hr, hi = nhr, nhi
        live = i < n_chunks
        hr_ref[...] = jnp.where(live, hr, hr_ref[...])
        hi_ref[...] = jnp.where(live, hi, hi_ref[...])

        kout = crt_ref.shape[1]
        ys = []
        for j in range(crt_ref.shape[0]):
            hrj = xr_s[prev, :, j * kout:(j + 1) * kout].astype(BF16)
            hij = xi_s[prev, :, j * kout:(j + 1) * kout].astype(BF16)
            ys.append(_dot(hrj, crt_ref[j]) - _dot(hij, cit_ref[j]))
        y = jnp.concatenate(ys, axis=1) + d_ref[...] * up_ref[...].reshape(rows, width)
        g = jax.nn.gelu(y)
        out = g * jax.nn.sigmoid(_dot(g.astype(BF16), wg_ref[...].astype(BF16)) + bg_ref[...])
        y_ref[...] = out.reshape(tc, nb, width)

    if n_chunks == 1:
        body(0, 0)
    else:
        for par in (0, 1):
            pl.when(i % 2 == par)(functools.partial(body, par, 1 - par))


def _ssm(u3, h0r, h0i, p, tc):
    t_len, nb, width = u3.shape
    n_state = h0r.shape[1]
    n_chunks = t_len // tc
    assert n_chunks * tc == t_len
    st = jax.ShapeDtypeStruct((nb, n_state), F32)
    st_spec = pl.BlockSpec((nb, n_state), lambda i: (0, 0))
    chunk = lambda index: pl.BlockSpec((tc, nb, width), lambda i: (index(i), 0, 0))
    return pl.pallas_call(
        functools.partial(_ssm_kernel, tc=tc, nb=nb, n_chunks=n_chunks),
        grid=(n_chunks + 1 if n_chunks > 1 else 1,),
        in_specs=[chunk(lambda i: jnp.minimum(i, n_chunks - 1)), chunk(lambda i: jnp.maximum(i - 1, 0)),
                  _const_spec((nb, n_state)), _const_spec((nb, n_state)),
                  _const_spec((1, n_state)), _const_spec((1, n_state)),
                  _const_spec(p["bre"].shape), _const_spec(p["bim"].shape),
                  _const_spec(p["crt"].shape), _const_spec(p["cit"].shape),
                  _const_spec((1, width)), _const_spec((width, width)), _const_spec((1, width))],
        out_specs=(chunk(lambda i: jnp.maximum(i - 1, 0)), st_spec, st_spec),
        out_shape=(jax.ShapeDtypeStruct((t_len, nb, width), F32), st, st),
        scratch_shapes=[pltpu.VMEM((min(n_chunks, 2), tc * nb, n_state), F32)] * 2
        + [pltpu.VMEM((2, nb, n_state), F32)],
        compiler_params=pltpu.CompilerParams(
            dimension_semantics=("arbitrary",), vmem_limit_bytes=VMEM_LIMIT),
        name="ssm",
    )(u3, u3, h0r, h0i, p["abr"], p["abi"], p["bre"], p["bim"], p["crt"], p["cit"],
      p["d_skip"], p["w_glu"], p["b_glu"])


def _post_head(x_ref, att_ref, ssm_ref, wo_ref, g2_ref, y_ref, xn_s):
    tm, a = y_ref.shape[0], wo_ref.shape[0] // 2
    mix = jnp.concatenate([att_ref[...].reshape(tm, a), ssm_ref[...].reshape(tm, a).astype(BF16)], axis=1)
    xm = x_ref[...] + _dot(mix, wo_ref[...].astype(BF16))
    y_ref[...] = xm
    r = lax.rsqrt(jnp.mean(xm * xm, axis=-1, keepdims=True) + EPS)
    xn_s[...] = ((xm * r) * g2_ref[...]).astype(BF16)


def _post_chunks(j0, j1, wgate_ref, wup_ref, cw_ref, cb_ref, xn_s, h_s, ap_s, hist_s, *, shift, fc):
    tm = xn_s.shape[0]
    off = ap_s.shape[1] - tm
    for j in range(j0, j1):
        cs = slice(j * fc, (j + 1) * fc)
        ap = ap_s.at[j % 2]
        gate = _dot(xn_s[...], wgate_ref[:, cs])
        up = _dot(xn_s[...], wup_ref[:, cs])
        ap[off - 2 * shift:off, :] = hist_s[:, cs]
        ap[off:off + tm, :] = gate
        conv = (cb_ref[:, cs] + cw_ref[0:1, cs] * ap[off - 2 * shift:off - 2 * shift + tm, :]
                + cw_ref[1:2, cs] * ap[off - shift:off - shift + tm, :] + cw_ref[2:3, cs] * gate)
        hist_s[:, cs] = ap[off + tm - 2 * shift:off + tm, :]
        h_s[:, cs] = (jax.nn.gelu(conv) * up).astype(BF16)


def _post_kernel(x_ref, att_ref, ssm_ref, *refs, shift, fc, n_lead):
    hist_src, (wo_ref, g2_ref, wgate_ref, wup_ref, cw_ref, cb_ref, wd_ref, y_ref, cst_ref,
               xn_s, h_s, ap_s, hist_s) = refs[:-13], refs[-13:]
    if n_lead:
        xl_ref, al_ref, sl_ref = hist_src
        lead_s = hist_s.at[1]
        hist_s = hist_s.at[0]

        @pl.when(jnp.logical_and(pl.program_id(0) == 0, pl.program_id(1) == 0))
        def _():
            mix = jnp.concatenate([al_ref[...], sl_ref[...].astype(BF16)], axis=1)
            xm = xl_ref[...] + _dot(mix, wo_ref[...].astype(BF16))
            r = lax.rsqrt(jnp.mean(xm * xm, axis=-1, keepdims=True) + EPS)
            gate = _dot(((xm * r) * g2_ref[...]).astype(BF16), wgate_ref[...])
            lead_s[...] = gate[n_lead - 2 * shift:n_lead, :]

        first_hist = lead_s
    else:
        first_hist, = hist_src
        hist_s = hist_s.at[0]

    @pl.when(pl.program_id(1) == 0)
    def _():
        hist_s[...] = first_hist[...]

    _post_head(x_ref, att_ref, ssm_ref, wo_ref, g2_ref, y_ref, xn_s)
    _post_chunks(0, wgate_ref.shape[1] // fc, wgate_ref, wup_ref, cw_ref, cb_ref, xn_s, h_s, ap_s, hist_s,
                 shift=shift, fc=fc)
    y_ref[...] += _dot(h_s[...], wd_ref[...])
    cst_ref[...] = hist_s[...]


def _post(x2d, att, ssm, hist, p, tm, nb, shift, att_spec, ssm_spec, lead=None, fc=256):
    rows, d = x2d.shape
    d_ff = p["w_gate"].shape[1]
    per_b = rows // nb
    nt = per_b // tm
    assert nt * tm == per_b and d_ff % fc == 0 and tm >= 2 * shift
    off = -(-2 * shift // 8) * 8
    row_spec = lambda w: pl.BlockSpec((tm, w), lambda b, t: (b * nt + t, 0))
    hist_spec = pl.BlockSpec((None, 2 * shift, d_ff), lambda b, t: (b, 0, 0))
    if lead is None:
        hist_src, hist_specs, n_lead = (hist,), [hist_spec], 0
    else:
        hist_src, hist_specs, n_lead = lead, [_const_spec(z.shape) for z in lead], lead[0].shape[0]
        assert n_lead >= 2 * shift
    return pl.pallas_call(
        functools.partial(_post_kernel, shift=shift, fc=fc, n_lead=n_lead),
        grid=(nb, nt),
        in_specs=[row_spec(d), att_spec, ssm_spec, *hist_specs,
                  _const_spec(p["w_out"].shape), _const_spec((1, d)),
                  _const_spec(p["w_gate"].shape), _const_spec(p["w_up"].shape),
                  _const_spec((3, d_ff)), _const_spec((1, d_ff)), _const_spec(p["w_down"].shape)],
        out_specs=(row_spec(d), hist_spec),
        out_shape=(jax.ShapeDtypeStruct((rows, d), F32), jax.ShapeDtypeStruct((nb, 2 * shift, d_ff), F32)),
        scratch_shapes=[pltpu.VMEM((tm, d), BF16), pltpu.VMEM((tm, d_ff), BF16),
                        pltpu.VMEM((2, off + tm, fc), F32), pltpu.VMEM((2, 2 * shift, d_ff), F32)],
        compiler_params=pltpu.CompilerParams(
            dimension_semantics=("arbitrary", "arbitrary"), vmem_limit_bytes=VMEM_LIMIT),
        name="post",
    )(x2d, att, ssm, *hist_src, p["w_out"], p["norm2"], p["w_gate"], p["w_up"], p["conv_w"], p["conv_b"],
      p["w_down"])


def _prepare_params(norm1, w_in, q_norm, k_norm, lam_q, lam_k, sub_norm, ssm_a_re, ssm_a_im, ssm_log_dt,
                    ssm_b_re, ssm_b_im, ssm_c_re, ssm_c_im, ssm_d, w_glu, b_glu, w_out, norm2, w_gate,
                    w_up, ffn_conv_w, ffn_conv_b, w_down):
    l = 0
    g, pdim = ssm_a_re[l].shape
    c = SSM_GROUP
    a = N_HEADS * HEAD_DIM
    abr, abi, bbr, bbi = _ssm_params(ssm_a_re[l], ssm_a_im[l], ssm_log_dt[l], ssm_b_re[l], ssm_b_im[l])
    gi = LANES // c

    def in_blocks(bb):
        bb = jnp.transpose(bb, (1, 0, 2)).reshape(g // gi, gi, c, 1, pdim)
        same = jnp.eye(gi, dtype=bool)[None, :, None, :, None]
        return jnp.where(same, bb, 0.0).astype(BF16).reshape(g // gi, gi * c, gi * pdim)

    go = MXU // c

    def out_blocks(cc):
        cc = jnp.transpose(cc.reshape(g // go, go, c, pdim), (0, 1, 3, 2))[:, :, :, None, :]
        same = jnp.eye(go, dtype=bool)[None, :, None, :, None]
        return jnp.where(same, cc, 0.0).astype(BF16).reshape(g // go, go * pdim, go * c)

    comp = jnp.arange(MXU) // QK_DIM
    ones_blk = (comp[:, None] == comp[None, :]).astype(BF16) * (1.0 / QK_DIM)
    return {
        "norm1": norm1[l].reshape(1, -1), "w_in": w_in[l],
        "q_gain": jnp.tile(q_norm[l].reshape(-1), N_HEADS).reshape(1, a),
        "k_gain": jnp.tile(k_norm[l].reshape(-1), N_HEADS).reshape(1, a),
        "ones_blk": ones_blk.astype(BF16),
        "lam_q": lam_q[l], "lam_k": lam_k[l],
        "sub_gain": (sub_norm[l] * (1.0 - LAM_INIT)).reshape(1, HEAD_DIM),
        "abr": abr.reshape(1, g * pdim), "abi": abi.reshape(1, g * pdim),
        "bre": in_blocks(bbr), "bim": in_blocks(bbi),
        "crt": out_blocks(ssm_c_re[l]), "cit": out_blocks(ssm_c_im[l]),
        "d_skip": ssm_d[l].reshape(1, g * c), "w_glu": w_glu[l], "b_glu": b_glu[l].reshape(1, -1),
        "w_out": w_out[l], "norm2": norm2[l].reshape(1, -1),
        "w_gate": w_gate[l].astype(BF16), "w_up": w_up[l].astype(BF16),
        "conv_w": ffn_conv_w[l], "conv_b": ffn_conv_b[l].reshape(1, -1), "w_down": w_down[l].astype(BF16),
    }


def _row_tile(length, cap=768):
    best = None
    for t in range(16, cap + 1, 16):
        if length % t == 0:
            best = t
    assert best is not None
    return best


def kernel(x_prompt, x_sample, cache_k, cache_v, state_ssm_re, state_ssm_im, state_ffn_conv, page_table, meta_tokens, norm1, w_in, q_norm, k_norm, lam_q, lam_k, sub_norm, ssm_a_re, ssm_a_im, ssm_log_dt, ssm_b_re, ssm_b_im, ssm_c_re, ssm_c_im, ssm_d, w_glu, b_glu, w_out, norm2, w_gate, w_up, ffn_conv_w, ffn_conv_b, w_down):
    assert norm1.shape[0] == 1, "single-layer stack"
    p = _prepare_params(norm1, w_in, q_norm, k_norm, lam_q, lam_k, sub_norm, ssm_a_re, ssm_a_im, ssm_log_dt,
                        ssm_b_re, ssm_b_im, ssm_c_re, ssm_c_im, ssm_d, w_glu, b_glu, w_out, norm2, w_gate,
                        w_up, ffn_conv_w, ffn_conv_b, w_down)
    nb, seq, d = x_prompt.shape
    db, t_new, _ = x_sample.shape
    a = N_HEADS * HEAD_DIM
    g, pdim = ssm_a_re.shape[1:]
    n_state = g * pdim
    d_ff = w_gate.shape[-1]
    length = seq + N_META

    meta = meta_tokens.astype(x_prompt.dtype)
    tm = _row_tile(length)
    qb, k, kb, v, vb, u = _project(x_prompt, meta, tm, p, u_time_major=True)
    att = _attn_prompt(qb.reshape(nb, length, a), kb.reshape(nb, length, a), vb.reshape(nb, length, a), p)
    zst = jnp.zeros((nb, n_state), F32)
    tc = _row_tile(length * nb, cap=768) // nb
    ys, hr, hi = _ssm(u.reshape(length, nb, a), zst, zst, p, tc)
    ys2 = ys.reshape(length, nb * a)
    tp = _row_tile(seq, cap=512)
    att_spec = pl.BlockSpec((pl.Element(1), pl.Element(tp), pl.Element(a)),
                            lambda b, t: (b, pl.multiple_of(N_META + t * tp, N_META), 0))
    ssm_spec = pl.BlockSpec((pl.Element(tp), pl.Element(a)),
                            lambda b, t: (pl.multiple_of(N_META + t * tp, N_META), pl.multiple_of(b * a, a)))
    yp, cst_p = _post(x_prompt.reshape(nb * seq, d), att, ys2, None, p, tp, nb, shift=1,
                      att_spec=att_spec, ssm_spec=ssm_spec, lead=(meta, att[0, :N_META], ys2[:N_META, :a]))
    y_prompt = yp.reshape(nb, seq, d)
    k_prompt = k.reshape(1, nb, length, N_HEADS, HEAD_DIM)
    v_prompt = v.reshape(1, nb, length, N_HEADS, HEAD_DIM)
    ssm_re_p = hr.reshape(1, nb, g, pdim)
    ssm_im_p = hi.reshape(1, nb, g, pdim)
    conv_p = cst_p[None]

    rows_s = db * t_new
    xs2 = jnp.transpose(x_sample, (1, 0, 2)).reshape(rows_s, d)
    qb, k, kb, v, vb, u = _project(xs2, None, rows_s, p, u_time_major=False)
    to_bm = lambda z: jnp.transpose(z.reshape(t_new, db, a), (1, 0, 2))
    q_hct = jnp.transpose(qb.reshape(t_new, db, N_HEADS, 1, HEAD_DIM), (1, 2, 3, 0, 4))
    q_rep = jnp.broadcast_to(q_hct, (db, N_HEADS, 2, t_new, HEAD_DIM)).reshape(db, N_HEADS * 2 * t_new, HEAD_DIM)
    n_pool, page = cache_k.shape[1:3]
    att = _attn_decode(q_rep, to_bm(kb), to_bm(vb), cache_k[0].reshape(n_pool, page * N_HEADS, HEAD_DIM),
                       cache_v[0].reshape(n_pool, page * N_HEADS, HEAD_DIM), page_table, p)
    att_tm = jnp.transpose(att, (1, 0, 2)).reshape(rows_s, a)
    ys, hr, hi = _ssm(u.reshape(t_new, db, a), state_ssm_re[0].reshape(db, n_state),
                      state_ssm_im[0].reshape(db, n_state), p, t_new)
    hist = jnp.transpose(state_ffn_conv[0], (1, 0, 2)).reshape(1, 2 * db, d_ff)
    whole = pl.BlockSpec((rows_s, a), lambda b, t: (0, 0))
    ysm, cst_s = _post(xs2, att_tm, ys.reshape(rows_s, a), hist, p, rows_s, 1, shift=db,
                       att_spec=whole, ssm_spec=whole)
    y_sample = jnp.transpose(ysm.reshape(t_new, db, d), (1, 0, 2))
    k_sample = jnp.transpose(k.reshape(t_new, db, N_HEADS, HEAD_DIM), (1, 0, 2, 3))[None]
    v_sample = jnp.transpose(v.reshape(t_new, db, N_HEADS, HEAD_DIM), (1, 0, 2, 3))[None]
    ssm_re_s = hr.reshape(1, db, g, pdim)
    ssm_im_s = hi.reshape(1, db, g, pdim)
    conv_s = jnp.transpose(cst_s.reshape(2, db, d_ff), (1, 0, 2))[None]

    return (y_prompt, y_sample, k_prompt, v_prompt, k_sample, v_sample,
            ssm_re_p, ssm_im_p, ssm_re_s, ssm_im_s, conv_p, conv_s)
```

```python
import functools
import math

import jax
import jax.numpy as jnp
from jax import lax
from jax.experimental import pallas as pl
from jax.experimental.pallas import tpu as pltpu

N_META = 16
N_HEADS = 4
QK_DIM = 64
HEAD_DIM = 2 * QK_DIM
SSM_GROUP = 16
SSM_STATE = 64
EPS = 1e-6
NEG = -1e30
LAM_INIT = 0.8 - 0.6 * math.exp(-0.3 * 0)
LOG2E = math.log2(math.e)

LANES = 128
MXU = 256
VMEM_LIMIT = 56 * 1024 * 1024

F32 = jnp.float32
BF16 = jnp.bfloat16


def _dot(a, b):
    return jnp.dot(a, b, preferred_element_type=F32)


def _dot_nt(a, b):
    return lax.dot_general(a, b, (((1,), (1,)), ((), ())), preferred_element_type=F32)


def _const_spec(shape):
    nd = len(shape)
    return pl.BlockSpec(shape, lambda *_: (0,) * nd, pipeline_mode=pl.Buffered(1))


def _lam(lq_ref, lk_ref):
    e = jnp.exp(jnp.sum(lq_ref[...] * lk_ref[...], axis=1, keepdims=True))
    return e[0:1] - e[1:2] + LAM_INIT


def _proj_kernel(x_ref, meta_ref, g1_ref, w_ref, qg_ref, kg_ref, ones_ref,
                 qb_ref, k_ref, kb_ref, v_ref, vb_ref, u_ref, wb_s, *, n_meta):
    outs = (qb_ref, k_ref, kb_ref, v_ref, vb_ref, u_ref)
    consts = (g1_ref, wb_s, qg_ref, kg_ref, ones_ref)
    tm = qb_ref.shape[0]

    @pl.when(jnp.logical_and(pl.program_id(0) == 0, pl.program_id(1) == 0))
    def _():
        wb_s[...] = w_ref[...].astype(BF16)

    if not n_meta:
        _proj_body(x_ref[...], consts, outs)
        return
    t = pl.program_id(1)

    @pl.when(t == 0)
    def _():
        _proj_body(jnp.concatenate([meta_ref[...], x_ref[0, 0:tm - n_meta, :]], axis=0), consts, outs)

    @pl.when(t != 0)
    def _():
        _proj_body(x_ref[0], consts, outs)


def _proj_body(x, consts, outs):
    g1_ref, w_ref, qg_ref, kg_ref, ones_ref = consts
    qb_ref, k_ref, kb_ref, v_ref, vb_ref, u_ref = outs
    a = N_HEADS * HEAD_DIM
    r = lax.rsqrt(jnp.mean(x * x, axis=-1, keepdims=True) + EPS)
    xn = ((x * r) * g1_ref[...]).astype(BF16)
    proj = _dot(xn, w_ref[...])

    def comp_norm(z, g):
        z2 = (z * z).astype(BF16)
        ms = jnp.concatenate(
            [_dot(z2[:, j * MXU:(j + 1) * MXU], ones_ref[...]) for j in range(a // MXU)], axis=1)
        return (z * lax.rsqrt(ms + EPS)) * g

    qn = comp_norm(proj[:, :a], qg_ref[...])
    kn = comp_norm(proj[:, a:2 * a], kg_ref[...])
    v = proj[:, 2 * a:3 * a]
    qb_ref[...] = (qn * (QK_DIM ** -0.5 * LOG2E)).astype(BF16)
    kb_ref[...] = kn.astype(BF16)
    vb_ref[...] = v.astype(BF16)
    u_ref[...] = proj[:, 3 * a:]
    tm = x.shape[0]
    for h in range(N_HEADS):
        k_ref[pl.ds(h, tm, stride=N_HEADS), :] = kn[:, h * HEAD_DIM:(h + 1) * HEAD_DIM]
        v_ref[pl.ds(h, tm, stride=N_HEADS), :] = v[:, h * HEAD_DIM:(h + 1) * HEAD_DIM]


def _project(x, meta, tm, p, u_time_major):
    a = N_HEADS * HEAD_DIM
    d = x.shape[-1]
    if meta is None:
        nb, n_meta = 1, 0
        per_b = x.shape[0]
        x_spec = pl.BlockSpec((tm, d), lambda b, t: (t, 0))
        meta = jnp.zeros((8, d), x.dtype)
    else:
        nb, n_meta = x.shape[0], meta.shape[0]
        per_b = n_meta + x.shape[1]
        x_spec = pl.BlockSpec(
            (pl.Element(1), pl.Element(tm), pl.Element(d)),
            lambda b, t: (b, pl.multiple_of(jnp.maximum(t * tm - n_meta, 0), 8), 0))
    rows = nb * per_b
    nt = per_b // tm
    assert nt * tm == per_b and n_meta % 8 == 0
    grid = (nb, nt)
    row_spec = lambda w: pl.BlockSpec((tm, w), lambda b, t: (b * nt + t, 0))
    head_spec = pl.BlockSpec((tm * N_HEADS, HEAD_DIM), lambda b, t: (b * nt + t, 0))
    if u_time_major:
        u_shape = jax.ShapeDtypeStruct((per_b, nb * a), F32)
        u_spec = pl.BlockSpec((tm, a), lambda b, t: (t, b))
    else:
        u_shape = jax.ShapeDtypeStruct((rows, a), F32)
        u_spec = row_spec(a)
    out_shape = (jax.ShapeDtypeStruct((rows, a), BF16), jax.ShapeDtypeStruct((rows * N_HEADS, HEAD_DIM), F32),
                 jax.ShapeDtypeStruct((rows, a), BF16), jax.ShapeDtypeStruct((rows * N_HEADS, HEAD_DIM), F32),
                 jax.ShapeDtypeStruct((rows, a), BF16), u_shape)
    return pl.pallas_call(
        functools.partial(_proj_kernel, n_meta=n_meta),
        grid=grid,
        in_specs=[x_spec, _const_spec(meta.shape), _const_spec((1, d)), _const_spec(p["w_in"].shape),
                  _const_spec((1, a)), _const_spec((1, a)), _const_spec((MXU, MXU))],
        out_specs=(row_spec(a), head_spec, row_spec(a), head_spec, row_spec(a), u_spec),
        out_shape=out_shape,
        scratch_shapes=[pltpu.VMEM(p["w_in"].shape, BF16)],
        compiler_params=pltpu.CompilerParams(
            dimension_semantics=("arbitrary", "arbitrary"), vmem_limit_bytes=VMEM_LIMIT),
        name="proj",
    )(x, meta, p["norm1"], p["w_in"], p["q_gain"], p["k_gain"], p["ones_blk"])


def _attn_prompt_kernel(q_ref, k_ref, v_ref, lq_ref, lk_ref, sg_ref, o_ref, acc_s, m_s, s_s, *, tq, n_tiles):
    lam = _lam(lq_ref, lk_ref)
    sg = sg_ref[...]
    lo = lax.broadcasted_iota(jnp.int32, (1, HEAD_DIM), 1) < QK_DIM
    n_ones = acc_s.shape[1] - HEAD_DIM

    def hcols(h):
        return slice(h * HEAD_DIM, (h + 1) * HEAD_DIM)

    def stack_q(q):
        z = jnp.zeros_like(q)
        return jnp.concatenate([jnp.where(lo, q, z), jnp.where(lo, z, q)], axis=0)

    def init(h, t2):
        m_s[h, :, 0:t2] = jnp.full((1, t2), NEG, F32)
        acc_s[h, :, 0:t2] = jnp.zeros((HEAD_DIM + n_ones, t2), F32)

    def scores(slot, h, q2, kt, mask):
        s = _dot_nt(kt, q2)
        if mask is not None:
            s = jnp.where(mask, s, NEG)
        s_s[slot, h, 0:kt.shape[0], 0:q2.shape[0]] = s

    def absorb(slot, h, nk, t2, vt):
        s = s_s[slot, h, 0:nk, 0:t2]
        m_prev = m_s[h, :, 0:t2]
        m_new = jnp.maximum(m_prev, jnp.max(s, axis=0, keepdims=True))
        alpha = jnp.exp2(m_prev - m_new)
        pr = jnp.exp2(s - m_new).astype(BF16)
        va = jnp.concatenate([vt.T, jnp.ones((n_ones, nk), BF16)], axis=0)
        acc_s[h, :, 0:t2] = alpha * acc_s[h, :, 0:t2] + _dot(va, pr)
        m_s[h, :, 0:t2] = m_new

    def finish(h, t):
        acc = acc_s[h, :, 0:2 * t]
        on = acc[0:HEAD_DIM] / acc[HEAD_DIM:HEAD_DIM + 1]
        o = on[:, 0:t] - lam * on[:, t:2 * t]
        r = lax.rsqrt(jnp.mean(o * o, axis=0, keepdims=True) + EPS)
        return ((o * r).T * sg).astype(o_ref.dtype)

    def causal_mask(t, nk, offset):
        qi = lax.broadcasted_iota(jnp.int32, (nk, 2 * t), 1)
        qi = jnp.where(qi >= t, qi - t, qi)
        ki = lax.broadcasted_iota(jnp.int32, (nk, 2 * t), 0)
        return ki <= qi + offset

    tmq = LANES
    for h in range(N_HEADS):
        init(h, 2 * tmq)
        scores(0, h, stack_q(q_ref[0:tmq, hcols(h)]), k_ref[0:tmq, hcols(h)], causal_mask(tmq, tmq, 0))
    for h in range(N_HEADS):
        absorb(0, h, tmq, 2 * tmq, v_ref[0:tmq, hcols(h)])
        o_ref[0:N_META, hcols(h)] = finish(h, tmq)[0:N_META]

    dk = tq + N_META

    def q_tile(j, carry):
        qs = pl.multiple_of(N_META + j * tq, N_META)
        q2 = [stack_q(q_ref[pl.ds(qs, tq), hcols(h)]) for h in range(N_HEADS)]
        ds = pl.multiple_of(j * tq, tq)
        dmask = causal_mask(tq, dk, N_META)
        for h in range(N_HEADS):
            init(h, 2 * tq)

        def plain_scores(slot, i):
            ks = pl.multiple_of(i * tq, tq)
            for h in range(N_HEADS):
                scores(slot, h, q2[h], k_ref[pl.ds(ks, tq), hcols(h)], None)

        def diag_scores(slot):
            for h in range(N_HEADS):
                scores(slot, h, q2[h], k_ref[pl.ds(ds, dk), hcols(h)], dmask)

        def plain_absorb(slot, i):
            ks = pl.multiple_of(i * tq, tq)
            for h in range(N_HEADS):
                absorb(slot, h, tq, 2 * tq, v_ref[pl.ds(ks, tq), hcols(h)])

        def diag_absorb(slot):
            for h in range(N_HEADS):
                absorb(slot, h, dk, 2 * tq, v_ref[pl.ds(ds, dk), hcols(h)])
                o_ref[pl.ds(qs, tq), hcols(h)] = finish(h, tq)

        @pl.when(j == 0)
        def _():
            diag_scores(0)
            diag_absorb(0)

        @pl.when(j > 0)
        def _():
            plain_scores(0, 0)

        n_pairs = (j - 1) // 2

        def k_pair(pi, c):
            i = 2 * pi
            plain_scores(1, i + 1)
            plain_absorb(0, i)
            plain_scores(0, i + 2)
            plain_absorb(1, i + 1)
            return c

        lax.fori_loop(0, n_pairs, k_pair, 0)

        @pl.when(jnp.logical_and(j > 0, j % 2 == 1))
        def _():
            diag_scores(1)
            plain_absorb(0, j - 1)
            diag_absorb(1)

        @pl.when(jnp.logical_and(j > 0, j % 2 == 0))
        def _():
            plain_scores(1, j - 1)
            plain_absorb(0, j - 2)
            diag_scores(0)
            plain_absorb(1, j - 1)
            diag_absorb(0)

        return carry

    lax.fori_loop(0, n_tiles, q_tile, 0)


def _attn_prompt(qb, kb, vb, p, tq=256):
    nb, length, a = qb.shape
    n_tiles = (length - N_META) // tq
    assert N_META + n_tiles * tq == length and tq % LANES == 0
    spec = pl.BlockSpec((None, length, a), lambda b: (b, 0, 0))
    return pl.pallas_call(
        functools.partial(_attn_prompt_kernel, tq=tq, n_tiles=n_tiles),
        grid=(nb,),
        in_specs=[spec, spec, spec, _const_spec((2, QK_DIM)), _const_spec((2, QK_DIM)),
                  _const_spec((1, HEAD_DIM))],
        out_specs=spec,
        out_shape=jax.ShapeDtypeStruct((nb, length, a), BF16),
        scratch_shapes=[pltpu.VMEM((N_HEADS, HEAD_DIM + 16, 2 * tq), F32), pltpu.VMEM((N_HEADS, 1, 2 * tq), F32),
                        pltpu.VMEM((2, N_HEADS, tq + N_META, 2 * tq), F32)],
        compiler_params=pltpu.CompilerParams(
            dimension_semantics=("arbitrary",), vmem_limit_bytes=VMEM_LIMIT),
        name="attn_prompt",
    )(qb, kb, vb, p["lam_q"], p["lam_k"], p["sub_gain"])


def _decode_entry(q, kn, vn, kp, vp, lam, sg, o_ref):
    t_new = kn.shape[0]
    grp = 2 * t_new
    nrow = N_HEADS * grp
    ncol = kp[0].shape[0]
    row = lax.broadcasted_iota(jnp.int32, (nrow, HEAD_DIM), 0)
    lane = lax.broadcasted_iota(jnp.int32, (nrow, HEAD_DIM), 1)
    comp_ok = (lane >= QK_DIM) == ((row % grp) >= t_new)
    t_of_row = lax.broadcasted_iota(jnp.int32, (nrow, 1), 0) % t_new
    head_ok = (lax.broadcasted_iota(jnp.int32, (nrow, ncol), 1) % N_HEADS
               == lax.broadcasted_iota(jnp.int32, (nrow, ncol), 0) // grp)

    def per_row_head(z, t):
        return jnp.concatenate(
            [jnp.broadcast_to(z[t:t + 1, h * HEAD_DIM:(h + 1) * HEAD_DIM], (grp, HEAD_DIM)) for h in range(N_HEADS)],
            axis=0)

    q32 = jnp.where(comp_ok, q.astype(F32), 0.0)
    qb = q32.astype(BF16)
    kn, vn = kn.astype(F32), vn.astype(F32)
    s_pages = [jnp.where(head_ok, _dot_nt(qb, k_ref[...].astype(BF16)), NEG) for k_ref in kp]
    s_new = []
    for t in range(t_new):
        sc = jnp.sum(q32 * per_row_head(kn, t), axis=1, keepdims=True)
        s_new.append(jnp.where(t_of_row >= t, sc, NEG))

    m = s_pages[0]
    for s in s_pages[1:]:
        m = jnp.maximum(m, s)
    m = jnp.max(m, axis=1, keepdims=True)
    for s in s_new:
        m = jnp.maximum(m, s)

    acc = jnp.zeros((nrow, HEAD_DIM), F32)
    psum = None
    for s, v_ref in zip(s_pages, vp):
        pr = jnp.exp2(s - m)
        psum = pr if psum is None else psum + pr
        acc = acc + _dot(pr.astype(BF16), v_ref[...].astype(BF16))
    lsum = jnp.sum(psum, axis=1, keepdims=True)
    for t in range(t_new):
        pr = jnp.exp2(s_new[t] - m)
        lsum = lsum + pr
        acc = acc + pr * per_row_head(vn, t)
    on = acc / lsum
    for h in range(N_HEADS):
        r0 = h * grp
        o = on[r0:r0 + t_new] - lam * on[r0 + t_new:r0 + grp]
        r = lax.rsqrt(jnp.mean(o * o, axis=-1, keepdims=True) + EPS)
        o_ref[:, h * HEAD_DIM:(h + 1) * HEAD_DIM] = ((o * r) * sg).astype(o_ref.dtype)


def _attn_decode_kernel(pt_ref, q_ref, kn_ref, vn_ref, lq_ref, lk_ref, sg_ref, ck_hbm, cv_hbm, o_ref,
                        kbuf, vbuf, sem, *, n_pages, ahead):
    g, n = pl.program_id(0), pl.num_programs(0)
    ring = kbuf.shape[0]

    def page_copies(slot, page_of):
        return [pltpu.make_async_copy(hbm.at[page_of(pg)], buf.at[slot, pg], sem.at[w, slot])
                for pg in range(n_pages) for w, (hbm, buf) in enumerate(((ck_hbm, kbuf), (cv_hbm, vbuf)))]

    def start(entry):
        for c in page_copies(entry % ring, lambda pg: pt_ref[entry * n_pages + pg]):
            c.start()

    @pl.when(g == 0)
    def _():
        for entry in range(ahead):
            start(entry)

    @pl.when(g + ahead < n)
    def _():
        start(g + ahead)

    slot = g % ring
    for c in page_copies(slot, lambda pg: 0):
        c.wait()
    _decode_entry(q_ref[...], kn_ref[...], vn_ref[...], [kbuf.at[slot, pg] for pg in range(n_pages)],
                  [vbuf.at[slot, pg] for pg in range(n_pages)], _lam(lq_ref, lk_ref), sg_ref[...], o_ref)


def _attn_decode(q_rep, kn, vn, cache_k, cache_v, page_table, p, ahead=2):
    db, nrow, _ = q_rep.shape
    t_new, a = kn.shape[1:]
    n_pages = page_table.shape[1]
    prow = cache_k.shape[1]
    assert db >= ahead
    const = lambda shape: pl.BlockSpec(shape, lambda g, pt_ref: (0,) * len(shape))
    per_entry = lambda r, w: pl.BlockSpec((None, r, w), lambda g, pt_ref: (g, 0, 0))
    grid_spec = pltpu.PrefetchScalarGridSpec(
        num_scalar_prefetch=1,
        grid=(db,),
        in_specs=[per_entry(nrow, HEAD_DIM), per_entry(t_new, a), per_entry(t_new, a),
                  const((2, QK_DIM)), const((2, QK_DIM)), const((1, HEAD_DIM)),
                  pl.BlockSpec(memory_space=pl.ANY), pl.BlockSpec(memory_space=pl.ANY)],
        out_specs=per_entry(t_new, a),
        scratch_shapes=[pltpu.VMEM((ahead + 1, n_pages, prow, HEAD_DIM), cache_k.dtype),
                        pltpu.VMEM((ahead + 1, n_pages, prow, HEAD_DIM), cache_v.dtype),
                        pltpu.SemaphoreType.DMA((2, ahead + 1))],
    )
    return pl.pallas_call(
        functools.partial(_attn_decode_kernel, n_pages=n_pages, ahead=ahead),
        grid_spec=grid_spec,
        out_shape=jax.ShapeDtypeStruct((db, t_new, a), BF16),
        compiler_params=pltpu.CompilerParams(
            dimension_semantics=("arbitrary",), vmem_limit_bytes=VMEM_LIMIT),
        name="attn_decode",
    )(page_table.reshape(-1), q_rep, kn, vn, p["lam_q"], p["lam_k"], p["sub_gain"], cache_k, cache_v)


def _ssm_param_kernel(ar_ref, ai_ref, ldt_ref, br_ref, bi_ref, abr_ref, abi_ref, bbr_ref, bbi_ref):
    ar, ai = ar_ref[...], ai_ref[...]
    dt = jnp.exp(ldt_ref[...])
    mag = jnp.exp(ar * dt)
    abr, abi = mag * jnp.cos(ai * dt), mag * jnp.sin(ai * dt)
    den = ar * ar + ai * ai
    nr, ni = abr - 1.0, abi
    gr, gi = (nr * ar + ni * ai) / den, (ni * ar - nr * ai) / den
    abr_ref[...] = abr
    abi_ref[...] = abi
    for c in range(SSM_GROUP):
        br, bi = br_ref[c], bi_ref[c]
        bbr_ref[c] = gr * br - gi * bi
        bbi_ref[c] = gr * bi + gi * br


def _ssm_params(a_re, a_im, log_dt, b_re, b_im):
    g, pdim = a_re.shape
    c = b_re.shape[-1]
    b_re_t = jnp.transpose(b_re, (2, 0, 1))
    b_im_t = jnp.transpose(b_im, (2, 0, 1))
    gp = jax.ShapeDtypeStruct((g, pdim), F32)
    cgp = jax.ShapeDtypeStruct((c, g, pdim), F32)
    return pl.pallas_call(_ssm_param_kernel, out_shape=(gp, gp, cgp, cgp), name="ssm_params")(
        a_re, a_im, log_dt.reshape(g, 1), b_re_t, b_im_t)


def _ssm_kernel(u_ref, up_ref, h0r_ref, h0i_ref, abr_ref, abi_ref, bre_ref, bim_ref, crt_ref, cit_ref,
                d_ref, wg_ref, bg_ref, y_ref, hr_ref, hi_ref, xr_s, xi_s, ab_s, *, tc, nb, n_chunks):
    i = pl.program_id(0)

    @pl.when(i == 0)
    def _():
        hr_ref[...] = h0r_ref[...]
        hi_ref[...] = h0i_ref[...]
        ab_s[0] = jnp.broadcast_to(abr_ref[...], ab_s.shape[1:])
        ab_s[1] = jnp.broadcast_to(abi_ref[...], ab_s.shape[1:])
        if n_chunks > 1:
            xr_s[1] = jnp.zeros(xr_s.shape[1:], F32)
            xi_s[1] = jnp.zeros(xi_s.shape[1:], F32)

    def body(cur, prev):
        rows, width = tc * nb, d_ref.shape[1]
        ub = u_ref[...].reshape(rows, width).astype(BF16)
        kin, nout = bre_ref.shape[1], bre_ref.shape[2]
        for mblk in range(bre_ref.shape[0]):
            um = ub[:, mblk * kin:(mblk + 1) * kin]
            xr_s[cur, :, mblk * nout:(mblk + 1) * nout] = _dot(um, bre_ref[mblk])
            xi_s[cur, :, mblk * nout:(mblk + 1) * nout] = _dot(um, bim_ref[mblk])

        hr, hi = hr_ref[...], hi_ref[...]
        for t in range(tc):
            r = slice(t * nb, (t + 1) * nb)
            abr, abi = ab_s[0], ab_s[1]
            nhr = abr * hr - abi * hi + xr_s[cur, r, :]
            nhi = abr * hi + abi * hr + xi_s[cur, r, :]
            xr_s[cur, r, :] = nhr
            xi_s[cur, r, :] = nhi
            hr, hi = nhr, nhi
        live = i < n_chunks
        hr_ref[...] = jnp.where(live, hr, hr_ref[...])
        hi_ref[...] = jnp.where(live, hi, hi_ref[...])

        kout = crt_ref.shape[1]
        ys = []
        for j in range(crt_ref.shape[0]):
            hrj = xr_s[prev, :, j * kout:(j + 1) * kout].astype(BF16)
            hij = xi_s[prev, :, j * kout:(j + 1) * kout].astype(BF16)
            ys.append(_dot(hrj, crt_ref[j]) - _dot(hij, cit_ref[j]))
        y = jnp.concatenate(ys, axis=1) + d_ref[...] * up_ref[...].reshape(rows, width)
        g = jax.nn.gelu(y)
        out = g * jax.nn.sigmoid(_dot(g.astype(BF16), wg_ref[...].astype(BF16)) + bg_ref[...])
        y_ref[...] = out.reshape(tc, nb, width)

    if n_chunks == 1:
        body(0, 0)
    else:
        for par in (0, 1):
            pl.when(i % 2 == par)(functools.partial(body, par, 1 - par))


def _ssm(u3, h0r, h0i, p, tc):
    t_len, nb, width = u3.shape
    n_state = h0r.shape[1]
    n_chunks = t_len // tc
    assert n_chunks * tc == t_len
    st = jax.ShapeDtypeStruct((nb, n_state), F32)
    st_spec = pl.BlockSpec((nb, n_state), lambda i: (0, 0))
    chunk = lambda index: pl.BlockSpec((tc, nb, width), lambda i: (index(i), 0, 0))
    return pl.pallas_call(
        functools.partial(_ssm_kernel, tc=tc, nb=nb, n_chunks=n_chunks),
        grid=(n_chunks + 1 if n_chunks > 1 else 1,),
        in_specs=[chunk(lambda i: jnp.minimum(i, n_chunks - 1)), chunk(lambda i: jnp.maximum(i - 1, 0)),
                  _const_spec((nb, n_state)), _const_spec((nb, n_state)),
                  _const_spec((1, n_state)), _const_spec((1, n_state)),
                  _const_spec(p["bre"].shape), _const_spec(p["bim"].shape),
                  _const_spec(p["crt"].shape), _const_spec(p["cit"].shape),
                  _const_spec((1, width)), _const_spec((width, width)), _const_spec((1, width))],
        out_specs=(chunk(lambda i: jnp.maximum(i - 1, 0)), st_spec, st_spec),
        out_shape=(jax.ShapeDtypeStruct((t_len, nb, width), F32), st, st),
        scratch_shapes=[pltpu.VMEM((min(n_chunks, 2), tc * nb, n_state), F32)] * 2
        + [pltpu.VMEM((2, nb, n_state), F32)],
        compiler_params=pltpu.CompilerParams(
            dimension_semantics=("arbitrary",), vmem_limit_bytes=VMEM_LIMIT),
        name="ssm",
    )(u3, u3, h0r, h0i, p["abr"], p["abi"], p["bre"], p["bim"], p["crt"], p["cit"],
      p["d_skip"], p["w_glu"], p["b_glu"])


def _post_head(x_ref, att_ref, ssm_ref, wo_ref, g2_ref, y_ref, xn_s):
    tm, a = y_ref.shape[0], wo_ref.shape[0] // 2
    mix = jnp.concatenate([att_ref[...].reshape(tm, a), ssm_ref[...].reshape(tm, a).astype(BF16)], axis=1)
    xm = x_ref[...] + _dot(mix, wo_ref[...].astype(BF16))
    y_ref[...] = xm
    r = lax.rsqrt(jnp.mean(xm * xm, axis=-1, keepdims=True) + EPS)
    xn_s[...] = ((xm * r) * g2_ref[...]).astype(BF16)


def _post_chunks(j0, j1, wgate_ref, wup_ref, cw_ref, cb_ref, xn_s, h_s, ap_s, hist_s, *, shift, fc):
    tm = xn_s.shape[0]
    off = ap_s.shape[1] - tm
    for j in range(j0, j1):
        cs = slice(j * fc, (j + 1) * fc)
        ap = ap_s.at[j % 2]
        gate = _dot(xn_s[...], wgate_ref[:, cs])
        up = _dot(xn_s[...], wup_ref[:, cs])
        ap[off - 2 * shift:off, :] = hist_s[:, cs]
        ap[off:off + tm, :] = gate
        conv = (cb_ref[:, cs] + cw_ref[0:1, cs] * ap[off - 2 * shift:off - 2 * shift + tm, :]
                + cw_ref[1:2, cs] * ap[off - shift:off - shift + tm, :] + cw_ref[2:3, cs] * gate)
        hist_s[:, cs] = ap[off + tm - 2 * shift:off + tm, :]
        h_s[:, cs] = (jax.nn.gelu(conv) * up).astype(BF16)


def _load_rounded(src_hbm, dst_s, stg, sem, sem_row, n_parts):
    rows = src_hbm.shape[0] // n_parts
    copy = lambda c: pltpu.make_async_copy(src_hbm.at[pl.ds(c * rows, rows)], stg.at[c % 2], sem.at[sem_row, c % 2])
    copy(0).start()
    for c in range(n_parts):
        if c + 1 < n_parts:
            copy(c + 1).start()
        copy(c).wait()
        dst_s[c * rows:(c + 1) * rows, :] = stg[c % 2].astype(BF16)


def _post_kernel(x_ref, att_ref, ssm_ref, *refs, shift, fc, n_lead, cast):
    n_hist = 3 if n_lead else 1
    hist_src, rest = refs[:n_hist], refs[n_hist:]
    wo_ref, g2_ref, wgate_ref, wup_ref, cw_ref, cb_ref, wd_ref, y_ref, cst_ref = rest[:9]
    first_step = jnp.logical_and(pl.program_id(0) == 0, pl.program_id(1) == 0)
    if cast:
        w_hbm, w_out_hbm = (wgate_ref, wup_ref, wd_ref), rest[9:12]
        xn_s, h_s, ap_s, hist_s, wgate_ref, wup_ref, wd_ref, stg_a, stg_b, sem = rest[12:]
        w_vmem = (wgate_ref, wup_ref, wd_ref)
        back = [pltpu.make_async_copy(w_vmem[k], w_out_hbm[k], sem.at[2, k]) for k in range(3)]

        @pl.when(first_step)
        def _():
            n_parts = wgate_ref.shape[0] // stg_a.shape[1]
            _load_rounded(w_hbm[0], wgate_ref, stg_a, sem, 0, n_parts)
            _load_rounded(w_hbm[1], wup_ref, stg_a, sem, 0, n_parts)
            _load_rounded(w_hbm[2], wd_ref, stg_b, sem, 1, wd_ref.shape[0] // stg_b.shape[1])
            for c in back:
                c.start()
    else:
        xn_s, h_s, ap_s, hist_s = rest[9:]
    if n_lead:
        xl_ref, al_ref, sl_ref = hist_src
        lead_s = hist_s.at[1]
        hist_s = hist_s.at[0]

        @pl.when(first_step)
        def _():
            mix = jnp.concatenate([al_ref[...], sl_ref[...].astype(BF16)], axis=1)
            xm = xl_ref[...] + _dot(mix, wo_ref[...].astype(BF16))
            r = lax.rsqrt(jnp.mean(xm * xm, axis=-1, keepdims=True) + EPS)
            gate = _dot(((xm * r) * g2_ref[...]).astype(BF16), wgate_ref[...])
            lead_s[...] = gate[n_lead - 2 * shift:n_lead, :]

        first_hist = lead_s
    else:
        first_hist, = hist_src
        hist_s = hist_s.at[0]

    @pl.when(pl.program_id(1) == 0)
    def _():
        hist_s[...] = first_hist[...]

    _post_head(x_ref, att_ref, ssm_ref, wo_ref, g2_ref, y_ref, xn_s)
    _post_chunks(0, wgate_ref.shape[1] // fc, wgate_ref, wup_ref, cw_ref, cb_ref, xn_s, h_s, ap_s, hist_s,
                 shift=shift, fc=fc)
    y_ref[...] += _dot(h_s[...], wd_ref[...])
    cst_ref[...] = hist_s[...]
    if cast:
        @pl.when(first_step)
        def _():
            for c in back:
                c.wait()


def _post(x2d, att, ssm, hist, p, tm, nb, shift, att_spec, ssm_spec, lead=None, fc=256, n_parts=8):
    rows, d = x2d.shape
    d_ff = p["w_gate"].shape[1]
    cast = p["w_gate"].dtype != BF16
    per_b = rows // nb
    nt = per_b // tm
    assert nt * tm == per_b and d_ff % fc == 0 and tm >= 2 * shift
    off = -(-2 * shift // 8) * 8
    row_spec = lambda w: pl.BlockSpec((tm, w), lambda b, t: (b * nt + t, 0))
    hist_spec = pl.BlockSpec((None, 2 * shift, d_ff), lambda b, t: (b, 0, 0))
    if lead is None:
        hist_src, hist_specs, n_lead = (hist,), [hist_spec], 0
    else:
        hist_src, hist_specs, n_lead = lead, [_const_spec(z.shape) for z in lead], lead[0].shape[0]
        assert n_lead >= 2 * shift
    ffn_w = (p["w_gate"], p["w_up"], p["w_down"])
    out_specs = [row_spec(d), hist_spec]
    out_shape = [jax.ShapeDtypeStruct((rows, d), F32), jax.ShapeDtypeStruct((nb, 2 * shift, d_ff), F32)]
    scratch = [pltpu.VMEM((tm, d), BF16), pltpu.VMEM((tm, d_ff), BF16),
               pltpu.VMEM((2, off + tm, fc), F32), pltpu.VMEM((2, 2 * shift, d_ff), F32)]
    if cast:
        assert d % n_parts == 0 and d_ff % n_parts == 0
        w_specs = [pl.BlockSpec(memory_space=pl.ANY)] * 3
        out_specs += [pl.BlockSpec(memory_space=pl.ANY)] * 3
        out_shape += [jax.ShapeDtypeStruct(w.shape, BF16) for w in ffn_w]
        scratch += [pltpu.VMEM(w.shape, BF16) for w in ffn_w]
        scratch += [pltpu.VMEM((2, d // n_parts, d_ff), F32), pltpu.VMEM((2, d_ff // n_parts, d), F32),
                    pltpu.SemaphoreType.DMA((3, 3))]
    else:
        w_specs = [_const_spec(w.shape) for w in ffn_w]
    return pl.pallas_call(
        functools.partial(_post_kernel, shift=shift, fc=fc, n_lead=n_lead, cast=cast),
        grid=(nb, nt),
        in_specs=[row_spec(d), att_spec, ssm_spec, *hist_specs,
                  _const_spec(p["w_out"].shape), _const_spec((1, d)), w_specs[0], w_specs[1],
                  _const_spec((3, d_ff)), _const_spec((1, d_ff)), w_specs[2]],
        out_specs=tuple(out_specs),
        out_shape=tuple(out_shape),
        scratch_shapes=scratch,
        compiler_params=pltpu.CompilerParams(
            dimension_semantics=("arbitrary", "arbitrary"), vmem_limit_bytes=VMEM_LIMIT),
        name="post",
    )(x2d, att, ssm, *hist_src, p["w_out"], p["norm2"], ffn_w[0], ffn_w[1], p["conv_w"], p["conv_b"], ffn_w[2])


def _prepare_params(norm1, w_in, q_norm, k_norm, lam_q, lam_k, sub_norm, ssm_a_re, ssm_a_im, ssm_log_dt,
                    ssm_b_re, ssm_b_im, ssm_c_re, ssm_c_im, ssm_d, w_glu, b_glu, w_out, norm2, w_gate,
                    w_up, ffn_conv_w, ffn_conv_b, w_down):
    l = 0
    g, pdim = ssm_a_re[l].shape
    c = SSM_GROUP
    a = N_HEADS * HEAD_DIM
    abr, abi, bbr, bbi = _ssm_params(ssm_a_re[l], ssm_a_im[l], ssm_log_dt[l], ssm_b_re[l], ssm_b_im[l])
    gi = LANES // c

    def in_blocks(bb):
        bb = jnp.transpose(bb, (1, 0, 2)).reshape(g // gi, gi, c, 1, pdim)
        same = jnp.eye(gi, dtype=bool)[None, :, None, :, None]
        return jnp.where(same, bb, 0.0).astype(BF16).reshape(g // gi, gi * c, gi * pdim)

    go = MXU // c

    def out_blocks(cc):
        cc = jnp.transpose(cc.reshape(g // go, go, c, pdim), (0, 1, 3, 2))[:, :, :, None, :]
        same = jnp.eye(go, dtype=bool)[None, :, None, :, None]
        return jnp.where(same, cc, 0.0).astype(BF16).reshape(g // go, go * pdim, go * c)

    comp = jnp.arange(MXU) // QK_DIM
    ones_blk = (comp[:, None] == comp[None, :]).astype(BF16) * (1.0 / QK_DIM)
    return {
        "norm1": norm1[l].reshape(1, -1), "w_in": w_in[l],
        "q_gain": jnp.tile(q_norm[l].reshape(-1), N_HEADS).reshape(1, a),
        "k_gain": jnp.tile(k_norm[l].reshape(-1), N_HEADS).reshape(1, a),
        "ones_blk": ones_blk.astype(BF16),
        "lam_q": lam_q[l], "lam_k": lam_k[l],
        "sub_gain": (sub_norm[l] * (1.0 - LAM_INIT)).reshape(1, HEAD_DIM),
        "abr": abr.reshape(1, g * pdim), "abi": abi.reshape(1, g * pdim),
        "bre": in_blocks(bbr), "bim": in_blocks(bbi),
        "crt": out_blocks(ssm_c_re[l]), "cit": out_blocks(ssm_c_im[l]),
        "d_skip": ssm_d[l].reshape(1, g * c), "w_glu": w_glu[l], "b_glu": b_glu[l].reshape(1, -1),
        "w_out": w_out[l], "norm2": norm2[l].reshape(1, -1),
        "w_gate": w_gate[l], "w_up": w_up[l],
        "conv_w": ffn_conv_w[l], "conv_b": ffn_conv_b[l].reshape(1, -1), "w_down": w_down[l],
    }


def _row_tile(length, cap=768):
    best = None
    for t in range(16, cap + 1, 16):
        if length % t == 0:
            best = t
    assert best is not None
    return best


def kernel(x_prompt, x_sample, cache_k, cache_v, state_ssm_re, state_ssm_im, state_ffn_conv, page_table, meta_tokens, norm1, w_in, q_norm, k_norm, lam_q, lam_k, sub_norm, ssm_a_re, ssm_a_im, ssm_log_dt, ssm_b_re, ssm_b_im, ssm_c_re, ssm_c_im, ssm_d, w_glu, b_glu, w_out, norm2, w_gate, w_up, ffn_conv_w, ffn_conv_b, w_down):
    assert norm1.shape[0] == 1, "single-layer stack"
    p = _prepare_params(norm1, w_in, q_norm, k_norm, lam_q, lam_k, sub_norm, ssm_a_re, ssm_a_im, ssm_log_dt,
                        ssm_b_re, ssm_b_im, ssm_c_re, ssm_c_im, ssm_d, w_glu, b_glu, w_out, norm2, w_gate,
                        w_up, ffn_conv_w, ffn_conv_b, w_down)
    nb, seq, d = x_prompt.shape
    db, t_new, _ = x_sample.shape
    a = N_HEADS * HEAD_DIM
    g, pdim = ssm_a_re.shape[1:]
    n_state = g * pdim
    d_ff = w_gate.shape[-1]
    length = seq + N_META

    meta = meta_tokens.astype(x_prompt.dtype)
    tm = _row_tile(length)
    qb, k, kb, v, vb, u = _project(x_prompt, meta, tm, p, u_time_major=True)
    att = _attn_prompt(qb.reshape(nb, length, a), kb.reshape(nb, length, a), vb.reshape(nb, length, a), p)
    zst = jnp.zeros((nb, n_state), F32)
    tc = _row_tile(length * nb, cap=768) // nb
    ys, hr, hi = _ssm(u.reshape(length, nb, a), zst, zst, p, tc)
    ys2 = ys.reshape(length, nb * a)
    tp = _row_tile(seq, cap=512)
    att_spec = pl.BlockSpec((pl.Element(1), pl.Element(tp), pl.Element(a)),
                            lambda b, t: (b, pl.multiple_of(N_META + t * tp, N_META), 0))
    ssm_spec = pl.BlockSpec((pl.Element(tp), pl.Element(a)),
                            lambda b, t: (pl.multiple_of(N_META + t * tp, N_META), pl.multiple_of(b * a, a)))
    yp, cst_p, *ffn_bf16 = _post(x_prompt.reshape(nb * seq, d), att, ys2, None, p, tp, nb, shift=1,
                                 att_spec=att_spec, ssm_spec=ssm_spec,
                                 lead=(meta, att[0, :N_META], ys2[:N_META, :a]))
    p = dict(p, w_gate=ffn_bf16[0], w_up=ffn_bf16[1], w_down=ffn_bf16[2])
    y_prompt = yp.reshape(nb, seq, d)
    k_prompt = k.reshape(1, nb, length, N_HEADS, HEAD_DIM)
    v_prompt = v.reshape(1, nb, length, N_HEADS, HEAD_DIM)
    ssm_re_p = hr.reshape(1, nb, g, pdim)
    ssm_im_p = hi.reshape(1, nb, g, pdim)
    conv_p = cst_p[None]

    rows_s = db * t_new
    xs2 = jnp.transpose(x_sample, (1, 0, 2)).reshape(rows_s, d)
    qb, k, kb, v, vb, u = _project(xs2, None, rows_s, p, u_time_major=False)
    to_bm = lambda z: jnp.transpose(z.reshape(t_new, db, a), (1, 0, 2))
    q_hct = jnp.transpose(qb.reshape(t_new, db, N_HEADS, 1, HEAD_DIM), (1, 2, 3, 0, 4))
    q_rep = jnp.broadcast_to(q_hct, (db, N_HEADS, 2, t_new, HEAD_DIM)).reshape(db, N_HEADS * 2 * t_new, HEAD_DIM)
    n_pool, page = cache_k.shape[1:3]
    att = _attn_decode(q_rep, to_bm(kb), to_bm(vb), cache_k[0].reshape(n_pool, page * N_HEADS, HEAD_DIM),
                       cache_v[0].reshape(n_pool, page * N_HEADS, HEAD_DIM), page_table, p)
    att_tm = jnp.transpose(att, (1, 0, 2)).reshape(rows_s, a)
    ys, hr, hi = _ssm(u.reshape(t_new, db, a), state_ssm_re[0].reshape(db, n_state),
                      state_ssm_im[0].reshape(db, n_state), p, t_new)
    hist = jnp.transpose(state_ffn_conv[0], (1, 0, 2)).reshape(1, 2 * db, d_ff)
    whole = pl.BlockSpec((rows_s, a), lambda b, t: (0, 0))
    ysm, cst_s = _post(xs2, att_tm, ys.reshape(rows_s, a), hist, p, rows_s, 1, shift=db,
                       att_spec=whole, ssm_spec=whole)
    y_sample = jnp.transpose(ysm.reshape(t_new, db, d), (1, 0, 2))
    k_sample = jnp.transpose(k.reshape(t_new, db, N_HEADS, HEAD_DIM), (1, 0, 2, 3))[None]
    v_sample = jnp.transpose(v.reshape(t_new, db, N_HEADS, HEAD_DIM), (1, 0, 2, 3))[None]
    ssm_re_s = hr.reshape(1, db, g, pdim)
    ssm_im_s = hi.reshape(1, db, g, pdim)
    conv_s = jnp.transpose(cst_s.reshape(2, db, d_ff), (1, 0, 2))[None]

    return (y_prompt, y_sample, k_prompt, v_prompt, k_sample, v_sample,
            ssm_re_p, ssm_im_p, ssm_re_s, ssm_im_s, conv_p, conv_s)
```

```python
import functools
import math

import jax
import jax.numpy as jnp
from jax import lax
from jax.experimental import pallas as pl
from jax.experimental.pallas import tpu as pltpu

N_META = 16
N_HEADS = 4
QK_DIM = 64
HEAD_DIM = 2 * QK_DIM
SSM_GROUP = 16
SSM_STATE = 64
EPS = 1e-6
NEG = -1e30
LAM_INIT = 0.8 - 0.6 * math.exp(-0.3 * 0)
LOG2E = math.log2(math.e)

LANES = 128
MXU = 256
VMEM_LIMIT = 56 * 1024 * 1024

F32 = jnp.float32
BF16 = jnp.bfloat16


def _dot(a, b):
    return jnp.dot(a, b, preferred_element_type=F32)


def _dot_nt(a, b):
    return lax.dot_general(a, b, (((1,), (1,)), ((), ())), preferred_element_type=F32)


def _const_spec(shape):
    nd = len(shape)
    return pl.BlockSpec(shape, lambda *_: (0,) * nd, pipeline_mode=pl.Buffered(1))


def _lam(lq_ref, lk_ref):
    e = jnp.exp(jnp.sum(lq_ref[...] * lk_ref[...], axis=1, keepdims=True))
    return e[0:1] - e[1:2] + LAM_INIT


def _proj_kernel(x_ref, meta_ref, g1_ref, w_ref, qg_ref, kg_ref, ones_ref,
                 qb_ref, k_ref, kb_ref, v_ref, vb_ref, u_ref, wb_s, *, n_meta):
    outs = (qb_ref, k_ref, kb_ref, v_ref, vb_ref, u_ref)
    consts = (g1_ref, wb_s, qg_ref, kg_ref, ones_ref)
    tm = qb_ref.shape[0]

    @pl.when(jnp.logical_and(pl.program_id(0) == 0, pl.program_id(1) == 0))
    def _():
        wb_s[...] = w_ref[...].astype(BF16)

    if not n_meta:
        _proj_body(x_ref[...], consts, outs)
        return
    t = pl.program_id(1)

    @pl.when(t == 0)
    def _():
        _proj_body(jnp.concatenate([meta_ref[...], x_ref[0, 0:tm - n_meta, :]], axis=0), consts, outs)

    @pl.when(t != 0)
    def _():
        _proj_body(x_ref[0], consts, outs)


def _proj_body(x, consts, outs):
    g1_ref, w_ref, qg_ref, kg_ref, ones_ref = consts
    qb_ref, k_ref, kb_ref, v_ref, vb_ref, u_ref = outs
    a = N_HEADS * HEAD_DIM
    r = lax.rsqrt(jnp.mean(x * x, axis=-1, keepdims=True) + EPS)
    xn = ((x * r) * g1_ref[...]).astype(BF16)
    proj = _dot(xn, w_ref[...])

    def comp_norm(z, g):
        z2 = (z * z).astype(BF16)
        ms = jnp.concatenate(
            [_dot(z2[:, j * MXU:(j + 1) * MXU], ones_ref[...]) for j in range(a // MXU)], axis=1)
        return (z * lax.rsqrt(ms + EPS)) * g

    qn = comp_norm(proj[:, :a], qg_ref[...])
    kn = comp_norm(proj[:, a:2 * a], kg_ref[...])
    v = proj[:, 2 * a:3 * a]
    qb_ref[...] = (qn * (QK_DIM ** -0.5 * LOG2E)).astype(BF16)
    kb_ref[...] = kn.astype(BF16)
    vb_ref[...] = v.astype(BF16)
    u_ref[...] = proj[:, 3 * a:]
    tm = x.shape[0]
    for h in range(N_HEADS):
        k_ref[pl.ds(h, tm, stride=N_HEADS), :] = kn[:, h * HEAD_DIM:(h + 1) * HEAD_DIM]
        v_ref[pl.ds(h, tm, stride=N_HEADS), :] = v[:, h * HEAD_DIM:(h + 1) * HEAD_DIM]


def _project(x, meta, tm, p, u_time_major):
    a = N_HEADS * HEAD_DIM
    d = x.shape[-1]
    if meta is None:
        nb, n_meta = 1, 0
        per_b = x.shape[0]
        x_spec = pl.BlockSpec((tm, d), lambda b, t: (t, 0))
        meta = jnp.zeros((8, d), x.dtype)
    else:
        nb, n_meta = x.shape[0], meta.shape[0]
        per_b = n_meta + x.shape[1]
        x_spec = pl.BlockSpec(
            (pl.Element(1), pl.Element(tm), pl.Element(d)),
            lambda b, t: (b, pl.multiple_of(jnp.maximum(t * tm - n_meta, 0), 8), 0))
    rows = nb * per_b
    nt = per_b // tm
    assert nt * tm == per_b and n_meta % 8 == 0
    grid = (nb, nt)
    row_spec = lambda w: pl.BlockSpec((tm, w), lambda b, t: (b * nt + t, 0))
    head_spec = pl.BlockSpec((tm * N_HEADS, HEAD_DIM), lambda b, t: (b * nt + t, 0))
    if u_time_major:
        u_shape = jax.ShapeDtypeStruct((per_b, nb * a), F32)
        u_spec = pl.BlockSpec((tm, a), lambda b, t: (t, b))
    else:
        u_shape = jax.ShapeDtypeStruct((rows, a), F32)
        u_spec = row_spec(a)
    out_shape = (jax.ShapeDtypeStruct((rows, a), BF16), jax.ShapeDtypeStruct((rows * N_HEADS, HEAD_DIM), F32),
                 jax.ShapeDtypeStruct((rows, a), BF16), jax.ShapeDtypeStruct((rows * N_HEADS, HEAD_DIM), F32),
                 jax.ShapeDtypeStruct((rows, a), BF16), u_shape)
    return pl.pallas_call(
        functools.partial(_proj_kernel, n_meta=n_meta),
        grid=grid,
        in_specs=[x_spec, _const_spec(meta.shape), _const_spec((1, d)), _const_spec(p["w_in"].shape),
                  _const_spec((1, a)), _const_spec((1, a)), _const_spec((MXU, MXU))],
        out_specs=(row_spec(a), head_spec, row_spec(a), head_spec, row_spec(a), u_spec),
        out_shape=out_shape,
        scratch_shapes=[pltpu.VMEM(p["w_in"].shape, BF16)],
        compiler_params=pltpu.CompilerParams(
            dimension_semantics=("arbitrary", "arbitrary"), vmem_limit_bytes=VMEM_LIMIT),
        name="proj",
    )(x, meta, p["norm1"], p["w_in"], p["q_gain"], p["k_gain"], p["ones_blk"])


def _attn_prompt_kernel(q_ref, k_ref, v_ref, lq_ref, lk_ref, sg_ref, o_ref, acc_s, m_s, s_s, *, tq, n_tiles):
    lam = _lam(lq_ref, lk_ref)
    sg = sg_ref[...]
    lo = lax.broadcasted_iota(jnp.int32, (1, HEAD_DIM), 1) < QK_DIM
    n_ones = acc_s.shape[1] - HEAD_DIM

    def hcols(h):
        return slice(h * HEAD_DIM, (h + 1) * HEAD_DIM)

    def stack_q(q):
        z = jnp.zeros_like(q)
        return jnp.concatenate([jnp.where(lo, q, z), jnp.where(lo, z, q)], axis=0)

    def init(h, t2):
        m_s[h, :, 0:t2] = jnp.full((1, t2), NEG, F32)
        acc_s[h, :, 0:t2] = jnp.zeros((HEAD_DIM + n_ones, t2), F32)

    def scores(slot, h, q2, kt, mask):
        s = _dot_nt(kt, q2)
        if mask is not None:
            s = jnp.where(mask, s, NEG)
        s_s[slot, h, 0:kt.shape[0], 0:q2.shape[0]] = s

    def absorb(slot, h, nk, t2, vt):
        s = s_s[slot, h, 0:nk, 0:t2]
        m_prev = m_s[h, :, 0:t2]
        m_new = jnp.maximum(m_prev, jnp.max(s, axis=0, keepdims=True))
        alpha = jnp.exp2(m_prev - m_new)
        pr = jnp.exp2(s - m_new).astype(BF16)
        va = jnp.concatenate([vt.T, jnp.ones((n_ones, nk), BF16)], axis=0)
        acc_s[h, :, 0:t2] = alpha * acc_s[h, :, 0:t2] + _dot(va, pr)
        m_s[h, :, 0:t2] = m_new

    def finish(h, t):
        acc = acc_s[h, :, 0:2 * t]
        on = acc[0:HEAD_DIM] / acc[HEAD_DIM:HEAD_DIM + 1]
        o = on[:, 0:t] - lam * on[:, t:2 * t]
        r = lax.rsqrt(jnp.mean(o * o, axis=0, keepdims=True) + EPS)
        return ((o * r).T * sg).astype(o_ref.dtype)

    def causal_mask(t, nk, offset):
        qi = lax.broadcasted_iota(jnp.int32, (nk, 2 * t), 1)
        qi = jnp.where(qi >= t, qi - t, qi)
        ki = lax.broadcasted_iota(jnp.int32, (nk, 2 * t), 0)
        return ki <= qi + offset

    tmq = LANES
    for h in range(N_HEADS):
        init(h, 2 * tmq)
        scores(0, h, stack_q(q_ref[0:tmq, hcols(h)]), k_ref[0:tmq, hcols(h)], causal_mask(tmq, tmq, 0))
    for h in range(N_HEADS):
        absorb(0, h, tmq, 2 * tmq, v_ref[0:tmq, hcols(h)])
        o_ref[0:N_META, hcols(h)] = finish(h, tmq)[0:N_META]

    dk = tq + N_META

    def q_tile(j, carry):
        qs = pl.multiple_of(N_META + j * tq, N_META)
        q2 = [stack_q(q_ref[pl.ds(qs, tq), hcols(h)]) for h in range(N_HEADS)]
        ds = pl.multiple_of(j * tq, tq)
        dmask = causal_mask(tq, dk, N_META)
        for h in range(N_HEADS):
            init(h, 2 * tq)

        def plain_scores(slot, i):
            ks = pl.multiple_of(i * tq, tq)
            for h in range(N_HEADS):
                scores(slot, h, q2[h], k_ref[pl.ds(ks, tq), hcols(h)], None)

        def diag_scores(slot):
            for h in range(N_HEADS):
                scores(slot, h, q2[h], k_ref[pl.ds(ds, dk), hcols(h)], dmask)

        def plain_absorb(slot, i):
            ks = pl.multiple_of(i * tq, tq)
            for h in range(N_HEADS):
                absorb(slot, h, tq, 2 * tq, v_ref[pl.ds(ks, tq), hcols(h)])

        def diag_absorb(slot):
            for h in range(N_HEADS):
                absorb(slot, h, dk, 2 * tq, v_ref[pl.ds(ds, dk), hcols(h)])
                o_ref[pl.ds(qs, tq), hcols(h)] = finish(h, tq)

        @pl.when(j == 0)
        def _():
            diag_scores(0)
            diag_absorb(0)

        @pl.when(j > 0)
        def _():
            plain_scores(0, 0)

        n_pairs = (j - 1) // 2

        def k_pair(pi, c):
            i = 2 * pi
            plain_scores(1, i + 1)
            plain_absorb(0, i)
            plain_scores(0, i + 2)
            plain_absorb(1, i + 1)
            return c

        lax.fori_loop(0, n_pairs, k_pair, 0)

        @pl.when(jnp.logical_and(j > 0, j % 2 == 1))
        def _():
            diag_scores(1)
            plain_absorb(0, j - 1)
            diag_absorb(1)

        @pl.when(jnp.logical_and(j > 0, j % 2 == 0))
        def _():
            plain_scores(1, j - 1)
            plain_absorb(0, j - 2)
            diag_scores(0)
            plain_absorb(1, j - 1)
            diag_absorb(0)

        return carry

    lax.fori_loop(0, n_tiles, q_tile, 0)


def _attn_prompt(qb, kb, vb, p, tq=256):
    nb, length, a = qb.shape
    n_tiles = (length - N_META) // tq
    assert N_META + n_tiles * tq == length and tq % LANES == 0
    spec = pl.BlockSpec((None, length, a), lambda b: (b, 0, 0))
    return pl.pallas_call(
        functools.partial(_attn_prompt_kernel, tq=tq, n_tiles=n_tiles),
        grid=(nb,),
        in_specs=[spec, spec, spec, _const_spec((2, QK_DIM)), _const_spec((2, QK_DIM)),
                  _const_spec((1, HEAD_DIM))],
        out_specs=spec,
        out_shape=jax.ShapeDtypeStruct((nb, length, a), BF16),
        scratch_shapes=[pltpu.VMEM((N_HEADS, HEAD_DIM + 16, 2 * tq), F32), pltpu.VMEM((N_HEADS, 1, 2 * tq), F32),
                        pltpu.VMEM((2, N_HEADS, tq + N_META, 2 * tq), F32)],
        compiler_params=pltpu.CompilerParams(
            dimension_semantics=("arbitrary",), vmem_limit_bytes=VMEM_LIMIT),
        name="attn_prompt",
    )(qb, kb, vb, p["lam_q"], p["lam_k"], p["sub_gain"])


def _decode_entry(q, kn, vn, kp, vp, lam, sg, o_ref):
    t_new = kn.shape[0]
    grp = 2 * t_new
    nrow = N_HEADS * grp
    ncol = kp[0].shape[0]
    row = lax.broadcasted_iota(jnp.int32, (nrow, HEAD_DIM), 0)
    lane = lax.broadcasted_iota(jnp.int32, (nrow, HEAD_DIM), 1)
    comp_ok = (lane >= QK_DIM) == ((row % grp) >= t_new)
    t_of_row = lax.broadcasted_iota(jnp.int32, (nrow, 1), 0) % t_new
    head_ok = (lax.broadcasted_iota(jnp.int32, (nrow, ncol), 1) % N_HEADS
               == lax.broadcasted_iota(jnp.int32, (nrow, ncol), 0) // grp)

    def per_row_head(z, t):
        return jnp.concatenate(
            [jnp.broadcast_to(z[t:t + 1, h * HEAD_DIM:(h + 1) * HEAD_DIM], (grp, HEAD_DIM)) for h in range(N_HEADS)],
            axis=0)

    q32 = jnp.where(comp_ok, q.astype(F32), 0.0)
    qb = q32.astype(BF16)
    kn, vn = kn.astype(F32), vn.astype(F32)
    s_pages = [jnp.where(head_ok, _dot_nt(qb, k_ref[...].astype(BF16)), NEG) for k_ref in kp]
    s_new = []
    for t in range(t_new):
        sc = jnp.sum(q32 * per_row_head(kn, t), axis=1, keepdims=True)
        s_new.append(jnp.where(t_of_row >= t, sc, NEG))

    m = s_pages[0]
    for s in s_pages[1:]:
        m = jnp.maximum(m, s)
    m = jnp.max(m, axis=1, keepdims=True)
    for s in s_new:
        m = jnp.maximum(m, s)

    acc = jnp.zeros((nrow, HEAD_DIM), F32)
    psum = None
    for s, v_ref in zip(s_pages, vp):
        pr = jnp.exp2(s - m)
        psum = pr if psum is None else psum + pr
        acc = acc + _dot(pr.astype(BF16), v_ref[...].astype(BF16))
    lsum = jnp.sum(psum, axis=1, keepdims=True)
    for t in range(t_new):
        pr = jnp.exp2(s_new[t] - m)
        lsum = lsum + pr
        acc = acc + pr * per_row_head(vn, t)
    on = acc / lsum
    for h in range(N_HEADS):
        r0 = h * grp
        o = on[r0:r0 + t_new] - lam * on[r0 + t_new:r0 + grp]
        r = lax.rsqrt(jnp.mean(o * o, axis=-1, keepdims=True) + EPS)
        o_ref[:, h * HEAD_DIM:(h + 1) * HEAD_DIM] = ((o * r) * sg).astype(o_ref.dtype)


def _attn_decode_kernel(pt_ref, q_ref, kn_ref, vn_ref, lq_ref, lk_ref, sg_ref, ck_hbm, cv_hbm, o_ref,
                        kbuf, vbuf, sem, *, n_pages, ahead):
    g, n = pl.program_id(0), pl.num_programs(0)
    ring = kbuf.shape[0]

    def page_copies(slot, page_of):
        return [pltpu.make_async_copy(hbm.at[page_of(pg)], buf.at[slot, pg], sem.at[w, slot])
                for pg in range(n_pages) for w, (hbm, buf) in enumerate(((ck_hbm, kbuf), (cv_hbm, vbuf)))]

    def start(entry):
        for c in page_copies(entry % ring, lambda pg: pt_ref[entry * n_pages + pg]):
            c.start()

    @pl.when(g == 0)
    def _():
        for entry in range(ahead):
            start(entry)

    @pl.when(g + ahead < n)
    def _():
        start(g + ahead)

    slot = g % ring
    for c in page_copies(slot, lambda pg: 0):
        c.wait()
    _decode_entry(q_ref[...], kn_ref[...], vn_ref[...], [kbuf.at[slot, pg] for pg in range(n_pages)],
                  [vbuf.at[slot, pg] for pg in range(n_pages)], _lam(lq_ref, lk_ref), sg_ref[...], o_ref)


def _attn_decode(q_rep, kn, vn, cache_k, cache_v, page_table, p, ahead=3):
    db, nrow, _ = q_rep.shape
    t_new, a = kn.shape[1:]
    n_pages = page_table.shape[1]
    prow = cache_k.shape[1]
    assert db >= ahead
    const = lambda shape: pl.BlockSpec(shape, lambda g, pt_ref: (0,) * len(shape))
    per_entry = lambda r, w: pl.BlockSpec((None, r, w), lambda g, pt_ref: (g, 0, 0))
    grid_spec = pltpu.PrefetchScalarGridSpec(
        num_scalar_prefetch=1,
        grid=(db,),
        in_specs=[per_entry(nrow, HEAD_DIM), per_entry(t_new, a), per_entry(t_new, a),
                  const((2, QK_DIM)), const((2, QK_DIM)), const((1, HEAD_DIM)),
                  pl.BlockSpec(memory_space=pl.ANY), pl.BlockSpec(memory_space=pl.ANY)],
        out_specs=per_entry(t_new, a),
        scratch_shapes=[pltpu.VMEM((ahead + 1, n_pages, prow, HEAD_DIM), cache_k.dtype),
                        pltpu.VMEM((ahead + 1, n_pages, prow, HEAD_DIM), cache_v.dtype),
                        pltpu.SemaphoreType.DMA((2, ahead + 1))],
    )
    return pl.pallas_call(
        functools.partial(_attn_decode_kernel, n_pages=n_pages, ahead=ahead),
        grid_spec=grid_spec,
        out_shape=jax.ShapeDtypeStruct((db, t_new, a), BF16),
        compiler_params=pltpu.CompilerParams(
            dimension_semantics=("arbitrary",), vmem_limit_bytes=VMEM_LIMIT),
        name="attn_decode",
    )(page_table.reshape(-1), q_rep, kn, vn, p["lam_q"], p["lam_k"], p["sub_gain"], cache_k, cache_v)


def _ssm_param_kernel(ar_ref, ai_ref, ldt_ref, br_ref, bi_ref, abr_ref, abi_ref, bbr_ref, bbi_ref):
    ar, ai = ar_ref[...], ai_ref[...]
    dt = jnp.exp(ldt_ref[...])
    mag = jnp.exp(ar * dt)
    abr, abi = mag * jnp.cos(ai * dt), mag * jnp.sin(ai * dt)
    den = ar * ar + ai * ai
    nr, ni = abr - 1.0, abi
    gr, gi = (nr * ar + ni * ai) / den, (ni * ar - nr * ai) / den
    abr_ref[...] = abr
    abi_ref[...] = abi
    for c in range(SSM_GROUP):
        br, bi = br_ref[c], bi_ref[c]
        bbr_ref[c] = gr * br - gi * bi
        bbi_ref[c] = gr * bi + gi * br


def _ssm_params(a_re, a_im, log_dt, b_re, b_im):
    g, pdim = a_re.shape
    c = b_re.shape[-1]
    b_re_t = jnp.transpose(b_re, (2, 0, 1))
    b_im_t = jnp.transpose(b_im, (2, 0, 1))
    gp = jax.ShapeDtypeStruct((g, pdim), F32)
    cgp = jax.ShapeDtypeStruct((c, g, pdim), F32)
    return pl.pallas_call(_ssm_param_kernel, out_shape=(gp, gp, cgp, cgp), name="ssm_params")(
        a_re, a_im, log_dt.reshape(g, 1), b_re_t, b_im_t)


def _ssm_kernel(u_ref, up_ref, h0r_ref, h0i_ref, abr_ref, abi_ref, bre_ref, bim_ref, crt_ref, cit_ref,
                d_ref, wg_ref, bg_ref, y_ref, hr_ref, hi_ref, xr_s, xi_s, ab_s, *, tc, nb, n_chunks):
    i = pl.program_id(0)

    @pl.when(i == 0)
    def _():
        hr_ref[...] = h0r_ref[...]
        hi_ref[...] = h0i_ref[...]
        ab_s[0] = jnp.broadcast_to(abr_ref[...], ab_s.shape[1:])
        ab_s[1] = jnp.broadcast_to(abi_ref[...], ab_s.shape[1:])
        if n_chunks > 1:
            xr_s[1] = jnp.zeros(xr_s.shape[1:], F32)
            xi_s[1] = jnp.zeros(xi_s.shape[1:], F32)

    def body(cur, prev):
        rows, width = tc * nb, d_ref.shape[1]
        ub = u_ref[...].reshape(rows, width).astype(BF16)
        kin, nout = bre_ref.shape[1], bre_ref.shape[2]
        for mblk in range(bre_ref.shape[0]):
            um = ub[:, mblk * kin:(mblk + 1) * kin]
            xr_s[cur, :, mblk * nout:(mblk + 1) * nout] = _dot(um, bre_ref[mblk])
            xi_s[cur, :, mblk * nout:(mblk + 1) * nout] = _dot(um, bim_ref[mblk])

        hr, hi = hr_ref[...], hi_ref[...]
        for t in range(tc):
            r = slice(t * nb, (t + 1) * nb)
            abr, abi = ab_s[0], ab_s[1]
            nhr = abr * hr - abi * hi + xr_s[cur, r, :]
            nhi = abr * hi + abi * hr + xi_s[cur, r, :]
            xr_s[cur, r, :] = nhr
            xi_s[cur, r, :] = nhi
            hr, hi = nhr, nhi
        live = i < n_chunks
        hr_ref[...] = jnp.where(live, hr, hr_ref[...])
        hi_ref[...] = jnp.where(live, hi, hi_ref[...])

        kout = crt_ref.shape[1]
        ys = []
        for j in range(crt_ref.shape[0]):
            hrj = xr_s[prev, :, j * kout:(j + 1) * kout].astype(BF16)
            hij = xi_s[prev, :, j * kout:(j + 1) * kout].astype(BF16)
            ys.append(_dot(hrj, crt_ref[j]) - _dot(hij, cit_ref[j]))
        y = jnp.concatenate(ys, axis=1) + d_ref[...] * up_ref[...].reshape(rows, width)
        g = jax.nn.gelu(y)
        out = g * jax.nn.sigmoid(_dot(g.astype(BF16), wg_ref[...].astype(BF16)) + bg_ref[...])
        y_ref[...] = out.reshape(tc, nb, width)

    if n_chunks == 1:
        body(0, 0)
    else:
        for par in (0, 1):
            pl.when(i % 2 == par)(functools.partial(body, par, 1 - par))


def _ssm(u3, h0r, h0i, p, tc):
    t_len, nb, width = u3.shape
    n_state = h0r.shape[1]
    n_chunks = t_len // tc
    assert n_chunks * tc == t_len
    st = jax.ShapeDtypeStruct((nb, n_state), F32)
    st_spec = pl.BlockSpec((nb, n_state), lambda i: (0, 0))
    chunk = lambda index: pl.BlockSpec((tc, nb, width), lambda i: (index(i), 0, 0))
    return pl.pallas_call(
        functools.partial(_ssm_kernel, tc=tc, nb=nb, n_chunks=n_chunks),
        grid=(n_chunks + 1 if n_chunks > 1 else 1,),
        in_specs=[chunk(lambda i: jnp.minimum(i, n_chunks - 1)), chunk(lambda i: jnp.maximum(i - 1, 0)),
                  _const_spec((nb, n_state)), _const_spec((nb, n_state)),
                  _const_spec((1, n_state)), _const_spec((1, n_state)),
                  _const_spec(p["bre"].shape), _const_spec(p["bim"].shape),
                  _const_spec(p["crt"].shape), _const_spec(p["cit"].shape),
                  _const_spec((1, width)), _const_spec((width, width)), _const_spec((1, width))],
        out_specs=(chunk(lambda i: jnp.maximum(i - 1, 0)), st_spec, st_spec),
        out_shape=(jax.ShapeDtypeStruct((t_len, nb, width), F32), st, st),
        scratch_shapes=[pltpu.VMEM((min(n_chunks, 2), tc * nb, n_state), F32)] * 2
        + [pltpu.VMEM((2, nb, n_state), F32)],
        compiler_params=pltpu.CompilerParams(
            dimension_semantics=("arbitrary",), vmem_limit_bytes=VMEM_LIMIT),
        name="ssm",
    )(u3, u3, h0r, h0i, p["abr"], p["abi"], p["bre"], p["bim"], p["crt"], p["cit"],
      p["d_skip"], p["w_glu"], p["b_glu"])


def _post_head(x_ref, att_ref, ssm_ref, wo_ref, g2_ref, y_ref, xn_s):
    tm, a = y_ref.shape[0], wo_ref.shape[0] // 2
    mix = jnp.concatenate([att_ref[...].reshape(tm, a), ssm_ref[...].reshape(tm, a).astype(BF16)], axis=1)
    xm = x_ref[...] + _dot(mix, wo_ref[...].astype(BF16))
    y_ref[...] = xm
    r = lax.rsqrt(jnp.mean(xm * xm, axis=-1, keepdims=True) + EPS)
    xn_s[...] = ((xm * r) * g2_ref[...]).astype(BF16)


def _post_chunks(j0, j1, wgate_ref, wup_ref, cw_ref, cb_ref, xn_s, h_s, ap_s, hist_s, *, shift, fc):
    tm = xn_s.shape[0]
    off = ap_s.shape[1] - tm
    for j in range(j0, j1):
        cs = slice(j * fc, (j + 1) * fc)
        ap = ap_s.at[j % 2]
        gate = _dot(xn_s[...], wgate_ref[:, cs])
        up = _dot(xn_s[...], wup_ref[:, cs])
        ap[off - 2 * shift:off, :] = hist_s[:, cs]
        ap[off:off + tm, :] = gate
        conv = (cb_ref[:, cs] + cw_ref[0:1, cs] * ap[off - 2 * shift:off - 2 * shift + tm, :]
                + cw_ref[1:2, cs] * ap[off - shift:off - shift + tm, :] + cw_ref[2:3, cs] * gate)
        hist_s[:, cs] = ap[off + tm - 2 * shift:off + tm, :]
        h_s[:, cs] = (jax.nn.gelu(conv) * up).astype(BF16)


def _load_rounded(src_hbm, dst_s, stg, sem, sem_row, n_parts):
    rows = src_hbm.shape[0] // n_parts
    copy = lambda c: pltpu.make_async_copy(src_hbm.at[pl.ds(c * rows, rows)], stg.at[c % 2], sem.at[sem_row, c % 2])
    copy(0).start()
    for c in range(n_parts):
        if c + 1 < n_parts:
            copy(c + 1).start()
        copy(c).wait()
        dst_s[c * rows:(c + 1) * rows, :] = stg[c % 2].astype(BF16)


def _post_kernel(x_ref, att_ref, ssm_ref, *refs, shift, fc, n_lead, cast):
    n_hist = 3 if n_lead else 1
    hist_src, rest = refs[:n_hist], refs[n_hist:]
    wo_ref, g2_ref, wgate_ref, wup_ref, cw_ref, cb_ref, wd_ref, y_ref, cst_ref = rest[:9]
    first_step = jnp.logical_and(pl.program_id(0) == 0, pl.program_id(1) == 0)
    if cast:
        w_hbm, w_out_hbm = (wgate_ref, wup_ref, wd_ref), rest[9:12]
        xn_s, h_s, ap_s, hist_s, wgate_ref, wup_ref, wd_ref, stg_a, stg_b, sem = rest[12:]
        w_vmem = (wgate_ref, wup_ref, wd_ref)
        back = [pltpu.make_async_copy(w_vmem[k], w_out_hbm[k], sem.at[2, k]) for k in range(3)]

        @pl.when(first_step)
        def _():
            n_parts = wgate_ref.shape[0] // stg_a.shape[1]
            _load_rounded(w_hbm[0], wgate_ref, stg_a, sem, 0, n_parts)
            _load_rounded(w_hbm[1], wup_ref, stg_a, sem, 0, n_parts)
            _load_rounded(w_hbm[2], wd_ref, stg_b, sem, 1, wd_ref.shape[0] // stg_b.shape[1])
            for c in back:
                c.start()
    else:
        xn_s, h_s, ap_s, hist_s = rest[9:]
    if n_lead:
        xl_ref, al_ref, sl_ref = hist_src
        lead_s = hist_s.at[1]
        hist_s = hist_s.at[0]

        @pl.when(first_step)
        def _():
            mix = jnp.concatenate([al_ref[...], sl_ref[...].astype(BF16)], axis=1)
            xm = xl_ref[...] + _dot(mix, wo_ref[...].astype(BF16))
            r = lax.rsqrt(jnp.mean(xm * xm, axis=-1, keepdims=True) + EPS)
            gate = _dot(((xm * r) * g2_ref[...]).astype(BF16), wgate_ref[...])
            lead_s[...] = gate[n_lead - 2 * shift:n_lead, :]

        first_hist = lead_s
    else:
        first_hist, = hist_src
        hist_s = hist_s.at[0]

    @pl.when(pl.program_id(1) == 0)
    def _():
        hist_s[...] = first_hist[...]

    _post_head(x_ref, att_ref, ssm_ref, wo_ref, g2_ref, y_ref, xn_s)
    _post_chunks(0, wgate_ref.shape[1] // fc, wgate_ref, wup_ref, cw_ref, cb_ref, xn_s, h_s, ap_s, hist_s,
                 shift=shift, fc=fc)
    y_ref[...] += _dot(h_s[...], wd_ref[...])
    cst_ref[...] = hist_s[...]
    if cast:
        @pl.when(first_step)
        def _():
            for c in back:
                c.wait()


def _post(x2d, att, ssm, hist, p, tm, nb, shift, att_spec, ssm_spec, lead=None, fc=256, n_parts=8):
    rows, d = x2d.shape
    d_ff = p["w_gate"].shape[1]
    cast = p["w_gate"].dtype != BF16
    per_b = rows // nb
    nt = per_b // tm
    assert nt * tm == per_b and d_ff % fc == 0 and tm >= 2 * shift
    off = -(-2 * shift // 8) * 8
    row_spec = lambda w: pl.BlockSpec((tm, w), lambda b, t: (b * nt + t, 0))
    hist_spec = pl.BlockSpec((None, 2 * shift, d_ff), lambda b, t: (b, 0, 0))
    if lead is None:
        hist_src, hist_specs, n_lead = (hist,), [hist_spec], 0
    else:
        hist_src, hist_specs, n_lead = lead, [_const_spec(z.shape) for z in lead], lead[0].shape[0]
        assert n_lead >= 2 * shift
    ffn_w = (p["w_gate"], p["w_up"], p["w_down"])
    out_specs = [row_spec(d), hist_spec]
    out_shape = [jax.ShapeDtypeStruct((rows, d), F32), jax.ShapeDtypeStruct((nb, 2 * shift, d_ff), F32)]
    scratch = [pltpu.VMEM((tm, d), BF16), pltpu.VMEM((tm, d_ff), BF16),
               pltpu.VMEM((2, off + tm, fc), F32), pltpu.VMEM((2, 2 * shift, d_ff), F32)]
    if cast:
        assert d % n_parts == 0 and d_ff % n_parts == 0
        w_specs = [pl.BlockSpec(memory_space=pl.ANY)] * 3
        out_specs += [pl.BlockSpec(memory_space=pl.ANY)] * 3
        out_shape += [jax.ShapeDtypeStruct(w.shape, BF16) for w in ffn_w]
        scratch += [pltpu.VMEM(w.shape, BF16) for w in ffn_w]
        scratch += [pltpu.VMEM((2, d // n_parts, d_ff), F32), pltpu.VMEM((2, d_ff // n_parts, d), F32),
                    pltpu.SemaphoreType.DMA((3, 3))]
    else:
        w_specs = [_const_spec(w.shape) for w in ffn_w]
    return pl.pallas_call(
        functools.partial(_post_kernel, shift=shift, fc=fc, n_lead=n_lead, cast=cast),
        grid=(nb, nt),
        in_specs=[row_spec(d), att_spec, ssm_spec, *hist_specs,
                  _const_spec(p["w_out"].shape), _const_spec((1, d)), w_specs[0], w_specs[1],
                  _const_spec((3, d_ff)), _const_spec((1, d_ff)), w_specs[2]],
        out_specs=tuple(out_specs),
        out_shape=tuple(out_shape),
        scratch_shapes=scratch,
        compiler_params=pltpu.CompilerParams(
            dimension_semantics=("arbitrary", "arbitrary"), vmem_limit_bytes=VMEM_LIMIT),
        name="post",
    )(x2d, att, ssm, *hist_src, p["w_out"], p["norm2"], ffn_w[0], ffn_w[1], p["conv_w"], p["conv_b"], ffn_w[2])


def _prepare_params(norm1, w_in, q_norm, k_norm, lam_q, lam_k, sub_norm, ssm_a_re, ssm_a_im, ssm_log_dt,
                    ssm_b_re, ssm_b_im, ssm_c_re, ssm_c_im, ssm_d, w_glu, b_glu, w_out, norm2, w_gate,
                    w_up, ffn_conv_w, ffn_conv_b, w_down):
    l = 0
    g, pdim = ssm_a_re[l].shape
    c = SSM_GROUP
    a = N_HEADS * HEAD_DIM
    abr, abi, bbr, bbi = _ssm_params(ssm_a_re[l], ssm_a_im[l], ssm_log_dt[l], ssm_b_re[l], ssm_b_im[l])
    gi = LANES // c

    def in_blocks(bb):
        bb = jnp.transpose(bb, (1, 0, 2)).reshape(g // gi, gi, c, 1, pdim)
        same = jnp.eye(gi, dtype=bool)[None, :, None, :, None]
        return jnp.where(same, bb, 0.0).astype(BF16).reshape(g // gi, gi * c, gi * pdim)

    go = MXU // c

    def out_blocks(cc):
        cc = jnp.transpose(cc.reshape(g // go, go, c, pdim), (0, 1, 3, 2))[:, :, :, None, :]
        same = jnp.eye(go, dtype=bool)[None, :, None, :, None]
        return jnp.where(same, cc, 0.0).astype(BF16).reshape(g // go, go * pdim, go * c)

    comp = jnp.arange(MXU) // QK_DIM
    ones_blk = (comp[:, None] == comp[None, :]).astype(BF16) * (1.0 / QK_DIM)
    return {
        "norm1": norm1[l].reshape(1, -1), "w_in": w_in[l],
        "q_gain": jnp.tile(q_norm[l].reshape(-1), N_HEADS).reshape(1, a),
        "k_gain": jnp.tile(k_norm[l].reshape(-1), N_HEADS).reshape(1, a),
        "ones_blk": ones_blk.astype(BF16),
        "lam_q": lam_q[l], "lam_k": lam_k[l],
        "sub_gain": (sub_norm[l] * (1.0 - LAM_INIT)).reshape(1, HEAD_DIM),
        "abr": abr.reshape(1, g * pdim), "abi": abi.reshape(1, g * pdim),
        "bre": in_blocks(bbr), "bim": in_blocks(bbi),
        "crt": out_blocks(ssm_c_re[l]), "cit": out_blocks(ssm_c_im[l]),
        "d_skip": ssm_d[l].reshape(1, g * c), "w_glu": w_glu[l], "b_glu": b_glu[l].reshape(1, -1),
        "w_out": w_out[l], "norm2": norm2[l].reshape(1, -1),
        "w_gate": w_gate[l], "w_up": w_up[l],
        "conv_w": ffn_conv_w[l], "conv_b": ffn_conv_b[l].reshape(1, -1), "w_down": w_down[l],
    }


def _row_tile(length, cap=768):
    best = None
    for t in range(16, cap + 1, 16):
        if length % t == 0:
            best = t
    assert best is not None
    return best


def kernel(x_prompt, x_sample, cache_k, cache_v, state_ssm_re, state_ssm_im, state_ffn_conv, page_table, meta_tokens, norm1, w_in, q_norm, k_norm, lam_q, lam_k, sub_norm, ssm_a_re, ssm_a_im, ssm_log_dt, ssm_b_re, ssm_b_im, ssm_c_re, ssm_c_im, ssm_d, w_glu, b_glu, w_out, norm2, w_gate, w_up, ffn_conv_w, ffn_conv_b, w_down):
    assert norm1.shape[0] == 1, "single-layer stack"
    p = _prepare_params(norm1, w_in, q_norm, k_norm, lam_q, lam_k, sub_norm, ssm_a_re, ssm_a_im, ssm_log_dt,
                        ssm_b_re, ssm_b_im, ssm_c_re, ssm_c_im, ssm_d, w_glu, b_glu, w_out, norm2, w_gate,
                        w_up, ffn_conv_w, ffn_conv_b, w_down)
    nb, seq, d = x_prompt.shape
    db, t_new, _ = x_sample.shape
    a = N_HEADS * HEAD_DIM
    g, pdim = ssm_a_re.shape[1:]
    n_state = g * pdim
    d_ff = w_gate.shape[-1]
    length = seq + N_META

    meta = meta_tokens.astype(x_prompt.dtype)
    tm = _row_tile(length)
    qb, k, kb, v, vb, u = _project(x_prompt, meta, tm, p, u_time_major=True)
    att = _attn_prompt(qb.reshape(nb, length, a), kb.reshape(nb, length, a), vb.reshape(nb, length, a), p)
    zst = jnp.zeros((nb, n_state), F32)
    tc = _row_tile(length * nb, cap=768) // nb
    ys, hr, hi = _ssm(u.reshape(length, nb, a), zst, zst, p, tc)
    ys2 = ys.reshape(length, nb * a)
    tp = _row_tile(seq, cap=512)
    att_spec = pl.BlockSpec((pl.Element(1), pl.Element(tp), pl.Element(a)),
                            lambda b, t: (b, pl.multiple_of(N_META + t * tp, N_META), 0))
    ssm_spec = pl.BlockSpec((pl.Element(tp), pl.Element(a)),
                            lambda b, t: (pl.multiple_of(N_META + t * tp, N_META), pl.multiple_of(b * a, a)))
    yp, cst_p, *ffn_bf16 = _post(x_prompt.reshape(nb * seq, d), att, ys2, None, p, tp, nb, shift=1,
                                 att_spec=att_spec, ssm_spec=ssm_spec,
                                 lead=(meta, att[0, :N_META], ys2[:N_META, :a]))
    p = dict(p, w_gate=ffn_bf16[0], w_up=ffn_bf16[1], w_down=ffn_bf16[2])
    y_prompt = yp.reshape(nb, seq, d)
    k_prompt = k.reshape(1, nb, length, N_HEADS, HEAD_DIM)
    v_prompt = v.reshape(1, nb, length, N_HEADS, HEAD_DIM)
    ssm_re_p = hr.reshape(1, nb, g, pdim)
    ssm_im_p = hi.reshape(1, nb, g, pdim)
    conv_p = cst_p[None]

    rows_s = db * t_new
    xs2 = jnp.transpose(x_sample, (1, 0, 2)).reshape(rows_s, d)
    ts = 2 * db
    qb, k, kb, v, vb, u = _project(xs2, None, ts, p, u_time_major=False)
    to_bm = lambda z: jnp.transpose(z.reshape(t_new, db, a), (1, 0, 2))
    q_hct = jnp.transpose(qb.reshape(t_new, db, N_HEADS, 1, HEAD_DIM), (1, 2, 3, 0, 4))
    q_rep = jnp.broadcast_to(q_hct, (db, N_HEADS, 2, t_new, HEAD_DIM)).reshape(db, N_HEADS * 2 * t_new, HEAD_DIM)
    n_pool, page = cache_k.shape[1:3]
    att = _attn_decode(q_rep, to_bm(kb), to_bm(vb), cache_k[0].reshape(n_pool, page * N_HEADS, HEAD_DIM),
                       cache_v[0].reshape(n_pool, page * N_HEADS, HEAD_DIM), page_table, p)
    att_tm = jnp.transpose(att, (1, 0, 2)).reshape(rows_s, a)
    ys, hr, hi = _ssm(u.reshape(t_new, db, a), state_ssm_re[0].reshape(db, n_state),
                      state_ssm_im[0].reshape(db, n_state), p, t_new)
    hist = jnp.transpose(state_ffn_conv[0], (1, 0, 2)).reshape(1, 2 * db, d_ff)
    rows_spec = pl.BlockSpec((ts, a), lambda b, t: (t, 0))
    ysm, cst_s = _post(xs2, att_tm, ys.reshape(rows_s, a), hist, p, ts, 1, shift=db,
                       att_spec=rows_spec, ssm_spec=rows_spec)
    y_sample = jnp.transpose(ysm.reshape(t_new, db, d), (1, 0, 2))
    k_sample = jnp.transpose(k.reshape(t_new, db, N_HEADS, HEAD_DIM), (1, 0, 2, 3))[None]
    v_sample = jnp.transpose(v.reshape(t_new, db, N_HEADS, HEAD_DIM), (1, 0, 2, 3))[None]
    ssm_re_s = hr.reshape(1, db, g, pdim)
    ssm_im_s = hi.reshape(1, db, g, pdim)
    conv_s = jnp.transpose(cst_s.reshape(2, db, d_ff), (1, 0, 2))[None]

    return (y_prompt, y_sample, k_prompt, v_prompt, k_sample, v_sample,
            ssm_re_p, ssm_im_p, ssm_re_s, ssm_im_s, conv_p, conv_s)
```

```python
import functools
import math

import jax
import jax.numpy as jnp
from jax import lax
from jax.experimental import pallas as pl
from jax.experimental.pallas import tpu as pltpu

N_META = 16
N_HEADS = 4
QK_DIM = 64
HEAD_DIM = 2 * QK_DIM
SSM_GROUP = 16
SSM_STATE = 64
EPS = 1e-6
NEG = -1e30
LAM_INIT = 0.8 - 0.6 * math.exp(-0.3 * 0)
LOG2E = math.log2(math.e)

LANES = 128
MXU = 256
VMEM_LIMIT = 56 * 1024 * 1024

F32 = jnp.float32
BF16 = jnp.bfloat16


def _dot(a, b):
    return jnp.dot(a, b, preferred_element_type=F32)


def _dot_nt(a, b):
    return lax.dot_general(a, b, (((1,), (1,)), ((), ())), preferred_element_type=F32)


def _const_spec(shape):
    nd = len(shape)
    return pl.BlockSpec(shape, lambda *_: (0,) * nd, pipeline_mode=pl.Buffered(1))


def _lam(lq_ref, lk_ref):
    e = jnp.exp(jnp.sum(lq_ref[...] * lk_ref[...], axis=1, keepdims=True))
    return e[0:1] - e[1:2] + LAM_INIT


def _proj_kernel(x_ref, meta_ref, g1_ref, w_ref, qg_ref, kg_ref, ones_ref,
                 qb_ref, k_ref, kb_ref, v_ref, vb_ref, u_ref, wb_s, *, n_meta):
    outs = (qb_ref, k_ref, kb_ref, v_ref, vb_ref, u_ref)
    consts = (g1_ref, wb_s, qg_ref, kg_ref, ones_ref)
    tm = qb_ref.shape[0]

    @pl.when(jnp.logical_and(pl.program_id(0) == 0, pl.program_id(1) == 0))
    def _():
        wb_s[...] = w_ref[...].astype(BF16)

    if not n_meta:
        _proj_body(x_ref[...], consts, outs)
        return
    t = pl.program_id(1)

    @pl.when(t == 0)
    def _():
        _proj_body(jnp.concatenate([meta_ref[...], x_ref[0, 0:tm - n_meta, :]], axis=0), consts, outs)

    @pl.when(t != 0)
    def _():
        _proj_body(x_ref[0], consts, outs)


def _proj_body(x, consts, outs):
    g1_ref, w_ref, qg_ref, kg_ref, ones_ref = consts
    qb_ref, k_ref, kb_ref, v_ref, vb_ref, u_ref = outs
    a = N_HEADS * HEAD_DIM
    r = lax.rsqrt(jnp.mean(x * x, axis=-1, keepdims=True) + EPS)
    xn = ((x * r) * g1_ref[...]).astype(BF16)
    proj = _dot(xn, w_ref[...])

    def comp_norm(z, g):
        z2 = (z * z).astype(BF16)
        ms = jnp.concatenate(
            [_dot(z2[:, j * MXU:(j + 1) * MXU], ones_ref[...]) for j in range(a // MXU)], axis=1)
        return (z * lax.rsqrt(ms + EPS)) * g

    qn = comp_norm(proj[:, :a], qg_ref[...])
    kn = comp_norm(proj[:, a:2 * a], kg_ref[...])
    v = proj[:, 2 * a:3 * a]
    qb_ref[...] = (qn * (QK_DIM ** -0.5 * LOG2E)).astype(BF16)
    kb_ref[...] = kn.astype(BF16)
    vb_ref[...] = v.astype(BF16)
    u_ref[...] = proj[:, 3 * a:]
    tm = x.shape[0]
    for h in range(N_HEADS):
        k_ref[pl.ds(h, tm, stride=N_HEADS), :] = kn[:, h * HEAD_DIM:(h + 1) * HEAD_DIM]
        v_ref[pl.ds(h, tm, stride=N_HEADS), :] = v[:, h * HEAD_DIM:(h + 1) * HEAD_DIM]


def _project(x, meta, tm, p, u_time_major):
    a = N_HEADS * HEAD_DIM
    d = x.shape[-1]
    if meta is None:
        nb, n_meta = 1, 0
        per_b = x.shape[0]
        x_spec = pl.BlockSpec((tm, d), lambda b, t: (t, 0))
        meta = jnp.zeros((8, d), x.dtype)
    else:
        nb, n_meta = x.shape[0], meta.shape[0]
        per_b = n_meta + x.shape[1]
        x_spec = pl.BlockSpec(
            (pl.Element(1), pl.Element(tm), pl.Element(d)),
            lambda b, t: (b, pl.multiple_of(jnp.maximum(t * tm - n_meta, 0), 8), 0))
    rows = nb * per_b
    nt = per_b // tm
    assert nt * tm == per_b and n_meta % 8 == 0
    grid = (nb, nt)
    row_spec = lambda w: pl.BlockSpec((tm, w), lambda b, t: (b * nt + t, 0))
    head_spec = pl.BlockSpec((tm * N_HEADS, HEAD_DIM), lambda b, t: (b * nt + t, 0))
    if u_time_major:
        u_shape = jax.ShapeDtypeStruct((per_b, nb * a), F32)
        u_spec = pl.BlockSpec((tm, a), lambda b, t: (t, b))
    else:
        u_shape = jax.ShapeDtypeStruct((rows, a), F32)
        u_spec = row_spec(a)
    out_shape = (jax.ShapeDtypeStruct((rows, a), BF16), jax.ShapeDtypeStruct((rows * N_HEADS, HEAD_DIM), F32),
                 jax.ShapeDtypeStruct((rows, a), BF16), jax.ShapeDtypeStruct((rows * N_HEADS, HEAD_DIM), F32),
                 jax.ShapeDtypeStruct((rows, a), BF16), u_shape)
    return pl.pallas_call(
        functools.partial(_proj_kernel, n_meta=n_meta),
        grid=grid,
        in_specs=[x_spec, _const_spec(meta.shape), _const_spec((1, d)), _const_spec(p["w_in"].shape),
                  _const_spec((1, a)), _const_spec((1, a)), _const_spec((MXU, MXU))],
        out_specs=(row_spec(a), head_spec, row_spec(a), head_spec, row_spec(a), u_spec),
        out_shape=out_shape,
        scratch_shapes=[pltpu.VMEM(p["w_in"].shape, BF16)],
        compiler_params=pltpu.CompilerParams(
            dimension_semantics=("arbitrary", "arbitrary"), vmem_limit_bytes=VMEM_LIMIT),
        name="proj",
    )(x, meta, p["norm1"], p["w_in"], p["q_gain"], p["k_gain"], p["ones_blk"])


def _attn_prompt_kernel(q_ref, k_ref, v_ref, lq_ref, lk_ref, sg_ref, o_ref, acc_s, m_s, s_s, *, tq, n_tiles):
    lam = _lam(lq_ref, lk_ref)
    sg = sg_ref[...]
    lo = lax.broadcasted_iota(jnp.int32, (1, HEAD_DIM), 1) < QK_DIM
    n_ones = acc_s.shape[1] - HEAD_DIM

    def hcols(h):
        return slice(h * HEAD_DIM, (h + 1) * HEAD_DIM)

    def stack_q(q):
        z = jnp.zeros_like(q)
        return jnp.concatenate([jnp.where(lo, q, z), jnp.where(lo, z, q)], axis=0)

    def init(h, t2):
        m_s[h, :, 0:t2] = jnp.full((1, t2), NEG, F32)
        acc_s[h, :, 0:t2] = jnp.zeros((HEAD_DIM + n_ones, t2), F32)

    def scores(slot, h, q2, kt, mask):
        s = _dot_nt(kt, q2)
        if mask is not None:
            s = jnp.where(mask, s, NEG)
        s_s[slot, h, 0:kt.shape[0], 0:q2.shape[0]] = s

    def absorb(slot, h, nk, t2, vt):
        s = s_s[slot, h, 0:nk, 0:t2]
        m_prev = m_s[h, :, 0:t2]
        m_new = jnp.maximum(m_prev, jnp.max(s, axis=0, keepdims=True))
        alpha = jnp.exp2(m_prev - m_new)
        pr = jnp.exp2(s - m_new).astype(BF16)
        va = jnp.concatenate([vt.T, jnp.ones((n_ones, nk), BF16)], axis=0)
        acc_s[h, :, 0:t2] = alpha * acc_s[h, :, 0:t2] + _dot(va, pr)
        m_s[h, :, 0:t2] = m_new

    def finish(h, t):
        acc = acc_s[h, :, 0:2 * t]
        on = acc[0:HEAD_DIM] / acc[HEAD_DIM:HEAD_DIM + 1]
        o = on[:, 0:t] - lam * on[:, t:2 * t]
        r = lax.rsqrt(jnp.mean(o * o, axis=0, keepdims=True) + EPS)
        return ((o * r).T * sg).astype(o_ref.dtype)

    def causal_mask(t, nk, offset):
        qi = lax.broadcasted_iota(jnp.int32, (nk, 2 * t), 1)
        qi = jnp.where(qi >= t, qi - t, qi)
        ki = lax.broadcasted_iota(jnp.int32, (nk, 2 * t), 0)
        return ki <= qi + offset

    tmq = LANES
    for h in range(N_HEADS):
        init(h, 2 * tmq)
        scores(0, h, stack_q(q_ref[0:tmq, hcols(h)]), k_ref[0:tmq, hcols(h)], causal_mask(tmq, tmq, 0))
    for h in range(N_HEADS):
        absorb(0, h, tmq, 2 * tmq, v_ref[0:tmq, hcols(h)])
        o_ref[0:N_META, hcols(h)] = finish(h, tmq)[0:N_META]

    dk = tq + N_META

    def q_tile(j, carry):
        qs = pl.multiple_of(N_META + j * tq, N_META)
        q2 = [stack_q(q_ref[pl.ds(qs, tq), hcols(h)]) for h in range(N_HEADS)]
        ds = pl.multiple_of(j * tq, tq)
        dmask = causal_mask(tq, dk, N_META)
        for h in range(N_HEADS):
            init(h, 2 * tq)

        def plain_scores(slot, i):
            ks = pl.multiple_of(i * tq, tq)
            for h in range(N_HEADS):
                scores(slot, h, q2[h], k_ref[pl.ds(ks, tq), hcols(h)], None)

        def diag_scores(slot):
            for h in range(N_HEADS):
                scores(slot, h, q2[h], k_ref[pl.ds(ds, dk), hcols(h)], dmask)

        def plain_absorb(slot, i):
            ks = pl.multiple_of(i * tq, tq)
            for h in range(N_HEADS):
                absorb(slot, h, tq, 2 * tq, v_ref[pl.ds(ks, tq), hcols(h)])

        def diag_absorb(slot):
            for h in range(N_HEADS):
                absorb(slot, h, dk, 2 * tq, v_ref[pl.ds(ds, dk), hcols(h)])
                o_ref[pl.ds(qs, tq), hcols(h)] = finish(h, tq)

        @pl.when(j == 0)
        def _():
            diag_scores(0)
            diag_absorb(0)

        @pl.when(j > 0)
        def _():
            plain_scores(0, 0)

        n_pairs = (j - 1) // 2

        def k_pair(pi, c):
            i = 2 * pi
            plain_scores(1, i + 1)
            plain_absorb(0, i)
            plain_scores(0, i + 2)
            plain_absorb(1, i + 1)
            return c

        lax.fori_loop(0, n_pairs, k_pair, 0)

        @pl.when(jnp.logical_and(j > 0, j % 2 == 1))
        def _():
            diag_scores(1)
            plain_absorb(0, j - 1)
            diag_absorb(1)

        @pl.when(jnp.logical_and(j > 0, j % 2 == 0))
        def _():
            plain_scores(1, j - 1)
            plain_absorb(0, j - 2)
            diag_scores(0)
            plain_absorb(1, j - 1)
            diag_absorb(0)

        return carry

    lax.fori_loop(0, n_tiles, q_tile, 0)


def _attn_prompt(qb, kb, vb, p, tq=256):
    nb, length, a = qb.shape
    n_tiles = (length - N_META) // tq
    assert N_META + n_tiles * tq == length and tq % LANES == 0
    spec = pl.BlockSpec((None, length, a), lambda b: (b, 0, 0))
    return pl.pallas_call(
        functools.partial(_attn_prompt_kernel, tq=tq, n_tiles=n_tiles),
        grid=(nb,),
        in_specs=[spec, spec, spec, _const_spec((2, QK_DIM)), _const_spec((2, QK_DIM)),
                  _const_spec((1, HEAD_DIM))],
        out_specs=spec,
        out_shape=jax.ShapeDtypeStruct((nb, length, a), BF16),
        scratch_shapes=[pltpu.VMEM((N_HEADS, HEAD_DIM + 16, 2 * tq), F32), pltpu.VMEM((N_HEADS, 1, 2 * tq), F32),
                        pltpu.VMEM((2, N_HEADS, tq + N_META, 2 * tq), F32)],
        compiler_params=pltpu.CompilerParams(
            dimension_semantics=("arbitrary",), vmem_limit_bytes=VMEM_LIMIT),
        name="attn_prompt",
    )(qb, kb, vb, p["lam_q"], p["lam_k"], p["sub_gain"])


def _decode_entry(q, kn, vn, kp, vp, lam, sg, o_ref):
    t_new = kn.shape[0]
    grp = 2 * t_new
    nrow = N_HEADS * grp
    ncol = kp[0].shape[0]
    row = lax.broadcasted_iota(jnp.int32, (nrow, HEAD_DIM), 0)
    lane = lax.broadcasted_iota(jnp.int32, (nrow, HEAD_DIM), 1)
    comp_ok = (lane >= QK_DIM) == ((row % grp) >= t_new)
    t_of_row = lax.broadcasted_iota(jnp.int32, (nrow, 1), 0) % t_new
    head_ok = (lax.broadcasted_iota(jnp.int32, (nrow, ncol), 1) % N_HEADS
               == lax.broadcasted_iota(jnp.int32, (nrow, ncol), 0) // grp)

    def per_row_head(z, t):
        return jnp.concatenate(
            [jnp.broadcast_to(z[t:t + 1, h * HEAD_DIM:(h + 1) * HEAD_DIM], (grp, HEAD_DIM)) for h in range(N_HEADS)],
            axis=0)

    q32 = jnp.where(comp_ok, q.astype(F32), 0.0)
    qb = q32.astype(BF16)
    kn, vn = kn.astype(F32), vn.astype(F32)
    s_pages = [jnp.where(head_ok, _dot_nt(qb, k_ref[...].astype(BF16)), NEG) for k_ref in kp]
    s_new = []
    for t in range(t_new):
        sc = jnp.sum(q32 * per_row_head(kn, t), axis=1, keepdims=True)
        s_new.append(jnp.where(t_of_row >= t, sc, NEG))

    m = s_pages[0]
    for s in s_pages[1:]:
        m = jnp.maximum(m, s)
    m = jnp.max(m, axis=1, keepdims=True)
    for s in s_new:
        m = jnp.maximum(m, s)

    acc = jnp.zeros((nrow, HEAD_DIM), F32)
    psum = None
    for s, v_ref in zip(s_pages, vp):
        pr = jnp.exp2(s - m)
        psum = pr if psum is None else psum + pr
        acc = acc + _dot(pr.astype(BF16), v_ref[...].astype(BF16))
    lsum = jnp.sum(psum, axis=1, keepdims=True)
    for t in range(t_new):
        pr = jnp.exp2(s_new[t] - m)
        lsum = lsum + pr
        acc = acc + pr * per_row_head(vn, t)
    on = acc / lsum
    for h in range(N_HEADS):
        r0 = h * grp
        o = on[r0:r0 + t_new] - lam * on[r0 + t_new:r0 + grp]
        r = lax.rsqrt(jnp.mean(o * o, axis=-1, keepdims=True) + EPS)
        o_ref[:, h * HEAD_DIM:(h + 1) * HEAD_DIM] = ((o * r) * sg).astype(o_ref.dtype)


def _attn_decode_kernel(pt_ref, q_ref, kn_ref, vn_ref, lq_ref, lk_ref, sg_ref, ck_hbm, cv_hbm, o_ref,
                        kbuf, vbuf, sem, *, n_pages, ahead):
    g, n = pl.program_id(0), pl.num_programs(0)
    ring = kbuf.shape[0]

    def page_copies(slot, page_of):
        return [pltpu.make_async_copy(hbm.at[page_of(pg)], buf.at[slot, pg], sem.at[w, slot])
                for pg in range(n_pages) for w, (hbm, buf) in enumerate(((ck_hbm, kbuf), (cv_hbm, vbuf)))]

    def start(entry):
        for c in page_copies(entry % ring, lambda pg: pt_ref[entry * n_pages + pg]):
            c.start()

    @pl.when(g == 0)
    def _():
        for entry in range(ahead):
            start(entry)

    @pl.when(g + ahead < n)
    def _():
        start(g + ahead)

    slot = g % ring
    for c in page_copies(slot, lambda pg: 0):
        c.wait()
    _decode_entry(q_ref[...], kn_ref[...], vn_ref[...], [kbuf.at[slot, pg] for pg in range(n_pages)],
                  [vbuf.at[slot, pg] for pg in range(n_pages)], _lam(lq_ref, lk_ref), sg_ref[...], o_ref)


def _attn_decode(q_rep, kn, vn, cache_k, cache_v, page_table, p, ahead=2):
    db, nrow, _ = q_rep.shape
    t_new, a = kn.shape[1:]
    n_pages = page_table.shape[1]
    prow = cache_k.shape[1]
    assert db >= ahead
    const = lambda shape: pl.BlockSpec(shape, lambda g, pt_ref: (0,) * len(shape))
    per_entry = lambda r, w: pl.BlockSpec((None, r, w), lambda g, pt_ref: (g, 0, 0))
    grid_spec = pltpu.PrefetchScalarGridSpec(
        num_scalar_prefetch=1,
        grid=(db,),
        in_specs=[per_entry(nrow, HEAD_DIM), per_entry(t_new, a), per_entry(t_new, a),
                  const((2, QK_DIM)), const((2, QK_DIM)), const((1, HEAD_DIM)),
                  pl.BlockSpec(memory_space=pl.ANY), pl.BlockSpec(memory_space=pl.ANY)],
        out_specs=per_entry(t_new, a),
        scratch_shapes=[pltpu.VMEM((ahead + 1, n_pages, prow, HEAD_DIM), cache_k.dtype),
                        pltpu.VMEM((ahead + 1, n_pages, prow, HEAD_DIM), cache_v.dtype),
                        pltpu.SemaphoreType.DMA((2, ahead + 1))],
    )
    return pl.pallas_call(
        functools.partial(_attn_decode_kernel, n_pages=n_pages, ahead=ahead),
        grid_spec=grid_spec,
        out_shape=jax.ShapeDtypeStruct((db, t_new, a), BF16),
        compiler_params=pltpu.CompilerParams(
            dimension_semantics=("arbitrary",), vmem_limit_bytes=VMEM_LIMIT),
        name="attn_decode",
    )(page_table.reshape(-1), q_rep, kn, vn, p["lam_q"], p["lam_k"], p["sub_gain"], cache_k, cache_v)


def _ssm_param_kernel(ar_ref, ai_ref, ldt_ref, br_ref, bi_ref, abr_ref, abi_ref, bbr_ref, bbi_ref):
    ar, ai = ar_ref[...], ai_ref[...]
    dt = jnp.exp(ldt_ref[...])
    mag = jnp.exp(ar * dt)
    abr, abi = mag * jnp.cos(ai * dt), mag * jnp.sin(ai * dt)
    den = ar * ar + ai * ai
    nr, ni = abr - 1.0, abi
    gr, gi = (nr * ar + ni * ai) / den, (ni * ar - nr * ai) / den
    abr_ref[...] = abr
    abi_ref[...] = abi
    for c in range(SSM_GROUP):
        br, bi = br_ref[c], bi_ref[c]
        bbr_ref[c] = gr * br - gi * bi
        bbi_ref[c] = gr * bi + gi * br


def _ssm_params(a_re, a_im, log_dt, b_re, b_im):
    g, pdim = a_re.shape
    c = b_re.shape[-1]
    b_re_t = jnp.transpose(b_re, (2, 0, 1))
    b_im_t = jnp.transpose(b_im, (2, 0, 1))
    gp = jax.ShapeDtypeStruct((g, pdim), F32)
    cgp = jax.ShapeDtypeStruct((c, g, pdim), F32)
    return pl.pallas_call(_ssm_param_kernel, out_shape=(gp, gp, cgp, cgp), name="ssm_params")(
        a_re, a_im, log_dt.reshape(g, 1), b_re_t, b_im_t)


def _ssm_kernel(u_ref, up_ref, h0r_ref, h0i_ref, abr_ref, abi_ref, bre_ref, bim_ref, crt_ref, cit_ref,
                d_ref, wg_ref, bg_ref, y_ref, hr_ref, hi_ref, xr_s, xi_s, ab_s, *, tc, nb, n_chunks):
    i = pl.program_id(0)

    @pl.when(i == 0)
    def _():
        hr_ref[...] = h0r_ref[...]
        hi_ref[...] = h0i_ref[...]
        ab_s[0] = jnp.broadcast_to(abr_ref[...], ab_s.shape[1:])
        ab_s[1] = jnp.broadcast_to(abi_ref[...], ab_s.shape[1:])
        if n_chunks > 1:
            xr_s[1] = jnp.zeros(xr_s.shape[1:], F32)
            xi_s[1] = jnp.zeros(xi_s.shape[1:], F32)

    def body(cur, prev):
        rows, width = tc * nb, d_ref.shape[1]
        ub = u_ref[...].reshape(rows, width).astype(BF16)
        kin, nout = bre_ref.shape[1], bre_ref.shape[2]
        for mblk in range(bre_ref.shape[0]):
            um = ub[:, mblk * kin:(mblk + 1) * kin]
            xr_s[cur, :, mblk * nout:(mblk + 1) * nout] = _dot(um, bre_ref[mblk])
            xi_s[cur, :, mblk * nout:(mblk + 1) * nout] = _dot(um, bim_ref[mblk])

        hr, hi = hr_ref[...], hi_ref[...]
        for t in range(tc):
            r = slice(t * nb, (t + 1) * nb)
            abr, abi = ab_s[0], ab_s[1]
            nhr = abr * hr - abi * hi + xr_s[cur, r, :]
            nhi = abr * hi + abi * hr + xi_s[cur, r, :]
            xr_s[cur, r, :] = nhr
            xi_s[cur, r, :] = nhi
            hr, hi = nhr, nhi
        live = i < n_chunks
        hr_ref[...] = jnp.where(live, hr, hr_ref[...])
        hi_ref[...] = jnp.where(live, hi, hi_ref[...])

        kout = crt_ref.shape[1]
        ys = []
        for j in range(crt_ref.shape[0]):
            hrj = xr_s[prev, :, j * kout:(j + 1) * kout].astype(BF16)
            hij = xi_s[prev, :, j * kout:(j + 1) * kout].astype(BF16)
            ys.append(_dot(hrj, crt_ref[j]) - _dot(hij, cit_ref[j]))
        y = jnp.concatenate(ys, axis=1) + d_ref[...] * up_ref[...].reshape(rows, width)
        g = jax.nn.gelu(y)
        out = g * jax.nn.sigmoid(_dot(g.astype(BF16), wg_ref[...].astype(BF16)) + bg_ref[...])
        y_ref[...] = out.reshape(tc, nb, width)

    if n_chunks == 1:
        body(0, 0)
    else:
        for par in (0, 1):
            pl.when(i % 2 == par)(functools.partial(body, par, 1 - par))


def _ssm(u3, h0r, h0i, p, tc):
    t_len, nb, width = u3.shape
    n_state = h0r.shape[1]
    n_chunks = t_len // tc
    assert n_chunks * tc == t_len
    st = jax.ShapeDtypeStruct((nb, n_state), F32)
    st_spec = pl.BlockSpec((nb, n_state), lambda i: (0, 0))
    chunk = lambda index: pl.BlockSpec((tc, nb, width), lambda i: (index(i), 0, 0))
    return pl.pallas_call(
        functools.partial(_ssm_kernel, tc=tc, nb=nb, n_chunks=n_chunks),
        grid=(n_chunks + 1 if n_chunks > 1 else 1,),
        in_specs=[chunk(lambda i: jnp.minimum(i, n_chunks - 1)), chunk(lambda i: jnp.maximum(i - 1, 0)),
                  _const_spec((nb, n_state)), _const_spec((nb, n_state)),
                  _const_spec((1, n_state)), _const_spec((1, n_state)),
                  _const_spec(p["bre"].shape), _const_spec(p["bim"].shape),
                  _const_spec(p["crt"].shape), _const_spec(p["cit"].shape),
                  _const_spec((1, width)), _const_spec((width, width)), _const_spec((1, width))],
        out_specs=(chunk(lambda i: jnp.maximum(i - 1, 0)), st_spec, st_spec),
        out_shape=(jax.ShapeDtypeStruct((t_len, nb, width), F32), st, st),
        scratch_shapes=[pltpu.VMEM((min(n_chunks, 2), tc * nb, n_state), F32)] * 2
        + [pltpu.VMEM((2, nb, n_state), F32)],
        compiler_params=pltpu.CompilerParams(
            dimension_semantics=("arbitrary",), vmem_limit_bytes=VMEM_LIMIT),
        name="ssm",
    )(u3, u3, h0r, h0i, p["abr"], p["abi"], p["bre"], p["bim"], p["crt"], p["cit"],
      p["d_skip"], p["w_glu"], p["b_glu"])


def _post_head(x_ref, att_ref, ssm_ref, wo_ref, g2_ref, y_ref, xn_s):
    tm, a = y_ref.shape[0], wo_ref.shape[0] // 2
    mix = jnp.concatenate([att_ref[...].reshape(tm, a), ssm_ref[...].reshape(tm, a).astype(BF16)], axis=1)
    xm = x_ref[...] + _dot(mix, wo_ref[...].astype(BF16))
    y_ref[...] = xm
    r = lax.rsqrt(jnp.mean(xm * xm, axis=-1, keepdims=True) + EPS)
    xn_s[...] = ((xm * r) * g2_ref[...]).astype(BF16)


def _post_chunks(j0, j1, wgate_ref, wup_ref, cw_ref, cb_ref, xn_s, h_s, ap_s, hist_s, *, shift, fc):
    tm = xn_s.shape[0]
    off = ap_s.shape[1] - tm
    for j in range(j0, j1):
        cs = slice(j * fc, (j + 1) * fc)
        ap = ap_s.at[j % 2]
        gate = _dot(xn_s[...], wgate_ref[:, cs])
        up = _dot(xn_s[...], wup_ref[:, cs])
        ap[off - 2 * shift:off, :] = hist_s[:, cs]
        ap[off:off + tm, :] = gate
        conv = (cb_ref[:, cs] + cw_ref[0:1, cs] * ap[off - 2 * shift:off - 2 * shift + tm, :]
                + cw_ref[1:2, cs] * ap[off - shift:off - shift + tm, :] + cw_ref[2:3, cs] * gate)
        hist_s[:, cs] = ap[off + tm - 2 * shift:off + tm, :]
        h_s[:, cs] = (jax.nn.gelu(conv) * up).astype(BF16)


def _load_rounded(src_hbm, dst_s, stg, sem, sem_row, n_parts):
    rows = src_hbm.shape[0] // n_parts
    copy = lambda c: pltpu.make_async_copy(src_hbm.at[pl.ds(c * rows, rows)], stg.at[c % 2], sem.at[sem_row, c % 2])
    copy(0).start()
    for c in range(n_parts):
        if c + 1 < n_parts:
            copy(c + 1).start()
        copy(c).wait()
        dst_s[c * rows:(c + 1) * rows, :] = stg[c % 2].astype(BF16)


def _post_kernel(x_ref, att_ref, ssm_ref, *refs, shift, fc, n_lead, cast):
    n_hist = 3 if n_lead else 1
    hist_src, rest = refs[:n_hist], refs[n_hist:]
    wo_ref, g2_ref, wgate_ref, wup_ref, cw_ref, cb_ref, wd_ref, y_ref, cst_ref = rest[:9]
    first_step = jnp.logical_and(pl.program_id(0) == 0, pl.program_id(1) == 0)
    if cast:
        w_hbm, w_out_hbm = (wgate_ref, wup_ref, wd_ref), rest[9:12]
        xn_s, h_s, ap_s, hist_s, wgate_ref, wup_ref, wd_ref, stg_a, stg_b, sem = rest[12:]
        w_vmem = (wgate_ref, wup_ref, wd_ref)
        back = [pltpu.make_async_copy(w_vmem[k], w_out_hbm[k], sem.at[2, k]) for k in range(3)]

        @pl.when(first_step)
        def _():
            n_parts = wgate_ref.shape[0] // stg_a.shape[1]
            _load_rounded(w_hbm[0], wgate_ref, stg_a, sem, 0, n_parts)
            _load_rounded(w_hbm[1], wup_ref, stg_a, sem, 0, n_parts)
            _load_rounded(w_hbm[2], wd_ref, stg_b, sem, 1, wd_ref.shape[0] // stg_b.shape[1])
            for c in back:
                c.start()
    else:
        xn_s, h_s, ap_s, hist_s = rest[9:]
    if n_lead:
        xl_ref, al_ref, sl_ref = hist_src
        lead_s = hist_s.at[1]
        hist_s = hist_s.at[0]

        @pl.when(first_step)
        def _():
            mix = jnp.concatenate([al_ref[...], sl_ref[...].astype(BF16)], axis=1)
            xm = xl_ref[...] + _dot(mix, wo_ref[...].astype(BF16))
            r = lax.rsqrt(jnp.mean(xm * xm, axis=-1, keepdims=True) + EPS)
            gate = _dot(((xm * r) * g2_ref[...]).astype(BF16), wgate_ref[...])
            lead_s[...] = gate[n_lead - 2 * shift:n_lead, :]

        first_hist = lead_s
    else:
        first_hist, = hist_src
        hist_s = hist_s.at[0]

    @pl.when(pl.program_id(1) == 0)
    def _():
        hist_s[...] = first_hist[...]

    _post_head(x_ref, att_ref, ssm_ref, wo_ref, g2_ref, y_ref, xn_s)
    _post_chunks(0, wgate_ref.shape[1] // fc, wgate_ref, wup_ref, cw_ref, cb_ref, xn_s, h_s, ap_s, hist_s,
                 shift=shift, fc=fc)
    y_ref[...] += _dot(h_s[...], wd_ref[...])
    cst_ref[...] = hist_s[...]
    if cast:
        @pl.when(first_step)
        def _():
            for c in back:
                c.wait()


def _post(x2d, att, ssm, hist, p, tm, nb, shift, att_spec, ssm_spec, lead=None, fc=256, n_parts=4):
    rows, d = x2d.shape
    d_ff = p["w_gate"].shape[1]
    cast = p["w_gate"].dtype != BF16
    per_b = rows // nb
    nt = per_b // tm
    assert nt * tm == per_b and d_ff % fc == 0 and tm >= 2 * shift
    off = -(-2 * shift // 8) * 8
    row_spec = lambda w: pl.BlockSpec((tm, w), lambda b, t: (b * nt + t, 0))
    hist_spec = pl.BlockSpec((None, 2 * shift, d_ff), lambda b, t: (b, 0, 0))
    if lead is None:
        hist_src, hist_specs, n_lead = (hist,), [hist_spec], 0
    else:
        hist_src, hist_specs, n_lead = lead, [_const_spec(z.shape) for z in lead], lead[0].shape[0]
        assert n_lead >= 2 * shift
    ffn_w = (p["w_gate"], p["w_up"], p["w_down"])
    out_specs = [row_spec(d), hist_spec]
    out_shape = [jax.ShapeDtypeStruct((rows, d), F32), jax.ShapeDtypeStruct((nb, 2 * shift, d_ff), F32)]
    scratch = [pltpu.VMEM((tm, d), BF16), pltpu.VMEM((tm, d_ff), BF16),
               pltpu.VMEM((2, off + tm, fc), F32), pltpu.VMEM((2, 2 * shift, d_ff), F32)]
    if cast:
        assert d % n_parts == 0 and d_ff % n_parts == 0
        w_specs = [pl.BlockSpec(memory_space=pl.ANY)] * 3
        out_specs += [pl.BlockSpec(memory_space=pl.ANY)] * 3
        out_shape += [jax.ShapeDtypeStruct(w.shape, BF16) for w in ffn_w]
        scratch += [pltpu.VMEM(w.shape, BF16) for w in ffn_w]
        scratch += [pltpu.VMEM((2, d // n_parts, d_ff), F32), pltpu.VMEM((2, d_ff // n_parts, d), F32),
                    pltpu.SemaphoreType.DMA((3, 3))]
    else:
        w_specs = [_const_spec(w.shape) for w in ffn_w]
    return pl.pallas_call(
        functools.partial(_post_kernel, shift=shift, fc=fc, n_lead=n_lead, cast=cast),
        grid=(nb, nt),
        in_specs=[row_spec(d), att_spec, ssm_spec, *hist_specs,
                  _const_spec(p["w_out"].shape), _const_spec((1, d)), w_specs[0], w_specs[1],
                  _const_spec((3, d_ff)), _const_spec((1, d_ff)), w_specs[2]],
        out_specs=tuple(out_specs),
        out_shape=tuple(out_shape),
        scratch_shapes=scratch,
        compiler_params=pltpu.CompilerParams(
            dimension_semantics=("arbitrary", "arbitrary"), vmem_limit_bytes=VMEM_LIMIT),
        name="post",
    )(x2d, att, ssm, *hist_src, p["w_out"], p["norm2"], ffn_w[0], ffn_w[1], p["conv_w"], p["conv_b"], ffn_w[2])


def _prepare_params(norm1, w_in, q_norm, k_norm, lam_q, lam_k, sub_norm, ssm_a_re, ssm_a_im, ssm_log_dt,
                    ssm_b_re, ssm_b_im, ssm_c_re, ssm_c_im, ssm_d, w_glu, b_glu, w_out, norm2, w_gate,
                    w_up, ffn_conv_w, ffn_conv_b, w_down):
    l = 0
    g, pdim = ssm_a_re[l].shape
    c = SSM_GROUP
    a = N_HEADS * HEAD_DIM
    abr, abi, bbr, bbi = _ssm_params(ssm_a_re[l], ssm_a_im[l], ssm_log_dt[l], ssm_b_re[l], ssm_b_im[l])
    gi = LANES // c

    def in_blocks(bb):
        bb = jnp.transpose(bb, (1, 0, 2)).reshape(g // gi, gi, c, 1, pdim)
        same = jnp.eye(gi, dtype=bool)[None, :, None, :, None]
        return jnp.where(same, bb, 0.0).astype(BF16).reshape(g // gi, gi * c, gi * pdim)

    go = MXU // c

    def out_blocks(cc):
        cc = jnp.transpose(cc.reshape(g // go, go, c, pdim), (0, 1, 3, 2))[:, :, :, None, :]
        same = jnp.eye(go, dtype=bool)[None, :, None, :, None]
        return jnp.where(same, cc, 0.0).astype(BF16).reshape(g // go, go * pdim, go * c)

    comp = jnp.arange(MXU) // QK_DIM
    ones_blk = (comp[:, None] == comp[None, :]).astype(BF16) * (1.0 / QK_DIM)
    return {
        "norm1": norm1[l].reshape(1, -1), "w_in": w_in[l],
        "q_gain": jnp.tile(q_norm[l].reshape(-1), N_HEADS).reshape(1, a),
        "k_gain": jnp.tile(k_norm[l].reshape(-1), N_HEADS).reshape(1, a),
        "ones_blk": ones_blk.astype(BF16),
        "lam_q": lam_q[l], "lam_k": lam_k[l],
        "sub_gain": (sub_norm[l] * (1.0 - LAM_INIT)).reshape(1, HEAD_DIM),
        "abr": abr.reshape(1, g * pdim), "abi": abi.reshape(1, g * pdim),
        "bre": in_blocks(bbr), "bim": in_blocks(bbi),
        "crt": out_blocks(ssm_c_re[l]), "cit": out_blocks(ssm_c_im[l]),
        "d_skip": ssm_d[l].reshape(1, g * c), "w_glu": w_glu[l], "b_glu": b_glu[l].reshape(1, -1),
        "w_out": w_out[l], "norm2": norm2[l].reshape(1, -1),
        "w_gate": w_gate[l], "w_up": w_up[l],
        "conv_w": ffn_conv_w[l], "conv_b": ffn_conv_b[l].reshape(1, -1), "w_down": w_down[l],
    }


def _row_tile(length, cap=768):
    best = None
    for t in range(16, cap + 1, 16):
        if length % t == 0:
            best = t
    assert best is not None
    return best


def kernel(x_prompt, x_sample, cache_k, cache_v, state_ssm_re, state_ssm_im, state_ffn_conv, page_table, meta_tokens, norm1, w_in, q_norm, k_norm, lam_q, lam_k, sub_norm, ssm_a_re, ssm_a_im, ssm_log_dt, ssm_b_re, ssm_b_im, ssm_c_re, ssm_c_im, ssm_d, w_glu, b_glu, w_out, norm2, w_gate, w_up, ffn_conv_w, ffn_conv_b, w_down):
    assert norm1.shape[0] == 1, "single-layer stack"
    p = _prepare_params(norm1, w_in, q_norm, k_norm, lam_q, lam_k, sub_norm, ssm_a_re, ssm_a_im, ssm_log_dt,
                        ssm_b_re, ssm_b_im, ssm_c_re, ssm_c_im, ssm_d, w_glu, b_glu, w_out, norm2, w_gate,
                        w_up, ffn_conv_w, ffn_conv_b, w_down)
    nb, seq, d = x_prompt.shape
    db, t_new, _ = x_sample.shape
    a = N_HEADS * HEAD_DIM
    g, pdim = ssm_a_re.shape[1:]
    n_state = g * pdim
    d_ff = w_gate.shape[-1]
    length = seq + N_META

    meta = meta_tokens.astype(x_prompt.dtype)
    tm = _row_tile(length)
    qb, k, kb, v, vb, u = _project(x_prompt, meta, tm, p, u_time_major=True)
    att = _attn_prompt(qb.reshape(nb, length, a), kb.reshape(nb, length, a), vb.reshape(nb, length, a), p)
    zst = jnp.zeros((nb, n_state), F32)
    tc = _row_tile(length * nb, cap=768) // nb
    ys, hr, hi = _ssm(u.reshape(length, nb, a), zst, zst, p, tc)
    ys2 = ys.reshape(length, nb * a)
    tp = _row_tile(seq, cap=512)
    att_spec = pl.BlockSpec((pl.Element(1), pl.Element(tp), pl.Element(a)),
                            lambda b, t: (b, pl.multiple_of(N_META + t * tp, N_META), 0))
    ssm_spec = pl.BlockSpec((pl.Element(tp), pl.Element(a)),
                            lambda b, t: (pl.multiple_of(N_META + t * tp, N_META), pl.multiple_of(b * a, a)))
    yp, cst_p, *ffn_bf16 = _post(x_prompt.reshape(nb * seq, d), att, ys2, None, p, tp, nb, shift=1,
                                 att_spec=att_spec, ssm_spec=ssm_spec,
                                 lead=(meta, att[0, :N_META], ys2[:N_META, :a]))
    p = dict(p, w_gate=ffn_bf16[0], w_up=ffn_bf16[1], w_down=ffn_bf16[2])
    y_prompt = yp.reshape(nb, seq, d)
    k_prompt = k.reshape(1, nb, length, N_HEADS, HEAD_DIM)
    v_prompt = v.reshape(1, nb, length, N_HEADS, HEAD_DIM)
    ssm_re_p = hr.reshape(1, nb, g, pdim)
    ssm_im_p = hi.reshape(1, nb, g, pdim)
    conv_p = cst_p[None]

    rows_s = db * t_new
    xs2 = jnp.transpose(x_sample, (1, 0, 2)).reshape(rows_s, d)
    ts = 2 * db
    qb, k, kb, v, vb, u = _project(xs2, None, ts, p, u_time_major=False)
    to_bm = lambda z: jnp.transpose(z.reshape(t_new, db, a), (1, 0, 2))
    q_hct = jnp.transpose(qb.reshape(t_new, db, N_HEADS, 1, HEAD_DIM), (1, 2, 3, 0, 4))
    q_rep = jnp.broadcast_to(q_hct, (db, N_HEADS, 2, t_new, HEAD_DIM)).reshape(db, N_HEADS * 2 * t_new, HEAD_DIM)
    n_pool, page = cache_k.shape[1:3]
    att = _attn_decode(q_rep, to_bm(kb), to_bm(vb), cache_k[0].reshape(n_pool, page * N_HEADS, HEAD_DIM),
                       cache_v[0].reshape(n_pool, page * N_HEADS, HEAD_DIM), page_table, p)
    att_tm = jnp.transpose(att, (1, 0, 2)).reshape(rows_s, a)
    ys, hr, hi = _ssm(u.reshape(t_new, db, a), state_ssm_re[0].reshape(db, n_state),
                      state_ssm_im[0].reshape(db, n_state), p, t_new)
    hist = jnp.transpose(state_ffn_conv[0], (1, 0, 2)).reshape(1, 2 * db, d_ff)
    rows_spec = pl.BlockSpec((ts, a), lambda b, t: (t, 0))
    ysm, cst_s = _post(xs2, att_tm, ys.reshape(rows_s, a), hist, p, ts, 1, shift=db,
                       att_spec=rows_spec, ssm_spec=rows_spec)
    y_sample = jnp.transpose(ysm.reshape(t_new, db, d), (1, 0, 2))
    k_sample = jnp.transpose(k.reshape(t_new, db, N_HEADS, HEAD_DIM), (1, 0, 2, 3))[None]
    v_sample = jnp.transpose(v.reshape(t_new, db, N_HEADS, HEAD_DIM), (1, 0, 2, 3))[None]
    ssm_re_s = hr.reshape(1, db, g, pdim)
    ssm_im_s = hi.reshape(1, db, g, pdim)
    conv_s = jnp.transpose(cst_s.reshape(2, db, d_ff), (1, 0, 2))[None]

    return (y_prompt, y_sample, k_prompt, v_prompt, k_sample, v_sample,
            ssm_re_p, ssm_im_p, ssm_re_s, ssm_im_s, conv_p, conv_s)
```
